```python
import math
import jax, jax.numpy as jnp
from jax import lax
import numpy as np

D_MODEL = 2048
BATCH = 8
SEQ = 8192
DEPTH = 2

N_MIXERS = 2
N_HEADS = 16
HEAD_DIM = D_MODEL // N_HEADS
DILATED_GROUPS = ((128, 1), (512, 4), (2048, 16))
N_GROUPS = len(DILATED_GROUPS)
D_FF = 5632
CONV_W = 3
NORM_EPS = 1e-5
ALIBI_MAX = 8.0
NEG_INF = -1e30

kernel_name = "hybrid_shortconv_dilated_alibi_encoder"


def alibi_slopes(n_heads):
    return jnp.asarray(2.0 ** (-ALIBI_MAX * np.arange(1, n_heads + 1) / n_heads), dtype=jnp.float32)


def rmsnorm(x, g):
    xf = x.astype(jnp.float32)
    y = xf * lax.rsqrt(jnp.mean(xf * xf, axis=-1, keepdims=True) + NORM_EPS)
    return (y * g.astype(jnp.float32)).astype(x.dtype)


def dwconv3(u, w, b):
    up = jnp.pad(u, ((0, 0), (1, 1), (0, 0)))
    return up[:, :-2] * w[0] + up[:, 1:-1] * w[1] + up[:, 2:] * w[2] + b


def short_conv_mixer(h, w_in, conv_w, conv_b, w_out):
    z = h @ w_in
    u, gate_b, gate_c = jnp.split(z, 3, axis=-1)
    y = gate_b * dwconv3(gate_c * u, conv_w, conv_b)
    return y @ w_out


def to_strided(x, d):
    B, S = x.shape[:2]
    rest = x.shape[2:]
    y = x.reshape((B, S // d, d) + rest)
    y = jnp.swapaxes(y, 1, 2)
    return y.reshape((B * d, S // d) + rest)


def from_strided(y, d, B):
    N, L = y.shape[:2]
    rest = y.shape[2:]
    z = y.reshape((B, d, L) + rest)
    z = jnp.swapaxes(z, 1, 2)
    return z.reshape((B, L * d) + rest)


def banded_attention(q, k, v, slopes, dil, half):
    N, L, H, Dh = q.shape
    blk = half
    nb = -(-L // blk)
    Lp = nb * blk
    qb = jnp.pad(q, ((0, 0), (0, Lp - L), (0, 0), (0, 0))).reshape(N, nb, blk, H, Dh)
    kp = jnp.pad(k, ((0, 0), (blk, Lp - L + blk), (0, 0), (0, 0))).reshape(N, nb + 2, blk, H, Dh)
    vp = jnp.pad(v, ((0, 0), (blk, Lp - L + blk), (0, 0), (0, 0))).reshape(N, nb + 2, blk, H, Dh)
    kw = jnp.concatenate([kp[:, :-2], kp[:, 1:-1], kp[:, 2:]], axis=2)
    vw = jnp.concatenate([vp[:, :-2], vp[:, 1:-1], vp[:, 2:]], axis=2)
    qi = jnp.arange(Lp).reshape(nb, blk)
    kj = jnp.arange(nb)[:, None] * blk + jnp.arange(3 * blk)[None, :] - blk
    delta = jnp.abs(kj[:, None, :] - qi[:, :, None])
    valid = (delta <= half) & ((kj >= 0) & (kj < L))[:, None, :]
    s = jnp.einsum('nbqhd,nbkhd->nbhqk', qb, kw).astype(jnp.float32) * (Dh ** -0.5)
    bias = -slopes[None, :, None, None] * (dil * delta).astype(jnp.float32)[:, None, :, :]
    s = jnp.where(valid[:, None, :, :][None], s + bias[None], NEG_INF)
    m = jnp.max(s, axis=-1, keepdims=True)
    p = jnp.exp(s - m)
    den = jnp.sum(p, axis=-1)
    o = jnp.einsum('nbhqk,nbkhd->nbqhd', p, vw.astype(jnp.float32))
    den_t = jnp.swapaxes(den, 2, 3)
    o = o / den_t[..., None]
    lse = jnp.swapaxes(m[..., 0], 2, 3) + jnp.log(den_t)
    return o.reshape(N, Lp, H, Dh)[:, :L], lse.reshape(N, Lp, H)[:, :L]


def dilated_attention_mixer(h, w_qkv, w_out):
    B, S, D = h.shape
    qkv = (h @ w_qkv).reshape(B, S, N_GROUPS, 3, N_HEADS, HEAD_DIM)
    slopes = alibi_slopes(N_HEADS)
    outs, lses = [], []
    for g, (window, dil) in enumerate(DILATED_GROUPS):
        half = (window // 2) // dil
        q = to_strided(qkv[:, :, g, 0], dil)
        k = to_strided(qkv[:, :, g, 1], dil)
        v = to_strided(qkv[:, :, g, 2], dil)
        o, lse = banded_attention(q, k, v, slopes, dil, half)
        outs.append(from_strided(o, dil, B))
        lses.append(from_strided(lse, dil, B))
    wts = jax.nn.softmax(jnp.stack(lses, axis=0), axis=0)
    o = jnp.sum(wts[..., None] * jnp.stack(outs, axis=0), axis=0)
    return o.reshape(B, S, D).astype(h.dtype) @ w_out


def conv_ffn(h, w_up, conv_w, conv_b, w_down):
    u = dwconv3(h @ w_up, conv_w, conv_b)
    a, b = jnp.split(u, 2, axis=-1)
    return (jax.nn.silu(a) * b) @ w_down


def _fwd_setup_inputs(seed: int = 0) -> dict:
    key = jax.random.key(seed)
    ks = jax.random.split(key, 16)
    D, F = D_MODEL, D_FF
    n_a = (DEPTH + N_MIXERS - 1) // N_MIXERS
    n_b = DEPTH // N_MIXERS
    nrm = lambda k, shape, s: jax.random.normal(k, shape, jnp.float32) * s
    return {
        "x": nrm(ks[0], (BATCH, SEQ, D), 1.0),
        "mix_norm_g": 1.0 + nrm(ks[1], (DEPTH, D), 0.02),
        "ffn_norm_g": 1.0 + nrm(ks[2], (DEPTH, D), 0.02),
        "final_norm_g": 1.0 + nrm(ks[3], (D,), 0.02),
        "sc_w_in": nrm(ks[4], (n_a, D, 3 * D), D ** -0.5),
        "sc_conv_w": nrm(ks[5], (n_a, CONV_W, D), CONV_W ** -0.5),
        "sc_conv_b": nrm(ks[6], (n_a, D), 0.01),
        "sc_w_out": nrm(ks[7], (n_a, D, D), D ** -0.5),
        "attn_w_qkv": nrm(ks[8], (n_b, D, N_GROUPS * 3 * N_HEADS * HEAD_DIM), D ** -0.5),
        "attn_w_out": nrm(ks[9], (n_b, N_HEADS * HEAD_DIM, D), (N_HEADS * HEAD_DIM) ** -0.5),
        "ffn_w_up": nrm(ks[10], (DEPTH, D, 2 * F), D ** -0.5),
        "ffn_conv_w": nrm(ks[11], (DEPTH, CONV_W, 2 * F), CONV_W ** -0.5),
        "ffn_conv_b": nrm(ks[12], (DEPTH, 2 * F), 0.01),
        "ffn_w_down": nrm(ks[13], (DEPTH, F, D), F ** -0.5),
    }


def _fwd_reference(x, mix_norm_g, ffn_norm_g, final_norm_g, sc_w_in, sc_conv_w, sc_conv_b, sc_w_out,
              attn_w_qkv, attn_w_out, ffn_w_up, ffn_conv_w, ffn_conv_b, ffn_w_down):
    for i in range(DEPTH):
        h = rmsnorm(x, mix_norm_g[i])
        j = i // N_MIXERS
        if i % N_MIXERS == 0:
            x = x + short_conv_mixer(h, sc_w_in[j], sc_conv_w[j], sc_conv_b[j], sc_w_out[j])
        else:
            x = x + dilated_attention_mixer(h, attn_w_qkv[j], attn_w_out[j])
        h = rmsnorm(x, ffn_norm_g[i])
        x = x + conv_ffn(h, ffn_w_up[i], ffn_conv_w[i], ffn_conv_b[i], ffn_w_down[i])
    return rmsnorm(x, final_norm_g)


import jax as _jax
import jax.numpy as _jnp

TWIN_FORMAT = 'train_step'
FWD_PARAMS = ['x', 'mix_norm_g', 'ffn_norm_g', 'final_norm_g', 'sc_w_in', 'sc_conv_w', 'sc_conv_b', 'sc_w_out', 'attn_w_qkv', 'attn_w_out', 'ffn_w_up', 'ffn_conv_w', 'ffn_conv_b', 'ffn_w_down']
TWIN_WEIGHTS = ['mix_norm_g', 'ffn_norm_g', 'final_norm_g', 'sc_w_in', 'sc_conv_w', 'sc_conv_b', 'sc_w_out', 'attn_w_qkv', 'attn_w_out', 'ffn_w_up', 'ffn_conv_w', 'ffn_conv_b', 'ffn_w_down']
TWIN_DIFF_INPUT = 'x'
TWIN_INPUTS = ['x', 'mix_norm_g', 'ffn_norm_g', 'final_norm_g', 'sc_w_in', 'sc_conv_w', 'sc_conv_b', 'sc_w_out', 'attn_w_qkv', 'attn_w_out', 'ffn_w_up', 'ffn_conv_w', 'ffn_conv_b', 'ffn_w_down', 'loss_target', 'm_mix_norm_g', 'm_ffn_norm_g', 'm_final_norm_g', 'm_sc_w_in', 'm_sc_conv_w', 'm_sc_conv_b', 'm_sc_w_out', 'm_attn_w_qkv', 'm_attn_w_out', 'm_ffn_w_up', 'm_ffn_conv_w', 'm_ffn_conv_b', 'm_ffn_w_down', 'v_mix_norm_g', 'v_ffn_norm_g', 'v_final_norm_g', 'v_sc_w_in', 'v_sc_conv_w', 'v_sc_conv_b', 'v_sc_w_out', 'v_attn_w_qkv', 'v_attn_w_out', 'v_ffn_w_up', 'v_ffn_conv_w', 'v_ffn_conv_b', 'v_ffn_w_down']
TWIN_OUTPUTS = ['loss', 'grad_x', 'grad_mix_norm_g', 'grad_ffn_norm_g', 'grad_final_norm_g', 'grad_sc_w_in', 'grad_sc_conv_w', 'grad_sc_conv_b', 'grad_sc_w_out', 'grad_attn_w_qkv', 'grad_attn_w_out', 'grad_ffn_w_up', 'grad_ffn_conv_w', 'grad_ffn_conv_b', 'grad_ffn_w_down', 'delta_mix_norm_g', 'delta_ffn_norm_g', 'delta_final_norm_g', 'delta_sc_w_in', 'delta_sc_conv_w', 'delta_sc_conv_b', 'delta_sc_w_out', 'delta_attn_w_qkv', 'delta_attn_w_out', 'delta_ffn_w_up', 'delta_ffn_conv_w', 'delta_ffn_conv_b', 'delta_ffn_w_down', 'new_m_mix_norm_g', 'new_m_ffn_norm_g', 'new_m_final_norm_g', 'new_m_sc_w_in', 'new_m_sc_conv_w', 'new_m_sc_conv_b', 'new_m_sc_w_out', 'new_m_attn_w_qkv', 'new_m_attn_w_out', 'new_m_ffn_w_up', 'new_m_ffn_conv_w', 'new_m_ffn_conv_b', 'new_m_ffn_w_down', 'new_v_mix_norm_g', 'new_v_ffn_norm_g', 'new_v_final_norm_g', 'new_v_sc_w_in', 'new_v_sc_conv_w', 'new_v_sc_conv_b', 'new_v_sc_w_out', 'new_v_attn_w_qkv', 'new_v_attn_w_out', 'new_v_ffn_w_up', 'new_v_ffn_conv_w', 'new_v_ffn_conv_b', 'new_v_ffn_w_down']
TWIN_LEAF_KINDS = {'loss': 'loss', 'grad_x': 'grad_x', 'grad_mix_norm_g': 'grad_w', 'grad_ffn_norm_g': 'grad_w', 'grad_final_norm_g': 'grad_w', 'grad_sc_w_in': 'grad_w', 'grad_sc_conv_w': 'grad_w', 'grad_sc_conv_b': 'grad_w', 'grad_sc_w_out': 'grad_w', 'grad_attn_w_qkv': 'grad_w', 'grad_attn_w_out': 'grad_w', 'grad_ffn_w_up': 'grad_w', 'grad_ffn_conv_w': 'grad_w', 'grad_ffn_conv_b': 'grad_w', 'grad_ffn_w_down': 'grad_w', 'delta_mix_norm_g': 'delta_w', 'delta_ffn_norm_g': 'delta_w', 'delta_final_norm_g': 'delta_w', 'delta_sc_w_in': 'delta_w', 'delta_sc_conv_w': 'delta_w', 'delta_sc_conv_b': 'delta_w', 'delta_sc_w_out': 'delta_w', 'delta_attn_w_qkv': 'delta_w', 'delta_attn_w_out': 'delta_w', 'delta_ffn_w_up': 'delta_w', 'delta_ffn_conv_w': 'delta_w', 'delta_ffn_conv_b': 'delta_w', 'delta_ffn_w_down': 'delta_w', 'new_m_mix_norm_g': 'new_m', 'new_m_ffn_norm_g': 'new_m', 'new_m_final_norm_g': 'new_m', 'new_m_sc_w_in': 'new_m', 'new_m_sc_conv_w': 'new_m', 'new_m_sc_conv_b': 'new_m', 'new_m_sc_w_out': 'new_m', 'new_m_attn_w_qkv': 'new_m', 'new_m_attn_w_out': 'new_m', 'new_m_ffn_w_up': 'new_m', 'new_m_ffn_conv_w': 'new_m', 'new_m_ffn_conv_b': 'new_m', 'new_m_ffn_w_down': 'new_m', 'new_v_mix_norm_g': 'new_v', 'new_v_ffn_norm_g': 'new_v', 'new_v_final_norm_g': 'new_v', 'new_v_sc_w_in': 'new_v', 'new_v_sc_conv_w': 'new_v', 'new_v_sc_conv_b': 'new_v', 'new_v_sc_w_out': 'new_v', 'new_v_attn_w_qkv': 'new_v', 'new_v_attn_w_out': 'new_v', 'new_v_ffn_w_up': 'new_v', 'new_v_ffn_conv_w': 'new_v', 'new_v_ffn_conv_b': 'new_v', 'new_v_ffn_w_down': 'new_v'}


def _forward(args):
    return _fwd_reference(*[args[k] for k in FWD_PARAMS])


def _output_shape():
    def fwd():
        inp = _fwd_setup_inputs(0)
        return _fwd_reference(*[inp[k] for k in FWD_PARAMS])
    out = _jax.eval_shape(fwd)
    return out.shape, out.dtype

N_MICROBATCH = 1
ADAM_LR = 0.001
ADAM_B1 = 0.9
ADAM_B2 = 0.999
ADAM_EPS = 1e-08
ADAM_WD = 0.01
ADAM_STEP = 10
PER_EXAMPLE_BATCH_AXIS = {'x': 0, 'loss_target': 0}
SHARED_INPUTS = []
_WEIGHT_DTYPES = {'mix_norm_g': _jnp.float32, 'ffn_norm_g': _jnp.float32, 'final_norm_g': _jnp.float32, 'sc_w_in': _jnp.float32, 'sc_conv_w': _jnp.float32, 'sc_conv_b': _jnp.float32, 'sc_w_out': _jnp.float32, 'attn_w_qkv': _jnp.float32, 'attn_w_out': _jnp.float32, 'ffn_w_up': _jnp.float32, 'ffn_conv_w': _jnp.float32, 'ffn_conv_b': _jnp.float32, 'ffn_w_down': _jnp.float32}
MOMENT_SCALE = {'mix_norm_g': 1.370660e-01, 'ffn_norm_g': 7.253793e-02, 'final_norm_g': 3.190343e+01, 'sc_w_in': 1.038445e-01, 'sc_conv_w': 1.060383e-01, 'sc_conv_b': 1.046584e-01, 'sc_w_out': 1.039846e-01, 'attn_w_qkv': 1.103683e-02, 'attn_w_out': 2.148307e-02, 'ffn_w_up': 3.076684e-02, 'ffn_conv_w': 3.098473e-02, 'ffn_conv_b': 3.027496e-02, 'ffn_w_down': 5.030292e-02}


def _to_microbatches(a, axis):
    t = _jnp.moveaxis(a, axis, 0)
    t = t.reshape((N_MICROBATCH, t.shape[0] // N_MICROBATCH) + t.shape[1:])
    return _jnp.moveaxis(t, 1, axis + 1)


def setup_inputs(seed: int = 0) -> dict:
    inp = _fwd_setup_inputs(seed)
    key = _jax.random.fold_in(_jax.random.key(seed), 7919)
    shape, _ = _output_shape()
    out = dict(inp)
    out["loss_target"] = _jax.random.normal(_jax.random.fold_in(key, 0), shape, _jnp.float32)
    for i, name in enumerate(TWIN_WEIGHTS):
        w = inp[name].astype(_jnp.float32)
        if MOMENT_SCALE is None:
            s = _jnp.sqrt(_jnp.mean(_jnp.square(w)) + 1e-30)
        else:
            s = MOMENT_SCALE[name]
        km, kv = _jax.random.split(_jax.random.fold_in(key, i + 1))
        out[name] = w
        out["m_" + name] = s * _jax.random.normal(km, w.shape, _jnp.float32)
        out["v_" + name] = (s * s) * _jax.random.uniform(kv, w.shape, _jnp.float32, 0.5, 1.5)
    if N_MICROBATCH > 1:
        for name, axis in PER_EXAMPLE_BATCH_AXIS.items():
            out[name] = _to_microbatches(out[name], axis)
    return {'x': out['x'], 'mix_norm_g': out['mix_norm_g'], 'ffn_norm_g': out['ffn_norm_g'], 'final_norm_g': out['final_norm_g'], 'sc_w_in': out['sc_w_in'], 'sc_conv_w': out['sc_conv_w'], 'sc_conv_b': out['sc_conv_b'], 'sc_w_out': out['sc_w_out'], 'attn_w_qkv': out['attn_w_qkv'], 'attn_w_out': out['attn_w_out'], 'ffn_w_up': out['ffn_w_up'], 'ffn_conv_w': out['ffn_conv_w'], 'ffn_conv_b': out['ffn_conv_b'], 'ffn_w_down': out['ffn_w_down'], 'loss_target': out['loss_target'], 'm_mix_norm_g': out['m_mix_norm_g'], 'm_ffn_norm_g': out['m_ffn_norm_g'], 'm_final_norm_g': out['m_final_norm_g'], 'm_sc_w_in': out['m_sc_w_in'], 'm_sc_conv_w': out['m_sc_conv_w'], 'm_sc_conv_b': out['m_sc_conv_b'], 'm_sc_w_out': out['m_sc_w_out'], 'm_attn_w_qkv': out['m_attn_w_qkv'], 'm_attn_w_out': out['m_attn_w_out'], 'm_ffn_w_up': out['m_ffn_w_up'], 'm_ffn_conv_w': out['m_ffn_conv_w'], 'm_ffn_conv_b': out['m_ffn_conv_b'], 'm_ffn_w_down': out['m_ffn_w_down'], 'v_mix_norm_g': out['v_mix_norm_g'], 'v_ffn_norm_g': out['v_ffn_norm_g'], 'v_final_norm_g': out['v_final_norm_g'], 'v_sc_w_in': out['v_sc_w_in'], 'v_sc_conv_w': out['v_sc_conv_w'], 'v_sc_conv_b': out['v_sc_conv_b'], 'v_sc_w_out': out['v_sc_w_out'], 'v_attn_w_qkv': out['v_attn_w_qkv'], 'v_attn_w_out': out['v_attn_w_out'], 'v_ffn_w_up': out['v_ffn_w_up'], 'v_ffn_conv_w': out['v_ffn_conv_w'], 'v_ffn_conv_b': out['v_ffn_conv_b'], 'v_ffn_w_down': out['v_ffn_w_down']}


def _loss(weights, diff, rest, loss_target):
    with _jax.named_scope("forward"):
        args = {**rest, TWIN_DIFF_INPUT: diff, **{k: w.astype(_WEIGHT_DTYPES[k]) for k, w in weights.items()}}
        y = _forward(args)
    with _jax.named_scope("loss_head"):
        err = _jnp.square(y.astype(_jnp.float32) - loss_target)
        return 0.5 * _jnp.sum(_jnp.mean(err, axis=-1)) if err.ndim else 0.5 * err


def _adamw(w, g, m, v):
    m = ADAM_B1 * m + (1.0 - ADAM_B1) * g
    v = ADAM_B2 * v + (1.0 - ADAM_B2) * _jnp.square(g)
    m_hat = m / (1.0 - ADAM_B1 ** ADAM_STEP)
    v_hat = v / (1.0 - ADAM_B2 ** ADAM_STEP)
    delta = -ADAM_LR * (m_hat / (_jnp.sqrt(v_hat) + ADAM_EPS) + ADAM_WD * w)
    return delta, m, v


def reference(x, mix_norm_g, ffn_norm_g, final_norm_g, sc_w_in, sc_conv_w, sc_conv_b, sc_w_out, attn_w_qkv, attn_w_out, ffn_w_up, ffn_conv_w, ffn_conv_b, ffn_w_down, loss_target, m_mix_norm_g, m_ffn_norm_g, m_final_norm_g, m_sc_w_in, m_sc_conv_w, m_sc_conv_b, m_sc_w_out, m_attn_w_qkv, m_attn_w_out, m_ffn_w_up, m_ffn_conv_w, m_ffn_conv_b, m_ffn_w_down, v_mix_norm_g, v_ffn_norm_g, v_final_norm_g, v_sc_w_in, v_sc_conv_w, v_sc_conv_b, v_sc_w_out, v_attn_w_qkv, v_attn_w_out, v_ffn_w_up, v_ffn_conv_w, v_ffn_conv_b, v_ffn_w_down):
    given = dict(x=x, mix_norm_g=mix_norm_g, ffn_norm_g=ffn_norm_g, final_norm_g=final_norm_g, sc_w_in=sc_w_in, sc_conv_w=sc_conv_w, sc_conv_b=sc_conv_b, sc_w_out=sc_w_out, attn_w_qkv=attn_w_qkv, attn_w_out=attn_w_out, ffn_w_up=ffn_w_up, ffn_conv_w=ffn_conv_w, ffn_conv_b=ffn_conv_b, ffn_w_down=ffn_w_down, loss_target=loss_target, m_mix_norm_g=m_mix_norm_g, m_ffn_norm_g=m_ffn_norm_g, m_final_norm_g=m_final_norm_g, m_sc_w_in=m_sc_w_in, m_sc_conv_w=m_sc_conv_w, m_sc_conv_b=m_sc_conv_b, m_sc_w_out=m_sc_w_out, m_attn_w_qkv=m_attn_w_qkv, m_attn_w_out=m_attn_w_out, m_ffn_w_up=m_ffn_w_up, m_ffn_conv_w=m_ffn_conv_w, m_ffn_conv_b=m_ffn_conv_b, m_ffn_w_down=m_ffn_w_down, v_mix_norm_g=v_mix_norm_g, v_ffn_norm_g=v_ffn_norm_g, v_final_norm_g=v_final_norm_g, v_sc_w_in=v_sc_w_in, v_sc_conv_w=v_sc_conv_w, v_sc_conv_b=v_sc_conv_b, v_sc_w_out=v_sc_w_out, v_attn_w_qkv=v_attn_w_qkv, v_attn_w_out=v_attn_w_out, v_ffn_w_up=v_ffn_w_up, v_ffn_conv_w=v_ffn_conv_w, v_ffn_conv_b=v_ffn_conv_b, v_ffn_w_down=v_ffn_w_down)
    weights = {n: given[n] for n in TWIN_WEIGHTS}
    shared = {n: given[n] for n in SHARED_INPUTS}
    per_example = {n: given[n] for n in ['x']}
    grad_fn = _jax.value_and_grad(_loss, argnums=(0, 1))

    def one_microbatch(ex, loss_target):
        ex = dict(ex)
        diff = ex.pop(TWIN_DIFF_INPUT)
        return grad_fn(weights, diff, {**shared, **ex}, loss_target)

    if N_MICROBATCH == 1:
        loss, (grad_w, grad_x) = one_microbatch(per_example, given["loss_target"])
    else:
        def body(carry, xs):
            loss_sum, grad_sum = carry
            l_k, (gw_k, gx_k) = one_microbatch(xs[0], xs[1])
            with _jax.named_scope("update"):
                return (loss_sum + l_k, _jax.tree.map(_jnp.add, grad_sum, gw_k)), gx_k

        init = (_jnp.zeros((), _jnp.float32), _jax.tree.map(_jnp.zeros_like, weights))
        (loss, grad_w), grad_x = _jax.lax.scan(body, init, (per_example, given["loss_target"]))
    with _jax.named_scope("update"):
        delta_w, new_m, new_v = {}, {}, {}
        for n in TWIN_WEIGHTS:
            delta_w[n], new_m[n], new_v[n] = _adamw(weights[n], grad_w[n], given["m_" + n], given["v_" + n])
    return (loss, grad_x, *[grad_w[n] for n in TWIN_WEIGHTS], *[delta_w[n] for n in TWIN_WEIGHTS],
            *[new_m[n] for n in TWIN_WEIGHTS], *[new_v[n] for n in TWIN_WEIGHTS])
```

```python
import functools
import math

import jax
import jax.numpy as jnp
from jax import lax
from jax.experimental import pallas as pl
from jax.experimental.pallas import tpu as pltpu

F32 = jnp.float32
BF = jnp.bfloat16
MESH = pl.DeviceIdType.MESH

HEAD_DIM = 128
ATTN_HALF = 64
ATTN_BLOCK = 128
DILATIONS = (1, 4, 16)
STAT_LANES = 128
HALO = 16
NORM_EPS = 1e-5
ALIBI_MAX = 8.0
NEG_INF = -1e30
N_CHIPS = 4
VMEM_LIMIT = 56 * 1024 * 1024

ADAM_LR = 0.001
ADAM_B1 = 0.9
ADAM_B2 = 0.999
ADAM_EPS = 1e-08
ADAM_WD = 0.01
ADAM_STEP = 10


def _pick(n, cands):
    for c in cands:
        if n % c == 0:
            return c
    raise ValueError(f"no tile for {n} in {cands}")


def _row_tile(rows, cols, max_elems=1 << 19):
    for c in (512, 256, 128, 64, 32, 16):
        if rows % c == 0 and c * cols <= max_elems:
            return c
    raise ValueError(f"no row tile for {rows}x{cols}")


def _params(*sem):
    return pltpu.CompilerParams(dimension_semantics=sem, vmem_limit_bytes=VMEM_LIMIT)


def _matmul(name, a, b, out_shape, grid, a_spec, b_spec, o_spec, contract, acc_shape,
            res=None, res_spec=None, prev=None):
    nk = grid[2]

    def body(*refs):
        refs = list(refs)
        if prev is not None:
            refs.pop(0)
        a_ref, b_ref = refs[0], refs[1]
        res_ref = refs[2] if res is not None else None
        o_ref = refs[3] if res is not None else refs[2]
        acc_ref = refs[-1]
        part = lax.dot_general(a_ref[...], b_ref[...], contract, preferred_element_type=F32)

        def finish(total):
            if res_ref is not None:
                total = total + res_ref[...]
            o_ref[...] = total.astype(o_ref.dtype)

        if nk == 1:
            finish(part)
        else:
            k = pl.program_id(2)

            @pl.when(k == 0)
            def _():
                acc_ref[...] = part

            @pl.when(jnp.logical_and(k > 0, k < nk - 1))
            def _():
                acc_ref[...] += part

            @pl.when(k == nk - 1)
            def _():
                finish(acc_ref[...] + part)

    operands, in_specs, aliases = [], [], {}
    if prev is not None:
        operands.append(prev)
        in_specs.append(pl.BlockSpec(memory_space=pl.ANY))
        aliases = {0: 0}
    operands += [a, b]
    in_specs += [a_spec, b_spec]
    if res is not None:
        operands.append(res)
        in_specs.append(res_spec)
    return pl.pallas_call(
        body, name=name, out_shape=out_shape, grid=grid, in_specs=in_specs, out_specs=o_spec,
        scratch_shapes=[pltpu.VMEM(acc_shape if nk > 1 else (8, 128), F32)],
        input_output_aliases=aliases,
        compiler_params=_params("parallel", "parallel", "arbitrary"),
    )(*operands)


NN = (((1,), (0,)), ((), ()))
NT = (((1,), (1,)), ((), ()))
TN = (((0,), (0,)), ((), ()))

_COL_TILES = (1536, 1408, 1024, 768, 512, 384, 256, 128)


def _mm_nn_col(name, a, w, layer, col_off=0, ncols=None, out_dtype=BF):
    M, K = a.shape
    _, _, R, C = w.shape
    assert R == K
    ncols = N_CHIPS * C if ncols is None else ncols
    tn = _pick(math.gcd(C, math.gcd(ncols, col_off) if col_off else ncols), _COL_TILES)
    tm = _pick(M, (1024, 512, 256))
    nb, off = C // tn, col_off // tn
    return _matmul(
        name, a, w, jax.ShapeDtypeStruct((M, ncols), out_dtype), (M // tm, ncols // tn, 1),
        pl.BlockSpec((tm, K), lambda i, j, k: (i, 0)),
        pl.BlockSpec((None, None, K, tn), lambda i, j, k: ((j + off) // nb, layer, 0, (j + off) % nb)),
        pl.BlockSpec((tm, tn), lambda i, j, k: (i, j)), NN, (tm, tn))


def _mm_nn_row(name, a, w, layer, res):
    M, K = a.shape
    _, _, R, C = w.shape
    assert N_CHIPS * R == K
    tk = _pick(R, (1408, 1024, 512, 256, 128))
    tm = _pick(M, (1024, 512, 256))
    tn = _pick(C, (1024, 512, 256))
    kb = R // tk
    return _matmul(
        name, a, w, jax.ShapeDtypeStruct((M, C), F32), (M // tm, C // tn, K // tk),
        pl.BlockSpec((tm, tk), lambda i, j, k: (i, k)),
        pl.BlockSpec((None, None, tk, tn), lambda i, j, k: (k // kb, layer, k % kb, j)),
        pl.BlockSpec((tm, tn), lambda i, j, k: (i, j)), NN, (tm, tn),
        res=res, res_spec=pl.BlockSpec((tm, tn), lambda i, j, k: (i, j)))


def _mm_nt_col(name, dy, w, layer, col_off=0, out_dtype=F32):
    M, n = dy.shape
    _, _, R, C = w.shape
    tk = _pick(math.gcd(C, math.gcd(n, col_off) if col_off else n), _COL_TILES)
    tm = _pick(M, (1024, 512, 256))
    tn = _pick(R, (1024, 512, 256))
    nb, off = C // tk, col_off // tk
    return _matmul(
        name, dy, w, jax.ShapeDtypeStruct((M, R), out_dtype), (M // tm, R // tn, n // tk),
        pl.BlockSpec((tm, tk), lambda i, j, k: (i, k)),
        pl.BlockSpec((None, None, tn, tk), lambda i, j, k: ((k + off) // nb, layer, j, (k + off) % nb)),
        pl.BlockSpec((tm, tn), lambda i, j, k: (i, j)), NT, (tm, tn))


def _mm_nt_row(name, dy, w, layer, out_dtype=BF):
    M, C2 = dy.shape
    _, _, R, C = w.shape
    assert C2 == C
    tn = _pick(R, (1408, 1024, 512, 256, 128))
    tm = _pick(M, (1024, 512, 256))
    rb = R // tn
    return _matmul(
        name, dy, w, jax.ShapeDtypeStruct((M, N_CHIPS * R), out_dtype), (M // tm, N_CHIPS * R // tn, 1),
        pl.BlockSpec((tm, C), lambda i, j, k: (i, 0)),
        pl.BlockSpec((None, None, tn, C), lambda i, j, k: (j // rb, layer, j % rb, 0)),
        pl.BlockSpec((tm, tn), lambda i, j, k: (i, j)), NT, (tm, tn))


def _half_index(rows, layer, tkx):
    if layer is None:
        hb = rows // 2 // tkx
        return rows // 2, lambda i: (i // hb, i % hb)
    return rows, lambda i: (layer, i)


def _mm_tn_col(name, xa, dy, C, col_off=0, prev=None, layer=None):
    M, K = xa.shape
    _, n = dy.shape
    tn = _pick(math.gcd(C, math.gcd(n, col_off) if col_off else n), _COL_TILES)
    tkx = _pick(K // 2 if layer is None else K, (1024, 512, 256, 128))
    tmr = _pick(M, (512, 256))
    rh, split = _half_index(K, layer, tkx)
    nb, off = C // tn, col_off // tn
    return _matmul(
        name, xa, dy, jax.ShapeDtypeStruct((N_CHIPS, 2, rh, C), BF), (K // tkx, n // tn, M // tmr),
        pl.BlockSpec((tmr, tkx), lambda i, j, k: (k, i)),
        pl.BlockSpec((tmr, tn), lambda i, j, k: (k, j)),
        pl.BlockSpec((None, None, tkx, tn), lambda i, j, k: ((j + off) // nb, *split(i), (j + off) % nb)),
        TN, (tkx, tn), prev=prev)


def _mm_tn_row(name, xa, dy, prev=None, layer=None):
    M, K = xa.shape
    _, C = dy.shape
    R = K // N_CHIPS
    tkx = _pick(R // 2 if layer is None else R, (1408, 1024, 512, 256, 128))
    tmr = _pick(M, (512, 256))
    tn = _pick(C, (1024, 512, 256))
    rh, split = _half_index(R, layer, tkx)
    rb = R // tkx
    return _matmul(
        name, xa, dy, jax.ShapeDtypeStruct((N_CHIPS, 2, rh, C), BF), (K // tkx, C // tn, M // tmr),
        pl.BlockSpec((tmr, tkx), lambda i, j, k: (k, i)),
        pl.BlockSpec((tmr, tn), lambda i, j, k: (k, j)),
        pl.BlockSpec((None, None, tkx, tn), lambda i, j, k: (i // rb, *split(i % rb), j)),
        TN, (tkx, tn), prev=prev)


NORM_ROWS = 256
LANES = 128


def _chunk_scratch(tm, width):
    return pltpu.VMEM((width // LANES, tm, LANES), F32)


def _store_chunks(scr, value):
    for c in range(scr.shape[0]):
        scr[c] = value[:, c * LANES:(c + 1) * LANES]


def _load_chunks(scr):
    return jnp.concatenate([scr[c] for c in range(scr.shape[0])], axis=1)


def _to_residue_major(scr, o_ref, d, dtype):
    tm = scr.shape[1]
    for c in range(scr.shape[0]):
        for res in range(d):
            o_ref[res, :, c * LANES:(c + 1) * LANES] = scr[c, pl.ds(res, tm // d, stride=d), :].astype(dtype)


def _to_natural(scr, ref, d):
    tm = scr.shape[1]
    for c in range(scr.shape[0]):
        for res in range(d):
            scr[c, pl.ds(res, tm // d, stride=d), :] = ref[res, :, c * LANES:(c + 1) * LANES].astype(F32)


def _rmsnorm_fwd(name, x, g, dilated=False):
    S, D = x.shape
    tm = NORM_ROWS
    dils = DILATIONS[1:] if dilated else ()

    def body(x_ref, g_ref, h_ref, *rest):
        xv = x_ref[...]
        r = lax.rsqrt(jnp.mean(xv * xv, axis=1, keepdims=True) + NORM_EPS)
        h = xv * r * g_ref[...]
        h_ref[...] = h.astype(BF)
        if dils:
            scr = rest[-1]
            _store_chunks(scr, h)
            for o_ref, d in zip(rest[:-1], dils):
                _to_residue_major(scr, o_ref, d, BF)

    out_shape = [jax.ShapeDtypeStruct((S, D), BF)]
    out_specs = [pl.BlockSpec((tm, D), lambda i: (i, 0))]
    for d in dils:
        out_shape.append(jax.ShapeDtypeStruct((d, S // d, D), BF))
        out_specs.append(pl.BlockSpec((d, tm // d, D), lambda i: (0, i, 0)))
    outs = pl.pallas_call(
        body, name=name, out_shape=out_shape, grid=(S // tm,),
        in_specs=[pl.BlockSpec((tm, D), lambda i: (i, 0)), pl.BlockSpec((1, D), lambda i: (0, 0))],
        out_specs=out_specs,
        scratch_shapes=[_chunk_scratch(tm, D)] if dils else [],
        compiler_params=_params("parallel"),
    )(x, g)
    return [outs[0]] + [o.reshape(S, D) for o in outs[1:]]


def _rmsnorm_bwd(name, x, g, dhs=(), dres=None, target=None):
    S, D = x.shape
    tm = NORM_ROWS
    n_dh = len(dhs)

    def body(*refs):
        refs = list(refs)
        x_ref, g_ref = refs[0], refs[1]
        dh_refs = refs[2:2 + n_dh]
        pos = 2 + n_dh
        dres_ref = tgt_ref = None
        if dres is not None:
            dres_ref = refs[pos]
            pos += 1
        if target is not None:
            tgt_ref = refs[pos]
            pos += 1
        dx_ref, dxb_ref, dg_ref = refs[pos:pos + 3]
        pos += 3
        loss_ref = None
        if target is not None:
            loss_ref = refs[pos]
            pos += 1
        scr = refs[pos] if any(d > 1 for _, d in dhs) else None
        i = pl.program_id(0)

        xv = x_ref[...]
        gv = g_ref[...]
        r = lax.rsqrt(jnp.mean(xv * xv, axis=1, keepdims=True) + NORM_EPS)
        xhat = xv * r
        if target is not None:
            err = xhat * gv - tgt_ref[...]
            dh = err * (1.0 / D)
            part = jnp.sum(jnp.sum(err * err, axis=1, keepdims=True), axis=0, keepdims=True) * (0.5 / D)
        else:
            dh = None
            for ref, d in zip(dh_refs, [d for _, d in dhs]):
                if d == 1:
                    v = ref[...]
                else:
                    _to_natural(scr, ref, d)
                    v = _load_chunks(scr)
                dh = v if dh is None else dh + v
        dxhat = dh * gv
        dx = r * (dxhat - xhat * jnp.mean(dxhat * xhat, axis=1, keepdims=True))
        if dres_ref is not None:
            dx = dx + dres_ref[...]
        dx_ref[...] = dx
        dxb_ref[...] = dx.astype(BF)
        dg = jnp.sum(dh * xhat, axis=0, keepdims=True)

        @pl.when(i == 0)
        def _():
            dg_ref[...] = dg
            if loss_ref is not None:
                loss_ref[...] = jnp.broadcast_to(part, loss_ref.shape)

        @pl.when(i > 0)
        def _():
            dg_ref[...] += dg
            if loss_ref is not None:
                loss_ref[...] += jnp.broadcast_to(part, loss_ref.shape)

    row = pl.BlockSpec((tm, D), lambda i: (i, 0))
    operands = [x, g]
    in_specs = [row, pl.BlockSpec((1, D), lambda i: (0, 0))]
    for arr, d in dhs:
        if d == 1:
            operands.append(arr)
            in_specs.append(row)
        else:
            operands.append(arr.reshape(d, S // d, D))
            in_specs.append(pl.BlockSpec((d, tm // d, D), lambda i: (0, i, 0)))
    if dres is not None:
        operands.append(dres)
        in_specs.append(row)
    if target is not None:
        operands.append(target)
        in_specs.append(row)
    out_shape = [jax.ShapeDtypeStruct((S, D), F32), jax.ShapeDtypeStruct((S, D), BF),
                 jax.ShapeDtypeStruct((1, D), F32)]
    out_specs = [row, row, pl.BlockSpec((1, D), lambda i: (0, 0))]
    if target is not None:
        out_shape.append(jax.ShapeDtypeStruct((1, STAT_LANES), F32))
        out_specs.append(pl.BlockSpec((1, STAT_LANES), lambda i: (0, 0)))
    scratch = [_chunk_scratch(tm, D)] if any(d > 1 for _, d in dhs) else []
    return pl.pallas_call(
        body, name=name, out_shape=out_shape, grid=(S // tm,), in_specs=in_specs, out_specs=out_specs,
        scratch_shapes=scratch, compiler_params=_params("arbitrary"),
    )(*operands)


CONV_ROWS = 128
CONV_COLS = 512


def _halo_specs(S, tm, width):
    nh = S // HALO
    per = tm // HALO
    cur = pl.BlockSpec((tm, width), lambda i: (i, 0))
    prev = pl.BlockSpec((HALO, width), lambda i: (jnp.maximum(i * per - 1, 0), 0))
    nxt = pl.BlockSpec((HALO, width), lambda i: (jnp.minimum((i + 1) * per, nh - 1), 0))
    return [cur, prev, nxt]


def _ext(refs, cs, inr):
    cur, prev, nxt = refs
    v = jnp.concatenate([prev[:, cs], cur[:, cs], nxt[:, cs]], axis=0).astype(F32)
    return jnp.where(inr, v, 0.0)


def _shift_prev(v):
    return pltpu.roll(v, 1, 0)


def _shift_next(v):
    return pltpu.roll(v, v.shape[0] - 1, 0)


def _conv3(v, w, cs, b=None):
    out = w[0:1, cs] * _shift_prev(v) + w[1:2, cs] * v + w[2:3, cs] * _shift_next(v)
    return out if b is None else out + b[:, cs]


def _in_range(i, tm, tc, S):
    row = lax.broadcasted_iota(jnp.int32, (tm + 2 * HALO, tc), 0) + (i * tm - HALO)
    return jnp.logical_and(row >= 0, row < S)


def _core(v, tm):
    return v[HALO:HALO + tm, :]


def _acc_rows(ref, i, rows):
    for r, cs, val in rows:
        ref[r:r + 1, cs] += val


def _zero_first(ref, i):
    @pl.when(i == 0)
    def _():
        ref[...] = jnp.zeros(ref.shape, ref.dtype)


def _sc_fwd(name, z, w, b):
    S, D3 = z.shape
    D = D3 // 3
    tm, tc = CONV_ROWS, _pick(D, (CONV_COLS, 256, 128))

    def body(zc, zp, zn, w_ref, b_ref, y_ref):
        i = pl.program_id(0)
        inr = _in_range(i, tm, tc, S)
        zr = (zc, zp, zn)
        for c in range(D // tc):
            cs = slice(c * tc, (c + 1) * tc)
            u = _ext(zr, cs, inr)
            gc = _ext(zr, slice(2 * D + c * tc, 2 * D + (c + 1) * tc), inr)
            conv = _conv3(gc * u, w_ref, cs, b_ref)
            gb = zc[:, D + c * tc:D + (c + 1) * tc].astype(F32)
            y_ref[:, cs] = (gb * _core(conv, tm)).astype(BF)

    return pl.pallas_call(
        body, name=name, out_shape=jax.ShapeDtypeStruct((S, D), BF), grid=(S // tm,),
        in_specs=_halo_specs(S, tm, D3) + [pl.BlockSpec((3, D), lambda i: (0, 0)),
                                           pl.BlockSpec((1, D), lambda i: (0, 0))],
        out_specs=pl.BlockSpec((tm, D), lambda i: (i, 0)),
        compiler_params=_params("parallel"),
    )(z, z, z, w, b)


def _sc_bwd(name, z, dy, w, b):
    S, D3 = z.shape
    D = D3 // 3
    tm, tc = CONV_ROWS, _pick(D, (CONV_COLS, 256, 128))

    def body(zc, zp, zn, dc_, dp_, dn_, w_ref, b_ref, dz_ref, dwb_ref):
        i = pl.program_id(0)
        inr = _in_range(i, tm, tc, S)
        _zero_first(dwb_ref, i)
        zr, dr = (zc, zp, zn), (dc_, dp_, dn_)
        for c in range(D // tc):
            cs = slice(c * tc, (c + 1) * tc)
            u = _ext(zr, cs, inr)
            gb = _ext(zr, slice(D + c * tc, D + (c + 1) * tc), inr)
            gc = _ext(zr, slice(2 * D + c * tc, 2 * D + (c + 1) * tc), inr)
            dyv = _ext(dr, cs, inr)
            p = gc * u
            conv = _conv3(p, w_ref, cs, b_ref)
            dconv = dyv * gb
            dp = w_ref[0:1, cs] * _shift_next(dconv) + w_ref[1:2, cs] * dconv + w_ref[2:3, cs] * _shift_prev(dconv)
            dz_ref[:, cs] = _core(dp * gc, tm).astype(BF)
            dz_ref[:, D + c * tc:D + (c + 1) * tc] = _core(dyv * conv, tm).astype(BF)
            dz_ref[:, 2 * D + c * tc:2 * D + (c + 1) * tc] = _core(dp * u, tm).astype(BF)
            dcc = _core(dconv, tm)
            _acc_rows(dwb_ref, i, [
                (0, cs, jnp.sum(dcc * _core(_shift_prev(p), tm), axis=0, keepdims=True)),
                (1, cs, jnp.sum(dcc * _core(p, tm), axis=0, keepdims=True)),
                (2, cs, jnp.sum(dcc * _core(_shift_next(p), tm), axis=0, keepdims=True)),
                (3, cs, jnp.sum(dcc, axis=0, keepdims=True))])

    return pl.pallas_call(
        body, name=name,
        out_shape=[jax.ShapeDtypeStruct((S, D3), BF), jax.ShapeDtypeStruct((4, D), F32)], grid=(S // tm,),
        in_specs=_halo_specs(S, tm, D3) + _halo_specs(S, tm, D) + [
            pl.BlockSpec((3, D), lambda i: (0, 0)), pl.BlockSpec((1, D), lambda i: (0, 0))],
        out_specs=[pl.BlockSpec((tm, D3), lambda i: (i, 0)), pl.BlockSpec((4, D), lambda i: (0, 0))],
        compiler_params=_params("arbitrary"),
    )(z, z, z, dy, dy, dy, w, b)


def _sigmoid(v):
    return 1.0 / (1.0 + jnp.exp(-v))


def _ffn_fwd(name, u, w, b):
    S, F2 = u.shape
    Fh = F2 // 2
    tm, tc = CONV_ROWS, _pick(Fh, (CONV_COLS, 256, 128))

    def body(uc, up, un, w_ref, b_ref, f_ref):
        i = pl.program_id(0)
        inr = _in_range(i, tm, tc, S)
        ur = (uc, up, un)
        for c in range(Fh // tc):
            ca = slice(c * tc, (c + 1) * tc)
            cb = slice(Fh + c * tc, Fh + (c + 1) * tc)
            va = _core(_conv3(_ext(ur, ca, inr), w_ref, ca, b_ref), tm)
            vb = _core(_conv3(_ext(ur, cb, inr), w_ref, cb, b_ref), tm)
            f_ref[:, ca] = (va * _sigmoid(va) * vb).astype(BF)

    return pl.pallas_call(
        body, name=name, out_shape=jax.ShapeDtypeStruct((S, Fh), BF), grid=(S // tm,),
        in_specs=_halo_specs(S, tm, F2) + [pl.BlockSpec((3, F2), lambda i: (0, 0)),
                                           pl.BlockSpec((1, F2), lambda i: (0, 0))],
        out_specs=pl.BlockSpec((tm, Fh), lambda i: (i, 0)),
        compiler_params=_params("parallel"),
    )(u, u, u, w, b)


def _ffn_bwd(name, u, df, w, b):
    S, F2 = u.shape
    Fh = F2 // 2
    tm, tc = CONV_ROWS, _pick(Fh, (CONV_COLS, 256, 128))

    def body(uc, up, un, dc_, dp_, dn_, w_ref, b_ref, du_ref, dwb_ref):
        i = pl.program_id(0)
        inr = _in_range(i, tm, tc, S)
        _zero_first(dwb_ref, i)
        ur, dr = (uc, up, un), (dc_, dp_, dn_)
        for c in range(Fh // tc):
            ca = slice(c * tc, (c + 1) * tc)
            cb = slice(Fh + c * tc, Fh + (c + 1) * tc)
            ua, ub = _ext(ur, ca, inr), _ext(ur, cb, inr)
            dfv = _ext(dr, ca, inr)
            va = _conv3(ua, w_ref, ca, b_ref)
            vb = _conv3(ub, w_ref, cb, b_ref)
            sg = _sigmoid(va)
            dva = dfv * vb * (sg * (1.0 + va * (1.0 - sg)))
            dvb = dfv * (va * sg)
            rows = []
            for cs, uu, dv in ((ca, ua, dva), (cb, ub, dvb)):
                dcore = w_ref[0:1, cs] * _shift_next(dv) + w_ref[1:2, cs] * dv + w_ref[2:3, cs] * _shift_prev(dv)
                du_ref[:, cs] = _core(dcore, tm).astype(BF)
                dvc = _core(dv, tm)
                rows += [
                    (0, cs, jnp.sum(dvc * _core(_shift_prev(uu), tm), axis=0, keepdims=True)),
                    (1, cs, jnp.sum(dvc * _core(uu, tm), axis=0, keepdims=True)),
                    (2, cs, jnp.sum(dvc * _core(_shift_next(uu), tm), axis=0, keepdims=True)),
                    (3, cs, jnp.sum(dvc, axis=0, keepdims=True))]
            _acc_rows(dwb_ref, i, rows)

    return pl.pallas_call(
        body, name=name,
        out_shape=[jax.ShapeDtypeStruct((S, F2), BF), jax.ShapeDtypeStruct((4, F2), F32)], grid=(S // tm,),
        in_specs=_halo_specs(S, tm, F2) + _halo_specs(S, tm, Fh) + [
            pl.BlockSpec((3, F2), lambda i: (0, 0)), pl.BlockSpec((1, F2), lambda i: (0, 0))],
        out_specs=[pl.BlockSpec((tm, F2), lambda i: (i, 0)), pl.BlockSpec((4, F2), lambda i: (0, 0))],
        compiler_params=_params("arbitrary"),
    )(u, u, u, df, df, df, w, b)


def _alibi_slopes(H):
    return [2.0 ** (-ALIBI_MAX * (h + 1) / H) for h in range(H)]


def _window_specs(S, width, col):
    n64 = S // ATTN_HALF
    cur = pl.BlockSpec((ATTN_BLOCK, width), lambda b: (b, col))
    prev = pl.BlockSpec((ATTN_HALF, width), lambda b: (jnp.maximum(2 * b - 1, 0), col))
    nxt = pl.BlockSpec((ATTN_HALF, width), lambda b: (jnp.minimum(2 * b + 2, n64 - 1), col))
    return [cur, prev, nxt]


def _fill_window(buf, cur, prev, nxt):
    buf[0:ATTN_HALF] = prev[...]
    buf[ATTN_HALF:ATTN_HALF + ATTN_BLOCK] = cur[...]
    buf[ATTN_HALF + ATTN_BLOCK:2 * ATTN_BLOCK] = nxt[...]


def _band(b, L, queries_in_rows_of_block):
    QB, W = ATTN_BLOCK, 2 * ATTN_BLOCK
    a_loc = (b * QB) % L
    if queries_in_rows_of_block:
        row = lax.broadcasted_iota(jnp.int32, (QB, W), 0)
        col = lax.broadcasted_iota(jnp.int32, (QB, W), 1)
        rel = col - ATTN_HALF - row
        other = a_loc - ATTN_HALF + col
    else:
        row = lax.broadcasted_iota(jnp.int32, (W, QB), 0)
        col = lax.broadcasted_iota(jnp.int32, (W, QB), 1)
        rel = col + ATTN_HALF - row
        other = a_loc - ATTN_HALF + row
    dist = jnp.abs(rel)
    valid = jnp.logical_and(dist <= ATTN_HALF, jnp.logical_and(other >= 0, other < L))
    return dist.astype(F32), valid


def _lane_col(stats, h):
    lane = lax.broadcasted_iota(jnp.int32, stats.shape, 1)
    return jnp.sum(jnp.where(lane == h, stats, 0.0), axis=1, keepdims=True)


def _attn_fwd(name, qkv, d):
    S, D3 = qkv.shape
    D = D3 // 3
    H = D // HEAD_DIM
    L = S // d
    scale = HEAD_DIM ** -0.5
    slopes = _alibi_slopes(H)

    def body(q_ref, kc, kp, kn, vc, vp, vn, o_ref, lse_ref, kbuf, vbuf):
        b = pl.program_id(0)
        _fill_window(kbuf, kc, kp, kn)
        _fill_window(vbuf, vc, vp, vn)
        dist, valid = _band(b, L, True)
        dist = dist * float(d)
        lane = lax.broadcasted_iota(jnp.int32, (ATTN_BLOCK, STAT_LANES), 1)
        lse = jnp.zeros((ATTN_BLOCK, STAT_LANES), F32)
        for h in range(H):
            cs = slice(h * HEAD_DIM, (h + 1) * HEAD_DIM)
            s = lax.dot_general(q_ref[:, cs], kbuf[:, cs], NT, preferred_element_type=F32) * scale
            s = jnp.where(valid, s - slopes[h] * dist, NEG_INF)
            m = jnp.max(s, axis=1, keepdims=True)
            p = jnp.exp(s - m)
            den = jnp.sum(p, axis=1, keepdims=True)
            o = jnp.dot(p.astype(BF), vbuf[:, cs], preferred_element_type=F32)
            o_ref[:, cs] = o / den
            lse = jnp.where(lane == h, m + jnp.log(den), lse)
        lse_ref[...] = lse

    return pl.pallas_call(
        body, name=name,
        out_shape=[jax.ShapeDtypeStruct((S, D), F32), jax.ShapeDtypeStruct((S, STAT_LANES), F32)],
        grid=(S // ATTN_BLOCK,),
        in_specs=[pl.BlockSpec((ATTN_BLOCK, D), lambda b: (b, 0))] + _window_specs(S, D, 1) + _window_specs(S, D, 2),
        out_specs=[pl.BlockSpec((ATTN_BLOCK, D), lambda b: (b, 0)),
                   pl.BlockSpec((ATTN_BLOCK, STAT_LANES), lambda b: (b, 0))],
        scratch_shapes=[pltpu.VMEM((2 * ATTN_BLOCK, D), BF), pltpu.VMEM((2 * ATTN_BLOCK, D), BF)],
        compiler_params=_params("parallel"),
    )(qkv, qkv, qkv, qkv, qkv, qkv, qkv)


def _dil_specs(S, tm, width):
    specs = [pl.BlockSpec((tm, width), lambda i: (i, 0))]
    for d in DILATIONS[1:]:
        specs.append(pl.BlockSpec((d, tm // d, width), lambda i: (0, i, 0)))
    return specs


def _attn_combine(name, outs, lses):
    S, D = outs[0].shape
    H = D // HEAD_DIM
    tm = NORM_ROWS

    def body(o1, o4, o16, l1, l4, l16, o_ref, ob_ref, lse_ref, oscr, lscr):
        ls = [l1[...]]
        for ref, d in zip((l4, l16), DILATIONS[1:]):
            _to_natural(lscr, ref, d)
            ls.append(lscr[0])
        top = jnp.maximum(jnp.maximum(ls[0], ls[1]), ls[2])
        es = [jnp.exp(l - top) for l in ls]
        tot = es[0] + es[1] + es[2]
        lse_ref[...] = top + jnp.log(tot)
        ws = [e / tot for e in es]
        for gi, (ref, d) in enumerate(zip((o1, o4, o16), DILATIONS)):
            if d > 1:
                _to_natural(oscr, ref, d)
            for h in range(H):
                cs = slice(h * HEAD_DIM, (h + 1) * HEAD_DIM)
                term = _lane_col(ws[gi], h) * (ref[:, cs] if d == 1 else oscr[h])
                if gi == 0:
                    o_ref[:, cs] = term
                else:
                    o_ref[:, cs] += term
        ob_ref[...] = o_ref[...].astype(BF)

    outs3 = [outs[0]] + [o.reshape(d, S // d, D) for o, d in zip(outs[1:], DILATIONS[1:])]
    lses3 = [lses[0]] + [l.reshape(d, S // d, STAT_LANES) for l, d in zip(lses[1:], DILATIONS[1:])]
    row = pl.BlockSpec((tm, D), lambda i: (i, 0))
    return pl.pallas_call(
        body, name=name,
        out_shape=[jax.ShapeDtypeStruct((S, D), F32), jax.ShapeDtypeStruct((S, D), BF),
                   jax.ShapeDtypeStruct((S, STAT_LANES), F32)],
        grid=(S // tm,),
        in_specs=_dil_specs(S, tm, D) + _dil_specs(S, tm, STAT_LANES),
        out_specs=[row, row, pl.BlockSpec((tm, STAT_LANES), lambda i: (i, 0))],
        scratch_shapes=[_chunk_scratch(tm, D), _chunk_scratch(tm, STAT_LANES)],
        compiler_params=_params("parallel"),
    )(*outs3, *lses3)


def _attn_bwd_prep(name, do, o32, lse):
    S, D = do.shape
    H = D // HEAD_DIM
    tm = NORM_ROWS
    dils = DILATIONS[1:]

    def body(do_ref, o_ref, lse_ref, dl_ref, do4, do16, l4, l16, d4, d16, scr, sscr):
        lane = lax.broadcasted_iota(jnp.int32, (tm, STAT_LANES), 1)
        delta = jnp.zeros((tm, STAT_LANES), F32)
        for h in range(H):
            cs = slice(h * HEAD_DIM, (h + 1) * HEAD_DIM)
            dov = do_ref[:, cs].astype(F32)
            scr[h] = dov
            delta = jnp.where(lane == h, jnp.sum(dov * o_ref[:, cs], axis=1, keepdims=True), delta)
        dl_ref[...] = delta
        for ref, d in zip((do4, do16), dils):
            _to_residue_major(scr, ref, d, BF)
        for val, refs in ((lse_ref[...], (l4, l16)), (delta, (d4, d16))):
            sscr[0] = val
            for ref, d in zip(refs, dils):
                _to_residue_major(sscr, ref, d, F32)

    def perm_shapes(width, dt):
        return [jax.ShapeDtypeStruct((d, S // d, width), dt) for d in dils]

    def perm_specs(width):
        return [pl.BlockSpec((d, tm // d, width), lambda i: (0, i, 0)) for d in dils]

    row = lambda w: pl.BlockSpec((tm, w), lambda i: (i, 0))
    outs = pl.pallas_call(
        body, name=name,
        out_shape=[jax.ShapeDtypeStruct((S, STAT_LANES), F32)]
        + perm_shapes(D, BF) + perm_shapes(STAT_LANES, F32) + perm_shapes(STAT_LANES, F32),
        grid=(S // tm,),
        in_specs=[row(D), row(D), row(STAT_LANES)],
        out_specs=[row(STAT_LANES)] + perm_specs(D) + perm_specs(STAT_LANES) + perm_specs(STAT_LANES),
        scratch_shapes=[_chunk_scratch(tm, D), _chunk_scratch(tm, STAT_LANES)],
        compiler_params=_params("parallel"),
    )(do, o32, lse)
    dos = [do] + [a.reshape(S, D) for a in outs[1:3]]
    lss = [lse] + [a.reshape(S, STAT_LANES) for a in outs[3:5]]
    dls = [outs[0]] + [a.reshape(S, STAT_LANES) for a in outs[5:7]]
    return dos, lss, dls


def _attn_bwd(name, qkv, do, lse, delta, d):
    S, D3 = qkv.shape
    D = D3 // 3
    H = D // HEAD_DIM
    L = S // d
    scale = HEAD_DIM ** -0.5
    slopes = _alibi_slopes(H)
    QB = ATTN_BLOCK

    def body(qc, qp, qn, kc, kp, kn, vc, vp, vn, dc_, dp_, dn_, lc, lp, ln, ec, ep, en,
             out_ref, qbuf, kbuf, vbuf, dobuf, lbuf, ebuf):
        b = pl.program_id(0)
        for buf, trio in ((qbuf, (qc, qp, qn)), (kbuf, (kc, kp, kn)), (vbuf, (vc, vp, vn)),
                          (dobuf, (dc_, dp_, dn_)), (lbuf, (lc, lp, ln)), (ebuf, (ec, ep, en))):
            _fill_window(buf, *trio)
        dist_q, valid_q = _band(b, L, True)
        dist_k, valid_k = _band(b, L, False)
        dist_q = dist_q * float(d)
        dist_k = dist_k * float(d)
        lse_c, del_c = lc[...], ec[...]
        lse_w, del_w = lbuf[...], ebuf[...]
        for h in range(H):
            cs = slice(h * HEAD_DIM, (h + 1) * HEAD_DIM)
            q, do_h = qc[:, cs], dc_[:, cs]
            kw, vw = kbuf[:, cs], vbuf[:, cs]
            s = lax.dot_general(q, kw, NT, preferred_element_type=F32) * scale - slopes[h] * dist_q
            p = jnp.where(valid_q, jnp.exp(s - _lane_col(lse_c, h)), 0.0)
            dp = lax.dot_general(do_h, vw, NT, preferred_element_type=F32)
            ds = p * (dp - _lane_col(del_c, h))
            dq = jnp.dot(ds.astype(BF), kw, preferred_element_type=F32) * scale
            out_ref[:, cs] = dq.astype(BF)
            qw, dow = qbuf[:, cs], dobuf[:, cs]
            k, v = kc[:, cs], vc[:, cs]
            s2 = lax.dot_general(qw, k, NT, preferred_element_type=F32) * scale - slopes[h] * dist_k
            p2 = jnp.where(valid_k, jnp.exp(s2 - _lane_col(lse_w, h)), 0.0)
            dv = lax.dot_general(p2.astype(BF), dow, TN, preferred_element_type=F32)
            dp2 = lax.dot_general(dow, v, NT, preferred_element_type=F32)
            ds2 = p2 * (dp2 - _lane_col(del_w, h))
            dk = lax.dot_general(ds2.astype(BF), qw, TN, preferred_element_type=F32) * scale
            out_ref[:, D + h * HEAD_DIM:D + (h + 1) * HEAD_DIM] = dk.astype(BF)
            out_ref[:, 2 * D + h * HEAD_DIM:2 * D + (h + 1) * HEAD_DIM] = dv.astype(BF)

    W = 2 * QB
    return pl.pallas_call(
        body, name=name, out_shape=jax.ShapeDtypeStruct((S, D3), BF), grid=(S // QB,),
        in_specs=(_window_specs(S, D, 0) + _window_specs(S, D, 1) + _window_specs(S, D, 2)
                  + _window_specs(S, D, 0) + _window_specs(S, STAT_LANES, 0) + _window_specs(S, STAT_LANES, 0)),
        out_specs=pl.BlockSpec((QB, D3), lambda b: (b, 0)),
        scratch_shapes=[pltpu.VMEM((W, D), BF), pltpu.VMEM((W, D), BF), pltpu.VMEM((W, D), BF),
                        pltpu.VMEM((W, D), BF), pltpu.VMEM((W, STAT_LANES), F32), pltpu.VMEM((W, STAT_LANES), F32)],
        compiler_params=_params("parallel"),
    )(qkv, qkv, qkv, qkv, qkv, qkv, qkv, qkv, qkv, do, do, do, lse, lse, lse, delta, delta, delta)


def _cast_bf16(name, w):
    R, C = w.shape
    tr = _row_tile(R, C)

    def body(w_ref, o_ref):
        o_ref[...] = w_ref[...].astype(BF)

    return pl.pallas_call(
        body, name=name, out_shape=jax.ShapeDtypeStruct((R, C), BF), grid=(R // tr,),
        in_specs=[pl.BlockSpec((tr, C), lambda i: (i, 0))], out_specs=pl.BlockSpec((tr, C), lambda i: (i, 0)),
        compiler_params=_params("parallel"),
    )(w)


def _pair_sum(name, ids, g, a):
    _, _, R, C = g.shape
    tr = _row_tile(R, C)

    def body(ids_ref, g_ref, a_ref, p32_ref, p16_ref):
        tot = g_ref[...].astype(F32) + a_ref[...].astype(F32)
        p32_ref[...] = tot
        p16_ref[...] = tot.astype(BF)

    slab = pl.BlockSpec((None, tr, C), lambda s, i, ids_ref: (s, i, 0))
    return pl.pallas_call(
        body, name=name,
        out_shape=[jax.ShapeDtypeStruct((N_CHIPS, R, C), F32), jax.ShapeDtypeStruct((N_CHIPS, R, C), BF)],
        grid_spec=pltpu.PrefetchScalarGridSpec(
            num_scalar_prefetch=1, grid=(N_CHIPS, R // tr),
            in_specs=[pl.BlockSpec((None, None, tr, C), lambda s, i, ids_ref: (s, ids_ref[1], i, 0)), slab],
            out_specs=[slab, slab]),
        compiler_params=_params("parallel", "parallel"),
    )(ids, g, a)


def _chip_sum(name, ids, p32, bsum):
    _, R, C = p32.shape
    tr = _row_tile(R, C)

    def body(ids_ref, p_ref, b_ref, o_ref):
        o_ref[...] = ((p_ref[...] + b_ref[0].astype(F32)) + b_ref[1].astype(F32)) + b_ref[2].astype(F32)

    return pl.pallas_call(
        body, name=name, out_shape=jax.ShapeDtypeStruct((2, R, C), F32),
        grid_spec=pltpu.PrefetchScalarGridSpec(
            num_scalar_prefetch=1, grid=(R // tr,),
            in_specs=[pl.BlockSpec((None, tr, C), lambda i, ids_ref: (ids_ref[0], i, 0)),
                      pl.BlockSpec((3, tr, C), lambda i, ids_ref: (0, i, 0))],
            out_specs=pl.BlockSpec((None, tr, C), lambda i, ids_ref: (ids_ref[1], i, 0))),
        compiler_params=_params("parallel"),
    )(ids, p32, bsum)


def _adamw(name, w, g, m, v):
    R, C = w.shape
    tr = _row_tile(R, C, 1 << 18) if R % 16 == 0 else R
    c1 = 1.0 - ADAM_B1 ** ADAM_STEP
    c2 = 1.0 - ADAM_B2 ** ADAM_STEP

    def body(w_ref, g_ref, m_ref, v_ref, d_ref, nm_ref, nv_ref):
        gv = g_ref[...]
        nm = ADAM_B1 * m_ref[...] + (1.0 - ADAM_B1) * gv
        nv = ADAM_B2 * v_ref[...] + (1.0 - ADAM_B2) * (gv * gv)
        nm_ref[...] = nm
        nv_ref[...] = nv
        d_ref[...] = -ADAM_LR * ((nm / c1) / (jnp.sqrt(nv / c2) + ADAM_EPS) + ADAM_WD * w_ref[...])

    spec = pl.BlockSpec((tr, C), lambda i: (i, 0))
    return pl.pallas_call(
        body, name=name, out_shape=[jax.ShapeDtypeStruct((R, C), F32)] * 3, grid=(R // tr,),
        in_specs=[spec] * 4, out_specs=[spec] * 3, compiler_params=_params("parallel"),
    )(w, g, m, v)


def _coords():
    return lax.axis_index("x"), lax.axis_index("y"), lax.axis_index("c")


def _flip(x, y, c, k):
    return (1 - x if k & 4 else x, 1 - y if k & 2 else y, 1 - c if k & 1 else c)


def _small_exchange(name, buf, reduce):
    rows = buf.shape[0]

    def body(x_ref, o_ref, land, send_sems, recv_sems):
        x, y, c = _coords()
        me = 4 * x + 2 * y + c

        def copy(k, sending):
            px, py, pc = _flip(x, y, c, k)
            slot = me if sending else 4 * px + 2 * py + pc
            return pltpu.make_async_remote_copy(
                src_ref=x_ref, dst_ref=land.at[slot], send_sem=send_sems.at[k - 1], recv_sem=recv_sems.at[k - 1],
                device_id=(px, py, pc), device_id_type=MESH)

        for k in range(1, 8):
            copy(k, True).start()
        land[me] = x_ref[...]
        for k in range(1, 8):
            copy(k, False).wait()
        if reduce:
            acc = land[0]
            for s in range(1, 8):
                acc = acc + land[s]
            o_ref[...] = acc
        else:
            o_ref[...] = land[...]

    out_shape = jax.ShapeDtypeStruct((rows, 128) if reduce else (8, rows, 128), F32)
    return pl.pallas_call(
        body, name=name, out_shape=out_shape,
        in_specs=[pl.BlockSpec(memory_space=pltpu.VMEM)], out_specs=pl.BlockSpec(memory_space=pltpu.VMEM),
        scratch_shapes=[pltpu.VMEM((8, rows, 128), F32), pltpu.SemaphoreType.DMA((7,)), pltpu.SemaphoreType.DMA((7,))],
        compiler_params=pltpu.CompilerParams(vmem_limit_bytes=VMEM_LIMIT),
    )(buf)


def _allgather_weights(name, shards):
    T = len(shards)

    def body(*refs):
        ins, outs = refs[:T], refs[T:2 * T]
        send_sems, recv_sems, local_sems = refs[2 * T:]
        x, y, c = _coords()
        chip = 2 * x + y
        sib = (x, y, 1 - c)
        chips = [_flip(x, y, c, k) for k in (4, 2, 6)]

        def slab(t, cx, cy, cc):
            return outs[t].at[2 * cx + cy, cc]

        def copy(t, k, src, dst, to):
            return pltpu.make_async_remote_copy(
                src_ref=src, dst_ref=dst, send_sem=send_sems.at[t, k], recv_sem=recv_sems.at[t, k],
                device_id=to, device_id_type=MESH)

        started, local = [], []
        for t in range(T):
            lc = pltpu.make_async_copy(ins[t].at[c], outs[t].at[chip, c], local_sems.at[t])
            lc.start()
            local.append(lc)
            for j, to in enumerate(chips):
                cp = copy(t, 1 + j, ins[t].at[c], slab(t, x, y, c), to)
                cp.start()
                started.append(cp)
        for t in range(T):
            cp = copy(t, 0, ins[t].at[c], slab(t, x, y, c), sib)
            cp.start()
            started.append(cp)
        for t in range(T):
            for j, (px, py, pc) in enumerate(chips):
                held = slab(t, px, py, c)
                copy(t, 1 + j, held, held, (px, py, pc)).wait_recv()
                cp = copy(t, 4 + j, held, held, sib)
                cp.start()
                started.append(cp)
        for t in range(T):
            got = slab(t, x, y, 1 - c)
            copy(t, 0, got, got, sib).wait_recv()
            for j, (px, py, pc) in enumerate(chips):
                got = slab(t, px, py, 1 - c)
                copy(t, 4 + j, got, got, sib).wait_recv()
        for cp in started:
            cp.wait_send()
        for lc in local:
            lc.wait()

    anyspec = pl.BlockSpec(memory_space=pl.ANY)
    return pl.pallas_call(
        body, name=name,
        out_shape=[jax.ShapeDtypeStruct((N_CHIPS,) + s.shape, s.dtype) for s in shards],
        in_specs=[anyspec] * T, out_specs=[anyspec] * T,
        scratch_shapes=[pltpu.SemaphoreType.DMA((T, 7)), pltpu.SemaphoreType.DMA((T, 7)),
                        pltpu.SemaphoreType.DMA((T,))],
    )(*shards)


def _pair_send(name, grads):
    T = len(grads)

    def body(*refs):
        ins, outs = refs[:T], refs[T:2 * T]
        send_sems, recv_sems = refs[2 * T:]
        x, y, c = _coords()
        sib = (x, y, 1 - c)
        cps = []
        for t in range(T):
            for s in range(N_CHIPS):
                cp = pltpu.make_async_remote_copy(
                    src_ref=ins[t].at[s, 1 - c], dst_ref=outs[t].at[s], send_sem=send_sems.at[t, s],
                    recv_sem=recv_sems.at[t, s], device_id=sib, device_id_type=MESH)
                cp.start()
                cps.append(cp)
        for cp in cps:
            cp.wait()

    anyspec = pl.BlockSpec(memory_space=pl.ANY)
    return pl.pallas_call(
        body, name=name,
        out_shape=[jax.ShapeDtypeStruct((N_CHIPS,) + g.shape[2:], g.dtype) for g in grads],
        in_specs=[anyspec] * T, out_specs=[anyspec] * T,
        scratch_shapes=[pltpu.SemaphoreType.DMA((T, N_CHIPS)), pltpu.SemaphoreType.DMA((T, N_CHIPS))],
    )(*grads)


def _chip_send(name, parts):
    T = len(parts)

    def body(*refs):
        ins, outs = refs[:T], refs[T:2 * T]
        send_sems, recv_sems = refs[2 * T:]
        x, y, c = _coords()
        cps = []
        for t in range(T):
            for j, k in enumerate((4, 2, 6)):
                px, py, pc = _flip(x, y, c, k)
                cp = pltpu.make_async_remote_copy(
                    src_ref=ins[t].at[2 * px + py], dst_ref=outs[t].at[j], send_sem=send_sems.at[t, j],
                    recv_sem=recv_sems.at[t, j], device_id=(px, py, pc), device_id_type=MESH)
                cp.start()
                cps.append(cp)
        for cp in cps:
            cp.wait()

    anyspec = pl.BlockSpec(memory_space=pl.ANY)
    return pl.pallas_call(
        body, name=name,
        out_shape=[jax.ShapeDtypeStruct((3,) + p.shape[1:], p.dtype) for p in parts],
        in_specs=[anyspec] * T, out_specs=[anyspec] * T,
        scratch_shapes=[pltpu.SemaphoreType.DMA((T, 3)), pltpu.SemaphoreType.DMA((T, 3))],
    )(*parts)


def _pair_fill(name, fulls):
    T = len(fulls)

    def body(*refs):
        outs = refs[T:2 * T]
        send_sems, recv_sems = refs[2 * T:]
        x, y, c = _coords()
        sib = (x, y, 1 - c)
        cps = []
        for t in range(T):
            send = pltpu.make_async_remote_copy(
                src_ref=outs[t].at[c], dst_ref=outs[t].at[c], send_sem=send_sems.at[t],
                recv_sem=recv_sems.at[t], device_id=sib, device_id_type=MESH)
            recv = pltpu.make_async_remote_copy(
                src_ref=outs[t].at[1 - c], dst_ref=outs[t].at[1 - c], send_sem=send_sems.at[t],
                recv_sem=recv_sems.at[t], device_id=sib, device_id_type=MESH)
            send.start()
            cps.append((send, recv))
        for send, recv in cps:
            send.wait_send()
            recv.wait_recv()

    anyspec = pl.BlockSpec(memory_space=pl.ANY)
    return pl.pallas_call(
        body, name=name,
        out_shape=[jax.ShapeDtypeStruct(f.shape, f.dtype) for f in fulls],
        in_specs=[anyspec] * T, out_specs=[anyspec] * T,
        input_output_aliases={t: t for t in range(T)},
        scratch_shapes=[pltpu.SemaphoreType.DMA((T,)), pltpu.SemaphoreType.DMA((T,))],
    )(*fulls)


def _pack(arrs):
    flat = jnp.concatenate([a.reshape(-1).astype(F32) for a in arrs])
    n = flat.shape[0]
    rows = -(-n // 1024) * 8
    return jnp.pad(flat, (0, rows * 128 - n)).reshape(rows, 128)


def _unpack(buf, shapes):
    flat = buf.reshape(-1)
    out, pos = [], 0
    for s in shapes:
        n = math.prod(s)
        out.append(flat[pos:pos + n].reshape(s))
        pos += n
    return out


def _as_halves(w):
    L, R, C = w.shape
    return w.reshape(2, L * R // 2, C)


def kernel(x, mix_norm_g, ffn_norm_g, final_norm_g, sc_w_in, sc_conv_w, sc_conv_b, sc_w_out, attn_w_qkv, attn_w_out, ffn_w_up, ffn_conv_w, ffn_conv_b, ffn_w_down, loss_target, m_mix_norm_g, m_ffn_norm_g, m_final_norm_g, m_sc_w_in, m_sc_conv_w, m_sc_conv_b, m_sc_w_out, m_attn_w_qkv, m_attn_w_out, m_ffn_w_up, m_ffn_conv_w, m_ffn_conv_b, m_ffn_w_down, v_mix_norm_g, v_ffn_norm_g, v_final_norm_g, v_sc_w_in, v_sc_conv_w, v_sc_conv_b, v_sc_w_out, v_attn_w_qkv, v_attn_w_out, v_ffn_w_up, v_ffn_conv_w, v_ffn_conv_b, v_ffn_w_down):
    S, D = x.shape[1], x.shape[2]
    xi, yi, ci = _coords()
    chip = 2 * xi + yi
    ids = jnp.stack([chip, ci]).astype(jnp.int32)
    x0 = x.reshape(S, D)
    tgt = loss_target.reshape(S, D)

    conv_shapes = [sc_conv_w.shape, ffn_conv_w.shape]
    allc = _small_exchange("gather_conv_w", _pack([sc_conv_w, ffn_conv_w]), reduce=False)
    per_chip = [_unpack(allc[2 * k], conv_shapes) for k in range(N_CHIPS)]
    scw = jnp.concatenate([p[0] for p in per_chip], axis=-1)[0]
    fcw = jnp.concatenate([p[1] for p in per_chip], axis=-1)
    scb = sc_conv_b

    big = [sc_w_in, sc_w_out, attn_w_qkv, attn_w_out, ffn_w_up, ffn_w_down]
    big_names = ["sc_w_in", "sc_w_out", "attn_w_qkv", "attn_w_out", "ffn_w_up", "ffn_w_down"]
    shards = []
    for nm, w in zip(big_names, big):
        L, R, C = w.shape
        shards.append(_as_halves(_cast_bf16("cast_" + nm, w.reshape(L * R, C)).reshape(L, R, C)))
    gathered = _allgather_weights("allgather_weights", shards)
    w_in, w_out, w_qkv, w_ao, w_up, w_dn = [
        g.reshape((N_CHIPS,) + w.shape) for g, w in zip(gathered, big)]

    h0 = _rmsnorm_fwd("norm_mix0", x0, mix_norm_g[0:1])[0]
    z = _mm_nn_col("sc_in", h0, w_in, 0)
    y = _sc_fwd("sc_gate", z, scw, scb)
    x1 = _mm_nn_row("sc_out", y, w_out, 0, x0)
    h1 = _rmsnorm_fwd("norm_ffn0", x1, ffn_norm_g[0:1])[0]
    u0 = _mm_nn_col("ffn_up0", h1, w_up, 0)
    f0 = _ffn_fwd("ffn_gate0", u0, fcw[0], ffn_conv_b[0:1])
    x2 = _mm_nn_row("ffn_down0", f0, w_dn, 0, x1)
    h2s = _rmsnorm_fwd("norm_mix1", x2, mix_norm_g[1:2], dilated=True)
    qkvs = [_mm_nn_col(f"attn_qkv{d}", h, w_qkv, 0, col_off=gi * 3 * D, ncols=3 * D)
            for gi, (h, d) in enumerate(zip(h2s, DILATIONS))]
    og, lg = zip(*[_attn_fwd(f"attn_fwd{d}", q, d) for q, d in zip(qkvs, DILATIONS)])
    o32, ob, lse = _attn_combine("attn_combine", list(og), list(lg))
    x3 = _mm_nn_row("attn_out", ob, w_ao, 0, x2)
    h3 = _rmsnorm_fwd("norm_ffn1", x3, ffn_norm_g[1:2])[0]
    u1 = _mm_nn_col("ffn_up1", h3, w_up, 1)
    f1 = _ffn_fwd("ffn_gate1", u1, fcw[1], ffn_conv_b[1:2])
    x4 = _mm_nn_row("ffn_down1", f1, w_dn, 1, x3)

    dx4, dx4b, dg_final, loss_part = _rmsnorm_bwd("loss_norm_bwd", x4, final_norm_g.reshape(1, D), target=tgt)

    def ffn_backward(layer, xin, h, u, f, dxo, dxob, gain, g_up, g_dn):
        df = _mm_nt_row(f"ffn_down_dx{layer}", dxob, w_dn, layer)
        g_dn = _mm_tn_row(f"ffn_down_dw{layer}", f, dxob, prev=g_dn, layer=layer)
        du, dwb = _ffn_bwd(f"ffn_gate_bwd{layer}", u, df, fcw[layer], ffn_conv_b[layer:layer + 1])
        dh = _mm_nt_col(f"ffn_up_dx{layer}", du, w_up, layer)
        g_up = _mm_tn_col(f"ffn_up_dw{layer}", h, du, w_up.shape[3], prev=g_up, layer=layer)
        dxi, dxib, dg = _rmsnorm_bwd(f"norm_ffn_bwd{layer}", xin, gain, dhs=[(dh, 1)], dres=dxo)
        return dxi, dxib, dg, dwb, g_up, g_dn

    dx3, dx3b, dg_ffn1, dwb_ffn1, g_up, g_dn = ffn_backward(1, x3, h3, u1, f1, dx4, dx4b, ffn_norm_g[1:2], None, None)

    do = _mm_nt_row("attn_out_dx", dx3b, w_ao, 0)
    g_ao = _mm_tn_row("attn_out_dw", ob, dx3b)
    dos, lss, dls = _attn_bwd_prep("attn_bwd_prep", do, o32, lse)
    dqkvs = [_attn_bwd(f"attn_bwd{d}", q, a, b, c_, d)
             for q, a, b, c_, d in zip(qkvs, dos, lss, dls, DILATIONS)]
    dh2s = [(_mm_nt_col(f"attn_qkv_dx{d}", dq, w_qkv, 0, col_off=gi * 3 * D), d)
            for gi, (dq, d) in enumerate(zip(dqkvs, DILATIONS))]
    g_qkv = None
    for gi, (h, dq, d) in enumerate(zip(h2s, dqkvs, DILATIONS)):
        g_qkv = _mm_tn_col(f"attn_qkv_dw{d}", h, dq, w_qkv.shape[3], col_off=gi * 3 * D, prev=g_qkv)
    dx2, dx2b, dg_mix1 = _rmsnorm_bwd("norm_mix_bwd1", x2, mix_norm_g[1:2], dhs=dh2s, dres=dx3)

    dx1, dx1b, dg_ffn0, dwb_ffn0, g_up, g_dn = ffn_backward(0, x1, h1, u0, f0, dx2, dx2b, ffn_norm_g[0:1], g_up, g_dn)

    dy = _mm_nt_row("sc_out_dx", dx1b, w_out, 0)
    g_out = _mm_tn_row("sc_out_dw", y, dx1b)
    dz, dwb_sc = _sc_bwd("sc_gate_bwd", z, dy, scw, scb)
    dh0 = _mm_nt_col("sc_in_dx", dz, w_in, 0)
    g_in = _mm_tn_col("sc_in_dw", h0, dz, w_in.shape[3])
    dx0, _, dg_mix0 = _rmsnorm_bwd("norm_mix_bwd0", x0, mix_norm_g[0:1], dhs=[(dh0, 1)], dres=dx1)

    grads = [g_in, g_out, g_qkv, g_ao, g_up, g_dn]
    from_sib = _pair_send("rs_pair_send", grads)
    p32s, p16s = zip(*[_pair_sum("rs_pair_sum_" + t, ids, g, a) for t, g, a in zip(big_names, grads, from_sib)])
    from_chips = _chip_send("rs_chip_send", list(p16s))
    fulls = [_chip_sum("rs_chip_sum_" + t, ids, p, b) for t, p, b in zip(big_names, p32s, from_chips)]
    fulls = _pair_fill("rs_pair_fill", fulls)
    big_grads = [f.reshape(w.shape) for f, w in zip(fulls, big)]

    dconv_sc = dwb_sc[0:3].reshape(1, 3, D)
    dbias_sc = dwb_sc[3:4]
    dconv_ffn = jnp.stack([dwb_ffn0[0:3], dwb_ffn1[0:3]])
    dbias_ffn = jnp.concatenate([dwb_ffn0[3:4], dwb_ffn1[3:4]], axis=0)
    small_parts = [jnp.concatenate([dg_mix0, dg_mix1], axis=0), jnp.concatenate([dg_ffn0, dg_ffn1], axis=0),
                   dg_final.reshape(D), dconv_sc, dbias_sc, dconv_ffn, dbias_ffn, loss_part[0, 0:1]]
    small_shapes = [a.shape for a in small_parts]
    summed = _unpack(_small_exchange("allreduce_small", _pack(small_parts), reduce=True), small_shapes)
    g_mix, g_ffn, g_final, g_scw_full, g_scb, g_fcw_full, g_fcb, loss = summed
    loss = loss.reshape(())
    cw = sc_conv_w.shape[2]
    g_scw = lax.dynamic_slice_in_dim(g_scw_full, chip * cw, cw, axis=2)
    fw = ffn_conv_w.shape[2]
    g_fcw = lax.dynamic_slice_in_dim(g_fcw_full, chip * fw, fw, axis=2)

    names = ["mix_norm_g", "ffn_norm_g", "final_norm_g", "sc_w_in", "sc_conv_w", "sc_conv_b", "sc_w_out",
             "attn_w_qkv", "attn_w_out", "ffn_w_up", "ffn_conv_w", "ffn_conv_b", "ffn_w_down"]
    ws = dict(zip(names, [mix_norm_g, ffn_norm_g, final_norm_g, sc_w_in, sc_conv_w, sc_conv_b, sc_w_out,
                          attn_w_qkv, attn_w_out, ffn_w_up, ffn_conv_w, ffn_conv_b, ffn_w_down]))
    ms = dict(zip(names, [m_mix_norm_g, m_ffn_norm_g, m_final_norm_g, m_sc_w_in, m_sc_conv_w, m_sc_conv_b, m_sc_w_out,
                          m_attn_w_qkv, m_attn_w_out, m_ffn_w_up, m_ffn_conv_w, m_ffn_conv_b, m_ffn_w_down]))
    vs = dict(zip(names, [v_mix_norm_g, v_ffn_norm_g, v_final_norm_g, v_sc_w_in, v_sc_conv_w, v_sc_conv_b, v_sc_w_out,
                          v_attn_w_qkv, v_attn_w_out, v_ffn_w_up, v_ffn_conv_w, v_ffn_conv_b, v_ffn_w_down]))
    gs = {"mix_norm_g": g_mix, "ffn_norm_g": g_ffn, "final_norm_g": g_final, "sc_conv_w": g_scw,
          "sc_conv_b": g_scb, "ffn_conv_w": g_fcw, "ffn_conv_b": g_fcb}
    gs.update(dict(zip(big_names, big_grads)))

    deltas, new_m, new_v = {}, {}, {}
    small_names = [n for n in names if n not in big_names]
    packed = [_pack([d[n] for n in small_names]) for d in (ws, gs, ms, vs)]
    outs = _adamw("adamw_small", *packed)
    shapes = [ws[n].shape for n in small_names]
    for res, o in zip((deltas, new_m, new_v), outs):
        res.update(dict(zip(small_names, _unpack(o, shapes))))
    for n in big_names:
        shp = ws[n].shape
        two_d = (shp[0] * shp[1], shp[2])
        outs = _adamw("adamw_" + n, *[d[n].reshape(two_d) for d in (ws, gs, ms, vs)])
        for res, o in zip((deltas, new_m, new_v), outs):
            res[n] = o.reshape(shp)

    return (loss, dx0.reshape(x.shape), *[gs[n] for n in names], *[deltas[n] for n in names],
            *[new_m[n] for n in names], *[new_v[n] for n in names])
```

```python
import functools
import math

import jax
import jax.numpy as jnp
from jax import lax
from jax.experimental import pallas as pl
from jax.experimental.pallas import tpu as pltpu

F32 = jnp.float32
BF = jnp.bfloat16
MESH = pl.DeviceIdType.MESH

HEAD_DIM = 128
ATTN_HALF = 64
ATTN_BLOCK = 128
DILATIONS = (1, 4, 16)
STAT_LANES = 128
HALO = 16
NORM_EPS = 1e-5
ALIBI_MAX = 8.0
NEG_INF = -1e30
N_CHIPS = 4
VMEM_LIMIT = 56 * 1024 * 1024

ADAM_LR = 0.001
ADAM_B1 = 0.9
ADAM_B2 = 0.999
ADAM_EPS = 1e-08
ADAM_WD = 0.01
ADAM_STEP = 10


def _pick(n, cands):
    for c in cands:
        if n % c == 0:
            return c
    raise ValueError(f"no tile for {n} in {cands}")


def _row_tile(rows, cols, max_elems=1 << 19):
    for c in (512, 256, 128, 64, 32, 16):
        if rows % c == 0 and c * cols <= max_elems:
            return c
    raise ValueError(f"no row tile for {rows}x{cols}")


def _params(*sem):
    return pltpu.CompilerParams(dimension_semantics=sem, vmem_limit_bytes=VMEM_LIMIT)


def _matmul(name, a, b, out_shape, grid, a_spec, b_spec, o_spec, contract, acc_shape,
            res=None, res_spec=None, prev=None):
    nk = grid[2]

    def body(*refs):
        refs = list(refs)
        if prev is not None:
            refs.pop(0)
        a_ref, b_ref = refs[0], refs[1]
        res_ref = refs[2] if res is not None else None
        o_ref = refs[3] if res is not None else refs[2]
        acc_ref = refs[-1]
        part = lax.dot_general(a_ref[...], b_ref[...], contract, preferred_element_type=F32)

        def finish(total):
            if res_ref is not None:
                total = total + res_ref[...]
            o_ref[...] = total.astype(o_ref.dtype)

        if nk == 1:
            finish(part)
        else:
            k = pl.program_id(2)

            @pl.when(k == 0)
            def _():
                acc_ref[...] = part

            @pl.when(jnp.logical_and(k > 0, k < nk - 1))
            def _():
                acc_ref[...] += part

            @pl.when(k == nk - 1)
            def _():
                finish(acc_ref[...] + part)

    operands, in_specs, aliases = [], [], {}
    if prev is not None:
        operands.append(prev)
        in_specs.append(pl.BlockSpec(memory_space=pl.ANY))
        aliases = {0: 0}
    operands += [a, b]
    in_specs += [a_spec, b_spec]
    if res is not None:
        operands.append(res)
        in_specs.append(res_spec)
    return pl.pallas_call(
        body, name=name, out_shape=out_shape, grid=grid, in_specs=in_specs, out_specs=o_spec,
        scratch_shapes=[pltpu.VMEM(acc_shape if nk > 1 else (8, 128), F32)],
        input_output_aliases=aliases,
        compiler_params=_params("parallel", "parallel", "arbitrary"),
    )(*operands)


NN = (((1,), (0,)), ((), ()))
NT = (((1,), (1,)), ((), ()))
TN = (((0,), (0,)), ((), ()))

_COL_TILES = (1536, 1408, 1024, 768, 512, 384, 256, 128)


def _mm_nn_col(name, a, w, layer, col_off=0, ncols=None, out_dtype=BF):
    M, K = a.shape
    _, _, R, C = w.shape
    assert R == K
    ncols = N_CHIPS * C if ncols is None else ncols
    tn = _pick(math.gcd(C, math.gcd(ncols, col_off) if col_off else ncols), _COL_TILES)
    tm = _pick(M, (1024, 512, 256))
    nb, off = C // tn, col_off // tn
    return _matmul(
        name, a, w, jax.ShapeDtypeStruct((M, ncols), out_dtype), (M // tm, ncols // tn, 1),
        pl.BlockSpec((tm, K), lambda i, j, k: (i, 0)),
        pl.BlockSpec((None, None, K, tn), lambda i, j, k: ((j + off) // nb, layer, 0, (j + off) % nb)),
        pl.BlockSpec((tm, tn), lambda i, j, k: (i, j)), NN, (tm, tn))


def _mm_nn_row(name, a, w, layer, res):
    M, K = a.shape
    _, _, R, C = w.shape
    assert N_CHIPS * R == K
    tk = _pick(R, (1408, 1024, 512, 256, 128))
    tm = _pick(M, (1024, 512, 256))
    tn = _pick(C, (1024, 512, 256))
    kb = R // tk
    return _matmul(
        name, a, w, jax.ShapeDtypeStruct((M, C), F32), (M // tm, C // tn, K // tk),
        pl.BlockSpec((tm, tk), lambda i, j, k: (i, k)),
        pl.BlockSpec((None, None, tk, tn), lambda i, j, k: (k // kb, layer, k % kb, j)),
        pl.BlockSpec((tm, tn), lambda i, j, k: (i, j)), NN, (tm, tn),
        res=res, res_spec=pl.BlockSpec((tm, tn), lambda i, j, k: (i, j)))


def _mm_nt_col(name, dy, w, layer, col_off=0, out_dtype=F32):
    M, n = dy.shape
    _, _, R, C = w.shape
    tk = _pick(math.gcd(C, math.gcd(n, col_off) if col_off else n), (2816,) + _COL_TILES)
    tm = _pick(M, (1024, 512, 256))
    tn = _pick(R, (1024, 512, 256))
    nb, off = C // tk, col_off // tk
    return _matmul(
        name, dy, w, jax.ShapeDtypeStruct((M, R), out_dtype), (M // tm, R // tn, n // tk),
        pl.BlockSpec((tm, tk), lambda i, j, k: (i, k)),
        pl.BlockSpec((None, None, tn, tk), lambda i, j, k: ((k + off) // nb, layer, j, (k + off) % nb)),
        pl.BlockSpec((tm, tn), lambda i, j, k: (i, j)), NT, (tm, tn))


def _mm_nt_row(name, dy, w, layer, out_dtype=BF):
    M, C2 = dy.shape
    _, _, R, C = w.shape
    assert C2 == C
    tn = _pick(R, (1408, 1024, 512, 256, 128))
    tm = _pick(M, (1024, 512, 256))
    rb = R // tn
    return _matmul(
        name, dy, w, jax.ShapeDtypeStruct((M, N_CHIPS * R), out_dtype), (M // tm, N_CHIPS * R // tn, 1),
        pl.BlockSpec((tm, C), lambda i, j, k: (i, 0)),
        pl.BlockSpec((None, None, tn, C), lambda i, j, k: (j // rb, layer, j % rb, 0)),
        pl.BlockSpec((tm, tn), lambda i, j, k: (i, j)), NT, (tm, tn))


_TN_DEPTH = (2048, 1024, 512, 256)


def _half_index(rows, layer, tkx):
    if layer is None:
        hb = rows // 2 // tkx
        return rows // 2, lambda i: (i // hb, i % hb)
    return rows, lambda i: (layer, i)


def _mm_tn_col(name, xa, dy, C, col_off=0, prev=None, layer=None):
    M, K = xa.shape
    _, n = dy.shape
    tn = _pick(math.gcd(C, math.gcd(n, col_off) if col_off else n), _COL_TILES)
    tkx = _pick(K // 2 if layer is None else K, (1024, 512, 256, 128))
    tmr = _pick(M, _TN_DEPTH)
    rh, split = _half_index(K, layer, tkx)
    nb, off = C // tn, col_off // tn
    return _matmul(
        name, xa, dy, jax.ShapeDtypeStruct((N_CHIPS, 2, rh, C), BF), (K // tkx, n // tn, M // tmr),
        pl.BlockSpec((tmr, tkx), lambda i, j, k: (k, i)),
        pl.BlockSpec((tmr, tn), lambda i, j, k: (k, j)),
        pl.BlockSpec((None, None, tkx, tn), lambda i, j, k: ((j + off) // nb, *split(i), (j + off) % nb)),
        TN, (tkx, tn), prev=prev)


def _mm_tn_row(name, xa, dy, prev=None, layer=None):
    M, K = xa.shape
    _, C = dy.shape
    R = K // N_CHIPS
    tkx = _pick(R // 2 if layer is None else R, (1408, 1024, 512, 256, 128))
    tmr = _pick(M, _TN_DEPTH)
    tn = _pick(C, (2048, 1024, 512, 256) if tkx <= 512 else (1024, 512, 256))
    rh, split = _half_index(R, layer, tkx)
    rb = R // tkx
    return _matmul(
        name, xa, dy, jax.ShapeDtypeStruct((N_CHIPS, 2, rh, C), BF), (K // tkx, C // tn, M // tmr),
        pl.BlockSpec((tmr, tkx), lambda i, j, k: (k, i)),
        pl.BlockSpec((tmr, tn), lambda i, j, k: (k, j)),
        pl.BlockSpec((None, None, tkx, tn), lambda i, j, k: (i // rb, *split(i % rb), j)),
        TN, (tkx, tn), prev=prev)


NORM_ROWS = 256
LANES = 128


def _chunk_scratch(tm, width):
    return pltpu.VMEM((width // LANES, tm, LANES), F32)


def _store_chunks(scr, value):
    for c in range(scr.shape[0]):
        scr[c] = value[:, c * LANES:(c + 1) * LANES]


def _load_chunks(scr):
    return jnp.concatenate([scr[c] for c in range(scr.shape[0])], axis=1)


def _to_residue_major(scr, o_ref, d, dtype):
    tm = scr.shape[1]
    for c in range(scr.shape[0]):
        for res in range(d):
            o_ref[res, :, c * LANES:(c + 1) * LANES] = scr[c, pl.ds(res, tm // d, stride=d), :].astype(dtype)


def _to_natural(scr, ref, d):
    tm = scr.shape[1]
    for c in range(scr.shape[0]):
        for res in range(d):
            scr[c, pl.ds(res, tm // d, stride=d), :] = ref[res, :, c * LANES:(c + 1) * LANES].astype(F32)


def _rmsnorm_fwd(name, x, g, dilated=False):
    S, D = x.shape
    tm = NORM_ROWS
    dils = DILATIONS[1:] if dilated else ()

    def body(x_ref, g_ref, h_ref, *rest):
        xv = x_ref[...]
        r = lax.rsqrt(jnp.mean(xv * xv, axis=1, keepdims=True) + NORM_EPS)
        h = xv * r * g_ref[...]
        h_ref[...] = h.astype(BF)
        if dils:
            scr = rest[-1]
            _store_chunks(scr, h)
            for o_ref, d in zip(rest[:-1], dils):
                _to_residue_major(scr, o_ref, d, BF)

    out_shape = [jax.ShapeDtypeStruct((S, D), BF)]
    out_specs = [pl.BlockSpec((tm, D), lambda i: (i, 0))]
    for d in dils:
        out_shape.append(jax.ShapeDtypeStruct((d, S // d, D), BF))
        out_specs.append(pl.BlockSpec((d, tm // d, D), lambda i: (0, i, 0)))
    outs = pl.pallas_call(
        body, name=name, out_shape=out_shape, grid=(S // tm,),
        in_specs=[pl.BlockSpec((tm, D), lambda i: (i, 0)), pl.BlockSpec((1, D), lambda i: (0, 0))],
        out_specs=out_specs,
        scratch_shapes=[_chunk_scratch(tm, D)] if dils else [],
        compiler_params=_params("parallel"),
    )(x, g)
    return [outs[0]] + [o.reshape(S, D) for o in outs[1:]]


def _rmsnorm_bwd(name, x, g, dhs=(), dres=None, target=None):
    S, D = x.shape
    tm = NORM_ROWS
    n_dh = len(dhs)

    def body(*refs):
        refs = list(refs)
        x_ref, g_ref = refs[0], refs[1]
        dh_refs = refs[2:2 + n_dh]
        pos = 2 + n_dh
        dres_ref = tgt_ref = None
        if dres is not None:
            dres_ref = refs[pos]
            pos += 1
        if target is not None:
            tgt_ref = refs[pos]
            pos += 1
        dx_ref, dxb_ref, dg_ref = refs[pos:pos + 3]
        pos += 3
        loss_ref = None
        if target is not None:
            loss_ref = refs[pos]
            pos += 1
        scr = refs[pos] if any(d > 1 for _, d in dhs) else None
        i = pl.program_id(0)

        xv = x_ref[...]
        gv = g_ref[...]
        r = lax.rsqrt(jnp.mean(xv * xv, axis=1, keepdims=True) + NORM_EPS)
        xhat = xv * r
        if target is not None:
            err = xhat * gv - tgt_ref[...]
            dh = err * (1.0 / D)
            part = jnp.sum(jnp.sum(err * err, axis=1, keepdims=True), axis=0, keepdims=True) * (0.5 / D)
        else:
            dh = None
            for ref, d in zip(dh_refs, [d for _, d in dhs]):
                if d == 1:
                    v = ref[...]
                else:
                    _to_natural(scr, ref, d)
                    v = _load_chunks(scr)
                dh = v if dh is None else dh + v
        dxhat = dh * gv
        dx = r * (dxhat - xhat * jnp.mean(dxhat * xhat, axis=1, keepdims=True))
        if dres_ref is not None:
            dx = dx + dres_ref[...]
        dx_ref[...] = dx
        dxb_ref[...] = dx.astype(BF)
        dg = jnp.sum(dh * xhat, axis=0, keepdims=True)

        @pl.when(i == 0)
        def _():
            dg_ref[...] = dg
            if loss_ref is not None:
                loss_ref[...] = jnp.broadcast_to(part, loss_ref.shape)

        @pl.when(i > 0)
        def _():
            dg_ref[...] += dg
            if loss_ref is not None:
                loss_ref[...] += jnp.broadcast_to(part, loss_ref.shape)

    row = pl.BlockSpec((tm, D), lambda i: (i, 0))
    operands = [x, g]
    in_specs = [row, pl.BlockSpec((1, D), lambda i: (0, 0))]
    for arr, d in dhs:
        if d == 1:
            operands.append(arr)
            in_specs.append(row)
        else:
            operands.append(arr.reshape(d, S // d, D))
            in_specs.append(pl.BlockSpec((d, tm // d, D), lambda i: (0, i, 0)))
    if dres is not None:
        operands.append(dres)
        in_specs.append(row)
    if target is not None:
        operands.append(target)
        in_specs.append(row)
    out_shape = [jax.ShapeDtypeStruct((S, D), F32), jax.ShapeDtypeStruct((S, D), BF),
                 jax.ShapeDtypeStruct((1, D), F32)]
    out_specs = [row, row, pl.BlockSpec((1, D), lambda i: (0, 0))]
    if target is not None:
        out_shape.append(jax.ShapeDtypeStruct((1, STAT_LANES), F32))
        out_specs.append(pl.BlockSpec((1, STAT_LANES), lambda i: (0, 0)))
    scratch = [_chunk_scratch(tm, D)] if any(d > 1 for _, d in dhs) else []
    return pl.pallas_call(
        body, name=name, out_shape=out_shape, grid=(S // tm,), in_specs=in_specs, out_specs=out_specs,
        scratch_shapes=scratch, compiler_params=_params("arbitrary"),
    )(*operands)


CONV_ROWS = 128
CONV_COLS = 512


def _halo_specs(S, tm, width):
    nh = S // HALO
    per = tm // HALO
    cur = pl.BlockSpec((tm, width), lambda i: (i, 0))
    prev = pl.BlockSpec((HALO, width), lambda i: (jnp.maximum(i * per - 1, 0), 0))
    nxt = pl.BlockSpec((HALO, width), lambda i: (jnp.minimum((i + 1) * per, nh - 1), 0))
    return [cur, prev, nxt]


def _ext(refs, cs, inr):
    cur, prev, nxt = refs
    v = jnp.concatenate([prev[:, cs], cur[:, cs], nxt[:, cs]], axis=0).astype(F32)
    return jnp.where(inr, v, 0.0)


def _shift_prev(v):
    return pltpu.roll(v, 1, 0)


def _shift_next(v):
    return pltpu.roll(v, v.shape[0] - 1, 0)


def _shifts(v):
    return _shift_prev(v), _shift_next(v)


def _conv3(v, w, cs, b=None, shifted=None):
    vp, vn = _shifts(v) if shifted is None else shifted
    out = w[0:1, cs] * vp + w[1:2, cs] * v + w[2:3, cs] * vn
    return out if b is None else out + b[:, cs]


def _in_range(i, tm, tc, S):
    row = lax.broadcasted_iota(jnp.int32, (tm + 2 * HALO, tc), 0) + (i * tm - HALO)
    return jnp.logical_and(row >= 0, row < S)


def _core(v, tm):
    return v[HALO:HALO + tm, :]


def _acc_rows(ref, i, rows):
    for r, cs, val in rows:
        ref[r:r + 1, cs] += val


def _zero_first(ref, i):
    @pl.when(i == 0)
    def _():
        ref[...] = jnp.zeros(ref.shape, ref.dtype)


def _sc_fwd(name, z, w, b):
    S, D3 = z.shape
    D = D3 // 3
    tm, tc = CONV_ROWS, _pick(D, (CONV_COLS, 256, 128))

    def body(zc, zp, zn, w_ref, b_ref, y_ref):
        i = pl.program_id(0)
        inr = _in_range(i, tm, tc, S)
        zr = (zc, zp, zn)
        for c in range(D // tc):
            cs = slice(c * tc, (c + 1) * tc)
            u = _ext(zr, cs, inr)
            gc = _ext(zr, slice(2 * D + c * tc, 2 * D + (c + 1) * tc), inr)
            conv = _conv3(gc * u, w_ref, cs, b_ref)
            gb = zc[:, D + c * tc:D + (c + 1) * tc].astype(F32)
            y_ref[:, cs] = (gb * _core(conv, tm)).astype(BF)

    return pl.pallas_call(
        body, name=name, out_shape=jax.ShapeDtypeStruct((S, D), BF), grid=(S // tm,),
        in_specs=_halo_specs(S, tm, D3) + [pl.BlockSpec((3, D), lambda i: (0, 0)),
                                           pl.BlockSpec((1, D), lambda i: (0, 0))],
        out_specs=pl.BlockSpec((tm, D), lambda i: (i, 0)),
        compiler_params=_params("parallel"),
    )(z, z, z, w, b)


def _sc_bwd(name, z, dy, w, b):
    S, D3 = z.shape
    D = D3 // 3
    tm, tc = CONV_ROWS, _pick(D, (CONV_COLS, 256, 128))

    def body(zc, zp, zn, dc_, dp_, dn_, w_ref, b_ref, dz_ref, dwb_ref):
        i = pl.program_id(0)
        inr = _in_range(i, tm, tc, S)
        _zero_first(dwb_ref, i)
        zr, dr = (zc, zp, zn), (dc_, dp_, dn_)
        for c in range(D // tc):
            cs = slice(c * tc, (c + 1) * tc)
            u = _ext(zr, cs, inr)
            gb = _ext(zr, slice(D + c * tc, D + (c + 1) * tc), inr)
            gc = _ext(zr, slice(2 * D + c * tc, 2 * D + (c + 1) * tc), inr)
            dyv = _ext(dr, cs, inr)
            p = gc * u
            p_prev, p_next = _shifts(p)
            conv = _conv3(p, w_ref, cs, b_ref, shifted=(p_prev, p_next))
            dconv = dyv * gb
            dp = w_ref[0:1, cs] * _shift_next(dconv) + w_ref[1:2, cs] * dconv + w_ref[2:3, cs] * _shift_prev(dconv)
            dz_ref[:, cs] = _core(dp * gc, tm).astype(BF)
            dz_ref[:, D + c * tc:D + (c + 1) * tc] = _core(dyv * conv, tm).astype(BF)
            dz_ref[:, 2 * D + c * tc:2 * D + (c + 1) * tc] = _core(dp * u, tm).astype(BF)
            dcc = _core(dconv, tm)
            _acc_rows(dwb_ref, i, [
                (0, cs, jnp.sum(dcc * _core(p_prev, tm), axis=0, keepdims=True)),
                (1, cs, jnp.sum(dcc * _core(p, tm), axis=0, keepdims=True)),
                (2, cs, jnp.sum(dcc * _core(p_next, tm), axis=0, keepdims=True)),
                (3, cs, jnp.sum(dcc, axis=0, keepdims=True))])

    return pl.pallas_call(
        body, name=name,
        out_shape=[jax.ShapeDtypeStruct((S, D3), BF), jax.ShapeDtypeStruct((4, D), F32)], grid=(S // tm,),
        in_specs=_halo_specs(S, tm, D3) + _halo_specs(S, tm, D) + [
            pl.BlockSpec((3, D), lambda i: (0, 0)), pl.BlockSpec((1, D), lambda i: (0, 0))],
        out_specs=[pl.BlockSpec((tm, D3), lambda i: (i, 0)), pl.BlockSpec((4, D), lambda i: (0, 0))],
        compiler_params=_params("arbitrary"),
    )(z, z, z, dy, dy, dy, w, b)


def _sigmoid(v):
    return 1.0 / (1.0 + jnp.exp(-v))


def _ffn_fwd(name, u, w, b):
    S, F2 = u.shape
    Fh = F2 // 2
    tm, tc = CONV_ROWS, _pick(Fh, (CONV_COLS, 256, 128))

    def body(uc, up, un, w_ref, b_ref, f_ref):
        i = pl.program_id(0)
        inr = _in_range(i, tm, tc, S)
        ur = (uc, up, un)
        for c in range(Fh // tc):
            ca = slice(c * tc, (c + 1) * tc)
            cb = slice(Fh + c * tc, Fh + (c + 1) * tc)
            va = _core(_conv3(_ext(ur, ca, inr), w_ref, ca, b_ref), tm)
            vb = _core(_conv3(_ext(ur, cb, inr), w_ref, cb, b_ref), tm)
            f_ref[:, ca] = (va * _sigmoid(va) * vb).astype(BF)

    return pl.pallas_call(
        body, name=name, out_shape=jax.ShapeDtypeStruct((S, Fh), BF), grid=(S // tm,),
        in_specs=_halo_specs(S, tm, F2) + [pl.BlockSpec((3, F2), lambda i: (0, 0)),
                                           pl.BlockSpec((1, F2), lambda i: (0, 0))],
        out_specs=pl.BlockSpec((tm, Fh), lambda i: (i, 0)),
        compiler_params=_params("parallel"),
    )(u, u, u, w, b)


def _ffn_bwd(name, u, df, w, b):
    S, F2 = u.shape
    Fh = F2 // 2
    tm, tc = CONV_ROWS, _pick(Fh, (CONV_COLS, 256, 128))

    def body(uc, up, un, dc_, dp_, dn_, w_ref, b_ref, du_ref, dwb_ref):
        i = pl.program_id(0)
        inr = _in_range(i, tm, tc, S)
        _zero_first(dwb_ref, i)
        ur, dr = (uc, up, un), (dc_, dp_, dn_)
        for c in range(Fh // tc):
            ca = slice(c * tc, (c + 1) * tc)
            cb = slice(Fh + c * tc, Fh + (c + 1) * tc)
            ua, ub = _ext(ur, ca, inr), _ext(ur, cb, inr)
            dfv = _ext(dr, ca, inr)
            sa, sb = _shifts(ua), _shifts(ub)
            va = _conv3(ua, w_ref, ca, b_ref, shifted=sa)
            vb = _conv3(ub, w_ref, cb, b_ref, shifted=sb)
            sg = _sigmoid(va)
            dva = dfv * vb * (sg * (1.0 + va * (1.0 - sg)))
            dvb = dfv * (va * sg)
            rows = []
            for cs, uu, (u_prev, u_next), dv in ((ca, ua, sa, dva), (cb, ub, sb, dvb)):
                dcore = w_ref[0:1, cs] * _shift_next(dv) + w_ref[1:2, cs] * dv + w_ref[2:3, cs] * _shift_prev(dv)
                du_ref[:, cs] = _core(dcore, tm).astype(BF)
                dvc = _core(dv, tm)
                rows += [
                    (0, cs, jnp.sum(dvc * _core(u_prev, tm), axis=0, keepdims=True)),
                    (1, cs, jnp.sum(dvc * _core(uu, tm), axis=0, keepdims=True)),
                    (2, cs, jnp.sum(dvc * _core(u_next, tm), axis=0, keepdims=True)),
                    (3, cs, jnp.sum(dvc, axis=0, keepdims=True))]
            _acc_rows(dwb_ref, i, rows)

    return pl.pallas_call(
        body, name=name,
        out_shape=[jax.ShapeDtypeStruct((S, F2), BF), jax.ShapeDtypeStruct((4, F2), F32)], grid=(S // tm,),
        in_specs=_halo_specs(S, tm, F2) + _halo_specs(S, tm, Fh) + [
            pl.BlockSpec((3, F2), lambda i: (0, 0)), pl.BlockSpec((1, F2), lambda i: (0, 0))],
        out_specs=[pl.BlockSpec((tm, F2), lambda i: (i, 0)), pl.BlockSpec((4, F2), lambda i: (0, 0))],
        compiler_params=_params("arbitrary"),
    )(u, u, u, df, df, df, w, b)


def _alibi_slopes(H):
    return [2.0 ** (-ALIBI_MAX * (h + 1) / H) for h in range(H)]


def _window_specs(S, width, col):
    n64 = S // ATTN_HALF
    cur = pl.BlockSpec((ATTN_BLOCK, width), lambda b: (b, col))
    prev = pl.BlockSpec((ATTN_HALF, width), lambda b: (jnp.maximum(2 * b - 1, 0), col))
    nxt = pl.BlockSpec((ATTN_HALF, width), lambda b: (jnp.minimum(2 * b + 2, n64 - 1), col))
    return [cur, prev, nxt]


def _fill_window(buf, cur, prev, nxt):
    buf[0:ATTN_HALF] = prev[...]
    buf[ATTN_HALF:ATTN_HALF + ATTN_BLOCK] = cur[...]
    buf[ATTN_HALF + ATTN_BLOCK:2 * ATTN_BLOCK] = nxt[...]


def _band(b, L, queries_in_rows_of_block):
    QB, W = ATTN_BLOCK, 2 * ATTN_BLOCK
    a_loc = (b * QB) % L
    if queries_in_rows_of_block:
        row = lax.broadcasted_iota(jnp.int32, (QB, W), 0)
        col = lax.broadcasted_iota(jnp.int32, (QB, W), 1)
        rel = col - ATTN_HALF - row
        other = a_loc - ATTN_HALF + col
    else:
        row = lax.broadcasted_iota(jnp.int32, (W, QB), 0)
        col = lax.broadcasted_iota(jnp.int32, (W, QB), 1)
        rel = col + ATTN_HALF - row
        other = a_loc - ATTN_HALF + row
    dist = jnp.abs(rel)
    valid = jnp.logical_and(dist <= ATTN_HALF, jnp.logical_and(other >= 0, other < L))
    return dist.astype(F32), valid


def _lane_col(stats, h):
    lane = lax.broadcasted_iota(jnp.int32, stats.shape, 1)
    return jnp.sum(jnp.where(lane == h, stats, 0.0), axis=1, keepdims=True)


def _attn_fwd(name, qkv, d):
    S, D3 = qkv.shape
    D = D3 // 3
    H = D // HEAD_DIM
    L = S // d
    scale = HEAD_DIM ** -0.5
    slopes = _alibi_slopes(H)

    def body(q_ref, kc, kp, kn, vc, vp, vn, o_ref, lse_ref, kbuf, vbuf):
        b = pl.program_id(0)
        _fill_window(kbuf, kc, kp, kn)
        _fill_window(vbuf, vc, vp, vn)
        dist, valid = _band(b, L, True)
        dist = dist * float(d)
        lane = lax.broadcasted_iota(jnp.int32, (ATTN_BLOCK, STAT_LANES), 1)
        lse = jnp.zeros((ATTN_BLOCK, STAT_LANES), F32)
        for h in range(H):
            cs = slice(h * HEAD_DIM, (h + 1) * HEAD_DIM)
            s = lax.dot_general(q_ref[:, cs], kbuf[:, cs], NT, preferred_element_type=F32) * scale
            s = jnp.where(valid, s - slopes[h] * dist, NEG_INF)
            m = jnp.max(s, axis=1, keepdims=True)
            p = jnp.exp(s - m)
            den = jnp.sum(p, axis=1, keepdims=True)
            o = jnp.dot(p.astype(BF), vbuf[:, cs], preferred_element_type=F32)
            o_ref[:, cs] = o / den
            lse = jnp.where(lane == h, m + jnp.log(den), lse)
        lse_ref[...] = lse

    return pl.pallas_call(
        body, name=name,
        out_shape=[jax.ShapeDtypeStruct((S, D), F32), jax.ShapeDtypeStruct((S, STAT_LANES), F32)],
        grid=(S // ATTN_BLOCK,),
        in_specs=[pl.BlockSpec((ATTN_BLOCK, D), lambda b: (b, 0))] + _window_specs(S, D, 1) + _window_specs(S, D, 2),
        out_specs=[pl.BlockSpec((ATTN_BLOCK, D), lambda b: (b, 0)),
                   pl.BlockSpec((ATTN_BLOCK, STAT_LANES), lambda b: (b, 0))],
        scratch_shapes=[pltpu.VMEM((2 * ATTN_BLOCK, D), BF), pltpu.VMEM((2 * ATTN_BLOCK, D), BF)],
        compiler_params=_params("parallel"),
    )(qkv, qkv, qkv, qkv, qkv, qkv, qkv)


def _dil_specs(S, tm, width):
    specs = [pl.BlockSpec((tm, width), lambda i: (i, 0))]
    for d in DILATIONS[1:]:
        specs.append(pl.BlockSpec((d, tm // d, width), lambda i: (0, i, 0)))
    return specs


def _attn_combine(name, outs, lses):
    S, D = outs[0].shape
    H = D // HEAD_DIM
    tm = NORM_ROWS

    def body(o1, o4, o16, l1, l4, l16, o_ref, ob_ref, lse_ref, oscr, lscr):
        ls = [l1[...]]
        for ref, d in zip((l4, l16), DILATIONS[1:]):
            _to_natural(lscr, ref, d)
            ls.append(lscr[0])
        top = jnp.maximum(jnp.maximum(ls[0], ls[1]), ls[2])
        es = [jnp.exp(l - top) for l in ls]
        tot = es[0] + es[1] + es[2]
        lse_ref[...] = top + jnp.log(tot)
        ws = [e / tot for e in es]
        for gi, (ref, d) in enumerate(zip((o1, o4, o16), DILATIONS)):
            if d > 1:
                _to_natural(oscr, ref, d)
            for h in range(H):
                cs = slice(h * HEAD_DIM, (h + 1) * HEAD_DIM)
                term = _lane_col(ws[gi], h) * (ref[:, cs] if d == 1 else oscr[h])
                if gi == 0:
                    o_ref[:, cs] = term
                else:
                    o_ref[:, cs] += term
        ob_ref[...] = o_ref[...].astype(BF)

    outs3 = [outs[0]] + [o.reshape(d, S // d, D) for o, d in zip(outs[1:], DILATIONS[1:])]
    lses3 = [lses[0]] + [l.reshape(d, S // d, STAT_LANES) for l, d in zip(lses[1:], DILATIONS[1:])]
    row = pl.BlockSpec((tm, D), lambda i: (i, 0))
    return pl.pallas_call(
        body, name=name,
        out_shape=[jax.ShapeDtypeStruct((S, D), F32), jax.ShapeDtypeStruct((S, D), BF),
                   jax.ShapeDtypeStruct((S, STAT_LANES), F32)],
        grid=(S // tm,),
        in_specs=_dil_specs(S, tm, D) + _dil_specs(S, tm, STAT_LANES),
        out_specs=[row, row, pl.BlockSpec((tm, STAT_LANES), lambda i: (i, 0))],
        scratch_shapes=[_chunk_scratch(tm, D), _chunk_scratch(tm, STAT_LANES)],
        compiler_params=_params("parallel"),
    )(*outs3, *lses3)


def _attn_bwd_prep(name, do, o32, lse):
    S, D = do.shape
    H = D // HEAD_DIM
    tm = NORM_ROWS
    dils = DILATIONS[1:]

    def body(do_ref, o_ref, lse_ref, dl_ref, do4, do16, l4, l16, d4, d16, scr, sscr):
        lane = lax.broadcasted_iota(jnp.int32, (tm, STAT_LANES), 1)
        delta = jnp.zeros((tm, STAT_LANES), F32)
        for h in range(H):
            cs = slice(h * HEAD_DIM, (h + 1) * HEAD_DIM)
            dov = do_ref[:, cs].astype(F32)
            scr[h] = dov
            delta = jnp.where(lane == h, jnp.sum(dov * o_ref[:, cs], axis=1, keepdims=True), delta)
        dl_ref[...] = delta
        for ref, d in zip((do4, do16), dils):
            _to_residue_major(scr, ref, d, BF)
        for val, refs in ((lse_ref[...], (l4, l16)), (delta, (d4, d16))):
            sscr[0] = val
            for ref, d in zip(refs, dils):
                _to_residue_major(sscr, ref, d, F32)

    def perm_shapes(width, dt):
        return [jax.ShapeDtypeStruct((d, S // d, width), dt) for d in dils]

    def perm_specs(width):
        return [pl.BlockSpec((d, tm // d, width), lambda i: (0, i, 0)) for d in dils]

    row = lambda w: pl.BlockSpec((tm, w), lambda i: (i, 0))
    outs = pl.pallas_call(
        body, name=name,
        out_shape=[jax.ShapeDtypeStruct((S, STAT_LANES), F32)]
        + perm_shapes(D, BF) + perm_shapes(STAT_LANES, F32) + perm_shapes(STAT_LANES, F32),
        grid=(S // tm,),
        in_specs=[row(D), row(D), row(STAT_LANES)],
        out_specs=[row(STAT_LANES)] + perm_specs(D) + perm_specs(STAT_LANES) + perm_specs(STAT_LANES),
        scratch_shapes=[_chunk_scratch(tm, D), _chunk_scratch(tm, STAT_LANES)],
        compiler_params=_params("parallel"),
    )(do, o32, lse)
    dos = [do] + [a.reshape(S, D) for a in outs[1:3]]
    lss = [lse] + [a.reshape(S, STAT_LANES) for a in outs[3:5]]
    dls = [outs[0]] + [a.reshape(S, STAT_LANES) for a in outs[5:7]]
    return dos, lss, dls


def _attn_bwd(name, qkv, do, lse, delta, d):
    S, D3 = qkv.shape
    D = D3 // 3
    H = D // HEAD_DIM
    L = S // d
    scale = HEAD_DIM ** -0.5
    slopes = _alibi_slopes(H)
    QB = ATTN_BLOCK

    def body(qc, qp, qn, kc, kp, kn, vc, vp, vn, dc_, dp_, dn_, lc, lp, ln, ec, ep, en,
             out_ref, qbuf, kbuf, vbuf, dobuf, lbuf, ebuf):
        b = pl.program_id(0)
        for buf, trio in ((qbuf, (qc, qp, qn)), (kbuf, (kc, kp, kn)), (vbuf, (vc, vp, vn)),
                          (dobuf, (dc_, dp_, dn_)), (lbuf, (lc, lp, ln)), (ebuf, (ec, ep, en))):
            _fill_window(buf, *trio)
        dist_q, valid_q = _band(b, L, True)
        dist_k, valid_k = _band(b, L, False)
        dist_q = dist_q * float(d)
        dist_k = dist_k * float(d)
        lse_c, del_c = lc[...], ec[...]
        lse_w, del_w = lbuf[...], ebuf[...]
        for h in range(H):
            cs = slice(h * HEAD_DIM, (h + 1) * HEAD_DIM)
            q, do_h = qc[:, cs], dc_[:, cs]
            kw, vw = kbuf[:, cs], vbuf[:, cs]
            s = lax.dot_general(q, kw, NT, preferred_element_type=F32) * scale - slopes[h] * dist_q
            p = jnp.where(valid_q, jnp.exp(s - _lane_col(lse_c, h)), 0.0)
            dp = lax.dot_general(do_h, vw, NT, preferred_element_type=F32)
            ds = p * (dp - _lane_col(del_c, h))
            dq = jnp.dot(ds.astype(BF), kw, preferred_element_type=F32) * scale
            out_ref[:, cs] = dq.astype(BF)
            qw, dow = qbuf[:, cs], dobuf[:, cs]
            k, v = kc[:, cs], vc[:, cs]
            s2 = lax.dot_general(qw, k, NT, preferred_element_type=F32) * scale - slopes[h] * dist_k
            p2 = jnp.where(valid_k, jnp.exp(s2 - _lane_col(lse_w, h)), 0.0)
            dv = lax.dot_general(p2.astype(BF), dow, TN, preferred_element_type=F32)
            dp2 = lax.dot_general(dow, v, NT, preferred_element_type=F32)
            ds2 = p2 * (dp2 - _lane_col(del_w, h))
            dk = lax.dot_general(ds2.astype(BF), qw, TN, preferred_element_type=F32) * scale
            out_ref[:, D + h * HEAD_DIM:D + (h + 1) * HEAD_DIM] = dk.astype(BF)
            out_ref[:, 2 * D + h * HEAD_DIM:2 * D + (h + 1) * HEAD_DIM] = dv.astype(BF)

    W = 2 * QB
    return pl.pallas_call(
        body, name=name, out_shape=jax.ShapeDtypeStruct((S, D3), BF), grid=(S // QB,),
        in_specs=(_window_specs(S, D, 0) + _window_specs(S, D, 1) + _window_specs(S, D, 2)
                  + _window_specs(S, D, 0) + _window_specs(S, STAT_LANES, 0) + _window_specs(S, STAT_LANES, 0)),
        out_specs=pl.BlockSpec((QB, D3), lambda b: (b, 0)),
        scratch_shapes=[pltpu.VMEM((W, D), BF), pltpu.VMEM((W, D), BF), pltpu.VMEM((W, D), BF),
                        pltpu.VMEM((W, D), BF), pltpu.VMEM((W, STAT_LANES), F32), pltpu.VMEM((W, STAT_LANES), F32)],
        compiler_params=_params("parallel"),
    )(qkv, qkv, qkv, qkv, qkv, qkv, qkv, qkv, qkv, do, do, do, lse, lse, lse, delta, delta, delta)


def _cast_bf16(name, w):
    R, C = w.shape
    tr = _row_tile(R, C)

    def body(w_ref, o_ref):
        o_ref[...] = w_ref[...].astype(BF)

    return pl.pallas_call(
        body, name=name, out_shape=jax.ShapeDtypeStruct((R, C), BF), grid=(R // tr,),
        in_specs=[pl.BlockSpec((tr, C), lambda i: (i, 0))], out_specs=pl.BlockSpec((tr, C), lambda i: (i, 0)),
        compiler_params=_params("parallel"),
    )(w)


def _pair_sum(name, g, a):
    _, _, R, C = g.shape
    tr = _row_tile(R, C)

    def body(g_ref, a_ref, p32_ref, p16_ref):
        tot = g_ref[...].astype(F32) + a_ref[...].astype(F32)
        p32_ref[...] = tot
        p16_ref[...] = tot.astype(BF)

    slab = pl.BlockSpec((None, tr, C), lambda s, i: (s, i, 0))
    return pl.pallas_call(
        body, name=name,
        out_shape=[jax.ShapeDtypeStruct((N_CHIPS, R, C), F32), jax.ShapeDtypeStruct((N_CHIPS, R, C), BF)],
        grid=(N_CHIPS, R // tr),
        in_specs=[pl.BlockSpec((None, None, tr, C), lambda s, i: (s, lax.axis_index("c"), i, 0)), slab],
        out_specs=[slab, slab],
        compiler_params=_params("parallel", "parallel"),
    )(g, a)


def _chip_sum(name, p32, bsum):
    _, R, C = p32.shape
    tr = _row_tile(R, C)

    def body(p_ref, b_ref, o_ref):
        o_ref[...] = ((p_ref[...] + b_ref[0].astype(F32)) + b_ref[1].astype(F32)) + b_ref[2].astype(F32)

    return pl.pallas_call(
        body, name=name, out_shape=jax.ShapeDtypeStruct((2, R, C), F32), grid=(R // tr,),
        in_specs=[pl.BlockSpec((None, tr, C), lambda i: (2 * lax.axis_index("x") + lax.axis_index("y"), i, 0)),
                  pl.BlockSpec((3, tr, C), lambda i: (0, i, 0))],
        out_specs=pl.BlockSpec((None, tr, C), lambda i: (lax.axis_index("c"), i, 0)),
        compiler_params=_params("parallel"),
    )(p32, bsum)


def _adamw(name, w, g, m, v):
    R, C = w.shape
    tr = _row_tile(R, C, 1 << 18) if R % 16 == 0 else R
    c1 = 1.0 - ADAM_B1 ** ADAM_STEP
    c2 = 1.0 - ADAM_B2 ** ADAM_STEP

    def body(w_ref, g_ref, m_ref, v_ref, d_ref, nm_ref, nv_ref):
        gv = g_ref[...]
        nm = ADAM_B1 * m_ref[...] + (1.0 - ADAM_B1) * gv
        nv = ADAM_B2 * v_ref[...] + (1.0 - ADAM_B2) * (gv * gv)
        nm_ref[...] = nm
        nv_ref[...] = nv
        d_ref[...] = -ADAM_LR * ((nm / c1) / (jnp.sqrt(nv / c2) + ADAM_EPS) + ADAM_WD * w_ref[...])

    spec = pl.BlockSpec((tr, C), lambda i: (i, 0))
    return pl.pallas_call(
        body, name=name, out_shape=[jax.ShapeDtypeStruct((R, C), F32)] * 3, grid=(R // tr,),
        in_specs=[spec] * 4, out_specs=[spec] * 3, compiler_params=_params("parallel"),
    )(w, g, m, v)


def _coords():
    return lax.axis_index("x"), lax.axis_index("y"), lax.axis_index("c")


def _flip(x, y, c, k):
    return (1 - x if k & 4 else x, 1 - y if k & 2 else y, 1 - c if k & 1 else c)


def _small_exchange(name, buf, reduce):
    rows = buf.shape[0]

    def body(x_ref, o_ref, land, send_sems, recv_sems):
        x, y, c = _coords()
        me = 4 * x + 2 * y + c

        def copy(k, sending):
            px, py, pc = _flip(x, y, c, k)
            slot = me if sending else 4 * px + 2 * py + pc
            return pltpu.make_async_remote_copy(
                src_ref=x_ref, dst_ref=land.at[slot], send_sem=send_sems.at[k - 1], recv_sem=recv_sems.at[k - 1],
                device_id=(px, py, pc), device_id_type=MESH)

        for k in range(1, 8):
            copy(k, True).start()
        land[me] = x_ref[...]
        for k in range(1, 8):
            copy(k, False).wait()
        if reduce:
            acc = land[0]
            for s in range(1, 8):
                acc = acc + land[s]
            o_ref[...] = acc
        else:
            o_ref[...] = land[...]

    out_shape = jax.ShapeDtypeStruct((rows, 128) if reduce else (8, rows, 128), F32)
    return pl.pallas_call(
        body, name=name, out_shape=out_shape,
        in_specs=[pl.BlockSpec(memory_space=pltpu.VMEM)], out_specs=pl.BlockSpec(memory_space=pltpu.VMEM),
        scratch_shapes=[pltpu.VMEM((8, rows, 128), F32), pltpu.SemaphoreType.DMA((7,)), pltpu.SemaphoreType.DMA((7,))],
        compiler_params=pltpu.CompilerParams(vmem_limit_bytes=VMEM_LIMIT),
    )(buf)


def _allgather_weights(name, shards):
    T = len(shards)

    def body(*refs):
        ins, outs = refs[:T], refs[T:2 * T]
        send_sems, recv_sems, local_sems = refs[2 * T:]
        x, y, c = _coords()
        chip = 2 * x + y
        sib = (x, y, 1 - c)
        chips = [_flip(x, y, c, k) for k in (4, 2, 6)]

        def slab(t, cx, cy, cc):
            return outs[t].at[2 * cx + cy, cc]

        def copy(t, k, src, dst, to):
            return pltpu.make_async_remote_copy(
                src_ref=src, dst_ref=dst, send_sem=send_sems.at[t, k], recv_sem=recv_sems.at[t, k],
                device_id=to, device_id_type=MESH)

        started, local = [], []
        for t in range(T):
            lc = pltpu.make_async_copy(ins[t].at[c], outs[t].at[chip, c], local_sems.at[t])
            lc.start()
            local.append(lc)
            for j, to in enumerate(chips):
                cp = copy(t, 1 + j, ins[t].at[c], slab(t, x, y, c), to)
                cp.start()
                started.append(cp)
        for t in range(T):
            cp = copy(t, 0, ins[t].at[c], slab(t, x, y, c), sib)
            cp.start()
            started.append(cp)
        for t in range(T):
            for j, (px, py, pc) in enumerate(chips):
                held = slab(t, px, py, c)
                copy(t, 1 + j, held, held, (px, py, pc)).wait_recv()
                cp = copy(t, 4 + j, held, held, sib)
                cp.start()
                started.append(cp)
        for t in range(T):
            got = slab(t, x, y, 1 - c)
            copy(t, 0, got, got, sib).wait_recv()
            for j, (px, py, pc) in enumerate(chips):
                got = slab(t, px, py, 1 - c)
                copy(t, 4 + j, got, got, sib).wait_recv()
        for cp in started:
            cp.wait_send()
        for lc in local:
            lc.wait()

    anyspec = pl.BlockSpec(memory_space=pl.ANY)
    return pl.pallas_call(
        body, name=name,
        out_shape=[jax.ShapeDtypeStruct((N_CHIPS,) + s.shape, s.dtype) for s in shards],
        in_specs=[anyspec] * T, out_specs=[anyspec] * T,
        scratch_shapes=[pltpu.SemaphoreType.DMA((T, 7)), pltpu.SemaphoreType.DMA((T, 7)),
                        pltpu.SemaphoreType.DMA((T,))],
    )(*shards)


def _pair_send(name, grads):
    T = len(grads)

    def body(*refs):
        ins, outs = refs[:T], refs[T:2 * T]
        send_sems, recv_sems = refs[2 * T:]
        x, y, c = _coords()
        sib = (x, y, 1 - c)
        cps = []
        for t in range(T):
            for s in range(N_CHIPS):
                cp = pltpu.make_async_remote_copy(
                    src_ref=ins[t].at[s, 1 - c], dst_ref=outs[t].at[s], send_sem=send_sems.at[t, s],
                    recv_sem=recv_sems.at[t, s], device_id=sib, device_id_type=MESH)
                cp.start()
                cps.append(cp)
        for cp in cps:
            cp.wait()

    anyspec = pl.BlockSpec(memory_space=pl.ANY)
    return pl.pallas_call(
        body, name=name,
        out_shape=[jax.ShapeDtypeStruct((N_CHIPS,) + g.shape[2:], g.dtype) for g in grads],
        in_specs=[anyspec] * T, out_specs=[anyspec] * T,
        scratch_shapes=[pltpu.SemaphoreType.DMA((T, N_CHIPS)), pltpu.SemaphoreType.DMA((T, N_CHIPS))],
    )(*grads)


def _chip_send(name, parts):
    T = len(parts)

    def body(*refs):
        ins, outs = refs[:T], refs[T:2 * T]
        send_sems, recv_sems = refs[2 * T:]
        x, y, c = _coords()
        cps = []
        for t in range(T):
            for j, k in enumerate((4, 2, 6)):
                px, py, pc = _flip(x, y, c, k)
                cp = pltpu.make_async_remote_copy(
                    src_ref=ins[t].at[2 * px + py], dst_ref=outs[t].at[j], send_sem=send_sems.at[t, j],
                    recv_sem=recv_sems.at[t, j], device_id=(px, py, pc), device_id_type=MESH)
                cp.start()
                cps.append(cp)
        for cp in cps:
            cp.wait()

    anyspec = pl.BlockSpec(memory_space=pl.ANY)
    return pl.pallas_call(
        body, name=name,
        out_shape=[jax.ShapeDtypeStruct((3,) + p.shape[1:], p.dtype) for p in parts],
        in_specs=[anyspec] * T, out_specs=[anyspec] * T,
        scratch_shapes=[pltpu.SemaphoreType.DMA((T, 3)), pltpu.SemaphoreType.DMA((T, 3))],
    )(*parts)


def _pair_fill(name, fulls):
    T = len(fulls)

    def body(*refs):
        outs = refs[T:2 * T]
        send_sems, recv_sems = refs[2 * T:]
        x, y, c = _coords()
        sib = (x, y, 1 - c)
        cps = []
        for t in range(T):
            send = pltpu.make_async_remote_copy(
                src_ref=outs[t].at[c], dst_ref=outs[t].at[c], send_sem=send_sems.at[t],
                recv_sem=recv_sems.at[t], device_id=sib, device_id_type=MESH)
            recv = pltpu.make_async_remote_copy(
                src_ref=outs[t].at[1 - c], dst_ref=outs[t].at[1 - c], send_sem=send_sems.at[t],
                recv_sem=recv_sems.at[t], device_id=sib, device_id_type=MESH)
            send.start()
            cps.append((send, recv))
        for send, recv in cps:
            send.wait_send()
            recv.wait_recv()

    anyspec = pl.BlockSpec(memory_space=pl.ANY)
    return pl.pallas_call(
        body, name=name,
        out_shape=[jax.ShapeDtypeStruct(f.shape, f.dtype) for f in fulls],
        in_specs=[anyspec] * T, out_specs=[anyspec] * T,
        input_output_aliases={t: t for t in range(T)},
        scratch_shapes=[pltpu.SemaphoreType.DMA((T,)), pltpu.SemaphoreType.DMA((T,))],
    )(*fulls)


def _pack(arrs):
    flat = jnp.concatenate([a.reshape(-1).astype(F32) for a in arrs])
    n = flat.shape[0]
    rows = -(-n // 1024) * 8
    return jnp.pad(flat, (0, rows * 128 - n)).reshape(rows, 128)


def _unpack(buf, shapes):
    flat = buf.reshape(-1)
    out, pos = [], 0
    for s in shapes:
        n = math.prod(s)
        out.append(flat[pos:pos + n].reshape(s))
        pos += n
    return out


def _as_halves(w):
    L, R, C = w.shape
    return w.reshape(2, L * R // 2, C)


def kernel(x, mix_norm_g, ffn_norm_g, final_norm_g, sc_w_in, sc_conv_w, sc_conv_b, sc_w_out, attn_w_qkv, attn_w_out, ffn_w_up, ffn_conv_w, ffn_conv_b, ffn_w_down, loss_target, m_mix_norm_g, m_ffn_norm_g, m_final_norm_g, m_sc_w_in, m_sc_conv_w, m_sc_conv_b, m_sc_w_out, m_attn_w_qkv, m_attn_w_out, m_ffn_w_up, m_ffn_conv_w, m_ffn_conv_b, m_ffn_w_down, v_mix_norm_g, v_ffn_norm_g, v_final_norm_g, v_sc_w_in, v_sc_conv_w, v_sc_conv_b, v_sc_w_out, v_attn_w_qkv, v_attn_w_out, v_ffn_w_up, v_ffn_conv_w, v_ffn_conv_b, v_ffn_w_down):
    S, D = x.shape[1], x.shape[2]
    xi, yi, ci = _coords()
    chip = 2 * xi + yi
    x0 = x.reshape(S, D)
    tgt = loss_target.reshape(S, D)

    conv_shapes = [sc_conv_w.shape, ffn_conv_w.shape]
    allc = _small_exchange("gather_conv_w", _pack([sc_conv_w, ffn_conv_w]), reduce=False)
    per_chip = [_unpack(allc[2 * k], conv_shapes) for k in range(N_CHIPS)]
    scw = jnp.concatenate([p[0] for p in per_chip], axis=-1)[0]
    fcw = jnp.concatenate([p[1] for p in per_chip], axis=-1)
    scb = sc_conv_b

    big = [sc_w_in, sc_w_out, attn_w_qkv, attn_w_out, ffn_w_up, ffn_w_down]
    big_names = ["sc_w_in", "sc_w_out", "attn_w_qkv", "attn_w_out", "ffn_w_up", "ffn_w_down"]
    shards = []
    for nm, w in zip(big_names, big):
        L, R, C = w.shape
        shards.append(_as_halves(_cast_bf16("cast_" + nm, w.reshape(L * R, C)).reshape(L, R, C)))
    gathered = _allgather_weights("allgather_weights", shards)
    w_in, w_out, w_qkv, w_ao, w_up, w_dn = [
        g.reshape((N_CHIPS,) + w.shape) for g, w in zip(gathered, big)]

    h0 = _rmsnorm_fwd("norm_mix0", x0, mix_norm_g[0:1])[0]
    z = _mm_nn_col("sc_in", h0, w_in, 0)
    y = _sc_fwd("sc_gate", z, scw, scb)
    x1 = _mm_nn_row("sc_out", y, w_out, 0, x0)
    h1 = _rmsnorm_fwd("norm_ffn0", x1, ffn_norm_g[0:1])[0]
    u0 = _mm_nn_col("ffn_up0", h1, w_up, 0)
    f0 = _ffn_fwd("ffn_gate0", u0, fcw[0], ffn_conv_b[0:1])
    x2 = _mm_nn_row("ffn_down0", f0, w_dn, 0, x1)
    h2s = _rmsnorm_fwd("norm_mix1", x2, mix_norm_g[1:2], dilated=True)
    qkvs = [_mm_nn_col(f"attn_qkv{d}", h, w_qkv, 0, col_off=gi * 3 * D, ncols=3 * D)
            for gi, (h, d) in enumerate(zip(h2s, DILATIONS))]
    og, lg = zip(*[_attn_fwd(f"attn_fwd{d}", q, d) for q, d in zip(qkvs, DILATIONS)])
    o32, ob, lse = _attn_combine("attn_combine", list(og), list(lg))
    x3 = _mm_nn_row("attn_out", ob, w_ao, 0, x2)
    h3 = _rmsnorm_fwd("norm_ffn1", x3, ffn_norm_g[1:2])[0]
    u1 = _mm_nn_col("ffn_up1", h3, w_up, 1)
    f1 = _ffn_fwd("ffn_gate1", u1, fcw[1], ffn_conv_b[1:2])
    x4 = _mm_nn_row("ffn_down1", f1, w_dn, 1, x3)

    dx4, dx4b, dg_final, loss_part = _rmsnorm_bwd("loss_norm_bwd", x4, final_norm_g.reshape(1, D), target=tgt)

    def ffn_backward(layer, xin, h, u, f, dxo, dxob, gain, g_up, g_dn):
        df = _mm_nt_row(f"ffn_down_dx{layer}", dxob, w_dn, layer)
        g_dn = _mm_tn_row(f"ffn_down_dw{layer}", f, dxob, prev=g_dn, layer=layer)
        du, dwb = _ffn_bwd(f"ffn_gate_bwd{layer}", u, df, fcw[layer], ffn_conv_b[layer:layer + 1])
        dh = _mm_nt_col(f"ffn_up_dx{layer}", du, w_up, layer)
        g_up = _mm_tn_col(f"ffn_up_dw{layer}", h, du, w_up.shape[3], prev=g_up, layer=layer)
        dxi, dxib, dg = _rmsnorm_bwd(f"norm_ffn_bwd{layer}", xin, gain, dhs=[(dh, 1)], dres=dxo)
        return dxi, dxib, dg, dwb, g_up, g_dn

    dx3, dx3b, dg_ffn1, dwb_ffn1, g_up, g_dn = ffn_backward(1, x3, h3, u1, f1, dx4, dx4b, ffn_norm_g[1:2], None, None)

    do = _mm_nt_row("attn_out_dx", dx3b, w_ao, 0)
    g_ao = _mm_tn_row("attn_out_dw", ob, dx3b)
    dos, lss, dls = _attn_bwd_prep("attn_bwd_prep", do, o32, lse)
    dqkvs = [_attn_bwd(f"attn_bwd{d}", q, a, b, c_, d)
             for q, a, b, c_, d in zip(qkvs, dos, lss, dls, DILATIONS)]
    dh2s = [(_mm_nt_col(f"attn_qkv_dx{d}", dq, w_qkv, 0, col_off=gi * 3 * D), d)
            for gi, (dq, d) in enumerate(zip(dqkvs, DILATIONS))]
    g_qkv = None
    for gi, (h, dq, d) in enumerate(zip(h2s, dqkvs, DILATIONS)):
        g_qkv = _mm_tn_col(f"attn_qkv_dw{d}", h, dq, w_qkv.shape[3], col_off=gi * 3 * D, prev=g_qkv)
    dx2, dx2b, dg_mix1 = _rmsnorm_bwd("norm_mix_bwd1", x2, mix_norm_g[1:2], dhs=dh2s, dres=dx3)

    dx1, dx1b, dg_ffn0, dwb_ffn0, g_up, g_dn = ffn_backward(0, x1, h1, u0, f0, dx2, dx2b, ffn_norm_g[0:1], g_up, g_dn)

    dy = _mm_nt_row("sc_out_dx", dx1b, w_out, 0)
    g_out = _mm_tn_row("sc_out_dw", y, dx1b)
    dz, dwb_sc = _sc_bwd("sc_gate_bwd", z, dy, scw, scb)
    dh0 = _mm_nt_col("sc_in_dx", dz, w_in, 0)
    g_in = _mm_tn_col("sc_in_dw", h0, dz, w_in.shape[3])
    dx0, _, dg_mix0 = _rmsnorm_bwd("norm_mix_bwd0", x0, mix_norm_g[0:1], dhs=[(dh0, 1)], dres=dx1)

    grads = [g_in, g_out, g_qkv, g_ao, g_up, g_dn]
    from_sib = _pair_send("rs_pair_send", grads)
    p32s, p16s = zip(*[_pair_sum("rs_pair_sum_" + t, g, a) for t, g, a in zip(big_names, grads, from_sib)])
    from_chips = _chip_send("rs_chip_send", list(p16s))
    fulls = [_chip_sum("rs_chip_sum_" + t, p, b) for t, p, b in zip(big_names, p32s, from_chips)]
    fulls = _pair_fill("rs_pair_fill", fulls)
    big_grads = [f.reshape(w.shape) for f, w in zip(fulls, big)]

    dconv_sc = dwb_sc[0:3].reshape(1, 3, D)
    dbias_sc = dwb_sc[3:4]
    dconv_ffn = jnp.stack([dwb_ffn0[0:3], dwb_ffn1[0:3]])
    dbias_ffn = jnp.concatenate([dwb_ffn0[3:4], dwb_ffn1[3:4]], axis=0)
    small_parts = [jnp.concatenate([dg_mix0, dg_mix1], axis=0), jnp.concatenate([dg_ffn0, dg_ffn1], axis=0),
                   dg_final.reshape(D), dconv_sc, dbias_sc, dconv_ffn, dbias_ffn, loss_part[0, 0:1]]
    small_shapes = [a.shape for a in small_parts]
    summed = _unpack(_small_exchange("allreduce_small", _pack(small_parts), reduce=True), small_shapes)
    g_mix, g_ffn, g_final, g_scw_full, g_scb, g_fcw_full, g_fcb, loss = summed
    loss = loss.reshape(())
    cw = sc_conv_w.shape[2]
    g_scw = lax.dynamic_slice_in_dim(g_scw_full, chip * cw, cw, axis=2)
    fw = ffn_conv_w.shape[2]
    g_fcw = lax.dynamic_slice_in_dim(g_fcw_full, chip * fw, fw, axis=2)

    names = ["mix_norm_g", "ffn_norm_g", "final_norm_g", "sc_w_in", "sc_conv_w", "sc_conv_b", "sc_w_out",
             "attn_w_qkv", "attn_w_out", "ffn_w_up", "ffn_conv_w", "ffn_conv_b", "ffn_w_down"]
    ws = dict(zip(names, [mix_norm_g, ffn_norm_g, final_norm_g, sc_w_in, sc_conv_w, sc_conv_b, sc_w_out,
                          attn_w_qkv, attn_w_out, ffn_w_up, ffn_conv_w, ffn_conv_b, ffn_w_down]))
    ms = dict(zip(names, [m_mix_norm_g, m_ffn_norm_g, m_final_norm_g, m_sc_w_in, m_sc_conv_w, m_sc_conv_b, m_sc_w_out,
                          m_attn_w_qkv, m_attn_w_out, m_ffn_w_up, m_ffn_conv_w, m_ffn_conv_b, m_ffn_w_down]))
    vs = dict(zip(names, [v_mix_norm_g, v_ffn_norm_g, v_final_norm_g, v_sc_w_in, v_sc_conv_w, v_sc_conv_b, v_sc_w_out,
                          v_attn_w_qkv, v_attn_w_out, v_ffn_w_up, v_ffn_conv_w, v_ffn_conv_b, v_ffn_w_down]))
    gs = {"mix_norm_g": g_mix, "ffn_norm_g": g_ffn, "final_norm_g": g_final, "sc_conv_w": g_scw,
          "sc_conv_b": g_scb, "ffn_conv_w": g_fcw, "ffn_conv_b": g_fcb}
    gs.update(dict(zip(big_names, big_grads)))

    deltas, new_m, new_v = {}, {}, {}
    small_names = [n for n in names if n not in big_names]
    packed = [_pack([d[n] for n in small_names]) for d in (ws, gs, ms, vs)]
    outs = _adamw("adamw_small", *packed)
    shapes = [ws[n].shape for n in small_names]
    for res, o in zip((deltas, new_m, new_v), outs):
        res.update(dict(zip(small_names, _unpack(o, shapes))))
    for n in big_names:
        shp = ws[n].shape
        two_d = (shp[0] * shp[1], shp[2])
        outs = _adamw("adamw_" + n, *[d[n].reshape(two_d) for d in (ws, gs, ms, vs)])
        for res, o in zip((deltas, new_m, new_v), outs):
            res[n] = o.reshape(shp)

    return (loss, dx0.reshape(x.shape), *[gs[n] for n in names], *[deltas[n] for n in names],
            *[new_m[n] for n in names], *[new_v[n] for n in names])
```

```python
import functools
import math

import jax
import jax.numpy as jnp
from jax import lax
from jax.experimental import pallas as pl
from jax.experimental.pallas import tpu as pltpu
from jax.experimental.pallas import tpu_sc as plsc

F32 = jnp.float32
BF = jnp.bfloat16
MESH = pl.DeviceIdType.MESH

HEAD_DIM = 128
ATTN_HALF = 64
ATTN_BLOCK = 128
DILATIONS = (1, 4, 16)
STAT_LANES = 128
HALO = 16
NORM_EPS = 1e-5
ALIBI_MAX = 8.0
NEG_INF = -1e30
N_CHIPS = 4
VMEM_LIMIT = 56 * 1024 * 1024

ADAM_LR = 0.001
ADAM_B1 = 0.9
ADAM_B2 = 0.999
ADAM_EPS = 1e-08
ADAM_WD = 0.01
ADAM_STEP = 10


def _pick(n, cands):
    for c in cands:
        if n % c == 0:
            return c
    raise ValueError(f"no tile for {n} in {cands}")


def _row_tile(rows, cols, max_elems=1 << 19):
    for c in (512, 256, 128, 64, 32, 16):
        if rows % c == 0 and c * cols <= max_elems:
            return c
    raise ValueError(f"no row tile for {rows}x{cols}")


def _params(*sem):
    return pltpu.CompilerParams(dimension_semantics=sem, vmem_limit_bytes=VMEM_LIMIT)


def _matmul(name, a, b, out_shape, grid, a_spec, b_spec, o_spec, contract, acc_shape,
            res=None, res_spec=None, prev=None):
    nk = grid[2]

    def body(*refs):
        refs = list(refs)
        if prev is not None:
            refs.pop(0)
        a_ref, b_ref = refs[0], refs[1]
        res_ref = refs[2] if res is not None else None
        o_ref = refs[3] if res is not None else refs[2]
        acc_ref = refs[-1]
        part = lax.dot_general(a_ref[...], b_ref[...], contract, preferred_element_type=F32)

        def finish(total):
            if res_ref is not None:
                total = total + res_ref[...]
            o_ref[...] = total.astype(o_ref.dtype)

        if nk == 1:
            finish(part)
        else:
            k = pl.program_id(2)

            @pl.when(k == 0)
            def _():
                acc_ref[...] = part

            @pl.when(jnp.logical_and(k > 0, k < nk - 1))
            def _():
                acc_ref[...] += part

            @pl.when(k == nk - 1)
            def _():
                finish(acc_ref[...] + part)

    operands, in_specs, aliases = [], [], {}
    if prev is not None:
        operands.append(prev)
        in_specs.append(pl.BlockSpec(memory_space=pl.ANY))
        aliases = {0: 0}
    operands += [a, b]
    in_specs += [a_spec, b_spec]
    if res is not None:
        operands.append(res)
        in_specs.append(res_spec)
    return pl.pallas_call(
        body, name=name, out_shape=out_shape, grid=grid, in_specs=in_specs, out_specs=o_spec,
        scratch_shapes=[pltpu.VMEM(acc_shape if nk > 1 else (8, 128), F32)],
        input_output_aliases=aliases,
        compiler_params=_params("parallel", "parallel", "arbitrary"),
    )(*operands)


NN = (((1,), (0,)), ((), ()))
NT = (((1,), (1,)), ((), ()))
TN = (((0,), (0,)), ((), ()))

_COL_TILES = (1536, 1408, 1024, 768, 512, 384, 256, 128)


def _mm_nn_col(name, a, w, layer, col_off=0, ncols=None, out_dtype=BF):
    M, K = a.shape
    _, _, R, C = w.shape
    assert R == K
    ncols = N_CHIPS * C if ncols is None else ncols
    tn = _pick(math.gcd(C, math.gcd(ncols, col_off) if col_off else ncols), _COL_TILES)
    tm = _pick(M, (1024, 512, 256))
    nb, off = C // tn, col_off // tn
    return _matmul(
        name, a, w, jax.ShapeDtypeStruct((M, ncols), out_dtype), (M // tm, ncols // tn, 1),
        pl.BlockSpec((tm, K), lambda i, j, k: (i, 0)),
        pl.BlockSpec((None, None, K, tn), lambda i, j, k: ((j + off) // nb, layer, 0, (j + off) % nb)),
        pl.BlockSpec((tm, tn), lambda i, j, k: (i, j)), NN, (tm, tn))


def _mm_nn_row(name, a, w, layer, res):
    M, K = a.shape
    _, _, R, C = w.shape
    assert N_CHIPS * R == K
    tk = _pick(R, (1408, 1024, 512, 256, 128))
    tm = _pick(M, (1024, 512, 256))
    tn = _pick(C, (1024, 512, 256))
    kb = R // tk
    return _matmul(
        name, a, w, jax.ShapeDtypeStruct((M, C), F32), (M // tm, C // tn, K // tk),
        pl.BlockSpec((tm, tk), lambda i, j, k: (i, k)),
        pl.BlockSpec((None, None, tk, tn), lambda i, j, k: (k // kb, layer, k % kb, j)),
        pl.BlockSpec((tm, tn), lambda i, j, k: (i, j)), NN, (tm, tn),
        res=res, res_spec=pl.BlockSpec((tm, tn), lambda i, j, k: (i, j)))


def _mm_nt_col(name, dy, w, layer, col_off=0, out_dtype=F32):
    M, n = dy.shape
    _, _, R, C = w.shape
    tk = _pick(math.gcd(C, math.gcd(n, col_off) if col_off else n), (2816,) + _COL_TILES)
    tm = _pick(M, (1024, 512, 256))
    tn = _pick(R, (1024, 512, 256))
    nb, off = C // tk, col_off // tk
    return _matmul(
        name, dy, w, jax.ShapeDtypeStruct((M, R), out_dtype), (M // tm, R // tn, n // tk),
        pl.BlockSpec((tm, tk), lambda i, j, k: (i, k)),
        pl.BlockSpec((None, None, tn, tk), lambda i, j, k: ((k + off) // nb, layer, j, (k + off) % nb)),
        pl.BlockSpec((tm, tn), lambda i, j, k: (i, j)), NT, (tm, tn))


def _mm_nt_row(name, dy, w, layer, out_dtype=BF):
    M, C2 = dy.shape
    _, _, R, C = w.shape
    assert C2 == C
    tn = _pick(R, (1408, 1024, 512, 256, 128))
    tm = _pick(M, (1024, 512, 256))
    rb = R // tn
    return _matmul(
        name, dy, w, jax.ShapeDtypeStruct((M, N_CHIPS * R), out_dtype), (M // tm, N_CHIPS * R // tn, 1),
        pl.BlockSpec((tm, C), lambda i, j, k: (i, 0)),
        pl.BlockSpec((None, None, tn, C), lambda i, j, k: (j // rb, layer, j % rb, 0)),
        pl.BlockSpec((tm, tn), lambda i, j, k: (i, j)), NT, (tm, tn))


_TN_DEPTH = (2048, 1024, 512, 256)


def _half_index(rows, layer, tkx):
    if layer is None:
        hb = rows // 2 // tkx
        return rows // 2, lambda i: (i // hb, i % hb)
    return rows, lambda i: (layer, i)


def _mm_tn_col(name, xa, dy, C, col_off=0, prev=None, layer=None):
    M, K = xa.shape
    _, n = dy.shape
    tn = _pick(math.gcd(C, math.gcd(n, col_off) if col_off else n), _COL_TILES)
    tkx = _pick(K // 2 if layer is None else K, (1024, 512, 256, 128))
    tmr = _pick(M, _TN_DEPTH)
    rh, split = _half_index(K, layer, tkx)
    nb, off = C // tn, col_off // tn
    return _matmul(
        name, xa, dy, jax.ShapeDtypeStruct((N_CHIPS, 2, rh, C), BF), (K // tkx, n // tn, M // tmr),
        pl.BlockSpec((tmr, tkx), lambda i, j, k: (k, i)),
        pl.BlockSpec((tmr, tn), lambda i, j, k: (k, j)),
        pl.BlockSpec((None, None, tkx, tn), lambda i, j, k: ((j + off) // nb, *split(i), (j + off) % nb)),
        TN, (tkx, tn), prev=prev)


def _mm_tn_row(name, xa, dy, prev=None, layer=None):
    M, K = xa.shape
    _, C = dy.shape
    R = K // N_CHIPS
    tkx = _pick(R // 2 if layer is None else R, (1408, 1024, 512, 256, 128))
    tmr = _pick(M, _TN_DEPTH)
    tn = _pick(C, (2048, 1024, 512, 256) if tkx <= 512 else (1024, 512, 256))
    rh, split = _half_index(R, layer, tkx)
    rb = R // tkx
    return _matmul(
        name, xa, dy, jax.ShapeDtypeStruct((N_CHIPS, 2, rh, C), BF), (K // tkx, C // tn, M // tmr),
        pl.BlockSpec((tmr, tkx), lambda i, j, k: (k, i)),
        pl.BlockSpec((tmr, tn), lambda i, j, k: (k, j)),
        pl.BlockSpec((None, None, tkx, tn), lambda i, j, k: (i // rb, *split(i % rb), j)),
        TN, (tkx, tn), prev=prev)


NORM_ROWS = 256
LANES = 128


def _chunk_scratch(tm, width):
    return pltpu.VMEM((width // LANES, tm, LANES), F32)


def _store_chunks(scr, value):
    for c in range(scr.shape[0]):
        scr[c] = value[:, c * LANES:(c + 1) * LANES]


def _load_chunks(scr):
    return jnp.concatenate([scr[c] for c in range(scr.shape[0])], axis=1)


def _to_residue_major(scr, o_ref, d, dtype):
    tm = scr.shape[1]
    for c in range(scr.shape[0]):
        for res in range(d):
            o_ref[res, :, c * LANES:(c + 1) * LANES] = scr[c, pl.ds(res, tm // d, stride=d), :].astype(dtype)


def _to_natural(scr, ref, d):
    tm = scr.shape[1]
    for c in range(scr.shape[0]):
        for res in range(d):
            scr[c, pl.ds(res, tm // d, stride=d), :] = ref[res, :, c * LANES:(c + 1) * LANES].astype(F32)


def _rmsnorm_fwd(name, x, g, dilated=False):
    S, D = x.shape
    tm = NORM_ROWS
    dils = DILATIONS[1:] if dilated else ()

    def body(x_ref, g_ref, h_ref, *rest):
        xv = x_ref[...]
        r = lax.rsqrt(jnp.mean(xv * xv, axis=1, keepdims=True) + NORM_EPS)
        h = xv * r * g_ref[...]
        h_ref[...] = h.astype(BF)
        if dils:
            scr = rest[-1]
            _store_chunks(scr, h)
            for o_ref, d in zip(rest[:-1], dils):
                _to_residue_major(scr, o_ref, d, BF)

    out_shape = [jax.ShapeDtypeStruct((S, D), BF)]
    out_specs = [pl.BlockSpec((tm, D), lambda i: (i, 0))]
    for d in dils:
        out_shape.append(jax.ShapeDtypeStruct((d, S // d, D), BF))
        out_specs.append(pl.BlockSpec((d, tm // d, D), lambda i: (0, i, 0)))
    outs = pl.pallas_call(
        body, name=name, out_shape=out_shape, grid=(S // tm,),
        in_specs=[pl.BlockSpec((tm, D), lambda i: (i, 0)), pl.BlockSpec((1, D), lambda i: (0, 0))],
        out_specs=out_specs,
        scratch_shapes=[_chunk_scratch(tm, D)] if dils else [],
        compiler_params=_params("parallel"),
    )(x, g)
    return [outs[0]] + [o.reshape(S, D) for o in outs[1:]]


def _rmsnorm_bwd(name, x, g, dhs=(), dres=None, target=None):
    S, D = x.shape
    tm = NORM_ROWS
    n_dh = len(dhs)

    def body(*refs):
        refs = list(refs)
        x_ref, g_ref = refs[0], refs[1]
        dh_refs = refs[2:2 + n_dh]
        pos = 2 + n_dh
        dres_ref = tgt_ref = None
        if dres is not None:
            dres_ref = refs[pos]
            pos += 1
        if target is not None:
            tgt_ref = refs[pos]
            pos += 1
        dx_ref, dxb_ref, dg_ref = refs[pos:pos + 3]
        pos += 3
        loss_ref = None
        if target is not None:
            loss_ref = refs[pos]
            pos += 1
        scr = refs[pos] if any(d > 1 for _, d in dhs) else None
        i = pl.program_id(0)

        xv = x_ref[...]
        gv = g_ref[...]
        r = lax.rsqrt(jnp.mean(xv * xv, axis=1, keepdims=True) + NORM_EPS)
        xhat = xv * r
        if target is not None:
            err = xhat * gv - tgt_ref[...]
            dh = err * (1.0 / D)
            part = jnp.sum(jnp.sum(err * err, axis=1, keepdims=True), axis=0, keepdims=True) * (0.5 / D)
        else:
            dh = None
            for ref, d in zip(dh_refs, [d for _, d in dhs]):
                if d == 1:
                    v = ref[...]
                else:
                    _to_natural(scr, ref, d)
                    v = _load_chunks(scr)
                dh = v if dh is None else dh + v
        dxhat = dh * gv
        dx = r * (dxhat - xhat * jnp.mean(dxhat * xhat, axis=1, keepdims=True))
        if dres_ref is not None:
            dx = dx + dres_ref[...]
        dx_ref[...] = dx
        dxb_ref[...] = dx.astype(BF)
        dg = jnp.sum(dh * xhat, axis=0, keepdims=True)

        @pl.when(i == 0)
        def _():
            dg_ref[...] = dg
            if loss_ref is not None:
                loss_ref[...] = jnp.broadcast_to(part, loss_ref.shape)

        @pl.when(i > 0)
        def _():
            dg_ref[...] += dg
            if loss_ref is not None:
                loss_ref[...] += jnp.broadcast_to(part, loss_ref.shape)

    row = pl.BlockSpec((tm, D), lambda i: (i, 0))
    operands = [x, g]
    in_specs = [row, pl.BlockSpec((1, D), lambda i: (0, 0))]
    for arr, d in dhs:
        if d == 1:
            operands.append(arr)
            in_specs.append(row)
        else:
            operands.append(arr.reshape(d, S // d, D))
            in_specs.append(pl.BlockSpec((d, tm // d, D), lambda i: (0, i, 0)))
    if dres is not None:
        operands.append(dres)
        in_specs.append(row)
    if target is not None:
        operands.append(target)
        in_specs.append(row)
    out_shape = [jax.ShapeDtypeStruct((S, D), F32), jax.ShapeDtypeStruct((S, D), BF),
                 jax.ShapeDtypeStruct((1, D), F32)]
    out_specs = [row, row, pl.BlockSpec((1, D), lambda i: (0, 0))]
    if target is not None:
        out_shape.append(jax.ShapeDtypeStruct((1, STAT_LANES), F32))
        out_specs.append(pl.BlockSpec((1, STAT_LANES), lambda i: (0, 0)))
    scratch = [_chunk_scratch(tm, D)] if any(d > 1 for _, d in dhs) else []
    return pl.pallas_call(
        body, name=name, out_shape=out_shape, grid=(S // tm,), in_specs=in_specs, out_specs=out_specs,
        scratch_shapes=scratch, compiler_params=_params("arbitrary"),
    )(*operands)


CONV_ROWS = 128
CONV_COLS = 512


def _halo_specs(S, tm, width):
    nh = S // HALO
    per = tm // HALO
    cur = pl.BlockSpec((tm, width), lambda i: (i, 0))
    prev = pl.BlockSpec((HALO, width), lambda i: (jnp.maximum(i * per - 1, 0), 0))
    nxt = pl.BlockSpec((HALO, width), lambda i: (jnp.minimum((i + 1) * per, nh - 1), 0))
    return [cur, prev, nxt]


def _ext(refs, cs, inr):
    cur, prev, nxt = refs
    v = jnp.concatenate([prev[:, cs], cur[:, cs], nxt[:, cs]], axis=0).astype(F32)
    return jnp.where(inr, v, 0.0)


def _shift_prev(v):
    return pltpu.roll(v, 1, 0)


def _shift_next(v):
    return pltpu.roll(v, v.shape[0] - 1, 0)


def _shifts(v):
    return _shift_prev(v), _shift_next(v)


def _conv3(v, w, cs, b=None, shifted=None):
    vp, vn = _shifts(v) if shifted is None else shifted
    out = w[0:1, cs] * vp + w[1:2, cs] * v + w[2:3, cs] * vn
    return out if b is None else out + b[:, cs]


def _in_range(i, tm, tc, S):
    row = lax.broadcasted_iota(jnp.int32, (tm + 2 * HALO, tc), 0) + (i * tm - HALO)
    return jnp.logical_and(row >= 0, row < S)


def _core(v, tm):
    return v[HALO:HALO + tm, :]


def _acc_rows(ref, i, rows):
    for r, cs, val in rows:
        ref[r:r + 1, cs] += val


def _zero_first(ref, i):
    @pl.when(i == 0)
    def _():
        ref[...] = jnp.zeros(ref.shape, ref.dtype)


def _sc_fwd(name, z, w, b):
    S, D3 = z.shape
    D = D3 // 3
    tm, tc = CONV_ROWS, _pick(D, (CONV_COLS, 256, 128))

    def body(zc, zp, zn, w_ref, b_ref, y_ref):
        i = pl.program_id(0)
        inr = _in_range(i, tm, tc, S)
        zr = (zc, zp, zn)
        for c in range(D // tc):
            cs = slice(c * tc, (c + 1) * tc)
            u = _ext(zr, cs, inr)
            gc = _ext(zr, slice(2 * D + c * tc, 2 * D + (c + 1) * tc), inr)
            conv = _conv3(gc * u, w_ref, cs, b_ref)
            gb = zc[:, D + c * tc:D + (c + 1) * tc].astype(F32)
            y_ref[:, cs] = (gb * _core(conv, tm)).astype(BF)

    return pl.pallas_call(
        body, name=name, out_shape=jax.ShapeDtypeStruct((S, D), BF), grid=(S // tm,),
        in_specs=_halo_specs(S, tm, D3) + [pl.BlockSpec((3, D), lambda i: (0, 0)),
                                           pl.BlockSpec((1, D), lambda i: (0, 0))],
        out_specs=pl.BlockSpec((tm, D), lambda i: (i, 0)),
        compiler_params=_params("parallel"),
    )(z, z, z, w, b)


def _sc_bwd(name, z, dy, w, b):
    S, D3 = z.shape
    D = D3 // 3
    tm, tc = CONV_ROWS, _pick(D, (CONV_COLS, 256, 128))

    def body(zc, zp, zn, dc_, dp_, dn_, w_ref, b_ref, dz_ref, dwb_ref):
        i = pl.program_id(0)
        inr = _in_range(i, tm, tc, S)
        _zero_first(dwb_ref, i)
        zr, dr = (zc, zp, zn), (dc_, dp_, dn_)
        for c in range(D // tc):
            cs = slice(c * tc, (c + 1) * tc)
            u = _ext(zr, cs, inr)
            gb = _ext(zr, slice(D + c * tc, D + (c + 1) * tc), inr)
            gc = _ext(zr, slice(2 * D + c * tc, 2 * D + (c + 1) * tc), inr)
            dyv = _ext(dr, cs, inr)
            p = gc * u
            p_prev, p_next = _shifts(p)
            conv = _conv3(p, w_ref, cs, b_ref, shifted=(p_prev, p_next))
            dconv = dyv * gb
            dp = w_ref[0:1, cs] * _shift_next(dconv) + w_ref[1:2, cs] * dconv + w_ref[2:3, cs] * _shift_prev(dconv)
            dz_ref[:, cs] = _core(dp * gc, tm).astype(BF)
            dz_ref[:, D + c * tc:D + (c + 1) * tc] = _core(dyv * conv, tm).astype(BF)
            dz_ref[:, 2 * D + c * tc:2 * D + (c + 1) * tc] = _core(dp * u, tm).astype(BF)
            dcc = _core(dconv, tm)
            _acc_rows(dwb_ref, i, [
                (0, cs, jnp.sum(dcc * _core(p_prev, tm), axis=0, keepdims=True)),
                (1, cs, jnp.sum(dcc * _core(p, tm), axis=0, keepdims=True)),
                (2, cs, jnp.sum(dcc * _core(p_next, tm), axis=0, keepdims=True)),
                (3, cs, jnp.sum(dcc, axis=0, keepdims=True))])

    return pl.pallas_call(
        body, name=name,
        out_shape=[jax.ShapeDtypeStruct((S, D3), BF), jax.ShapeDtypeStruct((4, D), F32)], grid=(S // tm,),
        in_specs=_halo_specs(S, tm, D3) + _halo_specs(S, tm, D) + [
            pl.BlockSpec((3, D), lambda i: (0, 0)), pl.BlockSpec((1, D), lambda i: (0, 0))],
        out_specs=[pl.BlockSpec((tm, D3), lambda i: (i, 0)), pl.BlockSpec((4, D), lambda i: (0, 0))],
        compiler_params=_params("arbitrary"),
    )(z, z, z, dy, dy, dy, w, b)


def _sigmoid(v):
    return 1.0 / (1.0 + jnp.exp(-v))


def _ffn_fwd(name, u, w, b):
    S, F2 = u.shape
    Fh = F2 // 2
    tm, tc = CONV_ROWS, _pick(Fh, (CONV_COLS, 256, 128))

    def body(uc, up, un, w_ref, b_ref, f_ref):
        i = pl.program_id(0)
        inr = _in_range(i, tm, tc, S)
        ur = (uc, up, un)
        for c in range(Fh // tc):
            ca = slice(c * tc, (c + 1) * tc)
            cb = slice(Fh + c * tc, Fh + (c + 1) * tc)
            va = _core(_conv3(_ext(ur, ca, inr), w_ref, ca, b_ref), tm)
            vb = _core(_conv3(_ext(ur, cb, inr), w_ref, cb, b_ref), tm)
            f_ref[:, ca] = (va * _sigmoid(va) * vb).astype(BF)

    return pl.pallas_call(
        body, name=name, out_shape=jax.ShapeDtypeStruct((S, Fh), BF), grid=(S // tm,),
        in_specs=_halo_specs(S, tm, F2) + [pl.BlockSpec((3, F2), lambda i: (0, 0)),
                                           pl.BlockSpec((1, F2), lambda i: (0, 0))],
        out_specs=pl.BlockSpec((tm, Fh), lambda i: (i, 0)),
        compiler_params=_params("parallel"),
    )(u, u, u, w, b)


def _ffn_bwd(name, u, df, w, b):
    S, F2 = u.shape
    Fh = F2 // 2
    tm, tc = CONV_ROWS, _pick(Fh, (CONV_COLS, 256, 128))

    def body(uc, up, un, dc_, dp_, dn_, w_ref, b_ref, du_ref, dwb_ref):
        i = pl.program_id(0)
        inr = _in_range(i, tm, tc, S)
        _zero_first(dwb_ref, i)
        ur, dr = (uc, up, un), (dc_, dp_, dn_)
        for c in range(Fh // tc):
            ca = slice(c * tc, (c + 1) * tc)
            cb = slice(Fh + c * tc, Fh + (c + 1) * tc)
            ua, ub = _ext(ur, ca, inr), _ext(ur, cb, inr)
            dfv = _ext(dr, ca, inr)
            sa, sb = _shifts(ua), _shifts(ub)
            va = _conv3(ua, w_ref, ca, b_ref, shifted=sa)
            vb = _conv3(ub, w_ref, cb, b_ref, shifted=sb)
            sg = _sigmoid(va)
            dva = dfv * vb * (sg * (1.0 + va * (1.0 - sg)))
            dvb = dfv * (va * sg)
            rows = []
            for cs, uu, (u_prev, u_next), dv in ((ca, ua, sa, dva), (cb, ub, sb, dvb)):
                dcore = w_ref[0:1, cs] * _shift_next(dv) + w_ref[1:2, cs] * dv + w_ref[2:3, cs] * _shift_prev(dv)
                du_ref[:, cs] = _core(dcore, tm).astype(BF)
                dvc = _core(dv, tm)
                rows += [
                    (0, cs, jnp.sum(dvc * _core(u_prev, tm), axis=0, keepdims=True)),
                    (1, cs, jnp.sum(dvc * _core(uu, tm), axis=0, keepdims=True)),
                    (2, cs, jnp.sum(dvc * _core(u_next, tm), axis=0, keepdims=True)),
                    (3, cs, jnp.sum(dvc, axis=0, keepdims=True))]
            _acc_rows(dwb_ref, i, rows)

    return pl.pallas_call(
        body, name=name,
        out_shape=[jax.ShapeDtypeStruct((S, F2), BF), jax.ShapeDtypeStruct((4, F2), F32)], grid=(S // tm,),
        in_specs=_halo_specs(S, tm, F2) + _halo_specs(S, tm, Fh) + [
            pl.BlockSpec((3, F2), lambda i: (0, 0)), pl.BlockSpec((1, F2), lambda i: (0, 0))],
        out_specs=[pl.BlockSpec((tm, F2), lambda i: (i, 0)), pl.BlockSpec((4, F2), lambda i: (0, 0))],
        compiler_params=_params("arbitrary"),
    )(u, u, u, df, df, df, w, b)


def _alibi_slopes(H):
    return [2.0 ** (-ALIBI_MAX * (h + 1) / H) for h in range(H)]


def _window_specs(S, width, col):
    n64 = S // ATTN_HALF
    cur = pl.BlockSpec((ATTN_BLOCK, width), lambda b: (b, col))
    prev = pl.BlockSpec((ATTN_HALF, width), lambda b: (jnp.maximum(2 * b - 1, 0), col))
    nxt = pl.BlockSpec((ATTN_HALF, width), lambda b: (jnp.minimum(2 * b + 2, n64 - 1), col))
    return [cur, prev, nxt]


def _fill_window(buf, cur, prev, nxt):
    buf[0:ATTN_HALF] = prev[...]
    buf[ATTN_HALF:ATTN_HALF + ATTN_BLOCK] = cur[...]
    buf[ATTN_HALF + ATTN_BLOCK:2 * ATTN_BLOCK] = nxt[...]


def _band(b, L, queries_in_rows_of_block):
    QB, W = ATTN_BLOCK, 2 * ATTN_BLOCK
    a_loc = (b * QB) % L
    if queries_in_rows_of_block:
        row = lax.broadcasted_iota(jnp.int32, (QB, W), 0)
        col = lax.broadcasted_iota(jnp.int32, (QB, W), 1)
        rel = col - ATTN_HALF - row
        other = a_loc - ATTN_HALF + col
    else:
        row = lax.broadcasted_iota(jnp.int32, (W, QB), 0)
        col = lax.broadcasted_iota(jnp.int32, (W, QB), 1)
        rel = col + ATTN_HALF - row
        other = a_loc - ATTN_HALF + row
    dist = jnp.abs(rel)
    valid = jnp.logical_and(dist <= ATTN_HALF, jnp.logical_and(other >= 0, other < L))
    return dist.astype(F32), valid


def _lane_col(stats, h):
    lane = lax.broadcasted_iota(jnp.int32, stats.shape, 1)
    return jnp.sum(jnp.where(lane == h, stats, 0.0), axis=1, keepdims=True)


def _attn_fwd(name, qkv, d):
    S, D3 = qkv.shape
    D = D3 // 3
    H = D // HEAD_DIM
    L = S // d
    scale = HEAD_DIM ** -0.5
    slopes = _alibi_slopes(H)

    def body(q_ref, kc, kp, kn, vc, vp, vn, o_ref, lse_ref, kbuf, vbuf):
        b = pl.program_id(0)
        _fill_window(kbuf, kc, kp, kn)
        _fill_window(vbuf, vc, vp, vn)
        dist, valid = _band(b, L, True)
        dist = dist * float(d)
        lane = lax.broadcasted_iota(jnp.int32, (ATTN_BLOCK, STAT_LANES), 1)
        lse = jnp.zeros((ATTN_BLOCK, STAT_LANES), F32)
        for h in range(H):
            cs = slice(h * HEAD_DIM, (h + 1) * HEAD_DIM)
            s = lax.dot_general(q_ref[:, cs], kbuf[:, cs], NT, preferred_element_type=F32) * scale
            s = jnp.where(valid, s - slopes[h] * dist, NEG_INF)
            m = jnp.max(s, axis=1, keepdims=True)
            p = jnp.exp(s - m)
            den = jnp.sum(p, axis=1, keepdims=True)
            o = jnp.dot(p.astype(BF), vbuf[:, cs], preferred_element_type=F32)
            o_ref[:, cs] = o / den
            lse = jnp.where(lane == h, m + jnp.log(den), lse)
        lse_ref[...] = lse

    return pl.pallas_call(
        body, name=name,
        out_shape=[jax.ShapeDtypeStruct((S, D), F32), jax.ShapeDtypeStruct((S, STAT_LANES), F32)],
        grid=(S // ATTN_BLOCK,),
        in_specs=[pl.BlockSpec((ATTN_BLOCK, D), lambda b: (b, 0))] + _window_specs(S, D, 1) + _window_specs(S, D, 2),
        out_specs=[pl.BlockSpec((ATTN_BLOCK, D), lambda b: (b, 0)),
                   pl.BlockSpec((ATTN_BLOCK, STAT_LANES), lambda b: (b, 0))],
        scratch_shapes=[pltpu.VMEM((2 * ATTN_BLOCK, D), BF), pltpu.VMEM((2 * ATTN_BLOCK, D), BF)],
        compiler_params=_params("parallel"),
    )(qkv, qkv, qkv, qkv, qkv, qkv, qkv)


def _dil_specs(S, tm, width):
    specs = [pl.BlockSpec((tm, width), lambda i: (i, 0))]
    for d in DILATIONS[1:]:
        specs.append(pl.BlockSpec((d, tm // d, width), lambda i: (0, i, 0)))
    return specs


def _attn_combine(name, outs, lses):
    S, D = outs[0].shape
    H = D // HEAD_DIM
    tm = NORM_ROWS

    def body(o1, o4, o16, l1, l4, l16, o_ref, ob_ref, lse_ref, oscr, lscr):
        ls = [l1[...]]
        for ref, d in zip((l4, l16), DILATIONS[1:]):
            _to_natural(lscr, ref, d)
            ls.append(lscr[0])
        top = jnp.maximum(jnp.maximum(ls[0], ls[1]), ls[2])
        es = [jnp.exp(l - top) for l in ls]
        tot = es[0] + es[1] + es[2]
        lse_ref[...] = top + jnp.log(tot)
        ws = [e / tot for e in es]
        for gi, (ref, d) in enumerate(zip((o1, o4, o16), DILATIONS)):
            if d > 1:
                _to_natural(oscr, ref, d)
            for h in range(H):
                cs = slice(h * HEAD_DIM, (h + 1) * HEAD_DIM)
                term = _lane_col(ws[gi], h) * (ref[:, cs] if d == 1 else oscr[h])
                if gi == 0:
                    o_ref[:, cs] = term
                else:
                    o_ref[:, cs] += term
        ob_ref[...] = o_ref[...].astype(BF)

    outs3 = [outs[0]] + [o.reshape(d, S // d, D) for o, d in zip(outs[1:], DILATIONS[1:])]
    lses3 = [lses[0]] + [l.reshape(d, S // d, STAT_LANES) for l, d in zip(lses[1:], DILATIONS[1:])]
    row = pl.BlockSpec((tm, D), lambda i: (i, 0))
    return pl.pallas_call(
        body, name=name,
        out_shape=[jax.ShapeDtypeStruct((S, D), F32), jax.ShapeDtypeStruct((S, D), BF),
                   jax.ShapeDtypeStruct((S, STAT_LANES), F32)],
        grid=(S // tm,),
        in_specs=_dil_specs(S, tm, D) + _dil_specs(S, tm, STAT_LANES),
        out_specs=[row, row, pl.BlockSpec((tm, STAT_LANES), lambda i: (i, 0))],
        scratch_shapes=[_chunk_scratch(tm, D), _chunk_scratch(tm, STAT_LANES)],
        compiler_params=_params("parallel"),
    )(*outs3, *lses3)


def _attn_bwd_prep(name, do, o32, lse):
    S, D = do.shape
    H = D // HEAD_DIM
    tm = NORM_ROWS
    dils = DILATIONS[1:]

    def body(do_ref, o_ref, lse_ref, dl_ref, do4, do16, l4, l16, d4, d16, scr, sscr):
        lane = lax.broadcasted_iota(jnp.int32, (tm, STAT_LANES), 1)
        delta = jnp.zeros((tm, STAT_LANES), F32)
        for h in range(H):
            cs = slice(h * HEAD_DIM, (h + 1) * HEAD_DIM)
            dov = do_ref[:, cs].astype(F32)
            scr[h] = dov
            delta = jnp.where(lane == h, jnp.sum(dov * o_ref[:, cs], axis=1, keepdims=True), delta)
        dl_ref[...] = delta
        for ref, d in zip((do4, do16), dils):
            _to_residue_major(scr, ref, d, BF)
        for val, refs in ((lse_ref[...], (l4, l16)), (delta, (d4, d16))):
            sscr[0] = val
            for ref, d in zip(refs, dils):
                _to_residue_major(sscr, ref, d, F32)

    def perm_shapes(width, dt):
        return [jax.ShapeDtypeStruct((d, S // d, width), dt) for d in dils]

    def perm_specs(width):
        return [pl.BlockSpec((d, tm // d, width), lambda i: (0, i, 0)) for d in dils]

    row = lambda w: pl.BlockSpec((tm, w), lambda i: (i, 0))
    outs = pl.pallas_call(
        body, name=name,
        out_shape=[jax.ShapeDtypeStruct((S, STAT_LANES), F32)]
        + perm_shapes(D, BF) + perm_shapes(STAT_LANES, F32) + perm_shapes(STAT_LANES, F32),
        grid=(S // tm,),
        in_specs=[row(D), row(D), row(STAT_LANES)],
        out_specs=[row(STAT_LANES)] + perm_specs(D) + perm_specs(STAT_LANES) + perm_specs(STAT_LANES),
        scratch_shapes=[_chunk_scratch(tm, D), _chunk_scratch(tm, STAT_LANES)],
        compiler_params=_params("parallel"),
    )(do, o32, lse)
    dos = [do] + [a.reshape(S, D) for a in outs[1:3]]
    lss = [lse] + [a.reshape(S, STAT_LANES) for a in outs[3:5]]
    dls = [outs[0]] + [a.reshape(S, STAT_LANES) for a in outs[5:7]]
    return dos, lss, dls


def _attn_bwd(name, qkv, do, lse, delta, d):
    S, D3 = qkv.shape
    D = D3 // 3
    H = D // HEAD_DIM
    L = S // d
    scale = HEAD_DIM ** -0.5
    slopes = _alibi_slopes(H)
    QB = ATTN_BLOCK

    def body(qc, qp, qn, kc, kp, kn, vc, vp, vn, dc_, dp_, dn_, lc, lp, ln, ec, ep, en,
             out_ref, qbuf, kbuf, vbuf, dobuf, lbuf, ebuf):
        b = pl.program_id(0)
        for buf, trio in ((qbuf, (qc, qp, qn)), (kbuf, (kc, kp, kn)), (vbuf, (vc, vp, vn)),
                          (dobuf, (dc_, dp_, dn_)), (lbuf, (lc, lp, ln)), (ebuf, (ec, ep, en))):
            _fill_window(buf, *trio)
        dist_q, valid_q = _band(b, L, True)
        dist_k, valid_k = _band(b, L, False)
        dist_q = dist_q * float(d)
        dist_k = dist_k * float(d)
        lse_c, del_c = lc[...], ec[...]
        lse_w, del_w = lbuf[...], ebuf[...]
        for h in range(H):
            cs = slice(h * HEAD_DIM, (h + 1) * HEAD_DIM)
            q, do_h = qc[:, cs], dc_[:, cs]
            kw, vw = kbuf[:, cs], vbuf[:, cs]
            s = lax.dot_general(q, kw, NT, preferred_element_type=F32) * scale - slopes[h] * dist_q
            p = jnp.where(valid_q, jnp.exp(s - _lane_col(lse_c, h)), 0.0)
            dp = lax.dot_general(do_h, vw, NT, preferred_element_type=F32)
            ds = p * (dp - _lane_col(del_c, h))
            dq = jnp.dot(ds.astype(BF), kw, preferred_element_type=F32) * scale
            out_ref[:, cs] = dq.astype(BF)
            qw, dow = qbuf[:, cs], dobuf[:, cs]
            k, v = kc[:, cs], vc[:, cs]
            s2 = lax.dot_general(qw, k, NT, preferred_element_type=F32) * scale - slopes[h] * dist_k
            p2 = jnp.where(valid_k, jnp.exp(s2 - _lane_col(lse_w, h)), 0.0)
            dv = lax.dot_general(p2.astype(BF), dow, TN, preferred_element_type=F32)
            dp2 = lax.dot_general(dow, v, NT, preferred_element_type=F32)
            ds2 = p2 * (dp2 - _lane_col(del_w, h))
            dk = lax.dot_general(ds2.astype(BF), qw, TN, preferred_element_type=F32) * scale
            out_ref[:, D + h * HEAD_DIM:D + (h + 1) * HEAD_DIM] = dk.astype(BF)
            out_ref[:, 2 * D + h * HEAD_DIM:2 * D + (h + 1) * HEAD_DIM] = dv.astype(BF)

    W = 2 * QB
    return pl.pallas_call(
        body, name=name, out_shape=jax.ShapeDtypeStruct((S, D3), BF), grid=(S // QB,),
        in_specs=(_window_specs(S, D, 0) + _window_specs(S, D, 1) + _window_specs(S, D, 2)
                  + _window_specs(S, D, 0) + _window_specs(S, STAT_LANES, 0) + _window_specs(S, STAT_LANES, 0)),
        out_specs=pl.BlockSpec((QB, D3), lambda b: (b, 0)),
        scratch_shapes=[pltpu.VMEM((W, D), BF), pltpu.VMEM((W, D), BF), pltpu.VMEM((W, D), BF),
                        pltpu.VMEM((W, D), BF), pltpu.VMEM((W, STAT_LANES), F32), pltpu.VMEM((W, STAT_LANES), F32)],
        compiler_params=_params("parallel"),
    )(qkv, qkv, qkv, qkv, qkv, qkv, qkv, qkv, qkv, do, do, do, lse, lse, lse, delta, delta, delta)


def _cast_bf16(name, w, layers):
    _, R, C = w.shape
    tr = _row_tile(R, C)
    first = layers[0]

    def body(w_ref, o_ref):
        o_ref[...] = w_ref[...].astype(BF)

    return pl.pallas_call(
        body, name=name, out_shape=jax.ShapeDtypeStruct((len(layers), R, C), BF), grid=(len(layers), R // tr),
        in_specs=[pl.BlockSpec((None, tr, C), lambda l, i: (first + l, i, 0))],
        out_specs=pl.BlockSpec((None, tr, C), lambda l, i: (l, i, 0)),
        compiler_params=_params("parallel", "parallel"),
    )(w)


def _pair_sum(name, g, a):
    _, _, R, C = g.shape
    tr = _row_tile(R, C)

    def body(g_ref, a_ref, p32_ref, p16_ref):
        tot = g_ref[...].astype(F32) + a_ref[...].astype(F32)
        p32_ref[...] = tot
        p16_ref[...] = tot.astype(BF)

    slab = pl.BlockSpec((None, tr, C), lambda s, i: (s, i, 0))
    return pl.pallas_call(
        body, name=name,
        out_shape=[jax.ShapeDtypeStruct((N_CHIPS, R, C), F32), jax.ShapeDtypeStruct((N_CHIPS, R, C), BF)],
        grid=(N_CHIPS, R // tr),
        in_specs=[pl.BlockSpec((None, None, tr, C), lambda s, i: (s, lax.axis_index("c"), i, 0)), slab],
        out_specs=[slab, slab],
        compiler_params=_params("parallel", "parallel"),
    )(g, a)


def _chip_sum(name, p32, bsum):
    _, R, C = p32.shape
    tr = _row_tile(R, C)

    def body(p_ref, b_ref, o_ref):
        o_ref[...] = ((p_ref[...] + b_ref[0].astype(F32)) + b_ref[1].astype(F32)) + b_ref[2].astype(F32)

    return pl.pallas_call(
        body, name=name, out_shape=jax.ShapeDtypeStruct((2, R, C), F32), grid=(R // tr,),
        in_specs=[pl.BlockSpec((None, tr, C), lambda i: (2 * lax.axis_index("x") + lax.axis_index("y"), i, 0)),
                  pl.BlockSpec((3, tr, C), lambda i: (0, i, 0))],
        out_specs=pl.BlockSpec((None, tr, C), lambda i: (lax.axis_index("c"), i, 0)),
        compiler_params=_params("parallel"),
    )(p32, bsum)


def _adamw(name, w, g, m, v):
    R, C = w.shape
    tr = _row_tile(R, C, 1 << 18) if R % 16 == 0 else R
    c1 = 1.0 - ADAM_B1 ** ADAM_STEP
    c2 = 1.0 - ADAM_B2 ** ADAM_STEP

    def body(w_ref, g_ref, m_ref, v_ref, d_ref, nm_ref, nv_ref):
        gv = g_ref[...]
        nm = ADAM_B1 * m_ref[...] + (1.0 - ADAM_B1) * gv
        nv = ADAM_B2 * v_ref[...] + (1.0 - ADAM_B2) * (gv * gv)
        nm_ref[...] = nm
        nv_ref[...] = nv
        d_ref[...] = -ADAM_LR * ((nm / c1) / (jnp.sqrt(nv / c2) + ADAM_EPS) + ADAM_WD * w_ref[...])

    spec = pl.BlockSpec((tr, C), lambda i: (i, 0))
    return pl.pallas_call(
        body, name=name, out_shape=[jax.ShapeDtypeStruct((R, C), F32)] * 3, grid=(R // tr,),
        in_specs=[spec] * 4, out_specs=[spec] * 3, compiler_params=_params("parallel"),
    )(w, g, m, v)


def _coords():
    return lax.axis_index("x"), lax.axis_index("y"), lax.axis_index("c")


def _flip(x, y, c, k):
    return (1 - x if k & 4 else x, 1 - y if k & 2 else y, 1 - c if k & 1 else c)


def _small_exchange(name, buf, reduce):
    rows = buf.shape[0]

    def body(x_ref, o_ref, land, send_sems, recv_sems):
        x, y, c = _coords()
        me = 4 * x + 2 * y + c

        def copy(k, sending):
            px, py, pc = _flip(x, y, c, k)
            slot = me if sending else 4 * px + 2 * py + pc
            return pltpu.make_async_remote_copy(
                src_ref=x_ref, dst_ref=land.at[slot], send_sem=send_sems.at[k - 1], recv_sem=recv_sems.at[k - 1],
                device_id=(px, py, pc), device_id_type=MESH)

        for k in range(1, 8):
            copy(k, True).start()
        land[me] = x_ref[...]
        for k in range(1, 8):
            copy(k, False).wait()
        if reduce:
            acc = land[0]
            for s in range(1, 8):
                acc = acc + land[s]
            o_ref[...] = acc
        else:
            o_ref[...] = land[...]

    out_shape = jax.ShapeDtypeStruct((rows, 128) if reduce else (8, rows, 128), F32)
    return pl.pallas_call(
        body, name=name, out_shape=out_shape,
        in_specs=[pl.BlockSpec(memory_space=pltpu.VMEM)], out_specs=pl.BlockSpec(memory_space=pltpu.VMEM),
        scratch_shapes=[pltpu.VMEM((8, rows, 128), F32), pltpu.SemaphoreType.DMA((7,)), pltpu.SemaphoreType.DMA((7,))],
        compiler_params=pltpu.CompilerParams(vmem_limit_bytes=VMEM_LIMIT),
    )(buf)


def _handshake(peers):
    barrier = pltpu.get_barrier_semaphore()
    for peer in peers:
        pl.semaphore_signal(barrier, inc=1, device_id=peer, device_id_type=MESH)
    pl.semaphore_wait(barrier, len(peers))


def _sequencer_mesh():
    return plsc.ScalarSubcoreMesh(axis_name="sequencer", num_cores=1)


def _allgather_weight(name, shard, collective_id):
    def body(in_ref, out_ref, send_sems, recv_sems, local_sem):
        x, y, c = _coords()
        chip = 2 * x + y
        sib = (x, y, 1 - c)
        chips = [_flip(x, y, c, k) for k in (4, 2, 6)]
        _handshake([sib] + chips)

        def slab(cx, cy, cc):
            return out_ref.at[2 * cx + cy, cc]

        def copy(k, src, dst, to):
            return pltpu.make_async_remote_copy(
                src_ref=src, dst_ref=dst, send_sem=send_sems.at[k], recv_sem=recv_sems.at[k],
                device_id=to, device_id_type=MESH)

        local = pltpu.make_async_copy(in_ref.at[c], out_ref.at[chip, c], local_sem)
        local.start()
        started = []
        for j, to in enumerate(chips):
            started.append(copy(1 + j, in_ref.at[c], slab(x, y, c), to))
        started.append(copy(0, in_ref.at[c], slab(x, y, c), sib))
        for cp in started:
            cp.start()
        for j, (px, py, pc) in enumerate(chips):
            held = slab(px, py, c)
            copy(1 + j, held, held, (px, py, pc)).wait_recv()
            cp = copy(4 + j, held, held, sib)
            cp.start()
            started.append(cp)
        got = slab(x, y, 1 - c)
        copy(0, got, got, sib).wait_recv()
        for j, (px, py, pc) in enumerate(chips):
            got = slab(px, py, 1 - c)
            copy(4 + j, got, got, sib).wait_recv()
        for cp in started:
            cp.wait_send()
        local.wait()

    return pl.kernel(
        body, out_type=jax.ShapeDtypeStruct((N_CHIPS,) + shard.shape, shard.dtype),
        mesh=_sequencer_mesh(), name=name,
        scratch_types=[pltpu.SemaphoreType.DMA((7,)), pltpu.SemaphoreType.DMA((7,)), pltpu.SemaphoreType.DMA],
        compiler_params=pltpu.CompilerParams(collective_id=collective_id),
    )(shard)


def _pair_send(name, grads):
    T = len(grads)

    def body(*refs):
        ins, outs = refs[:T], refs[T:2 * T]
        send_sems, recv_sems = refs[2 * T:]
        x, y, c = _coords()
        sib = (x, y, 1 - c)
        cps = []
        for t in range(T):
            for s in range(N_CHIPS):
                cp = pltpu.make_async_remote_copy(
                    src_ref=ins[t].at[s, 1 - c], dst_ref=outs[t].at[s], send_sem=send_sems.at[t, s],
                    recv_sem=recv_sems.at[t, s], device_id=sib, device_id_type=MESH)
                cp.start()
                cps.append(cp)
        for cp in cps:
            cp.wait()

    anyspec = pl.BlockSpec(memory_space=pl.ANY)
    return pl.pallas_call(
        body, name=name,
        out_shape=[jax.ShapeDtypeStruct((N_CHIPS,) + g.shape[2:], g.dtype) for g in grads],
        in_specs=[anyspec] * T, out_specs=[anyspec] * T,
        scratch_shapes=[pltpu.SemaphoreType.DMA((T, N_CHIPS)), pltpu.SemaphoreType.DMA((T, N_CHIPS))],
    )(*grads)


def _chip_send(name, part, collective_id):
    def body(in_ref, out_ref, send_sems, recv_sems):
        x, y, c = _coords()
        chips = [_flip(x, y, c, k) for k in (4, 2, 6)]
        _handshake(chips)
        cps = []
        for j, (px, py, pc) in enumerate(chips):
            cp = pltpu.make_async_remote_copy(
                src_ref=in_ref.at[2 * px + py], dst_ref=out_ref.at[j], send_sem=send_sems.at[j],
                recv_sem=recv_sems.at[j], device_id=(px, py, pc), device_id_type=MESH)
            cp.start()
            cps.append(cp)
        for cp in cps:
            cp.wait()

    return pl.kernel(
        body, out_type=jax.ShapeDtypeStruct((3,) + part.shape[1:], part.dtype),
        mesh=_sequencer_mesh(), name=name,
        scratch_types=[pltpu.SemaphoreType.DMA((3,)), pltpu.SemaphoreType.DMA((3,))],
        compiler_params=pltpu.CompilerParams(collective_id=collective_id),
    )(part)


def _pair_fill(name, fulls):
    T = len(fulls)

    def body(*refs):
        outs = refs[T:2 * T]
        send_sems, recv_sems = refs[2 * T:]
        x, y, c = _coords()
        sib = (x, y, 1 - c)
        cps = []
        for t in range(T):
            send = pltpu.make_async_remote_copy(
                src_ref=outs[t].at[c], dst_ref=outs[t].at[c], send_sem=send_sems.at[t],
                recv_sem=recv_sems.at[t], device_id=sib, device_id_type=MESH)
            recv = pltpu.make_async_remote_copy(
                src_ref=outs[t].at[1 - c], dst_ref=outs[t].at[1 - c], send_sem=send_sems.at[t],
                recv_sem=recv_sems.at[t], device_id=sib, device_id_type=MESH)
            send.start()
            cps.append((send, recv))
        for send, recv in cps:
            send.wait_send()
            recv.wait_recv()

    anyspec = pl.BlockSpec(memory_space=pl.ANY)
    return pl.pallas_call(
        body, name=name,
        out_shape=[jax.ShapeDtypeStruct(f.shape, f.dtype) for f in fulls],
        in_specs=[anyspec] * T, out_specs=[anyspec] * T,
        input_output_aliases={t: t for t in range(T)},
        scratch_shapes=[pltpu.SemaphoreType.DMA((T,)), pltpu.SemaphoreType.DMA((T,))],
    )(*fulls)


def _pack(arrs):
    flat = jnp.concatenate([a.reshape(-1).astype(F32) for a in arrs])
    n = flat.shape[0]
    rows = -(-n // 1024) * 8
    return jnp.pad(flat, (0, rows * 128 - n)).reshape(rows, 128)


def _unpack(buf, shapes):
    flat = buf.reshape(-1)
    out, pos = [], 0
    for s in shapes:
        n = math.prod(s)
        out.append(flat[pos:pos + n].reshape(s))
        pos += n
    return out


def kernel(x, mix_norm_g, ffn_norm_g, final_norm_g, sc_w_in, sc_conv_w, sc_conv_b, sc_w_out, attn_w_qkv, attn_w_out, ffn_w_up, ffn_conv_w, ffn_conv_b, ffn_w_down, loss_target, m_mix_norm_g, m_ffn_norm_g, m_final_norm_g, m_sc_w_in, m_sc_conv_w, m_sc_conv_b, m_sc_w_out, m_attn_w_qkv, m_attn_w_out, m_ffn_w_up, m_ffn_conv_w, m_ffn_conv_b, m_ffn_w_down, v_mix_norm_g, v_ffn_norm_g, v_final_norm_g, v_sc_w_in, v_sc_conv_w, v_sc_conv_b, v_sc_w_out, v_attn_w_qkv, v_attn_w_out, v_ffn_w_up, v_ffn_conv_w, v_ffn_conv_b, v_ffn_w_down):
    S, D = x.shape[1], x.shape[2]
    xi, yi, ci = _coords()
    chip = 2 * xi + yi
    x0 = x.reshape(S, D)
    tgt = loss_target.reshape(S, D)

    conv_shapes = [sc_conv_w.shape, ffn_conv_w.shape]
    allc = _small_exchange("gather_conv_w", _pack([sc_conv_w, ffn_conv_w]), reduce=False)
    per_chip = [_unpack(allc[2 * k], conv_shapes) for k in range(N_CHIPS)]
    scw = jnp.concatenate([p[0] for p in per_chip], axis=-1)[0]
    fcw = jnp.concatenate([p[1] for p in per_chip], axis=-1)
    scb = sc_conv_b

    big_names = ["sc_w_in", "sc_w_out", "attn_w_qkv", "attn_w_out", "ffn_w_up", "ffn_w_down"]
    big = dict(zip(big_names, [sc_w_in, sc_w_out, attn_w_qkv, attn_w_out, ffn_w_up, ffn_w_down]))
    n_gathers = [0]

    def gather(tag, w, layers):
        _, R, C = w.shape
        shard = _cast_bf16("cast_" + tag, w, layers).reshape(2, len(layers) * R // 2, C)
        cid = n_gathers[0]
        n_gathers[0] += 1
        return _allgather_weight("allgather_" + tag, shard, cid).reshape(N_CHIPS, len(layers), R, C)

    w_in = gather("sc_w_in", sc_w_in, (0,))
    w_out = gather("sc_w_out", sc_w_out, (0,))
    w_ups = [gather("ffn_w_up0", ffn_w_up, (0,))]
    w_dn = gather("ffn_w_down", ffn_w_down, (0, 1))
    w_qkv = gather("attn_w_qkv", attn_w_qkv, (0,))
    w_ao = gather("attn_w_out", attn_w_out, (0,))
    w_ups.append(gather("ffn_w_up1", ffn_w_up, (1,)))

    def reduce_scatter(tag, g):
        cid = n_gathers[0] + big_names.index(tag)
        from_sib = _pair_send("rs_pair_send_" + tag, [g])[0]
        p32, p16 = _pair_sum("rs_pair_sum_" + tag, g, from_sib)
        from_chips = _chip_send("rs_chip_send_" + tag, p16, cid)
        full = _chip_sum("rs_chip_sum_" + tag, p32, from_chips)
        return _pair_fill("rs_pair_fill_" + tag, [full])[0].reshape(big[tag].shape)

    h0 = _rmsnorm_fwd("norm_mix0", x0, mix_norm_g[0:1])[0]
    z = _mm_nn_col("sc_in", h0, w_in, 0)
    y = _sc_fwd("sc_gate", z, scw, scb)
    x1 = _mm_nn_row("sc_out", y, w_out, 0, x0)
    h1 = _rmsnorm_fwd("norm_ffn0", x1, ffn_norm_g[0:1])[0]
    u0 = _mm_nn_col("ffn_up0", h1, w_ups[0], 0)
    f0 = _ffn_fwd("ffn_gate0", u0, fcw[0], ffn_conv_b[0:1])
    x2 = _mm_nn_row("ffn_down0", f0, w_dn, 0, x1)
    h2s = _rmsnorm_fwd("norm_mix1", x2, mix_norm_g[1:2], dilated=True)
    qkvs = [_mm_nn_col(f"attn_qkv{d}", h, w_qkv, 0, col_off=gi * 3 * D, ncols=3 * D)
            for gi, (h, d) in enumerate(zip(h2s, DILATIONS))]
    og, lg = zip(*[_attn_fwd(f"attn_fwd{d}", q, d) for q, d in zip(qkvs, DILATIONS)])
    o32, ob, lse = _attn_combine("attn_combine", list(og), list(lg))
    x3 = _mm_nn_row("attn_out", ob, w_ao, 0, x2)
    h3 = _rmsnorm_fwd("norm_ffn1", x3, ffn_norm_g[1:2])[0]
    u1 = _mm_nn_col("ffn_up1", h3, w_ups[1], 0)
    f1 = _ffn_fwd("ffn_gate1", u1, fcw[1], ffn_conv_b[1:2])
    x4 = _mm_nn_row("ffn_down1", f1, w_dn, 1, x3)

    dx4, dx4b, dg_final, loss_part = _rmsnorm_bwd("loss_norm_bwd", x4, final_norm_g.reshape(1, D), target=tgt)

    def ffn_backward(layer, xin, h, u, f, dxo, dxob, gain, g_up, g_dn):
        df = _mm_nt_row(f"ffn_down_dx{layer}", dxob, w_dn, layer)
        g_dn = _mm_tn_row(f"ffn_down_dw{layer}", f, dxob, prev=g_dn, layer=layer)
        du, dwb = _ffn_bwd(f"ffn_gate_bwd{layer}", u, df, fcw[layer], ffn_conv_b[layer:layer + 1])
        dh = _mm_nt_col(f"ffn_up_dx{layer}", du, w_ups[layer], 0)
        g_up = _mm_tn_col(f"ffn_up_dw{layer}", h, du, w_ups[layer].shape[3], prev=g_up, layer=layer)
        dxi, dxib, dg = _rmsnorm_bwd(f"norm_ffn_bwd{layer}", xin, gain, dhs=[(dh, 1)], dres=dxo)
        return dxi, dxib, dg, dwb, g_up, g_dn

    dx3, dx3b, dg_ffn1, dwb_ffn1, g_up, g_dn = ffn_backward(1, x3, h3, u1, f1, dx4, dx4b, ffn_norm_g[1:2], None, None)

    big_grads = {}
    do = _mm_nt_row("attn_out_dx", dx3b, w_ao, 0)
    big_grads["attn_w_out"] = reduce_scatter("attn_w_out", _mm_tn_row("attn_out_dw", ob, dx3b))
    dos, lss, dls = _attn_bwd_prep("attn_bwd_prep", do, o32, lse)
    dqkvs = [_attn_bwd(f"attn_bwd{d}", q, a, b, c_, d)
             for q, a, b, c_, d in zip(qkvs, dos, lss, dls, DILATIONS)]
    dh2s = [(_mm_nt_col(f"attn_qkv_dx{d}", dq, w_qkv, 0, col_off=gi * 3 * D), d)
            for gi, (dq, d) in enumerate(zip(dqkvs, DILATIONS))]
    g_qkv = None
    for gi, (h, dq, d) in enumerate(zip(h2s, dqkvs, DILATIONS)):
        g_qkv = _mm_tn_col(f"attn_qkv_dw{d}", h, dq, w_qkv.shape[3], col_off=gi * 3 * D, prev=g_qkv)
    big_grads["attn_w_qkv"] = reduce_scatter("attn_w_qkv", g_qkv)
    dx2, dx2b, dg_mix1 = _rmsnorm_bwd("norm_mix_bwd1", x2, mix_norm_g[1:2], dhs=dh2s, dres=dx3)

    dx1, dx1b, dg_ffn0, dwb_ffn0, g_up, g_dn = ffn_backward(0, x1, h1, u0, f0, dx2, dx2b, ffn_norm_g[0:1], g_up, g_dn)
    big_grads["ffn_w_down"] = reduce_scatter("ffn_w_down", g_dn)
    big_grads["ffn_w_up"] = reduce_scatter("ffn_w_up", g_up)

    dy = _mm_nt_row("sc_out_dx", dx1b, w_out, 0)
    big_grads["sc_w_out"] = reduce_scatter("sc_w_out", _mm_tn_row("sc_out_dw", y, dx1b))
    dz, dwb_sc = _sc_bwd("sc_gate_bwd", z, dy, scw, scb)
    big_grads["sc_w_in"] = reduce_scatter("sc_w_in", _mm_tn_col("sc_in_dw", h0, dz, w_in.shape[3]))
    dh0 = _mm_nt_col("sc_in_dx", dz, w_in, 0)
    dx0, _, dg_mix0 = _rmsnorm_bwd("norm_mix_bwd0", x0, mix_norm_g[0:1], dhs=[(dh0, 1)], dres=dx1)

    dconv_sc = dwb_sc[0:3].reshape(1, 3, D)
    dbias_sc = dwb_sc[3:4]
    dconv_ffn = jnp.stack([dwb_ffn0[0:3], dwb_ffn1[0:3]])
    dbias_ffn = jnp.concatenate([dwb_ffn0[3:4], dwb_ffn1[3:4]], axis=0)
    small_parts = [jnp.concatenate([dg_mix0, dg_mix1], axis=0), jnp.concatenate([dg_ffn0, dg_ffn1], axis=0),
                   dg_final.reshape(D), dconv_sc, dbias_sc, dconv_ffn, dbias_ffn, loss_part[0, 0:1]]
    small_shapes = [a.shape for a in small_parts]
    summed = _unpack(_small_exchange("allreduce_small", _pack(small_parts), reduce=True), small_shapes)
    g_mix, g_ffn, g_final, g_scw_full, g_scb, g_fcw_full, g_fcb, loss = summed
    loss = loss.reshape(())
    cw = sc_conv_w.shape[2]
    g_scw = lax.dynamic_slice_in_dim(g_scw_full, chip * cw, cw, axis=2)
    fw = ffn_conv_w.shape[2]
    g_fcw = lax.dynamic_slice_in_dim(g_fcw_full, chip * fw, fw, axis=2)

    names = ["mix_norm_g", "ffn_norm_g", "final_norm_g", "sc_w_in", "sc_conv_w", "sc_conv_b", "sc_w_out",
             "attn_w_qkv", "attn_w_out", "ffn_w_up", "ffn_conv_w", "ffn_conv_b", "ffn_w_down"]
    ws = dict(zip(names, [mix_norm_g, ffn_norm_g, final_norm_g, sc_w_in, sc_conv_w, sc_conv_b, sc_w_out,
                          attn_w_qkv, attn_w_out, ffn_w_up, ffn_conv_w, ffn_conv_b, ffn_w_down]))
    ms = dict(zip(names, [m_mix_norm_g, m_ffn_norm_g, m_final_norm_g, m_sc_w_in, m_sc_conv_w, m_sc_conv_b, m_sc_w_out,
                          m_attn_w_qkv, m_attn_w_out, m_ffn_w_up, m_ffn_conv_w, m_ffn_conv_b, m_ffn_w_down]))
    vs = dict(zip(names, [v_mix_norm_g, v_ffn_norm_g, v_final_norm_g, v_sc_w_in, v_sc_conv_w, v_sc_conv_b, v_sc_w_out,
                          v_attn_w_qkv, v_attn_w_out, v_ffn_w_up, v_ffn_conv_w, v_ffn_conv_b, v_ffn_w_down]))
    gs = {"mix_norm_g": g_mix, "ffn_norm_g": g_ffn, "final_norm_g": g_final, "sc_conv_w": g_scw,
          "sc_conv_b": g_scb, "ffn_conv_w": g_fcw, "ffn_conv_b": g_fcb}
    gs.update(big_grads)

    deltas, new_m, new_v = {}, {}, {}
    small_names = [n for n in names if n not in big_names]
    packed = [_pack([d[n] for n in small_names]) for d in (ws, gs, ms, vs)]
    outs = _adamw("adamw_small", *packed)
    shapes = [ws[n].shape for n in small_names]
    for res, o in zip((deltas, new_m, new_v), outs):
        res.update(dict(zip(small_names, _unpack(o, shapes))))
    for n in big_names:
        shp = ws[n].shape
        two_d = (shp[0] * shp[1], shp[2])
        outs = _adamw("adamw_" + n, *[d[n].reshape(two_d) for d in (ws, gs, ms, vs)])
        for res, o in zip((deltas, new_m, new_v), outs):
            res[n] = o.reshape(shp)

    return (loss, dx0.reshape(x.shape), *[gs[n] for n in names], *[deltas[n] for n in names],
            *[new_m[n] for n in names], *[new_v[n] for n in names])
```

```python
import functools
import math

import jax
import jax.numpy as jnp
from jax import lax
from jax.experimental import pallas as pl
from jax.experimental.pallas import tpu as pltpu
from jax.experimental.pallas import tpu_sc as plsc

F32 = jnp.float32
BF = jnp.bfloat16
MESH = pl.DeviceIdType.MESH

HEAD_DIM = 128
ATTN_HALF = 64
ATTN_BLOCK = 128
DILATIONS = (1, 4, 16)
STAT_LANES = 128
HALO = 16
NORM_EPS = 1e-5
ALIBI_MAX = 8.0
NEG_INF = -1e30
N_CHIPS = 4
VMEM_LIMIT = 56 * 1024 * 1024

ADAM_LR = 0.001
ADAM_B1 = 0.9
ADAM_B2 = 0.999
ADAM_EPS = 1e-08
ADAM_WD = 0.01
ADAM_STEP = 10


def _pick(n, cands):
    for c in cands:
        if n % c == 0:
            return c
    raise ValueError(f"no tile for {n} in {cands}")


def _row_tile(rows, cols, max_elems=1 << 19):
    for c in (512, 256, 128, 64, 32, 16):
        if rows % c == 0 and c * cols <= max_elems:
            return c
    raise ValueError(f"no row tile for {rows}x{cols}")


def _params(*sem):
    return pltpu.CompilerParams(dimension_semantics=sem, vmem_limit_bytes=VMEM_LIMIT)


def _matmul(name, a, b, out_shape, grid, a_spec, b_spec, o_spec, contract, acc_shape,
            res=None, res_spec=None, prev=None):
    nk = grid[2]

    def body(*refs):
        refs = list(refs)
        if prev is not None:
            refs.pop(0)
        a_ref, b_ref = refs[0], refs[1]
        res_ref = refs[2] if res is not None else None
        o_ref = refs[3] if res is not None else refs[2]
        acc_ref = refs[-1]
        bv = b_ref[...]
        if bv.ndim == 3:
            bv = bv.reshape(bv.shape[0] * bv.shape[1], bv.shape[2])
        part = lax.dot_general(a_ref[...], bv, contract, preferred_element_type=F32)

        def finish(total):
            if res_ref is not None:
                total = total + res_ref[...]
            o_ref[...] = total.astype(o_ref.dtype)

        if nk == 1:
            finish(part)
        else:
            k = pl.program_id(2)

            @pl.when(k == 0)
            def _():
                acc_ref[...] = part

            @pl.when(jnp.logical_and(k > 0, k < nk - 1))
            def _():
                acc_ref[...] += part

            @pl.when(k == nk - 1)
            def _():
                finish(acc_ref[...] + part)

    operands, in_specs, aliases = [], [], {}
    if prev is not None:
        operands.append(prev)
        in_specs.append(pl.BlockSpec(memory_space=pl.ANY))
        aliases = {0: 0}
    operands += [a, b]
    in_specs += [a_spec, b_spec]
    if res is not None:
        operands.append(res)
        in_specs.append(res_spec)
    return pl.pallas_call(
        body, name=name, out_shape=out_shape, grid=grid, in_specs=in_specs, out_specs=o_spec,
        scratch_shapes=[pltpu.VMEM(acc_shape if nk > 1 else (8, 128), F32)],
        input_output_aliases=aliases,
        compiler_params=_params("parallel", "parallel", "arbitrary"),
    )(*operands)


NN = (((1,), (0,)), ((), ()))
NT = (((1,), (1,)), ((), ()))
TN = (((0,), (0,)), ((), ()))

_COL_TILES = (1536, 1408, 1024, 768, 512, 384, 256, 128)


def _mm_nn_col(name, a, w, layer, col_off=0, ncols=None, out_dtype=BF):
    M, K = a.shape
    _, _, R, C = w.shape
    assert R == K
    ncols = N_CHIPS * C if ncols is None else ncols
    tn = _pick(math.gcd(C, math.gcd(ncols, col_off) if col_off else ncols), _COL_TILES)
    tm = _pick(M, (1024, 512, 256))
    nb, off = C // tn, col_off // tn
    return _matmul(
        name, a, w, jax.ShapeDtypeStruct((M, ncols), out_dtype), (M // tm, ncols // tn, 1),
        pl.BlockSpec((tm, K), lambda i, j, k: (i, 0)),
        pl.BlockSpec((None, None, K, tn), lambda i, j, k: ((j + off) // nb, layer, 0, (j + off) % nb)),
        pl.BlockSpec((tm, tn), lambda i, j, k: (i, j)), NN, (tm, tn))


def _mm_nn_row(name, a, w, layer, res):
    M, K = a.shape
    _, _, R, C = w.shape
    assert N_CHIPS * R == K
    chips_per_step = N_CHIPS if K <= 2048 else 2
    tk = chips_per_step * R
    tm = _pick(M, (1024, 512, 256))
    tn = _pick(C, (1024, 512, 256))
    return _matmul(
        name, a, w, jax.ShapeDtypeStruct((M, C), F32), (M // tm, C // tn, K // tk),
        pl.BlockSpec((tm, tk), lambda i, j, k: (i, k)),
        pl.BlockSpec((chips_per_step, None, R, tn), lambda i, j, k: (k, layer, 0, j)),
        pl.BlockSpec((tm, tn), lambda i, j, k: (i, j)), NN, (tm, tn),
        res=res, res_spec=pl.BlockSpec((tm, tn), lambda i, j, k: (i, j)))


def _mm_nt_col(name, dy, w, layer, col_off=0, out_dtype=F32):
    M, n = dy.shape
    _, _, R, C = w.shape
    tk = _pick(math.gcd(C, math.gcd(n, col_off) if col_off else n), (2816,) + _COL_TILES)
    tm = _pick(M, (1024, 512, 256))
    tn = _pick(R, (1024, 512, 256))
    nb, off = C // tk, col_off // tk
    return _matmul(
        name, dy, w, jax.ShapeDtypeStruct((M, R), out_dtype), (M // tm, R // tn, n // tk),
        pl.BlockSpec((tm, tk), lambda i, j, k: (i, k)),
        pl.BlockSpec((None, None, tn, tk), lambda i, j, k: ((k + off) // nb, layer, j, (k + off) % nb)),
        pl.BlockSpec((tm, tn), lambda i, j, k: (i, j)), NT, (tm, tn))


def _mm_nt_row(name, dy, w, layer, out_dtype=BF):
    M, C2 = dy.shape
    _, _, R, C = w.shape
    assert C2 == C
    tn = _pick(R, (1408, 1024, 512, 256, 128))
    tm = _pick(M, (1024, 512, 256))
    rb = R // tn
    return _matmul(
        name, dy, w, jax.ShapeDtypeStruct((M, N_CHIPS * R), out_dtype), (M // tm, N_CHIPS * R // tn, 1),
        pl.BlockSpec((tm, C), lambda i, j, k: (i, 0)),
        pl.BlockSpec((None, None, tn, C), lambda i, j, k: (j // rb, layer, j % rb, 0)),
        pl.BlockSpec((tm, tn), lambda i, j, k: (i, j)), NT, (tm, tn))


_TN_DEPTH = (2048, 1024, 512, 256)


def _half_index(rows, layer, tkx):
    if layer is None:
        hb = rows // 2 // tkx
        return rows // 2, lambda i: (i // hb, i % hb)
    return rows, lambda i: (layer, i)


def _mm_tn_col(name, xa, dy, C, col_off=0, prev=None, layer=None):
    M, K = xa.shape
    _, n = dy.shape
    tn = _pick(math.gcd(C, math.gcd(n, col_off) if col_off else n), _COL_TILES)
    tkx = _pick(K // 2 if layer is None else K, (1024, 512, 256, 128))
    tmr = _pick(M, _TN_DEPTH)
    rh, split = _half_index(K, layer, tkx)
    nb, off = C // tn, col_off // tn
    return _matmul(
        name, xa, dy, jax.ShapeDtypeStruct((N_CHIPS, 2, rh, C), BF), (K // tkx, n // tn, M // tmr),
        pl.BlockSpec((tmr, tkx), lambda i, j, k: (k, i)),
        pl.BlockSpec((tmr, tn), lambda i, j, k: (k, j)),
        pl.BlockSpec((None, None, tkx, tn), lambda i, j, k: ((j + off) // nb, *split(i), (j + off) % nb)),
        TN, (tkx, tn), prev=prev)


def _mm_tn_row(name, xa, dy, prev=None, layer=None):
    M, K = xa.shape
    _, C = dy.shape
    R = K // N_CHIPS
    tkx = _pick(R // 2 if layer is None else R, (1408, 1024, 512, 256, 128))
    tmr = _pick(M, _TN_DEPTH)
    tn = _pick(C, (2048, 1024, 512, 256) if tkx <= 512 else (1024, 512, 256))
    rh, split = _half_index(R, layer, tkx)
    rb = R // tkx
    return _matmul(
        name, xa, dy, jax.ShapeDtypeStruct((N_CHIPS, 2, rh, C), BF), (K // tkx, C // tn, M // tmr),
        pl.BlockSpec((tmr, tkx), lambda i, j, k: (k, i)),
        pl.BlockSpec((tmr, tn), lambda i, j, k: (k, j)),
        pl.BlockSpec((None, None, tkx, tn), lambda i, j, k: (i // rb, *split(i % rb), j)),
        TN, (tkx, tn), prev=prev)


NORM_ROWS = 256
LANES = 128


def _chunk_scratch(tm, width):
    return pltpu.VMEM((width // LANES, tm, LANES), F32)


def _store_chunks(scr, value):
    for c in range(scr.shape[0]):
        scr[c] = value[:, c * LANES:(c + 1) * LANES]


def _load_chunks(scr):
    return jnp.concatenate([scr[c] for c in range(scr.shape[0])], axis=1)


def _to_residue_major(scr, o_ref, d, dtype):
    tm = scr.shape[1]
    for c in range(scr.shape[0]):
        for res in range(d):
            o_ref[res, :, c * LANES:(c + 1) * LANES] = scr[c, pl.ds(res, tm // d, stride=d), :].astype(dtype)


def _to_natural(scr, ref, d):
    tm = scr.shape[1]
    for c in range(scr.shape[0]):
        for res in range(d):
            scr[c, pl.ds(res, tm // d, stride=d), :] = ref[res, :, c * LANES:(c + 1) * LANES].astype(F32)


def _rmsnorm_fwd(name, x, g, dilated=False):
    S, D = x.shape
    tm = NORM_ROWS
    dils = DILATIONS[1:] if dilated else ()

    def body(x_ref, g_ref, h_ref, *rest):
        xv = x_ref[...]
        r = lax.rsqrt(jnp.mean(xv * xv, axis=1, keepdims=True) + NORM_EPS)
        h = xv * r * g_ref[...]
        h_ref[...] = h.astype(BF)
        if dils:
            scr = rest[-1]
            _store_chunks(scr, h)
            for o_ref, d in zip(rest[:-1], dils):
                _to_residue_major(scr, o_ref, d, BF)

    out_shape = [jax.ShapeDtypeStruct((S, D), BF)]
    out_specs = [pl.BlockSpec((tm, D), lambda i: (i, 0))]
    for d in dils:
        out_shape.append(jax.ShapeDtypeStruct((d, S // d, D), BF))
        out_specs.append(pl.BlockSpec((d, tm // d, D), lambda i: (0, i, 0)))
    outs = pl.pallas_call(
        body, name=name, out_shape=out_shape, grid=(S // tm,),
        in_specs=[pl.BlockSpec((tm, D), lambda i: (i, 0)), pl.BlockSpec((1, D), lambda i: (0, 0))],
        out_specs=out_specs,
        scratch_shapes=[_chunk_scratch(tm, D)] if dils else [],
        compiler_params=_params("parallel"),
    )(x, g)
    return [outs[0]] + [o.reshape(S, D) for o in outs[1:]]


def _rmsnorm_bwd(name, x, g, dhs=(), dres=None, target=None):
    S, D = x.shape
    tm = NORM_ROWS
    n_dh = len(dhs)

    def body(*refs):
        refs = list(refs)
        x_ref, g_ref = refs[0], refs[1]
        dh_refs = refs[2:2 + n_dh]
        pos = 2 + n_dh
        dres_ref = tgt_ref = None
        if dres is not None:
            dres_ref = refs[pos]
            pos += 1
        if target is not None:
            tgt_ref = refs[pos]
            pos += 1
        dx_ref, dxb_ref, dg_ref = refs[pos:pos + 3]
        pos += 3
        loss_ref = None
        if target is not None:
            loss_ref = refs[pos]
            pos += 1
        scr = refs[pos] if any(d > 1 for _, d in dhs) else None
        i = pl.program_id(0)

        xv = x_ref[...]
        gv = g_ref[...]
        r = lax.rsqrt(jnp.mean(xv * xv, axis=1, keepdims=True) + NORM_EPS)
        xhat = xv * r
        if target is not None:
            err = xhat * gv - tgt_ref[...]
            dh = err * (1.0 / D)
            part = jnp.sum(jnp.sum(err * err, axis=1, keepdims=True), axis=0, keepdims=True) * (0.5 / D)
        else:
            dh = None
            for ref, d in zip(dh_refs, [d for _, d in dhs]):
                if d == 1:
                    v = ref[...]
                else:
                    _to_natural(scr, ref, d)
                    v = _load_chunks(scr)
                dh = v if dh is None else dh + v
        dxhat = dh * gv
        dx = r * (dxhat - xhat * jnp.mean(dxhat * xhat, axis=1, keepdims=True))
        if dres_ref is not None:
            dx = dx + dres_ref[...]
        dx_ref[...] = dx
        dxb_ref[...] = dx.astype(BF)
        dg = jnp.sum(dh * xhat, axis=0, keepdims=True)

        @pl.when(i == 0)
        def _():
            dg_ref[...] = dg
            if loss_ref is not None:
                loss_ref[...] = jnp.broadcast_to(part, loss_ref.shape)

        @pl.when(i > 0)
        def _():
            dg_ref[...] += dg
            if loss_ref is not None:
                loss_ref[...] += jnp.broadcast_to(part, loss_ref.shape)

    row = pl.BlockSpec((tm, D), lambda i: (i, 0))
    operands = [x, g]
    in_specs = [row, pl.BlockSpec((1, D), lambda i: (0, 0))]
    for arr, d in dhs:
        if d == 1:
            operands.append(arr)
            in_specs.append(row)
        else:
            operands.append(arr.reshape(d, S // d, D))
            in_specs.append(pl.BlockSpec((d, tm // d, D), lambda i: (0, i, 0)))
    if dres is not None:
        operands.append(dres)
        in_specs.append(row)
    if target is not None:
        operands.append(target)
        in_specs.append(row)
    out_shape = [jax.ShapeDtypeStruct((S, D), F32), jax.ShapeDtypeStruct((S, D), BF),
                 jax.ShapeDtypeStruct((1, D), F32)]
    out_specs = [row, row, pl.BlockSpec((1, D), lambda i: (0, 0))]
    if target is not None:
        out_shape.append(jax.ShapeDtypeStruct((1, STAT_LANES), F32))
        out_specs.append(pl.BlockSpec((1, STAT_LANES), lambda i: (0, 0)))
    scratch = [_chunk_scratch(tm, D)] if any(d > 1 for _, d in dhs) else []
    return pl.pallas_call(
        body, name=name, out_shape=out_shape, grid=(S // tm,), in_specs=in_specs, out_specs=out_specs,
        scratch_shapes=scratch, compiler_params=_params("arbitrary"),
    )(*operands)


CONV_ROWS = 128
CONV_COLS = 512


def _halo_specs(S, tm, width):
    nh = S // HALO
    per = tm // HALO
    cur = pl.BlockSpec((tm, width), lambda i: (i, 0))
    prev = pl.BlockSpec((HALO, width), lambda i: (jnp.maximum(i * per - 1, 0), 0))
    nxt = pl.BlockSpec((HALO, width), lambda i: (jnp.minimum((i + 1) * per, nh - 1), 0))
    return [cur, prev, nxt]


def _ext(refs, cs, inr):
    cur, prev, nxt = refs
    v = jnp.concatenate([prev[:, cs], cur[:, cs], nxt[:, cs]], axis=0).astype(F32)
    return jnp.where(inr, v, 0.0)


def _shift_prev(v):
    return pltpu.roll(v, 1, 0)


def _shift_next(v):
    return pltpu.roll(v, v.shape[0] - 1, 0)


def _shifts(v):
    return _shift_prev(v), _shift_next(v)


def _conv3(v, w, cs, b=None, shifted=None):
    vp, vn = _shifts(v) if shifted is None else shifted
    out = w[0:1, cs] * vp + w[1:2, cs] * v + w[2:3, cs] * vn
    return out if b is None else out + b[:, cs]


def _in_range(i, tm, tc, S):
    row = lax.broadcasted_iota(jnp.int32, (tm + 2 * HALO, tc), 0) + (i * tm - HALO)
    return jnp.logical_and(row >= 0, row < S)


def _core(v, tm):
    return v[HALO:HALO + tm, :]


def _acc_rows(ref, i, rows):
    for r, cs, val in rows:
        ref[r:r + 1, cs] += val


def _zero_first(ref, i):
    @pl.when(i == 0)
    def _():
        ref[...] = jnp.zeros(ref.shape, ref.dtype)


def _sc_fwd(name, z, w, b):
    S, D3 = z.shape
    D = D3 // 3
    tm, tc = CONV_ROWS, _pick(D, (CONV_COLS, 256, 128))

    def body(zc, zp, zn, w_ref, b_ref, y_ref):
        i = pl.program_id(0)
        inr = _in_range(i, tm, tc, S)
        zr = (zc, zp, zn)
        for c in range(D // tc):
            cs = slice(c * tc, (c + 1) * tc)
            u = _ext(zr, cs, inr)
            gc = _ext(zr, slice(2 * D + c * tc, 2 * D + (c + 1) * tc), inr)
            conv = _conv3(gc * u, w_ref, cs, b_ref)
            gb = zc[:, D + c * tc:D + (c + 1) * tc].astype(F32)
            y_ref[:, cs] = (gb * _core(conv, tm)).astype(BF)

    return pl.pallas_call(
        body, name=name, out_shape=jax.ShapeDtypeStruct((S, D), BF), grid=(S // tm,),
        in_specs=_halo_specs(S, tm, D3) + [pl.BlockSpec((3, D), lambda i: (0, 0)),
                                           pl.BlockSpec((1, D), lambda i: (0, 0))],
        out_specs=pl.BlockSpec((tm, D), lambda i: (i, 0)),
        compiler_params=_params("parallel"),
    )(z, z, z, w, b)


def _sc_bwd(name, z, dy, w, b):
    S, D3 = z.shape
    D = D3 // 3
    tm, tc = CONV_ROWS, _pick(D, (CONV_COLS, 256, 128))

    def body(zc, zp, zn, dc_, dp_, dn_, w_ref, b_ref, dz_ref, dwb_ref):
        i = pl.program_id(0)
        inr = _in_range(i, tm, tc, S)
        _zero_first(dwb_ref, i)
        zr, dr = (zc, zp, zn), (dc_, dp_, dn_)
        for c in range(D // tc):
            cs = slice(c * tc, (c + 1) * tc)
            u = _ext(zr, cs, inr)
            gb = _ext(zr, slice(D + c * tc, D + (c + 1) * tc), inr)
            gc = _ext(zr, slice(2 * D + c * tc, 2 * D + (c + 1) * tc), inr)
            dyv = _ext(dr, cs, inr)
            p = gc * u
            p_prev, p_next = _shifts(p)
            conv = _conv3(p, w_ref, cs, b_ref, shifted=(p_prev, p_next))
            dconv = dyv * gb
            dp = w_ref[0:1, cs] * _shift_next(dconv) + w_ref[1:2, cs] * dconv + w_ref[2:3, cs] * _shift_prev(dconv)
            dz_ref[:, cs] = _core(dp * gc, tm).astype(BF)
            dz_ref[:, D + c * tc:D + (c + 1) * tc] = _core(dyv * conv, tm).astype(BF)
            dz_ref[:, 2 * D + c * tc:2 * D + (c + 1) * tc] = _core(dp * u, tm).astype(BF)
            dcc = _core(dconv, tm)
            _acc_rows(dwb_ref, i, [
                (0, cs, jnp.sum(dcc * _core(p_prev, tm), axis=0, keepdims=True)),
                (1, cs, jnp.sum(dcc * _core(p, tm), axis=0, keepdims=True)),
                (2, cs, jnp.sum(dcc * _core(p_next, tm), axis=0, keepdims=True)),
                (3, cs, jnp.sum(dcc, axis=0, keepdims=True))])

    return pl.pallas_call(
        body, name=name,
        out_shape=[jax.ShapeDtypeStruct((S, D3), BF), jax.ShapeDtypeStruct((4, D), F32)], grid=(S // tm,),
        in_specs=_halo_specs(S, tm, D3) + _halo_specs(S, tm, D) + [
            pl.BlockSpec((3, D), lambda i: (0, 0)), pl.BlockSpec((1, D), lambda i: (0, 0))],
        out_specs=[pl.BlockSpec((tm, D3), lambda i: (i, 0)), pl.BlockSpec((4, D), lambda i: (0, 0))],
        compiler_params=_params("arbitrary"),
    )(z, z, z, dy, dy, dy, w, b)


def _sigmoid(v):
    return 1.0 / (1.0 + jnp.exp(-v))


def _ffn_fwd(name, u, w, b):
    S, F2 = u.shape
    Fh = F2 // 2
    tm, tc = CONV_ROWS, _pick(Fh, (CONV_COLS, 256, 128))

    def body(uc, up, un, w_ref, b_ref, f_ref, v_ref):
        i = pl.program_id(0)
        inr = _in_range(i, tm, tc, S)
        ur = (uc, up, un)
        for c in range(Fh // tc):
            ca = slice(c * tc, (c + 1) * tc)
            cb = slice(Fh + c * tc, Fh + (c + 1) * tc)
            va = _core(_conv3(_ext(ur, ca, inr), w_ref, ca, b_ref), tm)
            vb = _core(_conv3(_ext(ur, cb, inr), w_ref, cb, b_ref), tm)
            v_ref[:, ca] = va.astype(BF)
            v_ref[:, cb] = vb.astype(BF)
            f_ref[:, ca] = (va * _sigmoid(va) * vb).astype(BF)

    return pl.pallas_call(
        body, name=name,
        out_shape=[jax.ShapeDtypeStruct((S, Fh), BF), jax.ShapeDtypeStruct((S, F2), BF)], grid=(S // tm,),
        in_specs=_halo_specs(S, tm, F2) + [pl.BlockSpec((3, F2), lambda i: (0, 0)),
                                           pl.BlockSpec((1, F2), lambda i: (0, 0))],
        out_specs=[pl.BlockSpec((tm, Fh), lambda i: (i, 0)), pl.BlockSpec((tm, F2), lambda i: (i, 0))],
        compiler_params=_params("parallel"),
    )(u, u, u, w, b)


def _ffn_bwd(name, u, v, df, w):
    S, F2 = u.shape
    Fh = F2 // 2
    tm, tc = CONV_ROWS, _pick(Fh, (CONV_COLS, 256, 128))

    def body(u_ref, vc, vp, vn, dc_, dp_, dn_, w_ref, du_ref, dwb_ref):
        i = pl.program_id(0)
        inr = _in_range(i, tm, tc, S)
        _zero_first(dwb_ref, i)
        vr, dr = (vc, vp, vn), (dc_, dp_, dn_)
        for c in range(Fh // tc):
            ca = slice(c * tc, (c + 1) * tc)
            cb = slice(Fh + c * tc, Fh + (c + 1) * tc)
            va, vb = _ext(vr, ca, inr), _ext(vr, cb, inr)
            dfv = _ext(dr, ca, inr)
            sg = _sigmoid(va)
            dva = dfv * vb * (sg * (1.0 + va * (1.0 - sg)))
            dvb = dfv * (va * sg)
            rows = []
            for cs, dv in ((ca, dva), (cb, dvb)):
                dv_prev, dv_next = _shifts(dv)
                dcore = w_ref[0:1, cs] * dv_next + w_ref[1:2, cs] * dv + w_ref[2:3, cs] * dv_prev
                du_ref[:, cs] = _core(dcore, tm).astype(BF)
                uu = u_ref[:, cs].astype(F32)
                dvc = _core(dv, tm)
                rows += [
                    (0, cs, jnp.sum(_core(dv_next, tm) * uu, axis=0, keepdims=True)),
                    (1, cs, jnp.sum(dvc * uu, axis=0, keepdims=True)),
                    (2, cs, jnp.sum(_core(dv_prev, tm) * uu, axis=0, keepdims=True)),
                    (3, cs, jnp.sum(dvc, axis=0, keepdims=True))]
            _acc_rows(dwb_ref, i, rows)

    return pl.pallas_call(
        body, name=name,
        out_shape=[jax.ShapeDtypeStruct((S, F2), BF), jax.ShapeDtypeStruct((4, F2), F32)], grid=(S // tm,),
        in_specs=[pl.BlockSpec((tm, F2), lambda i: (i, 0))] + _halo_specs(S, tm, F2) + _halo_specs(S, tm, Fh) + [
            pl.BlockSpec((3, F2), lambda i: (0, 0))],
        out_specs=[pl.BlockSpec((tm, F2), lambda i: (i, 0)), pl.BlockSpec((4, F2), lambda i: (0, 0))],
        compiler_params=_params("arbitrary"),
    )(u, v, v, v, df, df, df, w)


def _alibi_slopes(H):
    return [2.0 ** (-ALIBI_MAX * (h + 1) / H) for h in range(H)]


def _window_specs(S, width, col):
    n64 = S // ATTN_HALF
    cur = pl.BlockSpec((ATTN_BLOCK, width), lambda b: (b, col))
    prev = pl.BlockSpec((ATTN_HALF, width), lambda b: (jnp.maximum(2 * b - 1, 0), col))
    nxt = pl.BlockSpec((ATTN_HALF, width), lambda b: (jnp.minimum(2 * b + 2, n64 - 1), col))
    return [cur, prev, nxt]


def _fill_window(buf, cur, prev, nxt):
    buf[0:ATTN_HALF] = prev[...]
    buf[ATTN_HALF:ATTN_HALF + ATTN_BLOCK] = cur[...]
    buf[ATTN_HALF + ATTN_BLOCK:2 * ATTN_BLOCK] = nxt[...]


def _band(b, L, queries_in_rows_of_block):
    QB, W = ATTN_BLOCK, 2 * ATTN_BLOCK
    a_loc = (b * QB) % L
    if queries_in_rows_of_block:
        row = lax.broadcasted_iota(jnp.int32, (QB, W), 0)
        col = lax.broadcasted_iota(jnp.int32, (QB, W), 1)
        rel = col - ATTN_HALF - row
        other = a_loc - ATTN_HALF + col
    else:
        row = lax.broadcasted_iota(jnp.int32, (W, QB), 0)
        col = lax.broadcasted_iota(jnp.int32, (W, QB), 1)
        rel = col + ATTN_HALF - row
        other = a_loc - ATTN_HALF + row
    dist = jnp.abs(rel)
    valid = jnp.logical_and(dist <= ATTN_HALF, jnp.logical_and(other >= 0, other < L))
    return dist.astype(F32), valid


def _lane_col(stats, h):
    lane = lax.broadcasted_iota(jnp.int32, stats.shape, 1)
    return jnp.sum(jnp.where(lane == h, stats, 0.0), axis=1, keepdims=True)


def _attn_fwd(name, qkv, d):
    S, D3 = qkv.shape
    D = D3 // 3
    H = D // HEAD_DIM
    L = S // d
    scale = HEAD_DIM ** -0.5
    slopes = _alibi_slopes(H)

    def body(q_ref, kc, kp, kn, vc, vp, vn, o_ref, lse_ref, kbuf, vbuf):
        b = pl.program_id(0)
        _fill_window(kbuf, kc, kp, kn)
        _fill_window(vbuf, vc, vp, vn)
        dist, valid = _band(b, L, True)
        dist = dist * float(d)
        lane = lax.broadcasted_iota(jnp.int32, (ATTN_BLOCK, STAT_LANES), 1)
        lse = jnp.zeros((ATTN_BLOCK, STAT_LANES), F32)
        for h in range(H):
            cs = slice(h * HEAD_DIM, (h + 1) * HEAD_DIM)
            s = lax.dot_general(q_ref[:, cs], kbuf[:, cs], NT, preferred_element_type=F32) * scale
            s = jnp.where(valid, s - slopes[h] * dist, NEG_INF)
            m = jnp.max(s, axis=1, keepdims=True)
            p = jnp.exp(s - m)
            den = jnp.sum(p, axis=1, keepdims=True)
            o = jnp.dot(p.astype(BF), vbuf[:, cs], preferred_element_type=F32)
            o_ref[:, cs] = o / den
            lse = jnp.where(lane == h, m + jnp.log(den), lse)
        lse_ref[...] = lse

    return pl.pallas_call(
        body, name=name,
        out_shape=[jax.ShapeDtypeStruct((S, D), F32), jax.ShapeDtypeStruct((S, STAT_LANES), F32)],
        grid=(S // ATTN_BLOCK,),
        in_specs=[pl.BlockSpec((ATTN_BLOCK, D), lambda b: (b, 0))] + _window_specs(S, D, 1) + _window_specs(S, D, 2),
        out_specs=[pl.BlockSpec((ATTN_BLOCK, D), lambda b: (b, 0)),
                   pl.BlockSpec((ATTN_BLOCK, STAT_LANES), lambda b: (b, 0))],
        scratch_shapes=[pltpu.VMEM((2 * ATTN_BLOCK, D), BF), pltpu.VMEM((2 * ATTN_BLOCK, D), BF)],
        compiler_params=_params("parallel"),
    )(qkv, qkv, qkv, qkv, qkv, qkv, qkv)


def _dil_specs(S, tm, width):
    specs = [pl.BlockSpec((tm, width), lambda i: (i, 0))]
    for d in DILATIONS[1:]:
        specs.append(pl.BlockSpec((d, tm // d, width), lambda i: (0, i, 0)))
    return specs


def _attn_combine(name, outs, lses):
    S, D = outs[0].shape
    H = D // HEAD_DIM
    tm = NORM_ROWS

    def body(o1, o4, o16, l1, l4, l16, o_ref, ob_ref, lse_ref, oscr, lscr):
        ls = [l1[...]]
        for ref, d in zip((l4, l16), DILATIONS[1:]):
            _to_natural(lscr, ref, d)
            ls.append(lscr[0])
        top = jnp.maximum(jnp.maximum(ls[0], ls[1]), ls[2])
        es = [jnp.exp(l - top) for l in ls]
        tot = es[0] + es[1] + es[2]
        lse_ref[...] = top + jnp.log(tot)
        ws = [e / tot for e in es]
        for gi, (ref, d) in enumerate(zip((o1, o4, o16), DILATIONS)):
            if d > 1:
                _to_natural(oscr, ref, d)
            for h in range(H):
                cs = slice(h * HEAD_DIM, (h + 1) * HEAD_DIM)
                term = _lane_col(ws[gi], h) * (ref[:, cs] if d == 1 else oscr[h])
                if gi == 0:
                    o_ref[:, cs] = term
                else:
                    o_ref[:, cs] += term
        ob_ref[...] = o_ref[...].astype(BF)

    outs3 = [outs[0]] + [o.reshape(d, S // d, D) for o, d in zip(outs[1:], DILATIONS[1:])]
    lses3 = [lses[0]] + [l.reshape(d, S // d, STAT_LANES) for l, d in zip(lses[1:], DILATIONS[1:])]
    row = pl.BlockSpec((tm, D), lambda i: (i, 0))
    return pl.pallas_call(
        body, name=name,
        out_shape=[jax.ShapeDtypeStruct((S, D), F32), jax.ShapeDtypeStruct((S, D), BF),
                   jax.ShapeDtypeStruct((S, STAT_LANES), F32)],
        grid=(S // tm,),
        in_specs=_dil_specs(S, tm, D) + _dil_specs(S, tm, STAT_LANES),
        out_specs=[row, row, pl.BlockSpec((tm, STAT_LANES), lambda i: (i, 0))],
        scratch_shapes=[_chunk_scratch(tm, D), _chunk_scratch(tm, STAT_LANES)],
        compiler_params=_params("parallel"),
    )(*outs3, *lses3)


def _attn_bwd_prep(name, do, o32, lse):
    S, D = do.shape
    H = D // HEAD_DIM
    tm = NORM_ROWS
    dils = DILATIONS[1:]

    def body(do_ref, o_ref, lse_ref, dl_ref, do4, do16, l4, l16, d4, d16, scr, sscr):
        lane = lax.broadcasted_iota(jnp.int32, (tm, STAT_LANES), 1)
        delta = jnp.zeros((tm, STAT_LANES), F32)
        for h in range(H):
            cs = slice(h * HEAD_DIM, (h + 1) * HEAD_DIM)
            dov = do_ref[:, cs].astype(F32)
            scr[h] = dov
            delta = jnp.where(lane == h, jnp.sum(dov * o_ref[:, cs], axis=1, keepdims=True), delta)
        dl_ref[...] = delta
        for ref, d in zip((do4, do16), dils):
            _to_residue_major(scr, ref, d, BF)
        for val, refs in ((lse_ref[...], (l4, l16)), (delta, (d4, d16))):
            sscr[0] = val
            for ref, d in zip(refs, dils):
                _to_residue_major(sscr, ref, d, F32)

    def perm_shapes(width, dt):
        return [jax.ShapeDtypeStruct((d, S // d, width), dt) for d in dils]

    def perm_specs(width):
        return [pl.BlockSpec((d, tm // d, width), lambda i: (0, i, 0)) for d in dils]

    row = lambda w: pl.BlockSpec((tm, w), lambda i: (i, 0))
    outs = pl.pallas_call(
        body, name=name,
        out_shape=[jax.ShapeDtypeStruct((S, STAT_LANES), F32)]
        + perm_shapes(D, BF) + perm_shapes(STAT_LANES, F32) + perm_shapes(STAT_LANES, F32),
        grid=(S // tm,),
        in_specs=[row(D), row(D), row(STAT_LANES)],
        out_specs=[row(STAT_LANES)] + perm_specs(D) + perm_specs(STAT_LANES) + perm_specs(STAT_LANES),
        scratch_shapes=[_chunk_scratch(tm, D), _chunk_scratch(tm, STAT_LANES)],
        compiler_params=_params("parallel"),
    )(do, o32, lse)
    dos = [do] + [a.reshape(S, D) for a in outs[1:3]]
    lss = [lse] + [a.reshape(S, STAT_LANES) for a in outs[3:5]]
    dls = [outs[0]] + [a.reshape(S, STAT_LANES) for a in outs[5:7]]
    return dos, lss, dls


def _attn_bwd(name, qkv, do, lse, delta, d):
    S, D3 = qkv.shape
    D = D3 // 3
    H = D // HEAD_DIM
    L = S // d
    scale = HEAD_DIM ** -0.5
    slopes = _alibi_slopes(H)
    QB = ATTN_BLOCK

    def body(qc, qp, qn, kc, kp, kn, vc, vp, vn, dc_, dp_, dn_, lc, lp, ln, ec, ep, en,
             out_ref, qbuf, kbuf, vbuf, dobuf, lbuf, ebuf):
        b = pl.program_id(0)
        for buf, trio in ((qbuf, (qc, qp, qn)), (kbuf, (kc, kp, kn)), (vbuf, (vc, vp, vn)),
                          (dobuf, (dc_, dp_, dn_)), (lbuf, (lc, lp, ln)), (ebuf, (ec, ep, en))):
            _fill_window(buf, *trio)
        dist_q, valid_q = _band(b, L, True)
        dist_k, valid_k = _band(b, L, False)
        dist_q = dist_q * float(d)
        dist_k = dist_k * float(d)
        lse_c, del_c = lc[...], ec[...]
        lse_w, del_w = lbuf[...], ebuf[...]
        for h in range(H):
            cs = slice(h * HEAD_DIM, (h + 1) * HEAD_DIM)
            q, do_h = qc[:, cs], dc_[:, cs]
            kw, vw = kbuf[:, cs], vbuf[:, cs]
            s = lax.dot_general(q, kw, NT, preferred_element_type=F32) * scale - slopes[h] * dist_q
            p = jnp.where(valid_q, jnp.exp(s - _lane_col(lse_c, h)), 0.0)
            dp = lax.dot_general(do_h, vw, NT, preferred_element_type=F32)
            ds = p * (dp - _lane_col(del_c, h))
            dq = jnp.dot(ds.astype(BF), kw, preferred_element_type=F32) * scale
            out_ref[:, cs] = dq.astype(BF)
            qw, dow = qbuf[:, cs], dobuf[:, cs]
            k, v = kc[:, cs], vc[:, cs]
            s2 = lax.dot_general(qw, k, NT, preferred_element_type=F32) * scale - slopes[h] * dist_k
            p2 = jnp.where(valid_k, jnp.exp(s2 - _lane_col(lse_w, h)), 0.0)
            dv = lax.dot_general(p2.astype(BF), dow, TN, preferred_element_type=F32)
            dp2 = lax.dot_general(dow, v, NT, preferred_element_type=F32)
            ds2 = p2 * (dp2 - _lane_col(del_w, h))
            dk = lax.dot_general(ds2.astype(BF), qw, TN, preferred_element_type=F32) * scale
            out_ref[:, D + h * HEAD_DIM:D + (h + 1) * HEAD_DIM] = dk.astype(BF)
            out_ref[:, 2 * D + h * HEAD_DIM:2 * D + (h + 1) * HEAD_DIM] = dv.astype(BF)

    W = 2 * QB
    return pl.pallas_call(
        body, name=name, out_shape=jax.ShapeDtypeStruct((S, D3), BF), grid=(S // QB,),
        in_specs=(_window_specs(S, D, 0) + _window_specs(S, D, 1) + _window_specs(S, D, 2)
                  + _window_specs(S, D, 0) + _window_specs(S, STAT_LANES, 0) + _window_specs(S, STAT_LANES, 0)),
        out_specs=pl.BlockSpec((QB, D3), lambda b: (b, 0)),
        scratch_shapes=[pltpu.VMEM((W, D), BF), pltpu.VMEM((W, D), BF), pltpu.VMEM((W, D), BF),
                        pltpu.VMEM((W, D), BF), pltpu.VMEM((W, STAT_LANES), F32), pltpu.VMEM((W, STAT_LANES), F32)],
        compiler_params=_params("parallel"),
    )(qkv, qkv, qkv, qkv, qkv, qkv, qkv, qkv, qkv, do, do, do, lse, lse, lse, delta, delta, delta)


def _cast_bf16(name, w, layers):
    _, R, C = w.shape
    tr = _row_tile(R, C)
    first = layers[0]

    def body(w_ref, o_ref):
        o_ref[...] = w_ref[...].astype(BF)

    return pl.pallas_call(
        body, name=name, out_shape=jax.ShapeDtypeStruct((len(layers), R, C), BF), grid=(len(layers), R // tr),
        in_specs=[pl.BlockSpec((None, tr, C), lambda l, i: (first + l, i, 0))],
        out_specs=pl.BlockSpec((None, tr, C), lambda l, i: (l, i, 0)),
        compiler_params=_params("parallel", "parallel"),
    )(w)


N_PEERS = 7


def _grad_sum(name, g, others):
    _, _, R, C = g.shape
    tr = _row_tile(R, C, 1 << 18)

    def body(g_ref, b_ref, o_ref):
        tot = g_ref[...].astype(F32)
        for s in range(N_PEERS):
            tot = tot + b_ref[s].astype(F32)
        o_ref[...] = tot

    def mine(i):
        return (2 * lax.axis_index("x") + lax.axis_index("y"), lax.axis_index("c"), i, 0)

    return pl.pallas_call(
        body, name=name, out_shape=jax.ShapeDtypeStruct((2, R, C), F32), grid=(R // tr,),
        in_specs=[pl.BlockSpec((None, None, tr, C), mine),
                  pl.BlockSpec((N_PEERS, tr, C), lambda i: (0, i, 0))],
        out_specs=pl.BlockSpec((None, tr, C), lambda i: (lax.axis_index("c"), i, 0)),
        compiler_params=_params("parallel"),
    )(g, others)


def _adamw(name, w, g, m, v):
    R, C = w.shape
    tr = _row_tile(R, C, 1 << 18) if R % 16 == 0 else R
    c1 = 1.0 - ADAM_B1 ** ADAM_STEP
    c2 = 1.0 - ADAM_B2 ** ADAM_STEP

    def body(w_ref, g_ref, m_ref, v_ref, d_ref, nm_ref, nv_ref):
        gv = g_ref[...]
        nm = ADAM_B1 * m_ref[...] + (1.0 - ADAM_B1) * gv
        nv = ADAM_B2 * v_ref[...] + (1.0 - ADAM_B2) * (gv * gv)
        nm_ref[...] = nm
        nv_ref[...] = nv
        d_ref[...] = -ADAM_LR * ((nm / c1) / (jnp.sqrt(nv / c2) + ADAM_EPS) + ADAM_WD * w_ref[...])

    spec = pl.BlockSpec((tr, C), lambda i: (i, 0))
    return pl.pallas_call(
        body, name=name, out_shape=[jax.ShapeDtypeStruct((R, C), F32)] * 3, grid=(R // tr,),
        in_specs=[spec] * 4, out_specs=[spec] * 3, compiler_params=_params("parallel"),
    )(w, g, m, v)


def _coords():
    return lax.axis_index("x"), lax.axis_index("y"), lax.axis_index("c")


def _flip(x, y, c, k):
    return (1 - x if k & 4 else x, 1 - y if k & 2 else y, 1 - c if k & 1 else c)


def _small_exchange(name, buf, reduce):
    rows = buf.shape[0]

    def body(x_ref, o_ref, land, send_sems, recv_sems):
        x, y, c = _coords()
        me = 4 * x + 2 * y + c

        def copy(k, sending):
            px, py, pc = _flip(x, y, c, k)
            slot = me if sending else 4 * px + 2 * py + pc
            return pltpu.make_async_remote_copy(
                src_ref=x_ref, dst_ref=land.at[slot], send_sem=send_sems.at[k - 1], recv_sem=recv_sems.at[k - 1],
                device_id=(px, py, pc), device_id_type=MESH)

        for k in range(1, 8):
            copy(k, True).start()
        land[me] = x_ref[...]
        for k in range(1, 8):
            copy(k, False).wait()
        if reduce:
            acc = land[0]
            for s in range(1, 8):
                acc = acc + land[s]
            o_ref[...] = acc
        else:
            o_ref[...] = land[...]

    out_shape = jax.ShapeDtypeStruct((rows, 128) if reduce else (8, rows, 128), F32)
    return pl.pallas_call(
        body, name=name, out_shape=out_shape,
        in_specs=[pl.BlockSpec(memory_space=pltpu.VMEM)], out_specs=pl.BlockSpec(memory_space=pltpu.VMEM),
        scratch_shapes=[pltpu.VMEM((8, rows, 128), F32), pltpu.SemaphoreType.DMA((7,)), pltpu.SemaphoreType.DMA((7,))],
        compiler_params=pltpu.CompilerParams(vmem_limit_bytes=VMEM_LIMIT),
    )(buf)


def _handshake(peers):
    barrier = pltpu.get_barrier_semaphore()
    for peer in peers:
        pl.semaphore_signal(barrier, inc=1, device_id=peer, device_id_type=MESH)
    pl.semaphore_wait(barrier, len(peers))


def _sequencer_mesh():
    return plsc.ScalarSubcoreMesh(axis_name="sequencer", num_cores=1)


def _allgather_weight(name, shard, collective_id):
    def body(in_ref, out_ref, send_sems, recv_sems, local_sem):
        x, y, c = _coords()
        chip = 2 * x + y
        sib = (x, y, 1 - c)
        chips = [_flip(x, y, c, k) for k in (4, 2, 6)]
        _handshake([sib] + chips)

        def slab(cx, cy, cc):
            return out_ref.at[2 * cx + cy, cc]

        def copy(k, src, dst, to):
            return pltpu.make_async_remote_copy(
                src_ref=src, dst_ref=dst, send_sem=send_sems.at[k], recv_sem=recv_sems.at[k],
                device_id=to, device_id_type=MESH)

        local = pltpu.make_async_copy(in_ref.at[c], out_ref.at[chip, c], local_sem)
        local.start()
        started = []
        for j, to in enumerate(chips):
            started.append(copy(1 + j, in_ref.at[c], slab(x, y, c), to))
        started.append(copy(0, in_ref.at[c], slab(x, y, c), sib))
        for cp in started:
            cp.start()
        for j, (px, py, pc) in enumerate(chips):
            held = slab(px, py, c)
            copy(1 + j, held, held, (px, py, pc)).wait_recv()
            cp = copy(4 + j, held, held, sib)
            cp.start()
            started.append(cp)
        got = slab(x, y, 1 - c)
        copy(0, got, got, sib).wait_recv()
        for j, (px, py, pc) in enumerate(chips):
            got = slab(px, py, 1 - c)
            copy(4 + j, got, got, sib).wait_recv()
        for cp in started:
            cp.wait_send()
        local.wait()

    return pl.kernel(
        body, out_type=jax.ShapeDtypeStruct((N_CHIPS,) + shard.shape, shard.dtype),
        mesh=_sequencer_mesh(), name=name,
        scratch_types=[pltpu.SemaphoreType.DMA((7,)), pltpu.SemaphoreType.DMA((7,)), pltpu.SemaphoreType.DMA],
        compiler_params=pltpu.CompilerParams(collective_id=collective_id),
    )(shard)


def _grad_exchange(name, g, collective_id):
    def body(in_ref, out_ref, send_sems, recv_sems):
        x, y, c = _coords()
        chip = 2 * x + y
        sib = (x, y, 1 - c)
        chips = [_flip(x, y, c, k) for k in (4, 2, 6)]
        _handshake([sib] + [(px, py, cc) for px, py, _ in chips for cc in (0, 1)])
        cps = []
        for j, (px, py, _) in enumerate(chips):
            for cc in (0, 1):
                cps.append(pltpu.make_async_remote_copy(
                    src_ref=in_ref.at[2 * px + py, cc], dst_ref=out_ref.at[1 + 2 * j + c],
                    send_sem=send_sems.at[1 + 2 * j + cc], recv_sem=recv_sems.at[1 + 2 * j + c],
                    device_id=(px, py, cc), device_id_type=MESH))
        cps.append(pltpu.make_async_remote_copy(
            src_ref=in_ref.at[chip, 1 - c], dst_ref=out_ref.at[0], send_sem=send_sems.at[0],
            recv_sem=recv_sems.at[0], device_id=sib, device_id_type=MESH))
        for cp in cps:
            cp.start()
        for cp in cps:
            cp.wait_send()
        for slot in range(N_PEERS):
            pltpu.make_async_remote_copy(
                src_ref=out_ref.at[slot], dst_ref=out_ref.at[slot], send_sem=send_sems.at[slot],
                recv_sem=recv_sems.at[slot], device_id=sib, device_id_type=MESH).wait_recv()

    return pl.kernel(
        body, out_type=jax.ShapeDtypeStruct((N_PEERS,) + g.shape[2:], g.dtype),
        mesh=_sequencer_mesh(), name=name,
        scratch_types=[pltpu.SemaphoreType.DMA((N_PEERS,)), pltpu.SemaphoreType.DMA((N_PEERS,))],
        compiler_params=pltpu.CompilerParams(collective_id=collective_id),
    )(g)


def _pair_fill(name, fulls):
    T = len(fulls)

    def body(*refs):
        outs = refs[T:2 * T]
        send_sems, recv_sems = refs[2 * T:]
        x, y, c = _coords()
        sib = (x, y, 1 - c)
        cps = []
        for t in range(T):
            send = pltpu.make_async_remote_copy(
                src_ref=outs[t].at[c], dst_ref=outs[t].at[c], send_sem=send_sems.at[t],
                recv_sem=recv_sems.at[t], device_id=sib, device_id_type=MESH)
            recv = pltpu.make_async_remote_copy(
                src_ref=outs[t].at[1 - c], dst_ref=outs[t].at[1 - c], send_sem=send_sems.at[t],
                recv_sem=recv_sems.at[t], device_id=sib, device_id_type=MESH)
            send.start()
            cps.append((send, recv))
        for send, recv in cps:
            send.wait_send()
            recv.wait_recv()

    anyspec = pl.BlockSpec(memory_space=pl.ANY)
    return pl.pallas_call(
        body, name=name,
        out_shape=[jax.ShapeDtypeStruct(f.shape, f.dtype) for f in fulls],
        in_specs=[anyspec] * T, out_specs=[anyspec] * T,
        input_output_aliases={t: t for t in range(T)},
        scratch_shapes=[pltpu.SemaphoreType.DMA((T,)), pltpu.SemaphoreType.DMA((T,))],
    )(*fulls)


def _pack(arrs):
    flat = jnp.concatenate([a.reshape(-1).astype(F32) for a in arrs])
    n = flat.shape[0]
    rows = -(-n // 1024) * 8
    return jnp.pad(flat, (0, rows * 128 - n)).reshape(rows, 128)


def _unpack(buf, shapes):
    flat = buf.reshape(-1)
    out, pos = [], 0
    for s in shapes:
        n = math.prod(s)
        out.append(flat[pos:pos + n].reshape(s))
        pos += n
    return out


def kernel(x, mix_norm_g, ffn_norm_g, final_norm_g, sc_w_in, sc_conv_w, sc_conv_b, sc_w_out, attn_w_qkv, attn_w_out, ffn_w_up, ffn_conv_w, ffn_conv_b, ffn_w_down, loss_target, m_mix_norm_g, m_ffn_norm_g, m_final_norm_g, m_sc_w_in, m_sc_conv_w, m_sc_conv_b, m_sc_w_out, m_attn_w_qkv, m_attn_w_out, m_ffn_w_up, m_ffn_conv_w, m_ffn_conv_b, m_ffn_w_down, v_mix_norm_g, v_ffn_norm_g, v_final_norm_g, v_sc_w_in, v_sc_conv_w, v_sc_conv_b, v_sc_w_out, v_attn_w_qkv, v_attn_w_out, v_ffn_w_up, v_ffn_conv_w, v_ffn_conv_b, v_ffn_w_down):
    S, D = x.shape[1], x.shape[2]
    xi, yi, ci = _coords()
    chip = 2 * xi + yi
    x0 = x.reshape(S, D)
    tgt = loss_target.reshape(S, D)

    conv_shapes = [sc_conv_w.shape, ffn_conv_w.shape]
    allc = _small_exchange("gather_conv_w", _pack([sc_conv_w, ffn_conv_w]), reduce=False)
    per_chip = [_unpack(allc[2 * k], conv_shapes) for k in range(N_CHIPS)]
    scw = jnp.concatenate([p[0] for p in per_chip], axis=-1)[0]
    fcw = jnp.concatenate([p[1] for p in per_chip], axis=-1)
    scb = sc_conv_b

    big_names = ["sc_w_in", "sc_w_out", "attn_w_qkv", "attn_w_out", "ffn_w_up", "ffn_w_down"]
    big = dict(zip(big_names, [sc_w_in, sc_w_out, attn_w_qkv, attn_w_out, ffn_w_up, ffn_w_down]))
    n_gathers = [0]

    def gather(tag, w, layers):
        _, R, C = w.shape
        shard = _cast_bf16("cast_" + tag, w, layers).reshape(2, len(layers) * R // 2, C)
        cid = n_gathers[0]
        n_gathers[0] += 1
        return _allgather_weight("allgather_" + tag, shard, cid).reshape(N_CHIPS, len(layers), R, C)

    w_in = gather("sc_w_in", sc_w_in, (0,))
    w_out = gather("sc_w_out", sc_w_out, (0,))
    w_ups = [gather("ffn_w_up0", ffn_w_up, (0,))]
    w_dn = gather("ffn_w_down", ffn_w_down, (0, 1))
    w_qkv = gather("attn_w_qkv", attn_w_qkv, (0,))
    w_ao = gather("attn_w_out", attn_w_out, (0,))
    w_ups.append(gather("ffn_w_up1", ffn_w_up, (1,)))

    def reduce_scatter(tag, g):
        cid = n_gathers[0] + big_names.index(tag)
        others = _grad_exchange("rs_exchange_" + tag, g, cid)
        full = _grad_sum("rs_sum_" + tag, g, others)
        return _pair_fill("rs_pair_fill_" + tag, [full])[0].reshape(big[tag].shape)

    h0 = _rmsnorm_fwd("norm_mix0", x0, mix_norm_g[0:1])[0]
    z = _mm_nn_col("sc_in", h0, w_in, 0)
    y = _sc_fwd("sc_gate", z, scw, scb)
    x1 = _mm_nn_row("sc_out", y, w_out, 0, x0)
    h1 = _rmsnorm_fwd("norm_ffn0", x1, ffn_norm_g[0:1])[0]
    u0 = _mm_nn_col("ffn_up0", h1, w_ups[0], 0)
    f0, v0 = _ffn_fwd("ffn_gate0", u0, fcw[0], ffn_conv_b[0:1])
    x2 = _mm_nn_row("ffn_down0", f0, w_dn, 0, x1)
    h2s = _rmsnorm_fwd("norm_mix1", x2, mix_norm_g[1:2], dilated=True)
    qkvs = [_mm_nn_col(f"attn_qkv{d}", h, w_qkv, 0, col_off=gi * 3 * D, ncols=3 * D)
            for gi, (h, d) in enumerate(zip(h2s, DILATIONS))]
    og, lg = zip(*[_attn_fwd(f"attn_fwd{d}", q, d) for q, d in zip(qkvs, DILATIONS)])
    o32, ob, lse = _attn_combine("attn_combine", list(og), list(lg))
    x3 = _mm_nn_row("attn_out", ob, w_ao, 0, x2)
    h3 = _rmsnorm_fwd("norm_ffn1", x3, ffn_norm_g[1:2])[0]
    u1 = _mm_nn_col("ffn_up1", h3, w_ups[1], 0)
    f1, v1 = _ffn_fwd("ffn_gate1", u1, fcw[1], ffn_conv_b[1:2])
    x4 = _mm_nn_row("ffn_down1", f1, w_dn, 1, x3)

    dx4, dx4b, dg_final, loss_part = _rmsnorm_bwd("loss_norm_bwd", x4, final_norm_g.reshape(1, D), target=tgt)

    def ffn_backward(layer, xin, h, u, v, f, dxo, dxob, gain, g_up, g_dn):
        df = _mm_nt_row(f"ffn_down_dx{layer}", dxob, w_dn, layer)
        g_dn = _mm_tn_row(f"ffn_down_dw{layer}", f, dxob, prev=g_dn, layer=layer)
        du, dwb = _ffn_bwd(f"ffn_gate_bwd{layer}", u, v, df, fcw[layer])
        dh = _mm_nt_col(f"ffn_up_dx{layer}", du, w_ups[layer], 0)
        g_up = _mm_tn_col(f"ffn_up_dw{layer}", h, du, w_ups[layer].shape[3], prev=g_up, layer=layer)
        dxi, dxib, dg = _rmsnorm_bwd(f"norm_ffn_bwd{layer}", xin, gain, dhs=[(dh, 1)], dres=dxo)
        return dxi, dxib, dg, dwb, g_up, g_dn

    dx3, dx3b, dg_ffn1, dwb_ffn1, g_up, g_dn = ffn_backward(1, x3, h3, u1, v1, f1, dx4, dx4b, ffn_norm_g[1:2], None, None)

    big_grads = {}
    do = _mm_nt_row("attn_out_dx", dx3b, w_ao, 0)
    big_grads["attn_w_out"] = reduce_scatter("attn_w_out", _mm_tn_row("attn_out_dw", ob, dx3b))
    dos, lss, dls = _attn_bwd_prep("attn_bwd_prep", do, o32, lse)
    dqkvs = [_attn_bwd(f"attn_bwd{d}", q, a, b, c_, d)
             for q, a, b, c_, d in zip(qkvs, dos, lss, dls, DILATIONS)]
    dh2s = [(_mm_nt_col(f"attn_qkv_dx{d}", dq, w_qkv, 0, col_off=gi * 3 * D), d)
            for gi, (dq, d) in enumerate(zip(dqkvs, DILATIONS))]
    g_qkv = None
    for gi, (h, dq, d) in enumerate(zip(h2s, dqkvs, DILATIONS)):
        g_qkv = _mm_tn_col(f"attn_qkv_dw{d}", h, dq, w_qkv.shape[3], col_off=gi * 3 * D, prev=g_qkv)
    big_grads["attn_w_qkv"] = reduce_scatter("attn_w_qkv", g_qkv)
    dx2, dx2b, dg_mix1 = _rmsnorm_bwd("norm_mix_bwd1", x2, mix_norm_g[1:2], dhs=dh2s, dres=dx3)

    dx1, dx1b, dg_ffn0, dwb_ffn0, g_up, g_dn = ffn_backward(0, x1, h1, u0, v0, f0, dx2, dx2b, ffn_norm_g[0:1], g_up, g_dn)
    big_grads["ffn_w_down"] = reduce_scatter("ffn_w_down", g_dn)
    big_grads["ffn_w_up"] = reduce_scatter("ffn_w_up", g_up)

    dy = _mm_nt_row("sc_out_dx", dx1b, w_out, 0)
    big_grads["sc_w_out"] = reduce_scatter("sc_w_out", _mm_tn_row("sc_out_dw", y, dx1b))
    dz, dwb_sc = _sc_bwd("sc_gate_bwd", z, dy, scw, scb)
    big_grads["sc_w_in"] = reduce_scatter("sc_w_in", _mm_tn_col("sc_in_dw", h0, dz, w_in.shape[3]))
    dh0 = _mm_nt_col("sc_in_dx", dz, w_in, 0)
    dx0, _, dg_mix0 = _rmsnorm_bwd("norm_mix_bwd0", x0, mix_norm_g[0:1], dhs=[(dh0, 1)], dres=dx1)

    dconv_sc = dwb_sc[0:3].reshape(1, 3, D)
    dbias_sc = dwb_sc[3:4]
    dconv_ffn = jnp.stack([dwb_ffn0[0:3], dwb_ffn1[0:3]])
    dbias_ffn = jnp.concatenate([dwb_ffn0[3:4], dwb_ffn1[3:4]], axis=0)
    small_parts = [jnp.concatenate([dg_mix0, dg_mix1], axis=0), jnp.concatenate([dg_ffn0, dg_ffn1], axis=0),
                   dg_final.reshape(D), dconv_sc, dbias_sc, dconv_ffn, dbias_ffn, loss_part[0, 0:1]]
    small_shapes = [a.shape for a in small_parts]
    summed = _unpack(_small_exchange("allreduce_small", _pack(small_parts), reduce=True), small_shapes)
    g_mix, g_ffn, g_final, g_scw_full, g_scb, g_fcw_full, g_fcb, loss = summed
    loss = loss.reshape(())
    cw = sc_conv_w.shape[2]
    g_scw = lax.dynamic_slice_in_dim(g_scw_full, chip * cw, cw, axis=2)
    fw = ffn_conv_w.shape[2]
    g_fcw = lax.dynamic_slice_in_dim(g_fcw_full, chip * fw, fw, axis=2)

    names = ["mix_norm_g", "ffn_norm_g", "final_norm_g", "sc_w_in", "sc_conv_w", "sc_conv_b", "sc_w_out",
             "attn_w_qkv", "attn_w_out", "ffn_w_up", "ffn_conv_w", "ffn_conv_b", "ffn_w_down"]
    ws = dict(zip(names, [mix_norm_g, ffn_norm_g, final_norm_g, sc_w_in, sc_conv_w, sc_conv_b, sc_w_out,
                          attn_w_qkv, attn_w_out, ffn_w_up, ffn_conv_w, ffn_conv_b, ffn_w_down]))
    ms = dict(zip(names, [m_mix_norm_g, m_ffn_norm_g, m_final_norm_g, m_sc_w_in, m_sc_conv_w, m_sc_conv_b, m_sc_w_out,
                          m_attn_w_qkv, m_attn_w_out, m_ffn_w_up, m_ffn_conv_w, m_ffn_conv_b, m_ffn_w_down]))
    vs = dict(zip(names, [v_mix_norm_g, v_ffn_norm_g, v_final_norm_g, v_sc_w_in, v_sc_conv_w, v_sc_conv_b, v_sc_w_out,
                          v_attn_w_qkv, v_attn_w_out, v_ffn_w_up, v_ffn_conv_w, v_ffn_conv_b, v_ffn_w_down]))
    gs = {"mix_norm_g": g_mix, "ffn_norm_g": g_ffn, "final_norm_g": g_final, "sc_conv_w": g_scw,
          "sc_conv_b": g_scb, "ffn_conv_w": g_fcw, "ffn_conv_b": g_fcb}
    gs.update(big_grads)

    deltas, new_m, new_v = {}, {}, {}
    small_names = [n for n in names if n not in big_names]
    packed = [_pack([d[n] for n in small_names]) for d in (ws, gs, ms, vs)]
    outs = _adamw("adamw_small", *packed)
    shapes = [ws[n].shape for n in small_names]
    for res, o in zip((deltas, new_m, new_v), outs):
        res.update(dict(zip(small_names, _unpack(o, shapes))))
    for n in big_names:
        shp = ws[n].shape
        two_d = (shp[0] * shp[1], shp[2])
        outs = _adamw("adamw_" + n, *[d[n].reshape(two_d) for d in (ws, gs, ms, vs)])
        for res, o in zip((deltas, new_m, new_v), outs):
            res[n] = o.reshape(shp)

    return (loss, dx0.reshape(x.shape), *[gs[n] for n in names], *[deltas[n] for n in names],
            *[new_m[n] for n in names], *[new_v[n] for n in names])
```

```python
import functools
import math

import jax
import jax.numpy as jnp
from jax import lax
from jax.experimental import pallas as pl
from jax.experimental.pallas import tpu as pltpu
from jax.experimental.pallas import tpu_sc as plsc

F32 = jnp.float32
BF = jnp.bfloat16
MESH = pl.DeviceIdType.MESH

HEAD_DIM = 128
ATTN_HALF = 64
ATTN_BLOCK = 128
DILATIONS = (1, 4, 16)
STAT_LANES = 128
HALO = 16
NORM_EPS = 1e-5
ALIBI_MAX = 8.0
NEG_INF = -1e30
N_CHIPS = 4
VMEM_LIMIT = 56 * 1024 * 1024

ADAM_LR = 0.001
ADAM_B1 = 0.9
ADAM_B2 = 0.999
ADAM_EPS = 1e-08
ADAM_WD = 0.01
ADAM_STEP = 10


def _pick(n, cands):
    for c in cands:
        if n % c == 0:
            return c
    raise ValueError(f"no tile for {n} in {cands}")


def _row_tile(rows, cols, max_elems=1 << 19):
    for c in (512, 256, 128, 64, 32, 16):
        if rows % c == 0 and c * cols <= max_elems:
            return c
    raise ValueError(f"no row tile for {rows}x{cols}")


def _params(*sem):
    return pltpu.CompilerParams(dimension_semantics=sem, vmem_limit_bytes=VMEM_LIMIT)


def _matmul(name, a, b, out_shape, grid, a_spec, b_spec, o_spec, contract, acc_shape,
            res=None, res_spec=None, prev=None, b2_spec=None):
    nk = grid[2]
    n_b = 1 if b2_spec is None else 2

    def body(*refs):
        refs = list(refs)
        if prev is not None:
            refs.pop(0)
        a_ref, b_ref = refs[0], refs[1]
        res_ref = refs[1 + n_b] if res is not None else None
        o_ref = refs[2 + n_b] if res is not None else refs[1 + n_b]
        acc_ref = refs[-1]
        bv = b_ref[...]
        if bv.ndim == 3:
            bv = bv.reshape(bv.shape[0] * bv.shape[1], bv.shape[2])
        if b2_spec is None:
            part = lax.dot_general(a_ref[...], bv, contract, preferred_element_type=F32)
        else:
            half = a_ref.shape[1] // 2
            part = (lax.dot_general(a_ref[:, :half], bv, contract, preferred_element_type=F32)
                    + lax.dot_general(a_ref[:, half:], refs[2][...], contract, preferred_element_type=F32))

        def finish(total):
            if res_ref is not None:
                total = total + res_ref[...]
            o_ref[...] = total.astype(o_ref.dtype)

        if nk == 1:
            finish(part)
        else:
            k = pl.program_id(2)

            @pl.when(k == 0)
            def _():
                acc_ref[...] = part

            @pl.when(jnp.logical_and(k > 0, k < nk - 1))
            def _():
                acc_ref[...] += part

            @pl.when(k == nk - 1)
            def _():
                finish(acc_ref[...] + part)

    operands, in_specs, aliases = [], [], {}
    if prev is not None:
        operands.append(prev)
        in_specs.append(pl.BlockSpec(memory_space=pl.ANY))
        aliases = {0: 0}
    operands += [a, b]
    in_specs += [a_spec, b_spec]
    if b2_spec is not None:
        operands.append(b)
        in_specs.append(b2_spec)
    if res is not None:
        operands.append(res)
        in_specs.append(res_spec)
    return pl.pallas_call(
        body, name=name, out_shape=out_shape, grid=grid, in_specs=in_specs, out_specs=o_spec,
        scratch_shapes=[pltpu.VMEM(acc_shape if nk > 1 else (8, 128), F32)],
        input_output_aliases=aliases,
        compiler_params=_params("parallel", "parallel", "arbitrary"),
    )(*operands)


NN = (((1,), (0,)), ((), ()))
NT = (((1,), (1,)), ((), ()))
TN = (((0,), (0,)), ((), ()))

_COL_TILES = (1536, 1408, 1024, 768, 512, 384, 256, 128)
_WIDE_TILES = (2816,) + _COL_TILES


def _mm_nn_col(name, a, w, layer, col_off=0, ncols=None, out_dtype=BF):
    M, K = a.shape
    _, _, R, C = w.shape
    assert R == K
    ncols = N_CHIPS * C if ncols is None else ncols
    tn = _pick(math.gcd(C, math.gcd(ncols, col_off) if col_off else ncols), _WIDE_TILES)
    tm = _pick(M, (1024, 512, 256))
    nb, off = C // tn, col_off // tn
    return _matmul(
        name, a, w, jax.ShapeDtypeStruct((M, ncols), out_dtype), (M // tm, ncols // tn, 1),
        pl.BlockSpec((tm, K), lambda i, j, k: (i, 0)),
        pl.BlockSpec((None, None, K, tn), lambda i, j, k: ((j + off) // nb, layer, 0, (j + off) % nb)),
        pl.BlockSpec((tm, tn), lambda i, j, k: (i, j)), NN, (tm, tn))


def _mm_nn_row(name, a, w, layer, res):
    M, K = a.shape
    _, _, R, C = w.shape
    assert N_CHIPS * R == K
    chips_per_step = N_CHIPS if K <= 2048 else 2
    tk = chips_per_step * R
    tm = _pick(M, (1024, 512, 256))
    tn = _pick(C, (1024, 512, 256))
    return _matmul(
        name, a, w, jax.ShapeDtypeStruct((M, C), F32), (M // tm, C // tn, K // tk),
        pl.BlockSpec((tm, tk), lambda i, j, k: (i, k)),
        pl.BlockSpec((chips_per_step, None, R, tn), lambda i, j, k: (k, layer, 0, j)),
        pl.BlockSpec((tm, tn), lambda i, j, k: (i, j)), NN, (tm, tn),
        res=res, res_spec=pl.BlockSpec((tm, tn), lambda i, j, k: (i, j)))


def _mm_nt_col(name, dy, w, layer, col_off=0, out_dtype=F32):
    M, n = dy.shape
    _, _, R, C = w.shape
    tk = _pick(math.gcd(C, math.gcd(n, col_off) if col_off else n), _WIDE_TILES)
    tm = _pick(M, (1024, 512, 256))
    tn = _pick(R, (1024, 512, 256))
    nb, off = C // tk, col_off // tk
    per_step = 2 if (tk <= 1536 and (n // tk) % 2 == 0) else 1

    def w_block(t):
        return pl.BlockSpec((None, None, tn, tk), lambda i, j, k: (
            (per_step * k + t + off) // nb, layer, j, (per_step * k + t + off) % nb))

    return _matmul(
        name, dy, w, jax.ShapeDtypeStruct((M, R), out_dtype), (M // tm, R // tn, n // (per_step * tk)),
        pl.BlockSpec((tm, per_step * tk), lambda i, j, k: (i, k)), w_block(0),
        pl.BlockSpec((tm, tn), lambda i, j, k: (i, j)), NT, (tm, tn),
        b2_spec=w_block(1) if per_step == 2 else None)


def _mm_nt_row(name, dy, w, layer, out_dtype=BF):
    M, C2 = dy.shape
    _, _, R, C = w.shape
    assert C2 == C
    chips_per_tile = N_CHIPS if N_CHIPS * R <= 2048 else 2
    tn = chips_per_tile * R
    tm = _pick(M, (1024, 512, 256))
    return _matmul(
        name, dy, w, jax.ShapeDtypeStruct((M, N_CHIPS * R), out_dtype), (M // tm, N_CHIPS * R // tn, 1),
        pl.BlockSpec((tm, C), lambda i, j, k: (i, 0)),
        pl.BlockSpec((chips_per_tile, None, R, C), lambda i, j, k: (j, layer, 0, 0)),
        pl.BlockSpec((tm, tn), lambda i, j, k: (i, j)), NT, (tm, tn))


_TN_DEPTH = (2048, 1024, 512, 256)


def _half_index(rows, layer, tkx):
    if layer is None:
        hb = rows // 2 // tkx
        return rows // 2, lambda i: (i // hb, i % hb)
    return rows, lambda i: (layer, i)


def _mm_tn_col(name, xa, dy, C, col_off=0, prev=None, layer=None):
    M, K = xa.shape
    _, n = dy.shape
    tn = _pick(math.gcd(C, math.gcd(n, col_off) if col_off else n), _WIDE_TILES)
    tkx = _pick(K // 2 if layer is None else K, (1024, 512, 256, 128) if tn <= 1536 else (512, 256, 128))
    tmr = _pick(M, _TN_DEPTH)
    rh, split = _half_index(K, layer, tkx)
    nb, off = C // tn, col_off // tn
    return _matmul(
        name, xa, dy, jax.ShapeDtypeStruct((N_CHIPS, 2, rh, C), BF), (K // tkx, n // tn, M // tmr),
        pl.BlockSpec((tmr, tkx), lambda i, j, k: (k, i)),
        pl.BlockSpec((tmr, tn), lambda i, j, k: (k, j)),
        pl.BlockSpec((None, None, tkx, tn), lambda i, j, k: ((j + off) // nb, *split(i), (j + off) % nb)),
        TN, (tkx, tn), prev=prev)


def _mm_tn_row(name, xa, dy, prev=None, layer=None):
    M, K = xa.shape
    _, C = dy.shape
    R = K // N_CHIPS
    tkx = _pick(R // 2 if layer is None else R, (1408, 1024, 512, 256, 128))
    tmr = _pick(M, _TN_DEPTH)
    tn = _pick(C, (2048, 1024, 512, 256) if tkx <= 512 else (1024, 512, 256))
    rh, split = _half_index(R, layer, tkx)
    rb = R // tkx
    return _matmul(
        name, xa, dy, jax.ShapeDtypeStruct((N_CHIPS, 2, rh, C), BF), (K // tkx, C // tn, M // tmr),
        pl.BlockSpec((tmr, tkx), lambda i, j, k: (k, i)),
        pl.BlockSpec((tmr, tn), lambda i, j, k: (k, j)),
        pl.BlockSpec((None, None, tkx, tn), lambda i, j, k: (i // rb, *split(i % rb), j)),
        TN, (tkx, tn), prev=prev)


NORM_ROWS = 256
LANES = 128


def _chunk_scratch(tm, width):
    return pltpu.VMEM((width // LANES, tm, LANES), F32)


def _store_chunks(scr, value):
    for c in range(scr.shape[0]):
        scr[c] = value[:, c * LANES:(c + 1) * LANES]


def _load_chunks(scr):
    return jnp.concatenate([scr[c] for c in range(scr.shape[0])], axis=1)


def _to_residue_major(scr, o_ref, d, dtype):
    tm = scr.shape[1]
    for c in range(scr.shape[0]):
        for res in range(d):
            o_ref[res, :, c * LANES:(c + 1) * LANES] = scr[c, pl.ds(res, tm // d, stride=d), :].astype(dtype)


def _to_natural(scr, ref, d):
    tm = scr.shape[1]
    for c in range(scr.shape[0]):
        for res in range(d):
            scr[c, pl.ds(res, tm // d, stride=d), :] = ref[res, :, c * LANES:(c + 1) * LANES].astype(F32)


def _rmsnorm_fwd(name, x, g, dilated=False):
    S, D = x.shape
    tm = NORM_ROWS
    dils = DILATIONS[1:] if dilated else ()

    def body(x_ref, g_ref, h_ref, *rest):
        xv = x_ref[...]
        r = lax.rsqrt(jnp.mean(xv * xv, axis=1, keepdims=True) + NORM_EPS)
        h = xv * r * g_ref[...]
        h_ref[...] = h.astype(BF)
        if dils:
            scr = rest[-1]
            _store_chunks(scr, h)
            for o_ref, d in zip(rest[:-1], dils):
                _to_residue_major(scr, o_ref, d, BF)

    out_shape = [jax.ShapeDtypeStruct((S, D), BF)]
    out_specs = [pl.BlockSpec((tm, D), lambda i: (i, 0))]
    for d in dils:
        out_shape.append(jax.ShapeDtypeStruct((d, S // d, D), BF))
        out_specs.append(pl.BlockSpec((d, tm // d, D), lambda i: (0, i, 0)))
    outs = pl.pallas_call(
        body, name=name, out_shape=out_shape, grid=(S // tm,),
        in_specs=[pl.BlockSpec((tm, D), lambda i: (i, 0)), pl.BlockSpec((1, D), lambda i: (0, 0))],
        out_specs=out_specs,
        scratch_shapes=[_chunk_scratch(tm, D)] if dils else [],
        compiler_params=_params("parallel"),
    )(x, g)
    return [outs[0]] + [o.reshape(S, D) for o in outs[1:]]


def _rmsnorm_bwd(name, x, g, dhs=(), dres=None, target=None):
    S, D = x.shape
    tm = NORM_ROWS
    n_dh = len(dhs)

    def body(*refs):
        refs = list(refs)
        x_ref, g_ref = refs[0], refs[1]
        dh_refs = refs[2:2 + n_dh]
        pos = 2 + n_dh
        dres_ref = tgt_ref = None
        if dres is not None:
            dres_ref = refs[pos]
            pos += 1
        if target is not None:
            tgt_ref = refs[pos]
            pos += 1
        dx_ref, dxb_ref, dg_ref = refs[pos:pos + 3]
        pos += 3
        loss_ref = None
        if target is not None:
            loss_ref = refs[pos]
            pos += 1
        scr = refs[pos] if any(d > 1 for _, d in dhs) else None
        i = pl.program_id(0)

        xv = x_ref[...]
        gv = g_ref[...]
        r = lax.rsqrt(jnp.mean(xv * xv, axis=1, keepdims=True) + NORM_EPS)
        xhat = xv * r
        if target is not None:
            err = xhat * gv - tgt_ref[...]
            dh = err * (1.0 / D)
            part = jnp.sum(jnp.sum(err * err, axis=1, keepdims=True), axis=0, keepdims=True) * (0.5 / D)
        else:
            dh = None
            for ref, d in zip(dh_refs, [d for _, d in dhs]):
                if d == 1:
                    v = ref[...]
                else:
                    _to_natural(scr, ref, d)
                    v = _load_chunks(scr)
                dh = v if dh is None else dh + v
        dxhat = dh * gv
        dx = r * (dxhat - xhat * jnp.mean(dxhat * xhat, axis=1, keepdims=True))
        if dres_ref is not None:
            dx = dx + dres_ref[...]
        dx_ref[...] = dx
        dxb_ref[...] = dx.astype(BF)
        dg = jnp.sum(dh * xhat, axis=0, keepdims=True)

        @pl.when(i == 0)
        def _():
            dg_ref[...] = dg
            if loss_ref is not None:
                loss_ref[...] = jnp.broadcast_to(part, loss_ref.shape)

        @pl.when(i > 0)
        def _():
            dg_ref[...] += dg
            if loss_ref is not None:
                loss_ref[...] += jnp.broadcast_to(part, loss_ref.shape)

    row = pl.BlockSpec((tm, D), lambda i: (i, 0))
    operands = [x, g]
    in_specs = [row, pl.BlockSpec((1, D), lambda i: (0, 0))]
    for arr, d in dhs:
        if d == 1:
            operands.append(arr)
            in_specs.append(row)
        else:
            operands.append(arr.reshape(d, S // d, D))
            in_specs.append(pl.BlockSpec((d, tm // d, D), lambda i: (0, i, 0)))
    if dres is not None:
        operands.append(dres)
        in_specs.append(row)
    if target is not None:
        operands.append(target)
        in_specs.append(row)
    out_shape = [jax.ShapeDtypeStruct((S, D), F32), jax.ShapeDtypeStruct((S, D), BF),
                 jax.ShapeDtypeStruct((1, D), F32)]
    out_specs = [row, row, pl.BlockSpec((1, D), lambda i: (0, 0))]
    if target is not None:
        out_shape.append(jax.ShapeDtypeStruct((1, STAT_LANES), F32))
        out_specs.append(pl.BlockSpec((1, STAT_LANES), lambda i: (0, 0)))
    scratch = [_chunk_scratch(tm, D)] if any(d > 1 for _, d in dhs) else []
    return pl.pallas_call(
        body, name=name, out_shape=out_shape, grid=(S // tm,), in_specs=in_specs, out_specs=out_specs,
        scratch_shapes=scratch, compiler_params=_params("arbitrary"),
    )(*operands)


CONV_ROWS = 256
CONV_COLS = 256


def _halo_specs(S, tm, width):
    nh = S // HALO
    per = tm // HALO
    cur = pl.BlockSpec((tm, width), lambda i: (i, 0))
    prev = pl.BlockSpec((HALO, width), lambda i: (jnp.maximum(i * per - 1, 0), 0))
    nxt = pl.BlockSpec((HALO, width), lambda i: (jnp.minimum((i + 1) * per, nh - 1), 0))
    return [cur, prev, nxt]


def _ext(refs, cs, inr):
    cur, prev, nxt = refs
    v = jnp.concatenate([prev[:, cs], cur[:, cs], nxt[:, cs]], axis=0).astype(F32)
    return jnp.where(inr, v, 0.0)


def _shift_prev(v):
    return pltpu.roll(v, 1, 0)


def _shift_next(v):
    return pltpu.roll(v, v.shape[0] - 1, 0)


def _shifts(v):
    return _shift_prev(v), _shift_next(v)


def _conv3(v, w, cs, b=None, shifted=None):
    vp, vn = _shifts(v) if shifted is None else shifted
    out = w[0:1, cs] * vp + w[1:2, cs] * v + w[2:3, cs] * vn
    return out if b is None else out + b[:, cs]


def _in_range(i, tm, tc, S):
    row = lax.broadcasted_iota(jnp.int32, (tm + 2 * HALO, tc), 0) + (i * tm - HALO)
    return jnp.logical_and(row >= 0, row < S)


def _core(v, tm):
    return v[HALO:HALO + tm, :]


def _acc_rows(ref, i, rows):
    for r, cs, val in rows:
        ref[r:r + 1, cs] += val


def _zero_first(ref, i):
    @pl.when(i == 0)
    def _():
        ref[...] = jnp.zeros(ref.shape, ref.dtype)


def _sc_fwd(name, z, w, b):
    S, D3 = z.shape
    D = D3 // 3
    tm, tc = CONV_ROWS, _pick(D, (CONV_COLS, 256, 128))

    def body(zc, zp, zn, w_ref, b_ref, y_ref):
        i = pl.program_id(0)
        inr = _in_range(i, tm, tc, S)
        zr = (zc, zp, zn)
        for c in range(D // tc):
            cs = slice(c * tc, (c + 1) * tc)
            u = _ext(zr, cs, inr)
            gc = _ext(zr, slice(2 * D + c * tc, 2 * D + (c + 1) * tc), inr)
            conv = _conv3(gc * u, w_ref, cs, b_ref)
            gb = zc[:, D + c * tc:D + (c + 1) * tc].astype(F32)
            y_ref[:, cs] = (gb * _core(conv, tm)).astype(BF)

    return pl.pallas_call(
        body, name=name, out_shape=jax.ShapeDtypeStruct((S, D), BF), grid=(S // tm,),
        in_specs=_halo_specs(S, tm, D3) + [pl.BlockSpec((3, D), lambda i: (0, 0)),
                                           pl.BlockSpec((1, D), lambda i: (0, 0))],
        out_specs=pl.BlockSpec((tm, D), lambda i: (i, 0)),
        compiler_params=_params("parallel"),
    )(z, z, z, w, b)


def _sc_bwd(name, z, dy, w, b):
    S, D3 = z.shape
    D = D3 // 3
    tm, tc = CONV_ROWS, _pick(D, (CONV_COLS, 256, 128))

    def body(zc, zp, zn, dc_, dp_, dn_, w_ref, b_ref, dz_ref, dwb_ref):
        i = pl.program_id(0)
        inr = _in_range(i, tm, tc, S)
        _zero_first(dwb_ref, i)
        zr, dr = (zc, zp, zn), (dc_, dp_, dn_)
        for c in range(D // tc):
            cs = slice(c * tc, (c + 1) * tc)
            u = _ext(zr, cs, inr)
            gb = _ext(zr, slice(D + c * tc, D + (c + 1) * tc), inr)
            gc = _ext(zr, slice(2 * D + c * tc, 2 * D + (c + 1) * tc), inr)
            dyv = _ext(dr, cs, inr)
            p = gc * u
            p_prev, p_next = _shifts(p)
            conv = _conv3(p, w_ref, cs, b_ref, shifted=(p_prev, p_next))
            dconv = dyv * gb
            dp = w_ref[0:1, cs] * _shift_next(dconv) + w_ref[1:2, cs] * dconv + w_ref[2:3, cs] * _shift_prev(dconv)
            dz_ref[:, cs] = _core(dp * gc, tm).astype(BF)
            dz_ref[:, D + c * tc:D + (c + 1) * tc] = _core(dyv * conv, tm).astype(BF)
            dz_ref[:, 2 * D + c * tc:2 * D + (c + 1) * tc] = _core(dp * u, tm).astype(BF)
            dcc = _core(dconv, tm)
            _acc_rows(dwb_ref, i, [
                (0, cs, jnp.sum(dcc * _core(p_prev, tm), axis=0, keepdims=True)),
                (1, cs, jnp.sum(dcc * _core(p, tm), axis=0, keepdims=True)),
                (2, cs, jnp.sum(dcc * _core(p_next, tm), axis=0, keepdims=True)),
                (3, cs, jnp.sum(dcc, axis=0, keepdims=True))])

    return pl.pallas_call(
        body, name=name,
        out_shape=[jax.ShapeDtypeStruct((S, D3), BF), jax.ShapeDtypeStruct((4, D), F32)], grid=(S // tm,),
        in_specs=_halo_specs(S, tm, D3) + _halo_specs(S, tm, D) + [
            pl.BlockSpec((3, D), lambda i: (0, 0)), pl.BlockSpec((1, D), lambda i: (0, 0))],
        out_specs=[pl.BlockSpec((tm, D3), lambda i: (i, 0)), pl.BlockSpec((4, D), lambda i: (0, 0))],
        compiler_params=_params("arbitrary"),
    )(z, z, z, dy, dy, dy, w, b)


def _sigmoid(v):
    return 1.0 / (1.0 + jnp.exp(-v))


def _ffn_fwd(name, u, w, b):
    S, F2 = u.shape
    Fh = F2 // 2
    tm, tc = CONV_ROWS, _pick(Fh, (CONV_COLS, 256, 128))

    def body(uc, up, un, w_ref, b_ref, f_ref, v_ref):
        i = pl.program_id(0)
        inr = _in_range(i, tm, tc, S)
        ur = (uc, up, un)
        for c in range(Fh // tc):
            ca = slice(c * tc, (c + 1) * tc)
            cb = slice(Fh + c * tc, Fh + (c + 1) * tc)
            va = _core(_conv3(_ext(ur, ca, inr), w_ref, ca, b_ref), tm)
            vb = _core(_conv3(_ext(ur, cb, inr), w_ref, cb, b_ref), tm)
            v_ref[:, ca] = va.astype(BF)
            v_ref[:, cb] = vb.astype(BF)
            f_ref[:, ca] = (va * _sigmoid(va) * vb).astype(BF)

    return pl.pallas_call(
        body, name=name,
        out_shape=[jax.ShapeDtypeStruct((S, Fh), BF), jax.ShapeDtypeStruct((S, F2), BF)], grid=(S // tm,),
        in_specs=_halo_specs(S, tm, F2) + [pl.BlockSpec((3, F2), lambda i: (0, 0)),
                                           pl.BlockSpec((1, F2), lambda i: (0, 0))],
        out_specs=[pl.BlockSpec((tm, Fh), lambda i: (i, 0)), pl.BlockSpec((tm, F2), lambda i: (i, 0))],
        compiler_params=_params("parallel"),
    )(u, u, u, w, b)


def _ffn_bwd(name, u, v, df, w):
    S, F2 = u.shape
    Fh = F2 // 2
    tm, tc = CONV_ROWS, _pick(Fh, (CONV_COLS, 256, 128))

    def body(u_ref, vc, vp, vn, dc_, dp_, dn_, w_ref, du_ref, dwb_ref):
        i = pl.program_id(0)
        inr = _in_range(i, tm, tc, S)
        _zero_first(dwb_ref, i)
        vr, dr = (vc, vp, vn), (dc_, dp_, dn_)
        for c in range(Fh // tc):
            ca = slice(c * tc, (c + 1) * tc)
            cb = slice(Fh + c * tc, Fh + (c + 1) * tc)
            va, vb = _ext(vr, ca, inr), _ext(vr, cb, inr)
            dfv = _ext(dr, ca, inr)
            sg = _sigmoid(va)
            dva = dfv * vb * (sg * (1.0 + va * (1.0 - sg)))
            dvb = dfv * (va * sg)
            rows = []
            for cs, dv in ((ca, dva), (cb, dvb)):
                dv_prev, dv_next = _shifts(dv)
                dcore = w_ref[0:1, cs] * dv_next + w_ref[1:2, cs] * dv + w_ref[2:3, cs] * dv_prev
                du_ref[:, cs] = _core(dcore, tm).astype(BF)
                uu = u_ref[:, cs].astype(F32)
                dvc = _core(dv, tm)
                rows += [
                    (0, cs, jnp.sum(_core(dv_next, tm) * uu, axis=0, keepdims=True)),
                    (1, cs, jnp.sum(dvc * uu, axis=0, keepdims=True)),
                    (2, cs, jnp.sum(_core(dv_prev, tm) * uu, axis=0, keepdims=True)),
                    (3, cs, jnp.sum(dvc, axis=0, keepdims=True))]
            _acc_rows(dwb_ref, i, rows)

    return pl.pallas_call(
        body, name=name,
        out_shape=[jax.ShapeDtypeStruct((S, F2), BF), jax.ShapeDtypeStruct((4, F2), F32)], grid=(S // tm,),
        in_specs=[pl.BlockSpec((tm, F2), lambda i: (i, 0))] + _halo_specs(S, tm, F2) + _halo_specs(S, tm, Fh) + [
            pl.BlockSpec((3, F2), lambda i: (0, 0))],
        out_specs=[pl.BlockSpec((tm, F2), lambda i: (i, 0)), pl.BlockSpec((4, F2), lambda i: (0, 0))],
        compiler_params=_params("arbitrary"),
    )(u, v, v, v, df, df, df, w)


def _alibi_slopes(H):
    return [2.0 ** (-ALIBI_MAX * (h + 1) / H) for h in range(H)]


def _window_specs(S, width, col):
    n64 = S // ATTN_HALF
    cur = pl.BlockSpec((ATTN_BLOCK, width), lambda b: (b, col))
    prev = pl.BlockSpec((ATTN_HALF, width), lambda b: (jnp.maximum(2 * b - 1, 0), col))
    nxt = pl.BlockSpec((ATTN_HALF, width), lambda b: (jnp.minimum(2 * b + 2, n64 - 1), col))
    return [cur, prev, nxt]


def _fill_window(buf, cur, prev, nxt):
    buf[0:ATTN_HALF] = prev[...]
    buf[ATTN_HALF:ATTN_HALF + ATTN_BLOCK] = cur[...]
    buf[ATTN_HALF + ATTN_BLOCK:2 * ATTN_BLOCK] = nxt[...]


def _band(b, L, queries_in_rows_of_block):
    QB, W = ATTN_BLOCK, 2 * ATTN_BLOCK
    a_loc = (b * QB) % L
    if queries_in_rows_of_block:
        row = lax.broadcasted_iota(jnp.int32, (QB, W), 0)
        col = lax.broadcasted_iota(jnp.int32, (QB, W), 1)
        rel = col - ATTN_HALF - row
        other = a_loc - ATTN_HALF + col
    else:
        row = lax.broadcasted_iota(jnp.int32, (W, QB), 0)
        col = lax.broadcasted_iota(jnp.int32, (W, QB), 1)
        rel = col + ATTN_HALF - row
        other = a_loc - ATTN_HALF + row
    dist = jnp.abs(rel)
    valid = jnp.logical_and(dist <= ATTN_HALF, jnp.logical_and(other >= 0, other < L))
    return dist.astype(F32), valid


def _lane_col(stats, h):
    lane = lax.broadcasted_iota(jnp.int32, stats.shape, 1)
    return jnp.sum(jnp.where(lane == h, stats, 0.0), axis=1, keepdims=True)


def _attn_fwd(name, qkv, d):
    S, D3 = qkv.shape
    D = D3 // 3
    H = D // HEAD_DIM
    L = S // d
    scale = HEAD_DIM ** -0.5
    slopes = _alibi_slopes(H)

    def body(q_ref, kc, kp, kn, vc, vp, vn, o_ref, lse_ref, kbuf, vbuf):
        b = pl.program_id(0)
        _fill_window(kbuf, kc, kp, kn)
        _fill_window(vbuf, vc, vp, vn)
        dist, valid = _band(b, L, True)
        dist = dist * float(d)
        lane = lax.broadcasted_iota(jnp.int32, (ATTN_BLOCK, STAT_LANES), 1)
        lse = jnp.zeros((ATTN_BLOCK, STAT_LANES), F32)
        for h in range(H):
            cs = slice(h * HEAD_DIM, (h + 1) * HEAD_DIM)
            s = lax.dot_general(q_ref[:, cs], kbuf[:, cs], NT, preferred_element_type=F32) * scale
            s = jnp.where(valid, s - slopes[h] * dist, NEG_INF)
            m = jnp.max(s, axis=1, keepdims=True)
            p = jnp.exp(s - m)
            den = jnp.sum(p, axis=1, keepdims=True)
            o = jnp.dot(p.astype(BF), vbuf[:, cs], preferred_element_type=F32)
            o_ref[:, cs] = o / den
            lse = jnp.where(lane == h, m + jnp.log(den), lse)
        lse_ref[...] = lse

    return pl.pallas_call(
        body, name=name,
        out_shape=[jax.ShapeDtypeStruct((S, D), F32), jax.ShapeDtypeStruct((S, STAT_LANES), F32)],
        grid=(S // ATTN_BLOCK,),
        in_specs=[pl.BlockSpec((ATTN_BLOCK, D), lambda b: (b, 0))] + _window_specs(S, D, 1) + _window_specs(S, D, 2),
        out_specs=[pl.BlockSpec((ATTN_BLOCK, D), lambda b: (b, 0)),
                   pl.BlockSpec((ATTN_BLOCK, STAT_LANES), lambda b: (b, 0))],
        scratch_shapes=[pltpu.VMEM((2 * ATTN_BLOCK, D), BF), pltpu.VMEM((2 * ATTN_BLOCK, D), BF)],
        compiler_params=_params("parallel"),
    )(qkv, qkv, qkv, qkv, qkv, qkv, qkv)


def _dil_specs(S, tm, width):
    specs = [pl.BlockSpec((tm, width), lambda i: (i, 0))]
    for d in DILATIONS[1:]:
        specs.append(pl.BlockSpec((d, tm // d, width), lambda i: (0, i, 0)))
    return specs


def _attn_combine(name, outs, lses):
    S, D = outs[0].shape
    H = D // HEAD_DIM
    tm = NORM_ROWS

    def body(o1, o4, o16, l1, l4, l16, o_ref, ob_ref, lse_ref, oscr, lscr):
        ls = [l1[...]]
        for ref, d in zip((l4, l16), DILATIONS[1:]):
            _to_natural(lscr, ref, d)
            ls.append(lscr[0])
        top = jnp.maximum(jnp.maximum(ls[0], ls[1]), ls[2])
        es = [jnp.exp(l - top) for l in ls]
        tot = es[0] + es[1] + es[2]
        lse_ref[...] = top + jnp.log(tot)
        ws = [e / tot for e in es]
        for gi, (ref, d) in enumerate(zip((o1, o4, o16), DILATIONS)):
            if d > 1:
                _to_natural(oscr, ref, d)
            for h in range(H):
                cs = slice(h * HEAD_DIM, (h + 1) * HEAD_DIM)
                term = _lane_col(ws[gi], h) * (ref[:, cs] if d == 1 else oscr[h])
                if gi == 0:
                    o_ref[:, cs] = term
                else:
                    o_ref[:, cs] += term
        ob_ref[...] = o_ref[...].astype(BF)

    outs3 = [outs[0]] + [o.reshape(d, S // d, D) for o, d in zip(outs[1:], DILATIONS[1:])]
    lses3 = [lses[0]] + [l.reshape(d, S // d, STAT_LANES) for l, d in zip(lses[1:], DILATIONS[1:])]
    row = pl.BlockSpec((tm, D), lambda i: (i, 0))
    return pl.pallas_call(
        body, name=name,
        out_shape=[jax.ShapeDtypeStruct((S, D), F32), jax.ShapeDtypeStruct((S, D), BF),
                   jax.ShapeDtypeStruct((S, STAT_LANES), F32)],
        grid=(S // tm,),
        in_specs=_dil_specs(S, tm, D) + _dil_specs(S, tm, STAT_LANES),
        out_specs=[row, row, pl.BlockSpec((tm, STAT_LANES), lambda i: (i, 0))],
        scratch_shapes=[_chunk_scratch(tm, D), _chunk_scratch(tm, STAT_LANES)],
        compiler_params=_params("parallel"),
    )(*outs3, *lses3)


def _attn_bwd_prep(name, do, o32, lse):
    S, D = do.shape
    H = D // HEAD_DIM
    tm = NORM_ROWS
    dils = DILATIONS[1:]

    def body(do_ref, o_ref, lse_ref, dl_ref, do4, do16, l4, l16, d4, d16, scr, sscr):
        lane = lax.broadcasted_iota(jnp.int32, (tm, STAT_LANES), 1)
        delta = jnp.zeros((tm, STAT_LANES), F32)
        for h in range(H):
            cs = slice(h * HEAD_DIM, (h + 1) * HEAD_DIM)
            dov = do_ref[:, cs].astype(F32)
            scr[h] = dov
            delta = jnp.where(lane == h, jnp.sum(dov * o_ref[:, cs], axis=1, keepdims=True), delta)
        dl_ref[...] = delta
        for ref, d in zip((do4, do16), dils):
            _to_residue_major(scr, ref, d, BF)
        for val, refs in ((lse_ref[...], (l4, l16)), (delta, (d4, d16))):
            sscr[0] = val
            for ref, d in zip(refs, dils):
                _to_residue_major(sscr, ref, d, F32)

    def perm_shapes(width, dt):
        return [jax.ShapeDtypeStruct((d, S // d, width), dt) for d in dils]

    def perm_specs(width):
        return [pl.BlockSpec((d, tm // d, width), lambda i: (0, i, 0)) for d in dils]

    row = lambda w: pl.BlockSpec((tm, w), lambda i: (i, 0))
    outs = pl.pallas_call(
        body, name=name,
        out_shape=[jax.ShapeDtypeStruct((S, STAT_LANES), F32)]
        + perm_shapes(D, BF) + perm_shapes(STAT_LANES, F32) + perm_shapes(STAT_LANES, F32),
        grid=(S // tm,),
        in_specs=[row(D), row(D), row(STAT_LANES)],
        out_specs=[row(STAT_LANES)] + perm_specs(D) + perm_specs(STAT_LANES) + perm_specs(STAT_LANES),
        scratch_shapes=[_chunk_scratch(tm, D), _chunk_scratch(tm, STAT_LANES)],
        compiler_params=_params("parallel"),
    )(do, o32, lse)
    dos = [do] + [a.reshape(S, D) for a in outs[1:3]]
    lss = [lse] + [a.reshape(S, STAT_LANES) for a in outs[3:5]]
    dls = [outs[0]] + [a.reshape(S, STAT_LANES) for a in outs[5:7]]
    return dos, lss, dls


def _attn_bwd(name, qkv, do, lse, delta, d):
    S, D3 = qkv.shape
    D = D3 // 3
    H = D // HEAD_DIM
    L = S // d
    scale = HEAD_DIM ** -0.5
    slopes = _alibi_slopes(H)
    QB = ATTN_BLOCK

    def body(qc, qp, qn, kc, kp, kn, vc, vp, vn, dc_, dp_, dn_, lc, lp, ln, ec, ep, en,
             out_ref, qbuf, kbuf, vbuf, dobuf, lbuf, ebuf):
        b = pl.program_id(0)
        for buf, trio in ((qbuf, (qc, qp, qn)), (kbuf, (kc, kp, kn)), (vbuf, (vc, vp, vn)),
                          (dobuf, (dc_, dp_, dn_)), (lbuf, (lc, lp, ln)), (ebuf, (ec, ep, en))):
            _fill_window(buf, *trio)
        dist_q, valid_q = _band(b, L, True)
        dist_k, valid_k = _band(b, L, False)
        dist_q = dist_q * float(d)
        dist_k = dist_k * float(d)
        lse_c, del_c = lc[...], ec[...]
        lse_w, del_w = lbuf[...], ebuf[...]
        for h in range(H):
            cs = slice(h * HEAD_DIM, (h + 1) * HEAD_DIM)
            q, do_h = qc[:, cs], dc_[:, cs]
            kw, vw = kbuf[:, cs], vbuf[:, cs]
            s = lax.dot_general(q, kw, NT, preferred_element_type=F32) * scale - slopes[h] * dist_q
            p = jnp.where(valid_q, jnp.exp(s - _lane_col(lse_c, h)), 0.0)
            dp = lax.dot_general(do_h, vw, NT, preferred_element_type=F32)
            ds = p * (dp - _lane_col(del_c, h))
            dq = jnp.dot(ds.astype(BF), kw, preferred_element_type=F32) * scale
            out_ref[:, cs] = dq.astype(BF)
            qw, dow = qbuf[:, cs], dobuf[:, cs]
            k, v = kc[:, cs], vc[:, cs]
            s2 = lax.dot_general(qw, k, NT, preferred_element_type=F32) * scale - slopes[h] * dist_k
            p2 = jnp.where(valid_k, jnp.exp(s2 - _lane_col(lse_w, h)), 0.0)
            dv = lax.dot_general(p2.astype(BF), dow, TN, preferred_element_type=F32)
            dp2 = lax.dot_general(dow, v, NT, preferred_element_type=F32)
            ds2 = p2 * (dp2 - _lane_col(del_w, h))
            dk = lax.dot_general(ds2.astype(BF), qw, TN, preferred_element_type=F32) * scale
            out_ref[:, D + h * HEAD_DIM:D + (h + 1) * HEAD_DIM] = dk.astype(BF)
            out_ref[:, 2 * D + h * HEAD_DIM:2 * D + (h + 1) * HEAD_DIM] = dv.astype(BF)

    W = 2 * QB
    return pl.pallas_call(
        body, name=name, out_shape=jax.ShapeDtypeStruct((S, D3), BF), grid=(S // QB,),
        in_specs=(_window_specs(S, D, 0) + _window_specs(S, D, 1) + _window_specs(S, D, 2)
                  + _window_specs(S, D, 0) + _window_specs(S, STAT_LANES, 0) + _window_specs(S, STAT_LANES, 0)),
        out_specs=pl.BlockSpec((QB, D3), lambda b: (b, 0)),
        scratch_shapes=[pltpu.VMEM((W, D), BF), pltpu.VMEM((W, D), BF), pltpu.VMEM((W, D), BF),
                        pltpu.VMEM((W, D), BF), pltpu.VMEM((W, STAT_LANES), F32), pltpu.VMEM((W, STAT_LANES), F32)],
        compiler_params=_params("parallel"),
    )(qkv, qkv, qkv, qkv, qkv, qkv, qkv, qkv, qkv, do, do, do, lse, lse, lse, delta, delta, delta)


def _cast_bf16(name, w, layers):
    _, R, C = w.shape
    tr = _row_tile(R, C)
    first = layers[0]

    def body(w_ref, o_ref):
        o_ref[...] = w_ref[...].astype(BF)

    return pl.pallas_call(
        body, name=name, out_shape=jax.ShapeDtypeStruct((len(layers), R, C), BF), grid=(len(layers), R // tr),
        in_specs=[pl.BlockSpec((None, tr, C), lambda l, i: (first + l, i, 0))],
        out_specs=pl.BlockSpec((None, tr, C), lambda l, i: (l, i, 0)),
        compiler_params=_params("parallel", "parallel"),
    )(w)


N_PEERS = 7


def _grad_sum(name, g, others):
    _, _, R, C = g.shape
    tr = _row_tile(R, C, 1 << 18)

    def body(g_ref, b_ref, o_ref):
        tot = g_ref[...].astype(F32)
        for s in range(N_PEERS):
            tot = tot + b_ref[s].astype(F32)
        o_ref[...] = tot

    def mine(i):
        return (2 * lax.axis_index("x") + lax.axis_index("y"), lax.axis_index("c"), i, 0)

    return pl.pallas_call(
        body, name=name, out_shape=jax.ShapeDtypeStruct((2, R, C), F32), grid=(R // tr,),
        in_specs=[pl.BlockSpec((None, None, tr, C), mine),
                  pl.BlockSpec((N_PEERS, tr, C), lambda i: (0, i, 0))],
        out_specs=pl.BlockSpec((None, tr, C), lambda i: (lax.axis_index("c"), i, 0)),
        compiler_params=_params("parallel"),
    )(g, others)


def _adamw(name, w, g, m, v):
    R, C = w.shape
    tr = _row_tile(R, C, 1 << 18) if R % 16 == 0 else R
    c1 = 1.0 - ADAM_B1 ** ADAM_STEP
    c2 = 1.0 - ADAM_B2 ** ADAM_STEP

    def body(w_ref, g_ref, m_ref, v_ref, d_ref, nm_ref, nv_ref):
        gv = g_ref[...]
        nm = ADAM_B1 * m_ref[...] + (1.0 - ADAM_B1) * gv
        nv = ADAM_B2 * v_ref[...] + (1.0 - ADAM_B2) * (gv * gv)
        nm_ref[...] = nm
        nv_ref[...] = nv
        d_ref[...] = -ADAM_LR * ((nm / c1) / (jnp.sqrt(nv / c2) + ADAM_EPS) + ADAM_WD * w_ref[...])

    spec = pl.BlockSpec((tr, C), lambda i: (i, 0))
    return pl.pallas_call(
        body, name=name, out_shape=[jax.ShapeDtypeStruct((R, C), F32)] * 3, grid=(R // tr,),
        in_specs=[spec] * 4, out_specs=[spec] * 3, compiler_params=_params("parallel"),
    )(w, g, m, v)


def _coords():
    return lax.axis_index("x"), lax.axis_index("y"), lax.axis_index("c")


def _flip(x, y, c, k):
    return (1 - x if k & 4 else x, 1 - y if k & 2 else y, 1 - c if k & 1 else c)


def _small_exchange(name, buf, reduce):
    rows = buf.shape[0]

    def body(x_ref, o_ref, land, send_sems, recv_sems):
        x, y, c = _coords()
        me = 4 * x + 2 * y + c

        def copy(k, sending):
            px, py, pc = _flip(x, y, c, k)
            slot = me if sending else 4 * px + 2 * py + pc
            return pltpu.make_async_remote_copy(
                src_ref=x_ref, dst_ref=land.at[slot], send_sem=send_sems.at[k - 1], recv_sem=recv_sems.at[k - 1],
                device_id=(px, py, pc), device_id_type=MESH)

        for k in range(1, 8):
            copy(k, True).start()
        land[me] = x_ref[...]
        for k in range(1, 8):
            copy(k, False).wait()
        if reduce:
            acc = land[0]
            for s in range(1, 8):
                acc = acc + land[s]
            o_ref[...] = acc
        else:
            o_ref[...] = land[...]

    out_shape = jax.ShapeDtypeStruct((rows, 128) if reduce else (8, rows, 128), F32)
    return pl.pallas_call(
        body, name=name, out_shape=out_shape,
        in_specs=[pl.BlockSpec(memory_space=pltpu.VMEM)], out_specs=pl.BlockSpec(memory_space=pltpu.VMEM),
        scratch_shapes=[pltpu.VMEM((8, rows, 128), F32), pltpu.SemaphoreType.DMA((7,)), pltpu.SemaphoreType.DMA((7,))],
        compiler_params=pltpu.CompilerParams(vmem_limit_bytes=VMEM_LIMIT),
    )(buf)


def _handshake(peers):
    barrier = pltpu.get_barrier_semaphore()
    for peer in peers:
        pl.semaphore_signal(barrier, inc=1, device_id=peer, device_id_type=MESH)
    pl.semaphore_wait(barrier, len(peers))


def _sequencer_mesh():
    return plsc.ScalarSubcoreMesh(axis_name="sequencer", num_cores=1)


def _allgather_weight(name, shard, collective_id):
    def body(in_ref, out_ref, send_sems, recv_sems, local_sem):
        x, y, c = _coords()
        chip = 2 * x + y
        sib = (x, y, 1 - c)
        chips = [_flip(x, y, c, k) for k in (4, 2, 6)]
        _handshake([sib] + chips)

        def slab(cx, cy, cc):
            return out_ref.at[2 * cx + cy, cc]

        def copy(k, src, dst, to):
            return pltpu.make_async_remote_copy(
                src_ref=src, dst_ref=dst, send_sem=send_sems.at[k], recv_sem=recv_sems.at[k],
                device_id=to, device_id_type=MESH)

        local = pltpu.make_async_copy(in_ref.at[c], out_ref.at[chip, c], local_sem)
        local.start()
        started = []
        for j, to in enumerate(chips):
            started.append(copy(1 + j, in_ref.at[c], slab(x, y, c), to))
        started.append(copy(0, in_ref.at[c], slab(x, y, c), sib))
        for cp in started:
            cp.start()
        for j, (px, py, pc) in enumerate(chips):
            held = slab(px, py, c)
            copy(1 + j, held, held, (px, py, pc)).wait_recv()
            cp = copy(4 + j, held, held, sib)
            cp.start()
            started.append(cp)
        got = slab(x, y, 1 - c)
        copy(0, got, got, sib).wait_recv()
        for j, (px, py, pc) in enumerate(chips):
            got = slab(px, py, 1 - c)
            copy(4 + j, got, got, sib).wait_recv()
        for cp in started:
            cp.wait_send()
        local.wait()

    return pl.kernel(
        body, out_type=jax.ShapeDtypeStruct((N_CHIPS,) + shard.shape, shard.dtype),
        mesh=_sequencer_mesh(), name=name,
        scratch_types=[pltpu.SemaphoreType.DMA((7,)), pltpu.SemaphoreType.DMA((7,)), pltpu.SemaphoreType.DMA],
        compiler_params=pltpu.CompilerParams(collective_id=collective_id),
    )(shard)


def _grad_exchange(name, g, collective_id):
    def body(in_ref, out_ref, send_sems, recv_sems):
        x, y, c = _coords()
        chip = 2 * x + y
        sib = (x, y, 1 - c)
        chips = [_flip(x, y, c, k) for k in (4, 2, 6)]
        _handshake([sib] + [(px, py, cc) for px, py, _ in chips for cc in (0, 1)])
        cps = []
        for j, (px, py, _) in enumerate(chips):
            for cc in (0, 1):
                cps.append(pltpu.make_async_remote_copy(
                    src_ref=in_ref.at[2 * px + py, cc], dst_ref=out_ref.at[1 + 2 * j + c],
                    send_sem=send_sems.at[1 + 2 * j + cc], recv_sem=recv_sems.at[1 + 2 * j + c],
                    device_id=(px, py, cc), device_id_type=MESH))
        cps.append(pltpu.make_async_remote_copy(
            src_ref=in_ref.at[chip, 1 - c], dst_ref=out_ref.at[0], send_sem=send_sems.at[0],
            recv_sem=recv_sems.at[0], device_id=sib, device_id_type=MESH))
        for cp in cps:
            cp.start()
        for cp in cps:
            cp.wait_send()
        for slot in range(N_PEERS):
            pltpu.make_async_remote_copy(
                src_ref=out_ref.at[slot], dst_ref=out_ref.at[slot], send_sem=send_sems.at[slot],
                recv_sem=recv_sems.at[slot], device_id=sib, device_id_type=MESH).wait_recv()

    return pl.kernel(
        body, out_type=jax.ShapeDtypeStruct((N_PEERS,) + g.shape[2:], g.dtype),
        mesh=_sequencer_mesh(), name=name,
        scratch_types=[pltpu.SemaphoreType.DMA((N_PEERS,)), pltpu.SemaphoreType.DMA((N_PEERS,))],
        compiler_params=pltpu.CompilerParams(collective_id=collective_id),
    )(g)


def _pair_fill(name, fulls):
    T = len(fulls)

    def body(*refs):
        outs = refs[T:2 * T]
        send_sems, recv_sems = refs[2 * T:]
        x, y, c = _coords()
        sib = (x, y, 1 - c)
        cps = []
        for t in range(T):
            send = pltpu.make_async_remote_copy(
                src_ref=outs[t].at[c], dst_ref=outs[t].at[c], send_sem=send_sems.at[t],
                recv_sem=recv_sems.at[t], device_id=sib, device_id_type=MESH)
            recv = pltpu.make_async_remote_copy(
                src_ref=outs[t].at[1 - c], dst_ref=outs[t].at[1 - c], send_sem=send_sems.at[t],
                recv_sem=recv_sems.at[t], device_id=sib, device_id_type=MESH)
            send.start()
            cps.append((send, recv))
        for send, recv in cps:
            send.wait_send()
            recv.wait_recv()

    anyspec = pl.BlockSpec(memory_space=pl.ANY)
    return pl.pallas_call(
        body, name=name,
        out_shape=[jax.ShapeDtypeStruct(f.shape, f.dtype) for f in fulls],
        in_specs=[anyspec] * T, out_specs=[anyspec] * T,
        input_output_aliases={t: t for t in range(T)},
        scratch_shapes=[pltpu.SemaphoreType.DMA((T,)), pltpu.SemaphoreType.DMA((T,))],
    )(*fulls)


def _pack(arrs):
    flat = jnp.concatenate([a.reshape(-1).astype(F32) for a in arrs])
    n = flat.shape[0]
    rows = -(-n // 1024) * 8
    return jnp.pad(flat, (0, rows * 128 - n)).reshape(rows, 128)


def _unpack(buf, shapes):
    flat = buf.reshape(-1)
    out, pos = [], 0
    for s in shapes:
        n = math.prod(s)
        out.append(flat[pos:pos + n].reshape(s))
        pos += n
    return out


def kernel(x, mix_norm_g, ffn_norm_g, final_norm_g, sc_w_in, sc_conv_w, sc_conv_b, sc_w_out, attn_w_qkv, attn_w_out, ffn_w_up, ffn_conv_w, ffn_conv_b, ffn_w_down, loss_target, m_mix_norm_g, m_ffn_norm_g, m_final_norm_g, m_sc_w_in, m_sc_conv_w, m_sc_conv_b, m_sc_w_out, m_attn_w_qkv, m_attn_w_out, m_ffn_w_up, m_ffn_conv_w, m_ffn_conv_b, m_ffn_w_down, v_mix_norm_g, v_ffn_norm_g, v_final_norm_g, v_sc_w_in, v_sc_conv_w, v_sc_conv_b, v_sc_w_out, v_attn_w_qkv, v_attn_w_out, v_ffn_w_up, v_ffn_conv_w, v_ffn_conv_b, v_ffn_w_down):
    S, D = x.shape[1], x.shape[2]
    xi, yi, ci = _coords()
    chip = 2 * xi + yi
    x0 = x.reshape(S, D)
    tgt = loss_target.reshape(S, D)

    conv_shapes = [sc_conv_w.shape, ffn_conv_w.shape]
    allc = _small_exchange("gather_conv_w", _pack([sc_conv_w, ffn_conv_w]), reduce=False)
    per_chip = [_unpack(allc[2 * k], conv_shapes) for k in range(N_CHIPS)]
    scw = jnp.concatenate([p[0] for p in per_chip], axis=-1)[0]
    fcw = jnp.concatenate([p[1] for p in per_chip], axis=-1)
    scb = sc_conv_b

    big_names = ["sc_w_in", "sc_w_out", "attn_w_qkv", "attn_w_out", "ffn_w_up", "ffn_w_down"]
    big = dict(zip(big_names, [sc_w_in, sc_w_out, attn_w_qkv, attn_w_out, ffn_w_up, ffn_w_down]))
    n_gathers = [0]

    def gather(tag, w, layers):
        _, R, C = w.shape
        shard = _cast_bf16("cast_" + tag, w, layers).reshape(2, len(layers) * R // 2, C)
        cid = n_gathers[0]
        n_gathers[0] += 1
        return _allgather_weight("allgather_" + tag, shard, cid).reshape(N_CHIPS, len(layers), R, C)

    w_in = gather("sc_w_in", sc_w_in, (0,))
    w_out = gather("sc_w_out", sc_w_out, (0,))
    w_ups = [gather("ffn_w_up0", ffn_w_up, (0,))]
    w_dn = gather("ffn_w_down", ffn_w_down, (0, 1))
    w_qkv = gather("attn_w_qkv", attn_w_qkv, (0,))
    w_ao = gather("attn_w_out", attn_w_out, (0,))
    w_ups.append(gather("ffn_w_up1", ffn_w_up, (1,)))

    def reduce_scatter(tag, g):
        cid = n_gathers[0] + big_names.index(tag)
        others = _grad_exchange("rs_exchange_" + tag, g, cid)
        full = _grad_sum("rs_sum_" + tag, g, others)
        return _pair_fill("rs_pair_fill_" + tag, [full])[0].reshape(big[tag].shape)

    h0 = _rmsnorm_fwd("norm_mix0", x0, mix_norm_g[0:1])[0]
    z = _mm_nn_col("sc_in", h0, w_in, 0)
    y = _sc_fwd("sc_gate", z, scw, scb)
    x1 = _mm_nn_row("sc_out", y, w_out, 0, x0)
    h1 = _rmsnorm_fwd("norm_ffn0", x1, ffn_norm_g[0:1])[0]
    u0 = _mm_nn_col("ffn_up0", h1, w_ups[0], 0)
    f0, v0 = _ffn_fwd("ffn_gate0", u0, fcw[0], ffn_conv_b[0:1])
    x2 = _mm_nn_row("ffn_down0", f0, w_dn, 0, x1)
    h2s = _rmsnorm_fwd("norm_mix1", x2, mix_norm_g[1:2], dilated=True)
    qkvs = [_mm_nn_col(f"attn_qkv{d}", h, w_qkv, 0, col_off=gi * 3 * D, ncols=3 * D)
            for gi, (h, d) in enumerate(zip(h2s, DILATIONS))]
    og, lg = zip(*[_attn_fwd(f"attn_fwd{d}", q, d) for q, d in zip(qkvs, DILATIONS)])
    o32, ob, lse = _attn_combine("attn_combine", list(og), list(lg))
    x3 = _mm_nn_row("attn_out", ob, w_ao, 0, x2)
    h3 = _rmsnorm_fwd("norm_ffn1", x3, ffn_norm_g[1:2])[0]
    u1 = _mm_nn_col("ffn_up1", h3, w_ups[1], 0)
    f1, v1 = _ffn_fwd("ffn_gate1", u1, fcw[1], ffn_conv_b[1:2])
    x4 = _mm_nn_row("ffn_down1", f1, w_dn, 1, x3)

    dx4, dx4b, dg_final, loss_part = _rmsnorm_bwd("loss_norm_bwd", x4, final_norm_g.reshape(1, D), target=tgt)

    def ffn_backward(layer, xin, h, u, v, f, dxo, dxob, gain, g_up, g_dn):
        df = _mm_nt_row(f"ffn_down_dx{layer}", dxob, w_dn, layer)
        g_dn = _mm_tn_row(f"ffn_down_dw{layer}", f, dxob, prev=g_dn, layer=layer)
        du, dwb = _ffn_bwd(f"ffn_gate_bwd{layer}", u, v, df, fcw[layer])
        dh = _mm_nt_col(f"ffn_up_dx{layer}", du, w_ups[layer], 0)
        g_up = _mm_tn_col(f"ffn_up_dw{layer}", h, du, w_ups[layer].shape[3], prev=g_up, layer=layer)
        dxi, dxib, dg = _rmsnorm_bwd(f"norm_ffn_bwd{layer}", xin, gain, dhs=[(dh, 1)], dres=dxo)
        return dxi, dxib, dg, dwb, g_up, g_dn

    dx3, dx3b, dg_ffn1, dwb_ffn1, g_up, g_dn = ffn_backward(1, x3, h3, u1, v1, f1, dx4, dx4b, ffn_norm_g[1:2], None, None)

    big_grads = {}
    do = _mm_nt_row("attn_out_dx", dx3b, w_ao, 0)
    big_grads["attn_w_out"] = reduce_scatter("attn_w_out", _mm_tn_row("attn_out_dw", ob, dx3b))
    dos, lss, dls = _attn_bwd_prep("attn_bwd_prep", do, o32, lse)
    dqkvs = [_attn_bwd(f"attn_bwd{d}", q, a, b, c_, d)
             for q, a, b, c_, d in zip(qkvs, dos, lss, dls, DILATIONS)]
    dh2s = [(_mm_nt_col(f"attn_qkv_dx{d}", dq, w_qkv, 0, col_off=gi * 3 * D), d)
            for gi, (dq, d) in enumerate(zip(dqkvs, DILATIONS))]
    g_qkv = None
    for gi, (h, dq, d) in enumerate(zip(h2s, dqkvs, DILATIONS)):
        g_qkv = _mm_tn_col(f"attn_qkv_dw{d}", h, dq, w_qkv.shape[3], col_off=gi * 3 * D, prev=g_qkv)
    big_grads["attn_w_qkv"] = reduce_scatter("attn_w_qkv", g_qkv)
    dx2, dx2b, dg_mix1 = _rmsnorm_bwd("norm_mix_bwd1", x2, mix_norm_g[1:2], dhs=dh2s, dres=dx3)

    dx1, dx1b, dg_ffn0, dwb_ffn0, g_up, g_dn = ffn_backward(0, x1, h1, u0, v0, f0, dx2, dx2b, ffn_norm_g[0:1], g_up, g_dn)
    big_grads["ffn_w_down"] = reduce_scatter("ffn_w_down", g_dn)
    big_grads["ffn_w_up"] = reduce_scatter("ffn_w_up", g_up)

    dy = _mm_nt_row("sc_out_dx", dx1b, w_out, 0)
    big_grads["sc_w_out"] = reduce_scatter("sc_w_out", _mm_tn_row("sc_out_dw", y, dx1b))
    dz, dwb_sc = _sc_bwd("sc_gate_bwd", z, dy, scw, scb)
    big_grads["sc_w_in"] = reduce_scatter("sc_w_in", _mm_tn_col("sc_in_dw", h0, dz, w_in.shape[3]))
    dh0 = _mm_nt_col("sc_in_dx", dz, w_in, 0)
    dx0, _, dg_mix0 = _rmsnorm_bwd("norm_mix_bwd0", x0, mix_norm_g[0:1], dhs=[(dh0, 1)], dres=dx1)

    dconv_sc = dwb_sc[0:3].reshape(1, 3, D)
    dbias_sc = dwb_sc[3:4]
    dconv_ffn = jnp.stack([dwb_ffn0[0:3], dwb_ffn1[0:3]])
    dbias_ffn = jnp.concatenate([dwb_ffn0[3:4], dwb_ffn1[3:4]], axis=0)
    small_parts = [jnp.concatenate([dg_mix0, dg_mix1], axis=0), jnp.concatenate([dg_ffn0, dg_ffn1], axis=0),
                   dg_final.reshape(D), dconv_sc, dbias_sc, dconv_ffn, dbias_ffn, loss_part[0, 0:1]]
    small_shapes = [a.shape for a in small_parts]
    summed = _unpack(_small_exchange("allreduce_small", _pack(small_parts), reduce=True), small_shapes)
    g_mix, g_ffn, g_final, g_scw_full, g_scb, g_fcw_full, g_fcb, loss = summed
    loss = loss.reshape(())
    cw = sc_conv_w.shape[2]
    g_scw = lax.dynamic_slice_in_dim(g_scw_full, chip * cw, cw, axis=2)
    fw = ffn_conv_w.shape[2]
    g_fcw = lax.dynamic_slice_in_dim(g_fcw_full, chip * fw, fw, axis=2)

    names = ["mix_norm_g", "ffn_norm_g", "final_norm_g", "sc_w_in", "sc_conv_w", "sc_conv_b", "sc_w_out",
             "attn_w_qkv", "attn_w_out", "ffn_w_up", "ffn_conv_w", "ffn_conv_b", "ffn_w_down"]
    ws = dict(zip(names, [mix_norm_g, ffn_norm_g, final_norm_g, sc_w_in, sc_conv_w, sc_conv_b, sc_w_out,
                          attn_w_qkv, attn_w_out, ffn_w_up, ffn_conv_w, ffn_conv_b, ffn_w_down]))
    ms = dict(zip(names, [m_mix_norm_g, m_ffn_norm_g, m_final_norm_g, m_sc_w_in, m_sc_conv_w, m_sc_conv_b, m_sc_w_out,
                          m_attn_w_qkv, m_attn_w_out, m_ffn_w_up, m_ffn_conv_w, m_ffn_conv_b, m_ffn_w_down]))
    vs = dict(zip(names, [v_mix_norm_g, v_ffn_norm_g, v_final_norm_g, v_sc_w_in, v_sc_conv_w, v_sc_conv_b, v_sc_w_out,
                          v_attn_w_qkv, v_attn_w_out, v_ffn_w_up, v_ffn_conv_w, v_ffn_conv_b, v_ffn_w_down]))
    gs = {"mix_norm_g": g_mix, "ffn_norm_g": g_ffn, "final_norm_g": g_final, "sc_conv_w": g_scw,
          "sc_conv_b": g_scb, "ffn_conv_w": g_fcw, "ffn_conv_b": g_fcb}
    gs.update(big_grads)

    deltas, new_m, new_v = {}, {}, {}
    small_names = [n for n in names if n not in big_names]
    packed = [_pack([d[n] for n in small_names]) for d in (ws, gs, ms, vs)]
    outs = _adamw("adamw_small", *packed)
    shapes = [ws[n].shape for n in small_names]
    for res, o in zip((deltas, new_m, new_v), outs):
        res.update(dict(zip(small_names, _unpack(o, shapes))))
    for n in big_names:
        shp = ws[n].shape
        two_d = (shp[0] * shp[1], shp[2])
        outs = _adamw("adamw_" + n, *[d[n].reshape(two_d) for d in (ws, gs, ms, vs)])
        for res, o in zip((deltas, new_m, new_v), outs):
            res[n] = o.reshape(shp)

    return (loss, dx0.reshape(x.shape), *[gs[n] for n in names], *[deltas[n] for n in names],
            *[new_m[n] for n in names], *[new_v[n] for n in names])
```

```python
import functools
import math

import jax
import jax.numpy as jnp
from jax import lax
from jax.experimental import pallas as pl
from jax.experimental.pallas import tpu as pltpu
from jax.experimental.pallas import tpu_sc as plsc

F32 = jnp.float32
BF = jnp.bfloat16
MESH = pl.DeviceIdType.MESH

HEAD_DIM = 128
ATTN_HALF = 64
ATTN_BLOCK = 128
DILATIONS = (1, 4, 16)
STAT_LANES = 128
HALO = 16
NORM_EPS = 1e-5
ALIBI_MAX = 8.0
NEG_INF = -1e30
N_CHIPS = 4
VMEM_LIMIT = 56 * 1024 * 1024

ADAM_LR = 0.001
ADAM_B1 = 0.9
ADAM_B2 = 0.999
ADAM_EPS = 1e-08
ADAM_WD = 0.01
ADAM_STEP = 10


def _pick(n, cands):
    for c in cands:
        if n % c == 0:
            return c
    raise ValueError(f"no tile for {n} in {cands}")


def _row_tile(rows, cols, max_elems=1 << 19):
    for c in (512, 256, 128, 64, 32, 16):
        if rows % c == 0 and c * cols <= max_elems:
            return c
    raise ValueError(f"no row tile for {rows}x{cols}")


def _params(*sem):
    return pltpu.CompilerParams(dimension_semantics=sem, vmem_limit_bytes=VMEM_LIMIT)


def _matmul(name, a, b, out_shape, grid, a_spec, b_spec, o_spec, contract, acc_shape,
            res=None, res_spec=None, prev=None, b2_spec=None):
    nk = grid[2]
    n_b = 1 if b2_spec is None else 2

    def body(*refs):
        refs = list(refs)
        if prev is not None:
            refs.pop(0)
        a_ref, b_ref = refs[0], refs[1]
        res_ref = refs[1 + n_b] if res is not None else None
        o_ref = refs[2 + n_b] if res is not None else refs[1 + n_b]
        acc_ref = refs[-1]
        bv = b_ref[...]
        if bv.ndim == 3:
            bv = bv.reshape(bv.shape[0] * bv.shape[1], bv.shape[2])
        if b2_spec is None:
            part = lax.dot_general(a_ref[...], bv, contract, preferred_element_type=F32)
        else:
            half = a_ref.shape[1] // 2
            part = (lax.dot_general(a_ref[:, :half], bv, contract, preferred_element_type=F32)
                    + lax.dot_general(a_ref[:, half:], refs[2][...], contract, preferred_element_type=F32))

        def finish(total):
            if res_ref is not None:
                total = total + res_ref[...]
            o_ref[...] = total.reshape(o_ref.shape).astype(o_ref.dtype)

        if nk == 1:
            finish(part)
        else:
            k = pl.program_id(2)

            @pl.when(k == 0)
            def _():
                acc_ref[...] = part

            @pl.when(jnp.logical_and(k > 0, k < nk - 1))
            def _():
                acc_ref[...] += part

            @pl.when(k == nk - 1)
            def _():
                finish(acc_ref[...] + part)

    operands, in_specs, aliases = [], [], {}
    if prev is not None:
        operands.append(prev)
        in_specs.append(pl.BlockSpec(memory_space=pl.ANY))
        aliases = {0: 0}
    operands += [a, b]
    in_specs += [a_spec, b_spec]
    if b2_spec is not None:
        operands.append(b)
        in_specs.append(b2_spec)
    if res is not None:
        operands.append(res)
        in_specs.append(res_spec)
    return pl.pallas_call(
        body, name=name, out_shape=out_shape, grid=grid, in_specs=in_specs, out_specs=o_spec,
        scratch_shapes=[pltpu.VMEM(acc_shape if nk > 1 else (8, 128), F32)],
        input_output_aliases=aliases,
        compiler_params=_params("parallel", "parallel", "arbitrary"),
    )(*operands)


NN = (((1,), (0,)), ((), ()))
NT = (((1,), (1,)), ((), ()))
TN = (((0,), (0,)), ((), ()))

_COL_TILES = (1536, 1408, 1024, 768, 512, 384, 256, 128)
_WIDE_TILES = (2816,) + _COL_TILES


def _mm_nn_col(name, a, w, layer, col_off=0, ncols=None, out_dtype=BF):
    M, K = a.shape
    _, _, R, C = w.shape
    assert R == K
    ncols = N_CHIPS * C if ncols is None else ncols
    tn = _pick(math.gcd(C, math.gcd(ncols, col_off) if col_off else ncols), _WIDE_TILES)
    tm = _pick(M, (1024, 512, 256))
    nb, off = C // tn, col_off // tn
    return _matmul(
        name, a, w, jax.ShapeDtypeStruct((M, ncols), out_dtype), (M // tm, ncols // tn, 1),
        pl.BlockSpec((tm, K), lambda i, j, k: (i, 0)),
        pl.BlockSpec((None, None, K, tn), lambda i, j, k: ((j + off) // nb, layer, 0, (j + off) % nb)),
        pl.BlockSpec((tm, tn), lambda i, j, k: (i, j)), NN, (tm, tn))


def _mm_nn_row(name, a, w, layer, res):
    M, K = a.shape
    _, _, R, C = w.shape
    assert N_CHIPS * R == K
    chips_per_step = N_CHIPS if K <= 2048 else 2
    tk = chips_per_step * R
    tm = _pick(M, (1024, 512, 256))
    tn = _pick(C, (1024, 512, 256))
    return _matmul(
        name, a, w, jax.ShapeDtypeStruct((M, C), F32), (M // tm, C // tn, K // tk),
        pl.BlockSpec((tm, tk), lambda i, j, k: (i, k)),
        pl.BlockSpec((chips_per_step, None, R, tn), lambda i, j, k: (k, layer, 0, j)),
        pl.BlockSpec((tm, tn), lambda i, j, k: (i, j)), NN, (tm, tn),
        res=res, res_spec=pl.BlockSpec((tm, tn), lambda i, j, k: (i, j)))


def _mm_nt_col(name, dy, w, layer, col_off=0, out_dtype=F32):
    M, n = dy.shape
    _, _, R, C = w.shape
    tk = _pick(math.gcd(C, math.gcd(n, col_off) if col_off else n), _WIDE_TILES)
    tm = _pick(M, (1024, 512, 256))
    tn = _pick(R, (1024, 512, 256))
    nb, off = C // tk, col_off // tk
    per_step = 2 if (tk <= 1536 and (n // tk) % 2 == 0) else 1

    def w_block(t):
        return pl.BlockSpec((None, None, tn, tk), lambda i, j, k: (
            (per_step * k + t + off) // nb, layer, j, (per_step * k + t + off) % nb))

    return _matmul(
        name, dy, w, jax.ShapeDtypeStruct((M, R), out_dtype), (M // tm, R // tn, n // (per_step * tk)),
        pl.BlockSpec((tm, per_step * tk), lambda i, j, k: (i, k)), w_block(0),
        pl.BlockSpec((tm, tn), lambda i, j, k: (i, j)), NT, (tm, tn),
        b2_spec=w_block(1) if per_step == 2 else None)


def _mm_nt_row(name, dy, w, layer, out_dtype=BF):
    M, C2 = dy.shape
    _, _, R, C = w.shape
    assert C2 == C
    chips_per_tile = N_CHIPS if N_CHIPS * R <= 2048 else 2
    tn = chips_per_tile * R
    tm = _pick(M, (1024, 512, 256))
    return _matmul(
        name, dy, w, jax.ShapeDtypeStruct((M, N_CHIPS * R), out_dtype), (M // tm, N_CHIPS * R // tn, 1),
        pl.BlockSpec((tm, C), lambda i, j, k: (i, 0)),
        pl.BlockSpec((chips_per_tile, None, R, C), lambda i, j, k: (j, layer, 0, 0)),
        pl.BlockSpec((tm, tn), lambda i, j, k: (i, j)), NT, (tm, tn))


_TN_DEPTH = (2048, 1024, 512, 256)


def _half_index(rows, layer, tkx):
    if layer is None:
        hb = rows // 2 // tkx
        return rows // 2, lambda i: (i // hb, i % hb)
    return rows, lambda i: (layer, i)


def _mm_tn_col(name, xa, dy, C, col_off=0, prev=None, layer=None):
    M, K = xa.shape
    _, n = dy.shape
    tn = _pick(math.gcd(C, math.gcd(n, col_off) if col_off else n), _WIDE_TILES)
    tkx = _pick(K // 2 if layer is None else K, (1024, 512, 256, 128) if tn <= 1536 else (512, 256, 128))
    tmr = _pick(M, _TN_DEPTH)
    rh, split = _half_index(K, layer, tkx)
    nb, off = C // tn, col_off // tn
    return _matmul(
        name, xa, dy, jax.ShapeDtypeStruct((N_CHIPS, 2, rh, C), BF), (K // tkx, n // tn, M // tmr),
        pl.BlockSpec((tmr, tkx), lambda i, j, k: (k, i)),
        pl.BlockSpec((tmr, tn), lambda i, j, k: (k, j)),
        pl.BlockSpec((None, None, tkx, tn), lambda i, j, k: ((j + off) // nb, *split(i), (j + off) % nb)),
        TN, (tkx, tn), prev=prev)


def _mm_tn_row(name, xa, dy, prev=None, layer=None):
    M, K = xa.shape
    _, C = dy.shape
    R = K // N_CHIPS
    tmr = _pick(M, _TN_DEPTH)
    if layer is None and prev is None and K <= 2048:
        tn = _pick(C, (1024, 512, 256))
        return _matmul(
            name, xa, dy, jax.ShapeDtypeStruct((N_CHIPS, 2, R // 2, C), BF), (1, C // tn, M // tmr),
            pl.BlockSpec((tmr, K), lambda i, j, k: (k, 0)),
            pl.BlockSpec((tmr, tn), lambda i, j, k: (k, j)),
            pl.BlockSpec((N_CHIPS, 2, R // 2, tn), lambda i, j, k: (0, 0, 0, j)),
            TN, (K, tn))
    tkx = _pick(R // 2 if layer is None else R, (1408, 1024, 512, 256, 128))
    tn = _pick(C, (2048, 1024, 512, 256) if tkx <= 512 else (1024, 512, 256))
    rh, split = _half_index(R, layer, tkx)
    rb = R // tkx
    return _matmul(
        name, xa, dy, jax.ShapeDtypeStruct((N_CHIPS, 2, rh, C), BF), (K // tkx, C // tn, M // tmr),
        pl.BlockSpec((tmr, tkx), lambda i, j, k: (k, i)),
        pl.BlockSpec((tmr, tn), lambda i, j, k: (k, j)),
        pl.BlockSpec((None, None, tkx, tn), lambda i, j, k: (i // rb, *split(i % rb), j)),
        TN, (tkx, tn), prev=prev)


NORM_ROWS = 256
LANES = 128


def _chunk_scratch(tm, width):
    return pltpu.VMEM((width // LANES, tm, LANES), F32)


def _store_chunks(scr, value):
    for c in range(scr.shape[0]):
        scr[c] = value[:, c * LANES:(c + 1) * LANES]


def _load_chunks(scr):
    return jnp.concatenate([scr[c] for c in range(scr.shape[0])], axis=1)


def _to_residue_major(scr, o_ref, d, dtype):
    tm = scr.shape[1]
    for c in range(scr.shape[0]):
        for res in range(d):
            o_ref[res, :, c * LANES:(c + 1) * LANES] = scr[c, pl.ds(res, tm // d, stride=d), :].astype(dtype)


def _to_natural(scr, ref, d):
    tm = scr.shape[1]
    for c in range(scr.shape[0]):
        for res in range(d):
            scr[c, pl.ds(res, tm // d, stride=d), :] = ref[res, :, c * LANES:(c + 1) * LANES].astype(F32)


def _rmsnorm_fwd(name, x, g, dilated=False):
    S, D = x.shape
    tm = NORM_ROWS
    dils = DILATIONS[1:] if dilated else ()

    def body(x_ref, g_ref, h_ref, *rest):
        xv = x_ref[...]
        r = lax.rsqrt(jnp.mean(xv * xv, axis=1, keepdims=True) + NORM_EPS)
        h = xv * r * g_ref[...]
        h_ref[...] = h.astype(BF)
        if dils:
            scr = rest[-1]
            _store_chunks(scr, h)
            for o_ref, d in zip(rest[:-1], dils):
                _to_residue_major(scr, o_ref, d, BF)

    out_shape = [jax.ShapeDtypeStruct((S, D), BF)]
    out_specs = [pl.BlockSpec((tm, D), lambda i: (i, 0))]
    for d in dils:
        out_shape.append(jax.ShapeDtypeStruct((d, S // d, D), BF))
        out_specs.append(pl.BlockSpec((d, tm // d, D), lambda i: (0, i, 0)))
    outs = pl.pallas_call(
        body, name=name, out_shape=out_shape, grid=(S // tm,),
        in_specs=[pl.BlockSpec((tm, D), lambda i: (i, 0)), pl.BlockSpec((1, D), lambda i: (0, 0))],
        out_specs=out_specs,
        scratch_shapes=[_chunk_scratch(tm, D)] if dils else [],
        compiler_params=_params("parallel"),
    )(x, g)
    return [outs[0]] + [o.reshape(S, D) for o in outs[1:]]


def _rmsnorm_bwd(name, x, g, dhs=(), dres=None, target=None):
    S, D = x.shape
    tm = NORM_ROWS
    n_dh = len(dhs)

    def body(*refs):
        refs = list(refs)
        x_ref, g_ref = refs[0], refs[1]
        dh_refs = refs[2:2 + n_dh]
        pos = 2 + n_dh
        dres_ref = tgt_ref = None
        if dres is not None:
            dres_ref = refs[pos]
            pos += 1
        if target is not None:
            tgt_ref = refs[pos]
            pos += 1
        dx_ref, dxb_ref, dg_ref = refs[pos:pos + 3]
        pos += 3
        loss_ref = None
        if target is not None:
            loss_ref = refs[pos]
            pos += 1
        scr = refs[pos] if any(d > 1 for _, d in dhs) else None
        i = pl.program_id(0)

        xv = x_ref[...]
        gv = g_ref[...]
        r = lax.rsqrt(jnp.mean(xv * xv, axis=1, keepdims=True) + NORM_EPS)
        xhat = xv * r
        if target is not None:
            err = xhat * gv - tgt_ref[...]
            dh = err * (1.0 / D)
            part = jnp.sum(jnp.sum(err * err, axis=1, keepdims=True), axis=0, keepdims=True) * (0.5 / D)
        else:
            dh = None
            for ref, d in zip(dh_refs, [d for _, d in dhs]):
                if d == 1:
                    v = ref[...]
                else:
                    _to_natural(scr, ref, d)
                    v = _load_chunks(scr)
                dh = v if dh is None else dh + v
        dxhat = dh * gv
        dx = r * (dxhat - xhat * jnp.mean(dxhat * xhat, axis=1, keepdims=True))
        if dres_ref is not None:
            dx = dx + dres_ref[...]
        dx_ref[...] = dx
        dxb_ref[...] = dx.astype(BF)
        dg = jnp.sum(dh * xhat, axis=0, keepdims=True)

        @pl.when(i == 0)
        def _():
            dg_ref[...] = dg
            if loss_ref is not None:
                loss_ref[...] = jnp.broadcast_to(part, loss_ref.shape)

        @pl.when(i > 0)
        def _():
            dg_ref[...] += dg
            if loss_ref is not None:
                loss_ref[...] += jnp.broadcast_to(part, loss_ref.shape)

    row = pl.BlockSpec((tm, D), lambda i: (i, 0))
    operands = [x, g]
    in_specs = [row, pl.BlockSpec((1, D), lambda i: (0, 0))]
    for arr, d in dhs:
        if d == 1:
            operands.append(arr)
            in_specs.append(row)
        else:
            operands.append(arr.reshape(d, S // d, D))
            in_specs.append(pl.BlockSpec((d, tm // d, D), lambda i: (0, i, 0)))
    if dres is not None:
        operands.append(dres)
        in_specs.append(row)
    if target is not None:
        operands.append(target)
        in_specs.append(row)
    out_shape = [jax.ShapeDtypeStruct((S, D), F32), jax.ShapeDtypeStruct((S, D), BF),
                 jax.ShapeDtypeStruct((1, D), F32)]
    out_specs = [row, row, pl.BlockSpec((1, D), lambda i: (0, 0))]
    if target is not None:
        out_shape.append(jax.ShapeDtypeStruct((1, STAT_LANES), F32))
        out_specs.append(pl.BlockSpec((1, STAT_LANES), lambda i: (0, 0)))
    scratch = [_chunk_scratch(tm, D)] if any(d > 1 for _, d in dhs) else []
    return pl.pallas_call(
        body, name=name, out_shape=out_shape, grid=(S // tm,), in_specs=in_specs, out_specs=out_specs,
        scratch_shapes=scratch, compiler_params=_params("arbitrary"),
    )(*operands)


CONV_ROWS = 256
CONV_COLS = 256


def _halo_specs(S, tm, width):
    nh = S // HALO
    per = tm // HALO
    cur = pl.BlockSpec((tm, width), lambda i: (i, 0))
    prev = pl.BlockSpec((HALO, width), lambda i: (jnp.maximum(i * per - 1, 0), 0))
    nxt = pl.BlockSpec((HALO, width), lambda i: (jnp.minimum((i + 1) * per, nh - 1), 0))
    return [cur, prev, nxt]


def _ext(refs, cs, inr):
    cur, prev, nxt = refs
    v = jnp.concatenate([prev[:, cs], cur[:, cs], nxt[:, cs]], axis=0).astype(F32)
    return v if inr is None else jnp.where(inr, v, 0.0)


def _shift_prev(v):
    return pltpu.roll(v, 1, 0)


def _shift_next(v):
    return pltpu.roll(v, v.shape[0] - 1, 0)


def _shifts(v):
    return _shift_prev(v), _shift_next(v)


def _conv3(v, w, cs, b=None, shifted=None):
    vp, vn = _shifts(v) if shifted is None else shifted
    out = w[0:1, cs] * vp + w[1:2, cs] * v + w[2:3, cs] * vn
    return out if b is None else out + b[:, cs]


def _in_range(i, tm, tc, S):
    row = lax.broadcasted_iota(jnp.int32, (tm + 2 * HALO, tc), 0) + (i * tm - HALO)
    return jnp.logical_and(row >= 0, row < S)


def _core(v, tm):
    return v[HALO:HALO + tm, :]


def _acc_rows(ref, i, rows):
    for r, cs, val in rows:
        ref[r:r + 1, cs] += val


def _zero_first(ref, i):
    @pl.when(i == 0)
    def _():
        ref[...] = jnp.zeros(ref.shape, ref.dtype)


def _sc_fwd(name, z, w, b):
    S, D3 = z.shape
    D = D3 // 3
    tm, tc = CONV_ROWS, _pick(D, (CONV_COLS, 256, 128))

    def body(zc, zp, zn, w_ref, b_ref, y_ref):
        i = pl.program_id(0)
        inr = _in_range(i, tm, tc, S)
        zr = (zc, zp, zn)
        for c in range(D // tc):
            cs = slice(c * tc, (c + 1) * tc)
            u = _ext(zr, cs, inr)
            gc = _ext(zr, slice(2 * D + c * tc, 2 * D + (c + 1) * tc), None)
            conv = _conv3(gc * u, w_ref, cs, b_ref)
            gb = zc[:, D + c * tc:D + (c + 1) * tc].astype(F32)
            y_ref[:, cs] = (gb * _core(conv, tm)).astype(BF)

    return pl.pallas_call(
        body, name=name, out_shape=jax.ShapeDtypeStruct((S, D), BF), grid=(S // tm,),
        in_specs=_halo_specs(S, tm, D3) + [pl.BlockSpec((3, D), lambda i: (0, 0)),
                                           pl.BlockSpec((1, D), lambda i: (0, 0))],
        out_specs=pl.BlockSpec((tm, D), lambda i: (i, 0)),
        compiler_params=_params("parallel"),
    )(z, z, z, w, b)


def _sc_bwd(name, z, dy, w, b):
    S, D3 = z.shape
    D = D3 // 3
    tm, tc = CONV_ROWS, _pick(D, (CONV_COLS, 256, 128))

    def body(zc, zp, zn, dc_, dp_, dn_, w_ref, b_ref, dz_ref, dwb_ref):
        i = pl.program_id(0)
        inr = _in_range(i, tm, tc, S)
        _zero_first(dwb_ref, i)
        zr, dr = (zc, zp, zn), (dc_, dp_, dn_)
        for c in range(D // tc):
            cs = slice(c * tc, (c + 1) * tc)
            u = _ext(zr, cs, inr)
            gb = _ext(zr, slice(D + c * tc, D + (c + 1) * tc), None)
            gc = _ext(zr, slice(2 * D + c * tc, 2 * D + (c + 1) * tc), None)
            dyv = _ext(dr, cs, inr)
            p = gc * u
            p_prev, p_next = _shifts(p)
            conv = _conv3(p, w_ref, cs, b_ref, shifted=(p_prev, p_next))
            dconv = dyv * gb
            dp = w_ref[0:1, cs] * _shift_next(dconv) + w_ref[1:2, cs] * dconv + w_ref[2:3, cs] * _shift_prev(dconv)
            dz_ref[:, cs] = _core(dp * gc, tm).astype(BF)
            dz_ref[:, D + c * tc:D + (c + 1) * tc] = _core(dyv * conv, tm).astype(BF)
            dz_ref[:, 2 * D + c * tc:2 * D + (c + 1) * tc] = _core(dp * u, tm).astype(BF)
            dcc = _core(dconv, tm)
            _acc_rows(dwb_ref, i, [
                (0, cs, jnp.sum(dcc * _core(p_prev, tm), axis=0, keepdims=True)),
                (1, cs, jnp.sum(dcc * _core(p, tm), axis=0, keepdims=True)),
                (2, cs, jnp.sum(dcc * _core(p_next, tm), axis=0, keepdims=True)),
                (3, cs, jnp.sum(dcc, axis=0, keepdims=True))])

    return pl.pallas_call(
        body, name=name,
        out_shape=[jax.ShapeDtypeStruct((S, D3), BF), jax.ShapeDtypeStruct((4, D), F32)], grid=(S // tm,),
        in_specs=_halo_specs(S, tm, D3) + _halo_specs(S, tm, D) + [
            pl.BlockSpec((3, D), lambda i: (0, 0)), pl.BlockSpec((1, D), lambda i: (0, 0))],
        out_specs=[pl.BlockSpec((tm, D3), lambda i: (i, 0)), pl.BlockSpec((4, D), lambda i: (0, 0))],
        compiler_params=_params("arbitrary"),
    )(z, z, z, dy, dy, dy, w, b)


def _sigmoid(v):
    return 1.0 / (1.0 + jnp.exp(-v))


def _ffn_fwd(name, u, w, b):
    S, F2 = u.shape
    Fh = F2 // 2
    tm, tc = CONV_ROWS, _pick(Fh, (CONV_COLS, 256, 128))

    def body(uc, up, un, w_ref, b_ref, f_ref, v_ref):
        i = pl.program_id(0)
        inr = _in_range(i, tm, tc, S)
        ur = (uc, up, un)
        for c in range(Fh // tc):
            ca = slice(c * tc, (c + 1) * tc)
            cb = slice(Fh + c * tc, Fh + (c + 1) * tc)
            va = _core(_conv3(_ext(ur, ca, inr), w_ref, ca, b_ref), tm)
            vb = _core(_conv3(_ext(ur, cb, inr), w_ref, cb, b_ref), tm)
            v_ref[:, ca] = va.astype(BF)
            v_ref[:, cb] = vb.astype(BF)
            f_ref[:, ca] = (va * _sigmoid(va) * vb).astype(BF)

    return pl.pallas_call(
        body, name=name,
        out_shape=[jax.ShapeDtypeStruct((S, Fh), BF), jax.ShapeDtypeStruct((S, F2), BF)], grid=(S // tm,),
        in_specs=_halo_specs(S, tm, F2) + [pl.BlockSpec((3, F2), lambda i: (0, 0)),
                                           pl.BlockSpec((1, F2), lambda i: (0, 0))],
        out_specs=[pl.BlockSpec((tm, Fh), lambda i: (i, 0)), pl.BlockSpec((tm, F2), lambda i: (i, 0))],
        compiler_params=_params("parallel"),
    )(u, u, u, w, b)


def _ffn_bwd(name, u, v, df, w):
    S, F2 = u.shape
    Fh = F2 // 2
    tm, tc = CONV_ROWS, _pick(Fh, (CONV_COLS, 256, 128))

    def body(u_ref, vc, vp, vn, dc_, dp_, dn_, w_ref, du_ref, dwb_ref):
        i = pl.program_id(0)
        inr = _in_range(i, tm, tc, S)
        _zero_first(dwb_ref, i)
        vr, dr = (vc, vp, vn), (dc_, dp_, dn_)
        for c in range(Fh // tc):
            ca = slice(c * tc, (c + 1) * tc)
            cb = slice(Fh + c * tc, Fh + (c + 1) * tc)
            va, vb = _ext(vr, ca, None), _ext(vr, cb, None)
            dfv = _ext(dr, ca, inr)
            sg = _sigmoid(va)
            dva = dfv * vb * (sg * (1.0 + va * (1.0 - sg)))
            dvb = dfv * (va * sg)
            rows = []
            for cs, dv in ((ca, dva), (cb, dvb)):
                dv_prev, dv_next = _shifts(dv)
                dcore = w_ref[0:1, cs] * dv_next + w_ref[1:2, cs] * dv + w_ref[2:3, cs] * dv_prev
                du_ref[:, cs] = _core(dcore, tm).astype(BF)
                uu = u_ref[:, cs].astype(F32)
                dvc = _core(dv, tm)
                rows += [
                    (0, cs, jnp.sum(_core(dv_next, tm) * uu, axis=0, keepdims=True)),
                    (1, cs, jnp.sum(dvc * uu, axis=0, keepdims=True)),
                    (2, cs, jnp.sum(_core(dv_prev, tm) * uu, axis=0, keepdims=True)),
                    (3, cs, jnp.sum(dvc, axis=0, keepdims=True))]
            _acc_rows(dwb_ref, i, rows)

    return pl.pallas_call(
        body, name=name,
        out_shape=[jax.ShapeDtypeStruct((S, F2), BF), jax.ShapeDtypeStruct((4, F2), F32)], grid=(S // tm,),
        in_specs=[pl.BlockSpec((tm, F2), lambda i: (i, 0))] + _halo_specs(S, tm, F2) + _halo_specs(S, tm, Fh) + [
            pl.BlockSpec((3, F2), lambda i: (0, 0))],
        out_specs=[pl.BlockSpec((tm, F2), lambda i: (i, 0)), pl.BlockSpec((4, F2), lambda i: (0, 0))],
        compiler_params=_params("arbitrary"),
    )(u, v, v, v, df, df, df, w)


def _alibi_slopes(H):
    return [2.0 ** (-ALIBI_MAX * (h + 1) / H) for h in range(H)]


def _window_specs(S, width, col):
    n64 = S // ATTN_HALF
    cur = pl.BlockSpec((ATTN_BLOCK, width), lambda b: (b, col))
    prev = pl.BlockSpec((ATTN_HALF, width), lambda b: (jnp.maximum(2 * b - 1, 0), col))
    nxt = pl.BlockSpec((ATTN_HALF, width), lambda b: (jnp.minimum(2 * b + 2, n64 - 1), col))
    return [cur, prev, nxt]


def _fill_window(buf, cur, prev, nxt):
    buf[0:ATTN_HALF] = prev[...]
    buf[ATTN_HALF:ATTN_HALF + ATTN_BLOCK] = cur[...]
    buf[ATTN_HALF + ATTN_BLOCK:2 * ATTN_BLOCK] = nxt[...]


def _band(b, L):
    QB, W = ATTN_BLOCK, 2 * ATTN_BLOCK
    a_loc = (b * QB) % L
    row = lax.broadcasted_iota(jnp.int32, (QB, W), 0)
    col = lax.broadcasted_iota(jnp.int32, (QB, W), 1)
    dist = jnp.abs(col - ATTN_HALF - row)
    other = a_loc - ATTN_HALF + col
    valid = jnp.logical_and(dist <= ATTN_HALF, jnp.logical_and(other >= 0, other < L))
    return dist.astype(F32), valid


def _lane_col(stats, h):
    lane = lax.broadcasted_iota(jnp.int32, stats.shape, 1)
    return jnp.sum(jnp.where(lane == h, stats, 0.0), axis=1, keepdims=True)


def _attn_fwd(name, qkv, d):
    S, D3 = qkv.shape
    D = D3 // 3
    H = D // HEAD_DIM
    L = S // d
    scale = HEAD_DIM ** -0.5
    slopes = _alibi_slopes(H)

    def body(q_ref, kc, kp, kn, vc, vp, vn, o_ref, lse_ref, kbuf, vbuf):
        b = pl.program_id(0)
        _fill_window(kbuf, kc, kp, kn)
        _fill_window(vbuf, vc, vp, vn)
        dist, valid = _band(b, L)
        dist = dist * float(d)
        lane = lax.broadcasted_iota(jnp.int32, (ATTN_BLOCK, STAT_LANES), 1)
        lse = jnp.zeros((ATTN_BLOCK, STAT_LANES), F32)
        for h in range(H):
            cs = slice(h * HEAD_DIM, (h + 1) * HEAD_DIM)
            s = lax.dot_general(q_ref[:, cs], kbuf[:, cs], NT, preferred_element_type=F32) * scale
            s = jnp.where(valid, s - slopes[h] * dist, NEG_INF)
            m = jnp.max(s, axis=1, keepdims=True)
            p = jnp.exp(s - m)
            den = jnp.sum(p, axis=1, keepdims=True)
            o = jnp.dot(p.astype(BF), vbuf[:, cs], preferred_element_type=F32)
            o_ref[:, cs] = o / den
            lse = jnp.where(lane == h, m + jnp.log(den), lse)
        lse_ref[...] = lse

    return pl.pallas_call(
        body, name=name,
        out_shape=[jax.ShapeDtypeStruct((S, D), F32), jax.ShapeDtypeStruct((S, STAT_LANES), F32)],
        grid=(S // ATTN_BLOCK,),
        in_specs=[pl.BlockSpec((ATTN_BLOCK, D), lambda b: (b, 0))] + _window_specs(S, D, 1) + _window_specs(S, D, 2),
        out_specs=[pl.BlockSpec((ATTN_BLOCK, D), lambda b: (b, 0)),
                   pl.BlockSpec((ATTN_BLOCK, STAT_LANES), lambda b: (b, 0))],
        scratch_shapes=[pltpu.VMEM((2 * ATTN_BLOCK, D), BF), pltpu.VMEM((2 * ATTN_BLOCK, D), BF)],
        compiler_params=_params("parallel"),
    )(qkv, qkv, qkv, qkv, qkv, qkv, qkv)


def _dil_specs(S, tm, width):
    specs = [pl.BlockSpec((tm, width), lambda i: (i, 0))]
    for d in DILATIONS[1:]:
        specs.append(pl.BlockSpec((d, tm // d, width), lambda i: (0, i, 0)))
    return specs


def _attn_combine(name, outs, lses):
    S, D = outs[0].shape
    H = D // HEAD_DIM
    tm = NORM_ROWS

    def body(o1, o4, o16, l1, l4, l16, o_ref, ob_ref, lse_ref, oscr, lscr):
        ls = [l1[...]]
        for ref, d in zip((l4, l16), DILATIONS[1:]):
            _to_natural(lscr, ref, d)
            ls.append(lscr[0])
        top = jnp.maximum(jnp.maximum(ls[0], ls[1]), ls[2])
        es = [jnp.exp(l - top) for l in ls]
        tot = es[0] + es[1] + es[2]
        lse_ref[...] = top + jnp.log(tot)
        ws = [e / tot for e in es]
        for gi, (ref, d) in enumerate(zip((o1, o4, o16), DILATIONS)):
            if d > 1:
                _to_natural(oscr, ref, d)
            for h in range(H):
                cs = slice(h * HEAD_DIM, (h + 1) * HEAD_DIM)
                term = _lane_col(ws[gi], h) * (ref[:, cs] if d == 1 else oscr[h])
                if gi == 0:
                    o_ref[:, cs] = term
                else:
                    o_ref[:, cs] += term
        ob_ref[...] = o_ref[...].astype(BF)

    outs3 = [outs[0]] + [o.reshape(d, S // d, D) for o, d in zip(outs[1:], DILATIONS[1:])]
    lses3 = [lses[0]] + [l.reshape(d, S // d, STAT_LANES) for l, d in zip(lses[1:], DILATIONS[1:])]
    row = pl.BlockSpec((tm, D), lambda i: (i, 0))
    return pl.pallas_call(
        body, name=name,
        out_shape=[jax.ShapeDtypeStruct((S, D), F32), jax.ShapeDtypeStruct((S, D), BF),
                   jax.ShapeDtypeStruct((S, STAT_LANES), F32)],
        grid=(S // tm,),
        in_specs=_dil_specs(S, tm, D) + _dil_specs(S, tm, STAT_LANES),
        out_specs=[row, row, pl.BlockSpec((tm, STAT_LANES), lambda i: (i, 0))],
        scratch_shapes=[_chunk_scratch(tm, D), _chunk_scratch(tm, STAT_LANES)],
        compiler_params=_params("parallel"),
    )(*outs3, *lses3)


def _attn_bwd_prep(name, do, o32, lse):
    S, D = do.shape
    H = D // HEAD_DIM
    tm = NORM_ROWS
    dils = DILATIONS[1:]

    def body(do_ref, o_ref, lse_ref, dl_ref, do4, do16, l4, l16, d4, d16, scr, sscr):
        lane = lax.broadcasted_iota(jnp.int32, (tm, STAT_LANES), 1)
        delta = jnp.zeros((tm, STAT_LANES), F32)
        for h in range(H):
            cs = slice(h * HEAD_DIM, (h + 1) * HEAD_DIM)
            dov = do_ref[:, cs].astype(F32)
            scr[h] = dov
            delta = jnp.where(lane == h, jnp.sum(dov * o_ref[:, cs], axis=1, keepdims=True), delta)
        dl_ref[...] = delta
        for ref, d in zip((do4, do16), dils):
            _to_residue_major(scr, ref, d, BF)
        for val, refs in ((lse_ref[...], (l4, l16)), (delta, (d4, d16))):
            sscr[0] = val
            for ref, d in zip(refs, dils):
                _to_residue_major(sscr, ref, d, F32)

    def perm_shapes(width, dt):
        return [jax.ShapeDtypeStruct((d, S // d, width), dt) for d in dils]

    def perm_specs(width):
        return [pl.BlockSpec((d, tm // d, width), lambda i: (0, i, 0)) for d in dils]

    row = lambda w: pl.BlockSpec((tm, w), lambda i: (i, 0))
    outs = pl.pallas_call(
        body, name=name,
        out_shape=[jax.ShapeDtypeStruct((S, STAT_LANES), F32)]
        + perm_shapes(D, BF) + perm_shapes(STAT_LANES, F32) + perm_shapes(STAT_LANES, F32),
        grid=(S // tm,),
        in_specs=[row(D), row(D), row(STAT_LANES)],
        out_specs=[row(STAT_LANES)] + perm_specs(D) + perm_specs(STAT_LANES) + perm_specs(STAT_LANES),
        scratch_shapes=[_chunk_scratch(tm, D), _chunk_scratch(tm, STAT_LANES)],
        compiler_params=_params("parallel"),
    )(do, o32, lse)
    dos = [do] + [a.reshape(S, D) for a in outs[1:3]]
    lss = [lse] + [a.reshape(S, STAT_LANES) for a in outs[3:5]]
    dls = [outs[0]] + [a.reshape(S, STAT_LANES) for a in outs[5:7]]
    return dos, lss, dls


def _attn_bwd(name, qkv, do, lse, delta, d):
    S, D3 = qkv.shape
    D = D3 // 3
    H = D // HEAD_DIM
    L = S // d
    scale = HEAD_DIM ** -0.5
    slopes = _alibi_slopes(H)
    QB = ATTN_BLOCK

    def body(qc, qp, qn, kc, kp, kn, vc, vp, vn, dc_, dp_, dn_, lc, lp, ln, ec, ep, en,
             out_ref, qbuf, kbuf, vbuf, dobuf, lbuf, ebuf):
        b = pl.program_id(0)
        for buf, trio in ((qbuf, (qc, qp, qn)), (kbuf, (kc, kp, kn)), (vbuf, (vc, vp, vn)),
                          (dobuf, (dc_, dp_, dn_)), (lbuf, (lc, lp, ln)), (ebuf, (ec, ep, en))):
            _fill_window(buf, *trio)
        dist, valid = _band(b, L)
        dist = dist * float(d)
        lse_c, del_c = lc[...], ec[...]
        lse_w, del_w = lbuf[...].T, ebuf[...].T
        for h in range(H):
            cs = slice(h * HEAD_DIM, (h + 1) * HEAD_DIM)
            bias = slopes[h] * dist
            q, do_h = qc[:, cs], dc_[:, cs]
            kw, vw = kbuf[:, cs], vbuf[:, cs]
            s = lax.dot_general(q, kw, NT, preferred_element_type=F32) * scale - bias
            p = jnp.where(valid, jnp.exp(s - _lane_col(lse_c, h)), 0.0)
            dp = lax.dot_general(do_h, vw, NT, preferred_element_type=F32)
            ds = p * (dp - _lane_col(del_c, h))
            dq = jnp.dot(ds.astype(BF), kw, preferred_element_type=F32) * scale
            out_ref[:, cs] = dq.astype(BF)
            qw, dow = qbuf[:, cs], dobuf[:, cs]
            k, v = kc[:, cs], vc[:, cs]
            st = lax.dot_general(k, qw, NT, preferred_element_type=F32) * scale - bias
            pt = jnp.where(valid, jnp.exp(st - lse_w[h:h + 1, :]), 0.0)
            dv = jnp.dot(pt.astype(BF), dow, preferred_element_type=F32)
            dpt = lax.dot_general(v, dow, NT, preferred_element_type=F32)
            dst = pt * (dpt - del_w[h:h + 1, :])
            dk = jnp.dot(dst.astype(BF), qw, preferred_element_type=F32) * scale
            out_ref[:, D + h * HEAD_DIM:D + (h + 1) * HEAD_DIM] = dk.astype(BF)
            out_ref[:, 2 * D + h * HEAD_DIM:2 * D + (h + 1) * HEAD_DIM] = dv.astype(BF)

    W = 2 * QB
    return pl.pallas_call(
        body, name=name, out_shape=jax.ShapeDtypeStruct((S, D3), BF), grid=(S // QB,),
        in_specs=(_window_specs(S, D, 0) + _window_specs(S, D, 1) + _window_specs(S, D, 2)
                  + _window_specs(S, D, 0) + _window_specs(S, STAT_LANES, 0) + _window_specs(S, STAT_LANES, 0)),
        out_specs=pl.BlockSpec((QB, D3), lambda b: (b, 0)),
        scratch_shapes=[pltpu.VMEM((W, D), BF), pltpu.VMEM((W, D), BF), pltpu.VMEM((W, D), BF),
                        pltpu.VMEM((W, D), BF), pltpu.VMEM((W, STAT_LANES), F32), pltpu.VMEM((W, STAT_LANES), F32)],
        compiler_params=_params("parallel"),
    )(qkv, qkv, qkv, qkv, qkv, qkv, qkv, qkv, qkv, do, do, do, lse, lse, lse, delta, delta, delta)


def _cast_bf16(name, w, layers):
    _, R, C = w.shape
    tr = _row_tile(R, C)
    first = layers[0]

    def body(w_ref, o_ref):
        o_ref[...] = w_ref[...].astype(BF)

    return pl.pallas_call(
        body, name=name, out_shape=jax.ShapeDtypeStruct((len(layers), R, C), BF), grid=(len(layers), R // tr),
        in_specs=[pl.BlockSpec((None, tr, C), lambda l, i: (first + l, i, 0))],
        out_specs=pl.BlockSpec((None, tr, C), lambda l, i: (l, i, 0)),
        compiler_params=_params("parallel", "parallel"),
    )(w)


N_PEERS = 7


def _grad_sum(name, g, others):
    _, _, R, C = g.shape
    tr = _row_tile(R, C, 1 << 18)

    def body(g_ref, b_ref, o_ref):
        tot = g_ref[...].astype(F32)
        for s in range(N_PEERS):
            tot = tot + b_ref[s].astype(F32)
        o_ref[...] = tot

    def mine(i):
        return (2 * lax.axis_index("x") + lax.axis_index("y"), lax.axis_index("c"), i, 0)

    return pl.pallas_call(
        body, name=name, out_shape=jax.ShapeDtypeStruct((2, R, C), F32), grid=(R // tr,),
        in_specs=[pl.BlockSpec((None, None, tr, C), mine),
                  pl.BlockSpec((N_PEERS, tr, C), lambda i: (0, i, 0))],
        out_specs=pl.BlockSpec((None, tr, C), lambda i: (lax.axis_index("c"), i, 0)),
        compiler_params=_params("parallel"),
    )(g, others)


def _adamw(name, w, g, m, v, copy_g=False):
    R, C = w.shape
    tr = _row_tile(R, C, 1 << 18) if R % 16 == 0 else R
    c1 = 1.0 - ADAM_B1 ** ADAM_STEP
    c2 = 1.0 - ADAM_B2 ** ADAM_STEP

    def body(w_ref, g_ref, m_ref, v_ref, d_ref, nm_ref, nv_ref, *g_out):
        gv = g_ref[...]
        nm = ADAM_B1 * m_ref[...] + (1.0 - ADAM_B1) * gv
        nv = ADAM_B2 * v_ref[...] + (1.0 - ADAM_B2) * (gv * gv)
        nm_ref[...] = nm
        nv_ref[...] = nv
        d_ref[...] = -ADAM_LR * ((nm / c1) / (jnp.sqrt(nv / c2) + ADAM_EPS) + ADAM_WD * w_ref[...])
        if copy_g:
            g_out[0][...] = gv

    spec = pl.BlockSpec((tr, C), lambda i: (i, 0))
    n_out = 4 if copy_g else 3
    return pl.pallas_call(
        body, name=name, out_shape=[jax.ShapeDtypeStruct((R, C), F32)] * n_out, grid=(R // tr,),
        in_specs=[spec] * 4, out_specs=[spec] * n_out, compiler_params=_params("parallel"),
    )(w, g, m, v)


def _coords():
    return lax.axis_index("x"), lax.axis_index("y"), lax.axis_index("c")


def _flip(x, y, c, k):
    return (1 - x if k & 4 else x, 1 - y if k & 2 else y, 1 - c if k & 1 else c)


def _small_exchange(name, buf, reduce):
    rows = buf.shape[0]

    def body(x_ref, o_ref, land, send_sems, recv_sems):
        x, y, c = _coords()
        me = 4 * x + 2 * y + c

        def copy(k, sending):
            px, py, pc = _flip(x, y, c, k)
            slot = me if sending else 4 * px + 2 * py + pc
            return pltpu.make_async_remote_copy(
                src_ref=x_ref, dst_ref=land.at[slot], send_sem=send_sems.at[k - 1], recv_sem=recv_sems.at[k - 1],
                device_id=(px, py, pc), device_id_type=MESH)

        for k in range(1, 8):
            copy(k, True).start()
        land[me] = x_ref[...]
        for k in range(1, 8):
            copy(k, False).wait()
        if reduce:
            acc = land[0]
            for s in range(1, 8):
                acc = acc + land[s]
            o_ref[...] = acc
        else:
            o_ref[...] = land[...]

    out_shape = jax.ShapeDtypeStruct((rows, 128) if reduce else (8, rows, 128), F32)
    return pl.pallas_call(
        body, name=name, out_shape=out_shape,
        in_specs=[pl.BlockSpec(memory_space=pltpu.VMEM)], out_specs=pl.BlockSpec(memory_space=pltpu.VMEM),
        scratch_shapes=[pltpu.VMEM((8, rows, 128), F32), pltpu.SemaphoreType.DMA((7,)), pltpu.SemaphoreType.DMA((7,))],
        compiler_params=pltpu.CompilerParams(vmem_limit_bytes=VMEM_LIMIT),
    )(buf)


def _handshake(peers):
    barrier = pltpu.get_barrier_semaphore()
    for peer in peers:
        pl.semaphore_signal(barrier, inc=1, device_id=peer, device_id_type=MESH)
    pl.semaphore_wait(barrier, len(peers))


def _sequencer_mesh():
    return plsc.ScalarSubcoreMesh(axis_name="sequencer", num_cores=1)


def _allgather_weight(name, shard, collective_id):
    def body(in_ref, out_ref, send_sems, recv_sems, local_sem):
        x, y, c = _coords()
        chip = 2 * x + y
        sib = (x, y, 1 - c)
        chips = [_flip(x, y, c, k) for k in (4, 2, 6)]
        _handshake([sib] + chips)

        def slab(cx, cy, cc):
            return out_ref.at[2 * cx + cy, cc]

        def copy(k, src, dst, to):
            return pltpu.make_async_remote_copy(
                src_ref=src, dst_ref=dst, send_sem=send_sems.at[k], recv_sem=recv_sems.at[k],
                device_id=to, device_id_type=MESH)

        local = pltpu.make_async_copy(in_ref.at[c], out_ref.at[chip, c], local_sem)
        local.start()
        started = []
        for j, to in enumerate(chips):
            started.append(copy(1 + j, in_ref.at[c], slab(x, y, c), to))
        started.append(copy(0, in_ref.at[c], slab(x, y, c), sib))
        for cp in started:
            cp.start()
        for j, (px, py, pc) in enumerate(chips):
            held = slab(px, py, c)
            copy(1 + j, held, held, (px, py, pc)).wait_recv()
            cp = copy(4 + j, held, held, sib)
            cp.start()
            started.append(cp)
        got = slab(x, y, 1 - c)
        copy(0, got, got, sib).wait_recv()
        for j, (px, py, pc) in enumerate(chips):
            got = slab(px, py, 1 - c)
            copy(4 + j, got, got, sib).wait_recv()
        for cp in started:
            cp.wait_send()
        local.wait()

    return pl.kernel(
        body, out_type=jax.ShapeDtypeStruct((N_CHIPS,) + shard.shape, shard.dtype),
        mesh=_sequencer_mesh(), name=name,
        scratch_types=[pltpu.SemaphoreType.DMA((7,)), pltpu.SemaphoreType.DMA((7,)), pltpu.SemaphoreType.DMA],
        compiler_params=pltpu.CompilerParams(collective_id=collective_id),
    )(shard)


def _grad_exchange(name, g, collective_id):
    def body(in_ref, out_ref, send_sems, recv_sems):
        x, y, c = _coords()
        chip = 2 * x + y
        sib = (x, y, 1 - c)
        chips = [_flip(x, y, c, k) for k in (4, 2, 6)]
        _handshake([sib] + [(px, py, cc) for px, py, _ in chips for cc in (0, 1)])
        cps = []
        for j, (px, py, _) in enumerate(chips):
            for cc in (0, 1):
                cps.append(pltpu.make_async_remote_copy(
                    src_ref=in_ref.at[2 * px + py, cc], dst_ref=out_ref.at[1 + 2 * j + c],
                    send_sem=send_sems.at[1 + 2 * j + cc], recv_sem=recv_sems.at[1 + 2 * j + c],
                    device_id=(px, py, cc), device_id_type=MESH))
        cps.append(pltpu.make_async_remote_copy(
            src_ref=in_ref.at[chip, 1 - c], dst_ref=out_ref.at[0], send_sem=send_sems.at[0],
            recv_sem=recv_sems.at[0], device_id=sib, device_id_type=MESH))
        for cp in cps:
            cp.start()
        for cp in cps:
            cp.wait_send()
        for slot in range(N_PEERS):
            pltpu.make_async_remote_copy(
                src_ref=out_ref.at[slot], dst_ref=out_ref.at[slot], send_sem=send_sems.at[slot],
                recv_sem=recv_sems.at[slot], device_id=sib, device_id_type=MESH).wait_recv()

    return pl.kernel(
        body, out_type=jax.ShapeDtypeStruct((N_PEERS,) + g.shape[2:], g.dtype),
        mesh=_sequencer_mesh(), name=name,
        scratch_types=[pltpu.SemaphoreType.DMA((N_PEERS,)), pltpu.SemaphoreType.DMA((N_PEERS,))],
        compiler_params=pltpu.CompilerParams(collective_id=collective_id),
    )(g)


def _pair_fill(name, fulls):
    T = len(fulls)

    def body(*refs):
        outs = refs[T:2 * T]
        send_sems, recv_sems = refs[2 * T:]
        x, y, c = _coords()
        sib = (x, y, 1 - c)
        cps = []
        for t in range(T):
            send = pltpu.make_async_remote_copy(
                src_ref=outs[t].at[c], dst_ref=outs[t].at[c], send_sem=send_sems.at[t],
                recv_sem=recv_sems.at[t], device_id=sib, device_id_type=MESH)
            recv = pltpu.make_async_remote_copy(
                src_ref=outs[t].at[1 - c], dst_ref=outs[t].at[1 - c], send_sem=send_sems.at[t],
                recv_sem=recv_sems.at[t], device_id=sib, device_id_type=MESH)
            send.start()
            cps.append((send, recv))
        for send, recv in cps:
            send.wait_send()
            recv.wait_recv()

    anyspec = pl.BlockSpec(memory_space=pl.ANY)
    return pl.pallas_call(
        body, name=name,
        out_shape=[jax.ShapeDtypeStruct(f.shape, f.dtype) for f in fulls],
        in_specs=[anyspec] * T, out_specs=[anyspec] * T,
        input_output_aliases={t: t for t in range(T)},
        scratch_shapes=[pltpu.SemaphoreType.DMA((T,)), pltpu.SemaphoreType.DMA((T,))],
    )(*fulls)


def _pack(arrs):
    flat = jnp.concatenate([a.reshape(-1).astype(F32) for a in arrs])
    n = flat.shape[0]
    rows = -(-n // 1024) * 8
    return jnp.pad(flat, (0, rows * 128 - n)).reshape(rows, 128)


def _unpack(buf, shapes):
    flat = buf.reshape(-1)
    out, pos = [], 0
    for s in shapes:
        n = math.prod(s)
        out.append(flat[pos:pos + n].reshape(s))
        pos += n
    return out


def kernel(x, mix_norm_g, ffn_norm_g, final_norm_g, sc_w_in, sc_conv_w, sc_conv_b, sc_w_out, attn_w_qkv, attn_w_out, ffn_w_up, ffn_conv_w, ffn_conv_b, ffn_w_down, loss_target, m_mix_norm_g, m_ffn_norm_g, m_final_norm_g, m_sc_w_in, m_sc_conv_w, m_sc_conv_b, m_sc_w_out, m_attn_w_qkv, m_attn_w_out, m_ffn_w_up, m_ffn_conv_w, m_ffn_conv_b, m_ffn_w_down, v_mix_norm_g, v_ffn_norm_g, v_final_norm_g, v_sc_w_in, v_sc_conv_w, v_sc_conv_b, v_sc_w_out, v_attn_w_qkv, v_attn_w_out, v_ffn_w_up, v_ffn_conv_w, v_ffn_conv_b, v_ffn_w_down):
    S, D = x.shape[1], x.shape[2]
    xi, yi, ci = _coords()
    chip = 2 * xi + yi
    x0 = x.reshape(S, D)
    tgt = loss_target.reshape(S, D)

    conv_shapes = [sc_conv_w.shape, ffn_conv_w.shape]
    allc = _small_exchange("gather_conv_w", _pack([sc_conv_w, ffn_conv_w]), reduce=False)
    per_chip = [_unpack(allc[2 * k], conv_shapes) for k in range(N_CHIPS)]
    scw = jnp.concatenate([p[0] for p in per_chip], axis=-1)[0]
    fcw = jnp.concatenate([p[1] for p in per_chip], axis=-1)
    scb = sc_conv_b

    big_names = ["sc_w_in", "sc_w_out", "attn_w_qkv", "attn_w_out", "ffn_w_up", "ffn_w_down"]
    big = dict(zip(big_names, [sc_w_in, sc_w_out, attn_w_qkv, attn_w_out, ffn_w_up, ffn_w_down]))
    n_gathers = [0]

    def gather(tag, w, layers):
        _, R, C = w.shape
        shard = _cast_bf16("cast_" + tag, w, layers).reshape(2, len(layers) * R // 2, C)
        cid = n_gathers[0]
        n_gathers[0] += 1
        return _allgather_weight("allgather_" + tag, shard, cid).reshape(N_CHIPS, len(layers), R, C)

    w_in = gather("sc_w_in", sc_w_in, (0,))
    w_out = gather("sc_w_out", sc_w_out, (0,))
    w_ups = [gather("ffn_w_up0", ffn_w_up, (0,))]
    w_dn = gather("ffn_w_down", ffn_w_down, (0, 1))
    w_qkv = gather("attn_w_qkv", attn_w_qkv, (0,))
    w_ao = gather("attn_w_out", attn_w_out, (0,))
    w_ups.append(gather("ffn_w_up1", ffn_w_up, (1,)))

    def reduce_scatter(tag, g):
        cid = n_gathers[0] + big_names.index(tag)
        others = _grad_exchange("rs_exchange_" + tag, g, cid)
        full = _grad_sum("rs_sum_" + tag, g, others)
        return _pair_fill("rs_pair_fill_" + tag, [full])[0].reshape(big[tag].shape)

    h0 = _rmsnorm_fwd("norm_mix0", x0, mix_norm_g[0:1])[0]
    z = _mm_nn_col("sc_in", h0, w_in, 0)
    y = _sc_fwd("sc_gate", z, scw, scb)
    x1 = _mm_nn_row("sc_out", y, w_out, 0, x0)
    h1 = _rmsnorm_fwd("norm_ffn0", x1, ffn_norm_g[0:1])[0]
    u0 = _mm_nn_col("ffn_up0", h1, w_ups[0], 0)
    f0, v0 = _ffn_fwd("ffn_gate0", u0, fcw[0], ffn_conv_b[0:1])
    x2 = _mm_nn_row("ffn_down0", f0, w_dn, 0, x1)
    h2s = _rmsnorm_fwd("norm_mix1", x2, mix_norm_g[1:2], dilated=True)
    qkvs = [_mm_nn_col(f"attn_qkv{d}", h, w_qkv, 0, col_off=gi * 3 * D, ncols=3 * D)
            for gi, (h, d) in enumerate(zip(h2s, DILATIONS))]
    og, lg = zip(*[_attn_fwd(f"attn_fwd{d}", q, d) for q, d in zip(qkvs, DILATIONS)])
    o32, ob, lse = _attn_combine("attn_combine", list(og), list(lg))
    x3 = _mm_nn_row("attn_out", ob, w_ao, 0, x2)
    h3 = _rmsnorm_fwd("norm_ffn1", x3, ffn_norm_g[1:2])[0]
    u1 = _mm_nn_col("ffn_up1", h3, w_ups[1], 0)
    f1, v1 = _ffn_fwd("ffn_gate1", u1, fcw[1], ffn_conv_b[1:2])
    x4 = _mm_nn_row("ffn_down1", f1, w_dn, 1, x3)

    dx4, dx4b, dg_final, loss_part = _rmsnorm_bwd("loss_norm_bwd", x4, final_norm_g.reshape(1, D), target=tgt)

    def ffn_backward(layer, xin, h, u, v, f, dxo, dxob, gain, g_up, g_dn):
        df = _mm_nt_row(f"ffn_down_dx{layer}", dxob, w_dn, layer)
        g_dn = _mm_tn_row(f"ffn_down_dw{layer}", f, dxob, prev=g_dn, layer=layer)
        du, dwb = _ffn_bwd(f"ffn_gate_bwd{layer}", u, v, df, fcw[layer])
        dh = _mm_nt_col(f"ffn_up_dx{layer}", du, w_ups[layer], 0)
        g_up = _mm_tn_col(f"ffn_up_dw{layer}", h, du, w_ups[layer].shape[3], prev=g_up, layer=layer)
        dxi, dxib, dg = _rmsnorm_bwd(f"norm_ffn_bwd{layer}", xin, gain, dhs=[(dh, 1)], dres=dxo)
        return dxi, dxib, dg, dwb, g_up, g_dn

    dx3, dx3b, dg_ffn1, dwb_ffn1, g_up, g_dn = ffn_backward(1, x3, h3, u1, v1, f1, dx4, dx4b, ffn_norm_g[1:2], None, None)

    big_grads = {}
    do = _mm_nt_row("attn_out_dx", dx3b, w_ao, 0)
    big_grads["attn_w_out"] = reduce_scatter("attn_w_out", _mm_tn_row("attn_out_dw", ob, dx3b))
    dos, lss, dls = _attn_bwd_prep("attn_bwd_prep", do, o32, lse)
    dqkvs = [_attn_bwd(f"attn_bwd{d}", q, a, b, c_, d)
             for q, a, b, c_, d in zip(qkvs, dos, lss, dls, DILATIONS)]
    dh2s = [(_mm_nt_col(f"attn_qkv_dx{d}", dq, w_qkv, 0, col_off=gi * 3 * D), d)
            for gi, (dq, d) in enumerate(zip(dqkvs, DILATIONS))]
    g_qkv = None
    for gi, (h, dq, d) in enumerate(zip(h2s, dqkvs, DILATIONS)):
        g_qkv = _mm_tn_col(f"attn_qkv_dw{d}", h, dq, w_qkv.shape[3], col_off=gi * 3 * D, prev=g_qkv)
    big_grads["attn_w_qkv"] = reduce_scatter("attn_w_qkv", g_qkv)
    dx2, dx2b, dg_mix1 = _rmsnorm_bwd("norm_mix_bwd1", x2, mix_norm_g[1:2], dhs=dh2s, dres=dx3)

    dx1, dx1b, dg_ffn0, dwb_ffn0, g_up, g_dn = ffn_backward(0, x1, h1, u0, v0, f0, dx2, dx2b, ffn_norm_g[0:1], g_up, g_dn)
    big_grads["ffn_w_down"] = reduce_scatter("ffn_w_down", g_dn)
    big_grads["ffn_w_up"] = reduce_scatter("ffn_w_up", g_up)

    dy = _mm_nt_row("sc_out_dx", dx1b, w_out, 0)
    big_grads["sc_w_out"] = reduce_scatter("sc_w_out", _mm_tn_row("sc_out_dw", y, dx1b))
    dz, dwb_sc = _sc_bwd("sc_gate_bwd", z, dy, scw, scb)
    big_grads["sc_w_in"] = reduce_scatter("sc_w_in", _mm_tn_col("sc_in_dw", h0, dz, w_in.shape[3]))
    dh0 = _mm_nt_col("sc_in_dx", dz, w_in, 0)
    dx0, _, dg_mix0 = _rmsnorm_bwd("norm_mix_bwd0", x0, mix_norm_g[0:1], dhs=[(dh0, 1)], dres=dx1)

    dconv_sc = dwb_sc[0:3].reshape(1, 3, D)
    dbias_sc = dwb_sc[3:4]
    dconv_ffn = jnp.stack([dwb_ffn0[0:3], dwb_ffn1[0:3]])
    dbias_ffn = jnp.concatenate([dwb_ffn0[3:4], dwb_ffn1[3:4]], axis=0)
    small_parts = [jnp.concatenate([dg_mix0, dg_mix1], axis=0), jnp.concatenate([dg_ffn0, dg_ffn1], axis=0),
                   dg_final.reshape(D), dconv_sc, dbias_sc, dconv_ffn, dbias_ffn, loss_part[0, 0:1]]
    small_shapes = [a.shape for a in small_parts]
    summed = _unpack(_small_exchange("allreduce_small", _pack(small_parts), reduce=True), small_shapes)
    g_mix, g_ffn, g_final, g_scw_full, g_scb, g_fcw_full, g_fcb, loss = summed
    loss = loss.reshape(())
    cw = sc_conv_w.shape[2]
    g_scw = lax.dynamic_slice_in_dim(g_scw_full, chip * cw, cw, axis=2)
    fw = ffn_conv_w.shape[2]
    g_fcw = lax.dynamic_slice_in_dim(g_fcw_full, chip * fw, fw, axis=2)

    names = ["mix_norm_g", "ffn_norm_g", "final_norm_g", "sc_w_in", "sc_conv_w", "sc_conv_b", "sc_w_out",
             "attn_w_qkv", "attn_w_out", "ffn_w_up", "ffn_conv_w", "ffn_conv_b", "ffn_w_down"]
    ws = dict(zip(names, [mix_norm_g, ffn_norm_g, final_norm_g, sc_w_in, sc_conv_w, sc_conv_b, sc_w_out,
                          attn_w_qkv, attn_w_out, ffn_w_up, ffn_conv_w, ffn_conv_b, ffn_w_down]))
    ms = dict(zip(names, [m_mix_norm_g, m_ffn_norm_g, m_final_norm_g, m_sc_w_in, m_sc_conv_w, m_sc_conv_b, m_sc_w_out,
                          m_attn_w_qkv, m_attn_w_out, m_ffn_w_up, m_ffn_conv_w, m_ffn_conv_b, m_ffn_w_down]))
    vs = dict(zip(names, [v_mix_norm_g, v_ffn_norm_g, v_final_norm_g, v_sc_w_in, v_sc_conv_w, v_sc_conv_b, v_sc_w_out,
                          v_attn_w_qkv, v_attn_w_out, v_ffn_w_up, v_ffn_conv_w, v_ffn_conv_b, v_ffn_w_down]))
    gs = {"mix_norm_g": g_mix, "ffn_norm_g": g_ffn, "final_norm_g": g_final, "sc_conv_w": g_scw,
          "sc_conv_b": g_scb, "ffn_conv_w": g_fcw, "ffn_conv_b": g_fcb}
    gs.update(big_grads)

    deltas, new_m, new_v = {}, {}, {}
    small_names = [n for n in names if n not in big_names]
    packed = [_pack([d[n] for n in small_names]) for d in (ws, gs, ms, vs)]
    outs = _adamw("adamw_small", *packed)
    shapes = [ws[n].shape for n in small_names]
    for res, o in zip((deltas, new_m, new_v), outs):
        res.update(dict(zip(small_names, _unpack(o, shapes))))
    for n in big_names:
        shp = ws[n].shape
        two_d = (shp[0] * shp[1], shp[2])
        outs = _adamw("adamw_" + n, *[d[n].reshape(two_d) for d in (ws, gs, ms, vs)], copy_g=True)
        for res, o in zip((deltas, new_m, new_v, gs), outs):
            res[n] = o.reshape(shp)

    return (loss, dx0.reshape(x.shape), *[gs[n] for n in names], *[deltas[n] for n in names],
            *[new_m[n] for n in names], *[new_v[n] for n in names])
```

```python
import math

import jax
import jax.numpy as jnp
from jax import lax
from jax.experimental import pallas as pl
from jax.experimental.pallas import tpu as pltpu
from jax.experimental.pallas import tpu_sc as plsc

F32 = jnp.float32
BF = jnp.bfloat16
MESH = pl.DeviceIdType.MESH

HEAD_DIM = 128
ATTN_HALF = 64
ATTN_BLOCK = 128
DILATIONS = (1, 4, 16)
STAT_LANES = 128
HALO = 16
NORM_EPS = 1e-5
ALIBI_MAX = 8.0
NEG_INF = -1e30
N_CHIPS = 4
VMEM_LIMIT = 56 * 1024 * 1024

ADAM_LR = 0.001
ADAM_B1 = 0.9
ADAM_B2 = 0.999
ADAM_EPS = 1e-08
ADAM_WD = 0.01
ADAM_STEP = 10


def _pick(n, cands):
    for c in cands:
        if n % c == 0:
            return c
    raise ValueError(f"no tile for {n} in {cands}")


def _row_tile(rows, cols, max_elems=1 << 19):
    for c in (512, 256, 128, 64, 32, 16):
        if rows % c == 0 and c * cols <= max_elems:
            return c
    raise ValueError(f"no row tile for {rows}x{cols}")


def _params(*sem):
    return pltpu.CompilerParams(dimension_semantics=sem, vmem_limit_bytes=VMEM_LIMIT)


def _matmul(name, a, b, out_shape, grid, a_spec, b_spec, o_spec, contract, acc_shape,
            res=None, res_spec=None, prev=None, b2_spec=None):
    nk = grid[2]
    n_b = 1 if b2_spec is None else 2

    def body(*refs):
        refs = list(refs)
        if prev is not None:
            refs.pop(0)
        a_ref, b_ref = refs[0], refs[1]
        res_ref = refs[1 + n_b] if res is not None else None
        o_ref = refs[2 + n_b] if res is not None else refs[1 + n_b]
        acc_ref = refs[-1]
        bv = b_ref[...]
        if bv.ndim == 3:
            bv = bv.reshape(bv.shape[0] * bv.shape[1], bv.shape[2])
        if b2_spec is None:
            part = lax.dot_general(a_ref[...], bv, contract, preferred_element_type=F32)
        else:
            half = a_ref.shape[1] // 2
            part = (lax.dot_general(a_ref[:, :half], bv, contract, preferred_element_type=F32)
                    + lax.dot_general(a_ref[:, half:], refs[2][...], contract, preferred_element_type=F32))

        def finish(total):
            if res_ref is not None:
                total = total + res_ref[...]
            o_ref[...] = total.reshape(o_ref.shape).astype(o_ref.dtype)

        if nk == 1:
            finish(part)
        else:
            k = pl.program_id(2)

            @pl.when(k == 0)
            def _():
                acc_ref[...] = part

            @pl.when(jnp.logical_and(k > 0, k < nk - 1))
            def _():
                acc_ref[...] += part

            @pl.when(k == nk - 1)
            def _():
                finish(acc_ref[...] + part)

    operands, in_specs, aliases = [], [], {}
    if prev is not None:
        operands.append(prev)
        in_specs.append(pl.BlockSpec(memory_space=pl.ANY))
        aliases = {0: 0}
    operands += [a, b]
    in_specs += [a_spec, b_spec]
    if b2_spec is not None:
        operands.append(b)
        in_specs.append(b2_spec)
    if res is not None:
        operands.append(res)
        in_specs.append(res_spec)
    return pl.pallas_call(
        body, name=name, out_shape=out_shape, grid=grid, in_specs=in_specs, out_specs=o_spec,
        scratch_shapes=[pltpu.VMEM(acc_shape if nk > 1 else (8, 128), F32)],
        input_output_aliases=aliases,
        compiler_params=_params("parallel", "parallel", "arbitrary"),
    )(*operands)


NN = (((1,), (0,)), ((), ()))
NT = (((1,), (1,)), ((), ()))
TN = (((0,), (0,)), ((), ()))

_COL_TILES = (1536, 1408, 1024, 768, 512, 384, 256, 128)
_WIDE_TILES = (2816,) + _COL_TILES


def _mm_nn_col(name, a, w, layer, col_off=0, ncols=None, out_dtype=BF):
    M, K = a.shape
    _, _, R, C = w.shape
    assert R == K
    ncols = N_CHIPS * C if ncols is None else ncols
    tn = _pick(math.gcd(C, math.gcd(ncols, col_off) if col_off else ncols), _WIDE_TILES)
    tm = _pick(M, (1024, 512, 256))
    nb, off = C // tn, col_off // tn
    return _matmul(
        name, a, w, jax.ShapeDtypeStruct((M, ncols), out_dtype), (M // tm, ncols // tn, 1),
        pl.BlockSpec((tm, K), lambda i, j, k: (i, 0)),
        pl.BlockSpec((None, None, K, tn), lambda i, j, k: ((j + off) // nb, layer, 0, (j + off) % nb)),
        pl.BlockSpec((tm, tn), lambda i, j, k: (i, j)), NN, (tm, tn))


def _mm_nn_row(name, a, w, layer, res):
    M, K = a.shape
    _, _, R, C = w.shape
    assert N_CHIPS * R == K
    chips_per_step = N_CHIPS if K <= 2048 else 2
    tk = chips_per_step * R
    tm = _pick(M, (1024, 512, 256))
    tn = _pick(C, (1024, 512, 256))
    return _matmul(
        name, a, w, jax.ShapeDtypeStruct((M, C), F32), (M // tm, C // tn, K // tk),
        pl.BlockSpec((tm, tk), lambda i, j, k: (i, k)),
        pl.BlockSpec((chips_per_step, None, R, tn), lambda i, j, k: (k, layer, 0, j)),
        pl.BlockSpec((tm, tn), lambda i, j, k: (i, j)), NN, (tm, tn),
        res=res, res_spec=pl.BlockSpec((tm, tn), lambda i, j, k: (i, j)))


def _mm_nt_col(name, dy, w, layer, col_off=0, out_dtype=F32):
    M, n = dy.shape
    _, _, R, C = w.shape
    tk = _pick(math.gcd(C, math.gcd(n, col_off) if col_off else n), _WIDE_TILES)
    tm = _pick(M, (1024, 512, 256))
    tn = _pick(R, (1024, 512, 256))
    nb, off = C // tk, col_off // tk
    per_step = 2 if (tk <= 1536 and (n // tk) % 2 == 0) else 1

    def w_block(t):
        return pl.BlockSpec((None, None, tn, tk), lambda i, j, k: (
            (per_step * k + t + off) // nb, layer, j, (per_step * k + t + off) % nb))

    return _matmul(
        name, dy, w, jax.ShapeDtypeStruct((M, R), out_dtype), (M // tm, R // tn, n // (per_step * tk)),
        pl.BlockSpec((tm, per_step * tk), lambda i, j, k: (i, k)), w_block(0),
        pl.BlockSpec((tm, tn), lambda i, j, k: (i, j)), NT, (tm, tn),
        b2_spec=w_block(1) if per_step == 2 else None)


def _mm_nt_row(name, dy, w, layer, out_dtype=BF):
    M, C2 = dy.shape
    _, _, R, C = w.shape
    assert C2 == C
    chips_per_tile = N_CHIPS if N_CHIPS * R <= 2048 else 2
    tn = chips_per_tile * R
    tm = _pick(M, (1024, 512, 256))
    return _matmul(
        name, dy, w, jax.ShapeDtypeStruct((M, N_CHIPS * R), out_dtype), (M // tm, N_CHIPS * R // tn, 1),
        pl.BlockSpec((tm, C), lambda i, j, k: (i, 0)),
        pl.BlockSpec((chips_per_tile, None, R, C), lambda i, j, k: (j, layer, 0, 0)),
        pl.BlockSpec((tm, tn), lambda i, j, k: (i, j)), NT, (tm, tn))


_TN_DEPTH = (2048, 1024, 512, 256)


def _half_index(rows, layer, tkx):
    if layer is None:
        hb = rows // 2 // tkx
        return rows // 2, lambda i: (i // hb, i % hb)
    return rows, lambda i: (layer, i)


def _mm_tn_col(name, xa, dy, C, col_off=0, prev=None, layer=None):
    M, K = xa.shape
    _, n = dy.shape
    tn = _pick(math.gcd(C, math.gcd(n, col_off) if col_off else n), _WIDE_TILES)
    tkx = _pick(K // 2 if layer is None else K, (1024, 512, 256, 128) if tn <= 1536 else (512, 256, 128))
    tmr = _pick(M, _TN_DEPTH)
    rh, split = _half_index(K, layer, tkx)
    nb, off = C // tn, col_off // tn
    return _matmul(
        name, xa, dy, jax.ShapeDtypeStruct((N_CHIPS, 2, rh, C), BF), (K // tkx, n // tn, M // tmr),
        pl.BlockSpec((tmr, tkx), lambda i, j, k: (k, i)),
        pl.BlockSpec((tmr, tn), lambda i, j, k: (k, j)),
        pl.BlockSpec((None, None, tkx, tn), lambda i, j, k: ((j + off) // nb, *split(i), (j + off) % nb)),
        TN, (tkx, tn), prev=prev)


def _mm_tn_row(name, xa, dy, prev=None, layer=None):
    M, K = xa.shape
    _, C = dy.shape
    R = K // N_CHIPS
    tmr = _pick(M, _TN_DEPTH)
    if layer is None and prev is None and K <= 2048:
        tn = _pick(C, (1024, 512, 256))
        return _matmul(
            name, xa, dy, jax.ShapeDtypeStruct((N_CHIPS, 2, R // 2, C), BF), (1, C // tn, M // tmr),
            pl.BlockSpec((tmr, K), lambda i, j, k: (k, 0)),
            pl.BlockSpec((tmr, tn), lambda i, j, k: (k, j)),
            pl.BlockSpec((N_CHIPS, 2, R // 2, tn), lambda i, j, k: (0, 0, 0, j)),
            TN, (K, tn))
    tkx = _pick(R // 2 if layer is None else R, (1408, 1024, 512, 256, 128))
    tn = _pick(C, (2048, 1024, 512, 256) if tkx <= 512 else (1024, 512, 256))
    rh, split = _half_index(R, layer, tkx)
    rb = R // tkx
    return _matmul(
        name, xa, dy, jax.ShapeDtypeStruct((N_CHIPS, 2, rh, C), BF), (K // tkx, C // tn, M // tmr),
        pl.BlockSpec((tmr, tkx), lambda i, j, k: (k, i)),
        pl.BlockSpec((tmr, tn), lambda i, j, k: (k, j)),
        pl.BlockSpec((None, None, tkx, tn), lambda i, j, k: (i // rb, *split(i % rb), j)),
        TN, (tkx, tn), prev=prev)


NORM_ROWS = 256
LANES = 128


def _chunk_scratch(tm, width):
    return pltpu.VMEM((width // LANES, tm, LANES), F32)


def _store_chunks(scr, value):
    for c in range(scr.shape[0]):
        scr[c] = value[:, c * LANES:(c + 1) * LANES]


def _load_chunks(scr):
    return jnp.concatenate([scr[c] for c in range(scr.shape[0])], axis=1)


def _to_residue_major(scr, o_ref, d, dtype):
    tm = scr.shape[1]
    for c in range(scr.shape[0]):
        for res in range(d):
            o_ref[res, :, c * LANES:(c + 1) * LANES] = scr[c, pl.ds(res, tm // d, stride=d), :].astype(dtype)


def _to_natural(scr, ref, d):
    tm = scr.shape[1]
    for c in range(scr.shape[0]):
        for res in range(d):
            scr[c, pl.ds(res, tm // d, stride=d), :] = ref[res, :, c * LANES:(c + 1) * LANES].astype(F32)


def _rmsnorm_fwd(name, x, g, dilated=False):
    S, D = x.shape
    tm = NORM_ROWS
    dils = DILATIONS[1:] if dilated else ()

    def body(x_ref, g_ref, h_ref, *rest):
        xv = x_ref[...]
        r = lax.rsqrt(jnp.mean(xv * xv, axis=1, keepdims=True) + NORM_EPS)
        h = xv * r * g_ref[...]
        h_ref[...] = h.astype(BF)
        if dils:
            scr = rest[-1]
            _store_chunks(scr, h)
            for o_ref, d in zip(rest[:-1], dils):
                _to_residue_major(scr, o_ref, d, BF)

    out_shape = [jax.ShapeDtypeStruct((S, D), BF)]
    out_specs = [pl.BlockSpec((tm, D), lambda i: (i, 0))]
    for d in dils:
        out_shape.append(jax.ShapeDtypeStruct((d, S // d, D), BF))
        out_specs.append(pl.BlockSpec((d, tm // d, D), lambda i: (0, i, 0)))
    outs = pl.pallas_call(
        body, name=name, out_shape=out_shape, grid=(S // tm,),
        in_specs=[pl.BlockSpec((tm, D), lambda i: (i, 0)), pl.BlockSpec((1, D), lambda i: (0, 0))],
        out_specs=out_specs,
        scratch_shapes=[_chunk_scratch(tm, D)] if dils else [],
        compiler_params=_params("parallel"),
    )(x, g)
    return [outs[0]] + [o.reshape(S, D) for o in outs[1:]]


def _rmsnorm_bwd(name, x, g, dhs=(), dres=None, target=None):
    S, D = x.shape
    tm = NORM_ROWS
    n_dh = len(dhs)

    def body(*refs):
        refs = list(refs)
        x_ref, g_ref = refs[0], refs[1]
        dh_refs = refs[2:2 + n_dh]
        pos = 2 + n_dh
        dres_ref = tgt_ref = None
        if dres is not None:
            dres_ref = refs[pos]
            pos += 1
        if target is not None:
            tgt_ref = refs[pos]
            pos += 1
        dx_ref, dxb_ref, dg_ref = refs[pos:pos + 3]
        pos += 3
        loss_ref = None
        if target is not None:
            loss_ref = refs[pos]
            pos += 1
        scr = refs[pos] if any(d > 1 for _, d in dhs) else None
        i = pl.program_id(0)

        xv = x_ref[...]
        gv = g_ref[...]
        r = lax.rsqrt(jnp.mean(xv * xv, axis=1, keepdims=True) + NORM_EPS)
        xhat = xv * r
        if target is not None:
            err = xhat * gv - tgt_ref[...]
            dh = err * (1.0 / D)
            part = jnp.sum(jnp.sum(err * err, axis=1, keepdims=True), axis=0, keepdims=True) * (0.5 / D)
        else:
            dh = None
            for ref, d in zip(dh_refs, [d for _, d in dhs]):
                if d == 1:
                    v = ref[...]
                else:
                    _to_natural(scr, ref, d)
                    v = _load_chunks(scr)
                dh = v if dh is None else dh + v
        dxhat = dh * gv
        dx = r * (dxhat - xhat * jnp.mean(dxhat * xhat, axis=1, keepdims=True))
        if dres_ref is not None:
            dx = dx + dres_ref[...]
        dx_ref[...] = dx
        dxb_ref[...] = dx.astype(BF)
        dg = jnp.sum(dh * xhat, axis=0, keepdims=True)

        @pl.when(i == 0)
        def _():
            dg_ref[...] = dg
            if loss_ref is not None:
                loss_ref[...] = jnp.broadcast_to(part, loss_ref.shape)

        @pl.when(i > 0)
        def _():
            dg_ref[...] += dg
            if loss_ref is not None:
                loss_ref[...] += jnp.broadcast_to(part, loss_ref.shape)

    row = pl.BlockSpec((tm, D), lambda i: (i, 0))
    operands = [x, g]
    in_specs = [row, pl.BlockSpec((1, D), lambda i: (0, 0))]
    for arr, d in dhs:
        if d == 1:
            operands.append(arr)
            in_specs.append(row)
        else:
            operands.append(arr.reshape(d, S // d, D))
            in_specs.append(pl.BlockSpec((d, tm // d, D), lambda i: (0, i, 0)))
    if dres is not None:
        operands.append(dres)
        in_specs.append(row)
    if target is not None:
        operands.append(target)
        in_specs.append(row)
    out_shape = [jax.ShapeDtypeStruct((S, D), F32), jax.ShapeDtypeStruct((S, D), BF),
                 jax.ShapeDtypeStruct((1, D), F32)]
    out_specs = [row, row, pl.BlockSpec((1, D), lambda i: (0, 0))]
    if target is not None:
        out_shape.append(jax.ShapeDtypeStruct((1, STAT_LANES), F32))
        out_specs.append(pl.BlockSpec((1, STAT_LANES), lambda i: (0, 0)))
    scratch = [_chunk_scratch(tm, D)] if any(d > 1 for _, d in dhs) else []
    return pl.pallas_call(
        body, name=name, out_shape=out_shape, grid=(S // tm,), in_specs=in_specs, out_specs=out_specs,
        scratch_shapes=scratch, compiler_params=_params("arbitrary"),
    )(*operands)


CONV_ROWS = 256
CONV_COLS = 256


def _halo_specs(S, tm, width):
    nh = S // HALO
    per = tm // HALO
    cur = pl.BlockSpec((tm, width), lambda i: (i, 0))
    prev = pl.BlockSpec((HALO, width), lambda i: (jnp.maximum(i * per - 1, 0), 0))
    nxt = pl.BlockSpec((HALO, width), lambda i: (jnp.minimum((i + 1) * per, nh - 1), 0))
    return [cur, prev, nxt]


def _ext(refs, cs, inr):
    cur, prev, nxt = refs
    v = jnp.concatenate([prev[:, cs], cur[:, cs], nxt[:, cs]], axis=0).astype(F32)
    return v if inr is None else jnp.where(inr, v, 0.0)


def _shift_prev(v):
    return pltpu.roll(v, 1, 0)


def _shift_next(v):
    return pltpu.roll(v, v.shape[0] - 1, 0)


def _shifts(v):
    return _shift_prev(v), _shift_next(v)


def _conv3(v, w, cs, b=None, shifted=None):
    vp, vn = _shifts(v) if shifted is None else shifted
    out = w[0:1, cs] * vp + w[1:2, cs] * v + w[2:3, cs] * vn
    return out if b is None else out + b[:, cs]


def _in_range(i, tm, tc, S):
    row = lax.broadcasted_iota(jnp.int32, (tm + 2 * HALO, tc), 0) + (i * tm - HALO)
    return jnp.logical_and(row >= 0, row < S)


def _core(v, tm):
    return v[HALO:HALO + tm, :]


def _acc_rows(ref, rows):
    for r, cs, val in rows:
        ref[r:r + 1, cs] += val


def _zero_first(ref, i):
    @pl.when(i == 0)
    def _():
        ref[...] = jnp.zeros(ref.shape, ref.dtype)


def _sc_fwd(name, z, w, b):
    S, D3 = z.shape
    D = D3 // 3
    tm, tc = CONV_ROWS, _pick(D, (CONV_COLS, 256, 128))

    def body(zc, zp, zn, w_ref, b_ref, y_ref):
        i = pl.program_id(0)
        inr = _in_range(i, tm, tc, S)
        zr = (zc, zp, zn)
        for c in range(D // tc):
            cs = slice(c * tc, (c + 1) * tc)
            u = _ext(zr, cs, inr)
            gc = _ext(zr, slice(2 * D + c * tc, 2 * D + (c + 1) * tc), None)
            conv = _conv3(gc * u, w_ref, cs, b_ref)
            gb = zc[:, D + c * tc:D + (c + 1) * tc].astype(F32)
            y_ref[:, cs] = (gb * _core(conv, tm)).astype(BF)

    return pl.pallas_call(
        body, name=name, out_shape=jax.ShapeDtypeStruct((S, D), BF), grid=(S // tm,),
        in_specs=_halo_specs(S, tm, D3) + [pl.BlockSpec((3, D), lambda i: (0, 0)),
                                           pl.BlockSpec((1, D), lambda i: (0, 0))],
        out_specs=pl.BlockSpec((tm, D), lambda i: (i, 0)),
        compiler_params=_params("parallel"),
    )(z, z, z, w, b)


def _sc_bwd(name, z, dy, w, b):
    S, D3 = z.shape
    D = D3 // 3
    tm, tc = CONV_ROWS, _pick(D, (CONV_COLS, 256, 128))

    def body(zc, zp, zn, dc_, dp_, dn_, w_ref, b_ref, dz_ref, dwb_ref):
        i = pl.program_id(0)
        inr = _in_range(i, tm, tc, S)
        _zero_first(dwb_ref, i)
        zr, dr = (zc, zp, zn), (dc_, dp_, dn_)
        for c in range(D // tc):
            cs = slice(c * tc, (c + 1) * tc)
            u = _ext(zr, cs, inr)
            gb = _ext(zr, slice(D + c * tc, D + (c + 1) * tc), None)
            gc = _ext(zr, slice(2 * D + c * tc, 2 * D + (c + 1) * tc), None)
            dyv = _ext(dr, cs, inr)
            p = gc * u
            p_prev, p_next = _shifts(p)
            conv = _conv3(p, w_ref, cs, b_ref, shifted=(p_prev, p_next))
            dconv = dyv * gb
            dp = w_ref[0:1, cs] * _shift_next(dconv) + w_ref[1:2, cs] * dconv + w_ref[2:3, cs] * _shift_prev(dconv)
            dz_ref[:, cs] = _core(dp * gc, tm).astype(BF)
            dz_ref[:, D + c * tc:D + (c + 1) * tc] = _core(dyv * conv, tm).astype(BF)
            dz_ref[:, 2 * D + c * tc:2 * D + (c + 1) * tc] = _core(dp * u, tm).astype(BF)
            dcc = _core(dconv, tm)
            _acc_rows(dwb_ref, [
                (0, cs, jnp.sum(dcc * _core(p_prev, tm), axis=0, keepdims=True)),
                (1, cs, jnp.sum(dcc * _core(p, tm), axis=0, keepdims=True)),
                (2, cs, jnp.sum(dcc * _core(p_next, tm), axis=0, keepdims=True)),
                (3, cs, jnp.sum(dcc, axis=0, keepdims=True))])

    return pl.pallas_call(
        body, name=name,
        out_shape=[jax.ShapeDtypeStruct((S, D3), BF), jax.ShapeDtypeStruct((4, D), F32)], grid=(S // tm,),
        in_specs=_halo_specs(S, tm, D3) + _halo_specs(S, tm, D) + [
            pl.BlockSpec((3, D), lambda i: (0, 0)), pl.BlockSpec((1, D), lambda i: (0, 0))],
        out_specs=[pl.BlockSpec((tm, D3), lambda i: (i, 0)), pl.BlockSpec((4, D), lambda i: (0, 0))],
        compiler_params=_params("arbitrary"),
    )(z, z, z, dy, dy, dy, w, b)


def _sigmoid(v):
    return 1.0 / (1.0 + jnp.exp(-v))


def _ffn_fwd(name, u, w, b):
    S, F2 = u.shape
    Fh = F2 // 2
    tm, tc = CONV_ROWS, _pick(Fh, (CONV_COLS, 256, 128))

    def body(uc, up, un, w_ref, b_ref, f_ref, v_ref):
        i = pl.program_id(0)
        inr = _in_range(i, tm, tc, S)
        ur = (uc, up, un)
        for c in range(Fh // tc):
            ca = slice(c * tc, (c + 1) * tc)
            cb = slice(Fh + c * tc, Fh + (c + 1) * tc)
            va = _core(_conv3(_ext(ur, ca, inr), w_ref, ca, b_ref), tm)
            vb = _core(_conv3(_ext(ur, cb, inr), w_ref, cb, b_ref), tm)
            v_ref[:, ca] = va.astype(BF)
            v_ref[:, cb] = vb.astype(BF)
            f_ref[:, ca] = (va * _sigmoid(va) * vb).astype(BF)

    return pl.pallas_call(
        body, name=name,
        out_shape=[jax.ShapeDtypeStruct((S, Fh), BF), jax.ShapeDtypeStruct((S, F2), BF)], grid=(S // tm,),
        in_specs=_halo_specs(S, tm, F2) + [pl.BlockSpec((3, F2), lambda i: (0, 0)),
                                           pl.BlockSpec((1, F2), lambda i: (0, 0))],
        out_specs=[pl.BlockSpec((tm, Fh), lambda i: (i, 0)), pl.BlockSpec((tm, F2), lambda i: (i, 0))],
        compiler_params=_params("parallel"),
    )(u, u, u, w, b)


def _ffn_bwd(name, u, v, df, w):
    S, F2 = u.shape
    Fh = F2 // 2
    tm, tc = CONV_ROWS, _pick(Fh, (CONV_COLS, 256, 128))

    def body(u_ref, vc, vp, vn, dc_, dp_, dn_, w_ref, du_ref, dwb_ref):
        i = pl.program_id(0)
        inr = _in_range(i, tm, tc, S)
        _zero_first(dwb_ref, i)
        vr, dr = (vc, vp, vn), (dc_, dp_, dn_)
        for c in range(Fh // tc):
            ca = slice(c * tc, (c + 1) * tc)
            cb = slice(Fh + c * tc, Fh + (c + 1) * tc)
            va, vb = _ext(vr, ca, None), _ext(vr, cb, None)
            dfv = _ext(dr, ca, inr)
            sg = _sigmoid(va)
            dva = dfv * vb * (sg * (1.0 + va * (1.0 - sg)))
            dvb = dfv * (va * sg)
            rows = []
            for cs, dv in ((ca, dva), (cb, dvb)):
                dv_prev, dv_next = _shifts(dv)
                dcore = w_ref[0:1, cs] * dv_next + w_ref[1:2, cs] * dv + w_ref[2:3, cs] * dv_prev
                du_ref[:, cs] = _core(dcore, tm).astype(BF)
                uu = u_ref[:, cs].astype(F32)
                dvc = _core(dv, tm)
                rows += [
                    (0, cs, jnp.sum(_core(dv_next, tm) * uu, axis=0, keepdims=True)),
                    (1, cs, jnp.sum(dvc * uu, axis=0, keepdims=True)),
                    (2, cs, jnp.sum(_core(dv_prev, tm) * uu, axis=0, keepdims=True)),
                    (3, cs, jnp.sum(dvc, axis=0, keepdims=True))]
            _acc_rows(dwb_ref, rows)

    return pl.pallas_call(
        body, name=name,
        out_shape=[jax.ShapeDtypeStruct((S, F2), BF), jax.ShapeDtypeStruct((4, F2), F32)], grid=(S // tm,),
        in_specs=[pl.BlockSpec((tm, F2), lambda i: (i, 0))] + _halo_specs(S, tm, F2) + _halo_specs(S, tm, Fh) + [
            pl.BlockSpec((3, F2), lambda i: (0, 0))],
        out_specs=[pl.BlockSpec((tm, F2), lambda i: (i, 0)), pl.BlockSpec((4, F2), lambda i: (0, 0))],
        compiler_params=_params("arbitrary"),
    )(u, v, v, v, df, df, df, w)


def _alibi_slopes(H):
    return [2.0 ** (-ALIBI_MAX * (h + 1) / H) for h in range(H)]


def _window_specs(S, width, col):
    n64 = S // ATTN_HALF
    cur = pl.BlockSpec((ATTN_BLOCK, width), lambda b: (b, col))
    prev = pl.BlockSpec((ATTN_HALF, width), lambda b: (jnp.maximum(2 * b - 1, 0), col))
    nxt = pl.BlockSpec((ATTN_HALF, width), lambda b: (jnp.minimum(2 * b + 2, n64 - 1), col))
    return [cur, prev, nxt]


def _fill_window(buf, cur, prev, nxt):
    buf[0:ATTN_HALF] = prev[...]
    buf[ATTN_HALF:ATTN_HALF + ATTN_BLOCK] = cur[...]
    buf[ATTN_HALF + ATTN_BLOCK:2 * ATTN_BLOCK] = nxt[...]


def _band(b, L):
    QB, W = ATTN_BLOCK, 2 * ATTN_BLOCK
    a_loc = (b * QB) % L
    row = lax.broadcasted_iota(jnp.int32, (QB, W), 0)
    col = lax.broadcasted_iota(jnp.int32, (QB, W), 1)
    dist = jnp.abs(col - ATTN_HALF - row)
    other = a_loc - ATTN_HALF + col
    valid = jnp.logical_and(dist <= ATTN_HALF, jnp.logical_and(other >= 0, other < L))
    return dist.astype(F32), valid


def _lane_col(stats, h):
    lane = lax.broadcasted_iota(jnp.int32, stats.shape, 1)
    return jnp.sum(jnp.where(lane == h, stats, 0.0), axis=1, keepdims=True)


def _attn_fwd(name, qkv, d):
    S, D3 = qkv.shape
    D = D3 // 3
    H = D // HEAD_DIM
    L = S // d
    scale = HEAD_DIM ** -0.5
    slopes = _alibi_slopes(H)

    def body(q_ref, kc, kp, kn, vc, vp, vn, o_ref, lse_ref, kbuf, vbuf):
        b = pl.program_id(0)
        _fill_window(kbuf, kc, kp, kn)
        _fill_window(vbuf, vc, vp, vn)
        dist, valid = _band(b, L)
        dist = dist * float(d)
        lane = lax.broadcasted_iota(jnp.int32, (ATTN_BLOCK, STAT_LANES), 1)
        lse = jnp.zeros((ATTN_BLOCK, STAT_LANES), F32)
        for h in range(H):
            cs = slice(h * HEAD_DIM, (h + 1) * HEAD_DIM)
            s = lax.dot_general(q_ref[:, cs], kbuf[:, cs], NT, preferred_element_type=F32) * scale
            s = jnp.where(valid, s - slopes[h] * dist, NEG_INF)
            m = jnp.max(s, axis=1, keepdims=True)
            p = jnp.exp(s - m)
            den = jnp.sum(p, axis=1, keepdims=True)
            o = jnp.dot(p.astype(BF), vbuf[:, cs], preferred_element_type=F32)
            o_ref[:, cs] = (o / den).astype(BF)
            lse = jnp.where(lane == h, m + jnp.log(den), lse)
        lse_ref[...] = lse

    return pl.pallas_call(
        body, name=name,
        out_shape=[jax.ShapeDtypeStruct((S, D), BF), jax.ShapeDtypeStruct((S, STAT_LANES), F32)],
        grid=(S // ATTN_BLOCK,),
        in_specs=[pl.BlockSpec((ATTN_BLOCK, D), lambda b: (b, 0))] + _window_specs(S, D, 1) + _window_specs(S, D, 2),
        out_specs=[pl.BlockSpec((ATTN_BLOCK, D), lambda b: (b, 0)),
                   pl.BlockSpec((ATTN_BLOCK, STAT_LANES), lambda b: (b, 0))],
        scratch_shapes=[pltpu.VMEM((2 * ATTN_BLOCK, D), BF), pltpu.VMEM((2 * ATTN_BLOCK, D), BF)],
        compiler_params=_params("parallel"),
    )(qkv, qkv, qkv, qkv, qkv, qkv, qkv)


def _dil_specs(S, tm, width):
    specs = [pl.BlockSpec((tm, width), lambda i: (i, 0))]
    for d in DILATIONS[1:]:
        specs.append(pl.BlockSpec((d, tm // d, width), lambda i: (0, i, 0)))
    return specs


def _attn_combine(name, outs, lses):
    S, D = outs[0].shape
    H = D // HEAD_DIM
    tm = NORM_ROWS

    def body(o1, o4, o16, l1, l4, l16, o_ref, ob_ref, lse_ref, oscr, lscr):
        ls = [l1[...]]
        for ref, d in zip((l4, l16), DILATIONS[1:]):
            _to_natural(lscr, ref, d)
            ls.append(lscr[0])
        top = jnp.maximum(jnp.maximum(ls[0], ls[1]), ls[2])
        es = [jnp.exp(l - top) for l in ls]
        tot = es[0] + es[1] + es[2]
        lse_ref[...] = top + jnp.log(tot)
        ws = [e / tot for e in es]
        for gi, (ref, d) in enumerate(zip((o1, o4, o16), DILATIONS)):
            if d > 1:
                _to_natural(oscr, ref, d)
            for h in range(H):
                cs = slice(h * HEAD_DIM, (h + 1) * HEAD_DIM)
                term = _lane_col(ws[gi], h) * (ref[:, cs] if d == 1 else oscr[h])
                if gi == 0:
                    o_ref[:, cs] = term
                else:
                    o_ref[:, cs] += term
        ob_ref[...] = o_ref[...].astype(BF)

    outs3 = [outs[0]] + [o.reshape(d, S // d, D) for o, d in zip(outs[1:], DILATIONS[1:])]
    lses3 = [lses[0]] + [l.reshape(d, S // d, STAT_LANES) for l, d in zip(lses[1:], DILATIONS[1:])]
    row = pl.BlockSpec((tm, D), lambda i: (i, 0))
    return pl.pallas_call(
        body, name=name,
        out_shape=[jax.ShapeDtypeStruct((S, D), F32), jax.ShapeDtypeStruct((S, D), BF),
                   jax.ShapeDtypeStruct((S, STAT_LANES), F32)],
        grid=(S // tm,),
        in_specs=_dil_specs(S, tm, D) + _dil_specs(S, tm, STAT_LANES),
        out_specs=[row, row, pl.BlockSpec((tm, STAT_LANES), lambda i: (i, 0))],
        scratch_shapes=[_chunk_scratch(tm, D), _chunk_scratch(tm, STAT_LANES)],
        compiler_params=_params("parallel"),
    )(*outs3, *lses3)


def _attn_bwd_prep(name, do, o32, lse):
    S, D = do.shape
    H = D // HEAD_DIM
    tm = NORM_ROWS
    dils = DILATIONS[1:]

    def body(do_ref, o_ref, lse_ref, dl_ref, do4, do16, l4, l16, d4, d16, scr, sscr):
        lane = lax.broadcasted_iota(jnp.int32, (tm, STAT_LANES), 1)
        delta = jnp.zeros((tm, STAT_LANES), F32)
        for h in range(H):
            cs = slice(h * HEAD_DIM, (h + 1) * HEAD_DIM)
            dov = do_ref[:, cs].astype(F32)
            scr[h] = dov
            delta = jnp.where(lane == h, jnp.sum(dov * o_ref[:, cs], axis=1, keepdims=True), delta)
        dl_ref[...] = delta
        for ref, d in zip((do4, do16), dils):
            _to_residue_major(scr, ref, d, BF)
        for val, refs in ((lse_ref[...], (l4, l16)), (delta, (d4, d16))):
            sscr[0] = val
            for ref, d in zip(refs, dils):
                _to_residue_major(sscr, ref, d, F32)

    def perm_shapes(width, dt):
        return [jax.ShapeDtypeStruct((d, S // d, width), dt) for d in dils]

    def perm_specs(width):
        return [pl.BlockSpec((d, tm // d, width), lambda i: (0, i, 0)) for d in dils]

    row = lambda w: pl.BlockSpec((tm, w), lambda i: (i, 0))
    outs = pl.pallas_call(
        body, name=name,
        out_shape=[jax.ShapeDtypeStruct((S, STAT_LANES), F32)]
        + perm_shapes(D, BF) + perm_shapes(STAT_LANES, F32) + perm_shapes(STAT_LANES, F32),
        grid=(S // tm,),
        in_specs=[row(D), row(D), row(STAT_LANES)],
        out_specs=[row(STAT_LANES)] + perm_specs(D) + perm_specs(STAT_LANES) + perm_specs(STAT_LANES),
        scratch_shapes=[_chunk_scratch(tm, D), _chunk_scratch(tm, STAT_LANES)],
        compiler_params=_params("parallel"),
    )(do, o32, lse)
    dos = [do] + [a.reshape(S, D) for a in outs[1:3]]
    lss = [lse] + [a.reshape(S, STAT_LANES) for a in outs[3:5]]
    dls = [outs[0]] + [a.reshape(S, STAT_LANES) for a in outs[5:7]]
    return dos, lss, dls


def _attn_bwd(name, qkv, do, lse, delta, d):
    S, D3 = qkv.shape
    D = D3 // 3
    H = D // HEAD_DIM
    L = S // d
    scale = HEAD_DIM ** -0.5
    slopes = _alibi_slopes(H)
    QB = ATTN_BLOCK

    def body(qc, qp, qn, kc, kp, kn, vc, vp, vn, dc_, dp_, dn_, lc, lp, ln, ec, ep, en,
             out_ref, qbuf, kbuf, vbuf, dobuf, lbuf, ebuf):
        b = pl.program_id(0)
        for buf, trio in ((qbuf, (qc, qp, qn)), (kbuf, (kc, kp, kn)), (vbuf, (vc, vp, vn)),
                          (dobuf, (dc_, dp_, dn_)), (lbuf, (lc, lp, ln)), (ebuf, (ec, ep, en))):
            _fill_window(buf, *trio)
        dist, valid = _band(b, L)
        dist = dist * float(d)
        lse_c, del_c = lc[...], ec[...]
        lse_w, del_w = lbuf[...].T, ebuf[...].T
        for h in range(H):
            cs = slice(h * HEAD_DIM, (h + 1) * HEAD_DIM)
            bias = slopes[h] * dist
            q, do_h = qc[:, cs], dc_[:, cs]
            kw, vw = kbuf[:, cs], vbuf[:, cs]
            s = lax.dot_general(q, kw, NT, preferred_element_type=F32) * scale - bias
            p = jnp.where(valid, jnp.exp(s - _lane_col(lse_c, h)), 0.0)
            dp = lax.dot_general(do_h, vw, NT, preferred_element_type=F32)
            ds = p * (dp - _lane_col(del_c, h))
            dq = jnp.dot(ds.astype(BF), kw, preferred_element_type=F32) * scale
            out_ref[:, cs] = dq.astype(BF)
            qw, dow = qbuf[:, cs], dobuf[:, cs]
            k, v = kc[:, cs], vc[:, cs]
            st = lax.dot_general(k, qw, NT, preferred_element_type=F32) * scale - bias
            pt = jnp.where(valid, jnp.exp(st - lse_w[h:h + 1, :]), 0.0)
            dv = jnp.dot(pt.astype(BF), dow, preferred_element_type=F32)
            dpt = lax.dot_general(v, dow, NT, preferred_element_type=F32)
            dst = pt * (dpt - del_w[h:h + 1, :])
            dk = jnp.dot(dst.astype(BF), qw, preferred_element_type=F32) * scale
            out_ref[:, D + h * HEAD_DIM:D + (h + 1) * HEAD_DIM] = dk.astype(BF)
            out_ref[:, 2 * D + h * HEAD_DIM:2 * D + (h + 1) * HEAD_DIM] = dv.astype(BF)

    W = 2 * QB
    return pl.pallas_call(
        body, name=name, out_shape=jax.ShapeDtypeStruct((S, D3), BF), grid=(S // QB,),
        in_specs=(_window_specs(S, D, 0) + _window_specs(S, D, 1) + _window_specs(S, D, 2)
                  + _window_specs(S, D, 0) + _window_specs(S, STAT_LANES, 0) + _window_specs(S, STAT_LANES, 0)),
        out_specs=pl.BlockSpec((QB, D3), lambda b: (b, 0)),
        scratch_shapes=[pltpu.VMEM((W, D), BF), pltpu.VMEM((W, D), BF), pltpu.VMEM((W, D), BF),
                        pltpu.VMEM((W, D), BF), pltpu.VMEM((W, STAT_LANES), F32), pltpu.VMEM((W, STAT_LANES), F32)],
        compiler_params=_params("parallel"),
    )(qkv, qkv, qkv, qkv, qkv, qkv, qkv, qkv, qkv, do, do, do, lse, lse, lse, delta, delta, delta)


def _cast_bf16(name, w, layers):
    _, R, C = w.shape
    tr = _row_tile(R, C)
    first = layers[0]

    def body(w_ref, o_ref):
        o_ref[...] = w_ref[...].astype(BF)

    return pl.pallas_call(
        body, name=name, out_shape=jax.ShapeDtypeStruct((len(layers), R, C), BF), grid=(len(layers), R // tr),
        in_specs=[pl.BlockSpec((None, tr, C), lambda l, i: (first + l, i, 0))],
        out_specs=pl.BlockSpec((None, tr, C), lambda l, i: (l, i, 0)),
        compiler_params=_params("parallel", "parallel"),
    )(w)


N_PEERS = 7


def _grad_sum(name, g, others):
    _, _, R, C = g.shape
    tr = _row_tile(R, C, 1 << 18)

    def body(g_ref, b_ref, o_ref):
        tot = g_ref[...].astype(F32)
        for s in range(N_PEERS):
            tot = tot + b_ref[s].astype(F32)
        o_ref[...] = tot

    def mine(i):
        return (2 * lax.axis_index("x") + lax.axis_index("y"), lax.axis_index("c"), i, 0)

    return pl.pallas_call(
        body, name=name, out_shape=jax.ShapeDtypeStruct((2, R, C), F32), grid=(R // tr,),
        in_specs=[pl.BlockSpec((None, None, tr, C), mine),
                  pl.BlockSpec((N_PEERS, tr, C), lambda i: (0, i, 0))],
        out_specs=pl.BlockSpec((None, tr, C), lambda i: (lax.axis_index("c"), i, 0)),
        compiler_params=_params("parallel"),
    )(g, others)


def _adamw(name, w, g, m, v, copy_g=False):
    R, C = w.shape
    tr = _row_tile(R, C, 1 << 18) if R % 16 == 0 else R
    c1 = 1.0 - ADAM_B1 ** ADAM_STEP
    c2 = 1.0 - ADAM_B2 ** ADAM_STEP

    def body(w_ref, g_ref, m_ref, v_ref, d_ref, nm_ref, nv_ref, *g_out):
        gv = g_ref[...]
        nm = ADAM_B1 * m_ref[...] + (1.0 - ADAM_B1) * gv
        nv = ADAM_B2 * v_ref[...] + (1.0 - ADAM_B2) * (gv * gv)
        nm_ref[...] = nm
        nv_ref[...] = nv
        d_ref[...] = -ADAM_LR * ((nm / c1) / (jnp.sqrt(nv / c2) + ADAM_EPS) + ADAM_WD * w_ref[...])
        if copy_g:
            g_out[0][...] = gv

    spec = pl.BlockSpec((tr, C), lambda i: (i, 0))
    n_out = 4 if copy_g else 3
    return pl.pallas_call(
        body, name=name, out_shape=[jax.ShapeDtypeStruct((R, C), F32)] * n_out, grid=(R // tr,),
        in_specs=[spec] * 4, out_specs=[spec] * n_out, compiler_params=_params("parallel"),
    )(w, g, m, v)


def _coords():
    return lax.axis_index("x"), lax.axis_index("y"), lax.axis_index("c")


def _flip(x, y, c, k):
    return (1 - x if k & 4 else x, 1 - y if k & 2 else y, 1 - c if k & 1 else c)


def _small_exchange(name, buf, reduce):
    rows = buf.shape[0]

    def body(x_ref, o_ref, land, send_sems, recv_sems):
        x, y, c = _coords()
        me = 4 * x + 2 * y + c

        def copy(k, sending):
            px, py, pc = _flip(x, y, c, k)
            slot = me if sending else 4 * px + 2 * py + pc
            return pltpu.make_async_remote_copy(
                src_ref=x_ref, dst_ref=land.at[slot], send_sem=send_sems.at[k - 1], recv_sem=recv_sems.at[k - 1],
                device_id=(px, py, pc), device_id_type=MESH)

        for k in range(1, 8):
            copy(k, True).start()
        land[me] = x_ref[...]
        for k in range(1, 8):
            copy(k, False).wait()
        if reduce:
            acc = land[0]
            for s in range(1, 8):
                acc = acc + land[s]
            o_ref[...] = acc
        else:
            o_ref[...] = land[...]

    out_shape = jax.ShapeDtypeStruct((rows, 128) if reduce else (8, rows, 128), F32)
    return pl.pallas_call(
        body, name=name, out_shape=out_shape,
        in_specs=[pl.BlockSpec(memory_space=pltpu.VMEM)], out_specs=pl.BlockSpec(memory_space=pltpu.VMEM),
        scratch_shapes=[pltpu.VMEM((8, rows, 128), F32), pltpu.SemaphoreType.DMA((7,)), pltpu.SemaphoreType.DMA((7,))],
        compiler_params=pltpu.CompilerParams(vmem_limit_bytes=VMEM_LIMIT),
    )(buf)


def _handshake(peers):
    barrier = pltpu.get_barrier_semaphore()
    for peer in peers:
        pl.semaphore_signal(barrier, inc=1, device_id=peer, device_id_type=MESH)
    pl.semaphore_wait(barrier, len(peers))


def _sequencer_mesh():
    return plsc.ScalarSubcoreMesh(axis_name="sequencer", num_cores=1)


def _allgather_weight(name, shard, collective_id):
    def body(in_ref, out_ref, send_sems, recv_sems, local_sem):
        x, y, c = _coords()
        chip = 2 * x + y
        sib = (x, y, 1 - c)
        chips = [_flip(x, y, c, k) for k in (4, 2, 6)]
        _handshake([sib] + chips)

        def slab(cx, cy, cc):
            return out_ref.at[2 * cx + cy, cc]

        def copy(k, src, dst, to):
            return pltpu.make_async_remote_copy(
                src_ref=src, dst_ref=dst, send_sem=send_sems.at[k], recv_sem=recv_sems.at[k],
                device_id=to, device_id_type=MESH)

        local = pltpu.make_async_copy(in_ref.at[c], out_ref.at[chip, c], local_sem)
        local.start()
        started = []
        for j, to in enumerate(chips):
            started.append(copy(1 + j, in_ref.at[c], slab(x, y, c), to))
        started.append(copy(0, in_ref.at[c], slab(x, y, c), sib))
        for cp in started:
            cp.start()
        for j, (px, py, pc) in enumerate(chips):
            held = slab(px, py, c)
            copy(1 + j, held, held, (px, py, pc)).wait_recv()
            cp = copy(4 + j, held, held, sib)
            cp.start()
            started.append(cp)
        got = slab(x, y, 1 - c)
        copy(0, got, got, sib).wait_recv()
        for j, (px, py, pc) in enumerate(chips):
            got = slab(px, py, 1 - c)
            copy(4 + j, got, got, sib).wait_recv()
        for cp in started:
            cp.wait_send()
        local.wait()

    return pl.kernel(
        body, out_type=jax.ShapeDtypeStruct((N_CHIPS,) + shard.shape, shard.dtype),
        mesh=_sequencer_mesh(), name=name,
        scratch_types=[pltpu.SemaphoreType.DMA((7,)), pltpu.SemaphoreType.DMA((7,)), pltpu.SemaphoreType.DMA],
        compiler_params=pltpu.CompilerParams(collective_id=collective_id),
    )(shard)


def _grad_exchange(name, g, collective_id):
    def body(in_ref, out_ref, send_sems, recv_sems):
        x, y, c = _coords()
        chip = 2 * x + y
        sib = (x, y, 1 - c)
        chips = [_flip(x, y, c, k) for k in (4, 2, 6)]
        _handshake([sib] + [(px, py, cc) for px, py, _ in chips for cc in (0, 1)])
        cps = []
        for j, (px, py, _) in enumerate(chips):
            for cc in (0, 1):
                cps.append(pltpu.make_async_remote_copy(
                    src_ref=in_ref.at[2 * px + py, cc], dst_ref=out_ref.at[1 + 2 * j + c],
                    send_sem=send_sems.at[1 + 2 * j + cc], recv_sem=recv_sems.at[1 + 2 * j + c],
                    device_id=(px, py, cc), device_id_type=MESH))
        cps.append(pltpu.make_async_remote_copy(
            src_ref=in_ref.at[chip, 1 - c], dst_ref=out_ref.at[0], send_sem=send_sems.at[0],
            recv_sem=recv_sems.at[0], device_id=sib, device_id_type=MESH))
        for cp in cps:
            cp.start()
        for cp in cps:
            cp.wait_send()
        for slot in range(N_PEERS):
            pltpu.make_async_remote_copy(
                src_ref=out_ref.at[slot], dst_ref=out_ref.at[slot], send_sem=send_sems.at[slot],
                recv_sem=recv_sems.at[slot], device_id=sib, device_id_type=MESH).wait_recv()

    return pl.kernel(
        body, out_type=jax.ShapeDtypeStruct((N_PEERS,) + g.shape[2:], g.dtype),
        mesh=_sequencer_mesh(), name=name,
        scratch_types=[pltpu.SemaphoreType.DMA((N_PEERS,)), pltpu.SemaphoreType.DMA((N_PEERS,))],
        compiler_params=pltpu.CompilerParams(collective_id=collective_id),
    )(g)


def _pair_fill(name, fulls):
    T = len(fulls)

    def body(*refs):
        outs = refs[T:2 * T]
        send_sems, recv_sems = refs[2 * T:]
        x, y, c = _coords()
        sib = (x, y, 1 - c)
        cps = []
        for t in range(T):
            send = pltpu.make_async_remote_copy(
                src_ref=outs[t].at[c], dst_ref=outs[t].at[c], send_sem=send_sems.at[t],
                recv_sem=recv_sems.at[t], device_id=sib, device_id_type=MESH)
            recv = pltpu.make_async_remote_copy(
                src_ref=outs[t].at[1 - c], dst_ref=outs[t].at[1 - c], send_sem=send_sems.at[t],
                recv_sem=recv_sems.at[t], device_id=sib, device_id_type=MESH)
            send.start()
            cps.append((send, recv))
        for send, recv in cps:
            send.wait_send()
            recv.wait_recv()

    anyspec = pl.BlockSpec(memory_space=pl.ANY)
    return pl.pallas_call(
        body, name=name,
        out_shape=[jax.ShapeDtypeStruct(f.shape, f.dtype) for f in fulls],
        in_specs=[anyspec] * T, out_specs=[anyspec] * T,
        input_output_aliases={t: t for t in range(T)},
        scratch_shapes=[pltpu.SemaphoreType.DMA((T,)), pltpu.SemaphoreType.DMA((T,))],
    )(*fulls)


def _pack(arrs):
    flat = jnp.concatenate([a.reshape(-1).astype(F32) for a in arrs])
    n = flat.shape[0]
    rows = -(-n // 1024) * 8
    return jnp.pad(flat, (0, rows * 128 - n)).reshape(rows, 128)


def _unpack(buf, shapes):
    flat = buf.reshape(-1)
    out, pos = [], 0
    for s in shapes:
        n = math.prod(s)
        out.append(flat[pos:pos + n].reshape(s))
        pos += n
    return out


def kernel(x, mix_norm_g, ffn_norm_g, final_norm_g, sc_w_in, sc_conv_w, sc_conv_b, sc_w_out, attn_w_qkv, attn_w_out, ffn_w_up, ffn_conv_w, ffn_conv_b, ffn_w_down, loss_target, m_mix_norm_g, m_ffn_norm_g, m_final_norm_g, m_sc_w_in, m_sc_conv_w, m_sc_conv_b, m_sc_w_out, m_attn_w_qkv, m_attn_w_out, m_ffn_w_up, m_ffn_conv_w, m_ffn_conv_b, m_ffn_w_down, v_mix_norm_g, v_ffn_norm_g, v_final_norm_g, v_sc_w_in, v_sc_conv_w, v_sc_conv_b, v_sc_w_out, v_attn_w_qkv, v_attn_w_out, v_ffn_w_up, v_ffn_conv_w, v_ffn_conv_b, v_ffn_w_down):
    S, D = x.shape[1], x.shape[2]
    xi, yi, ci = _coords()
    chip = 2 * xi + yi
    x0 = x.reshape(S, D)
    tgt = loss_target.reshape(S, D)

    conv_shapes = [sc_conv_w.shape, ffn_conv_w.shape]
    allc = _small_exchange("gather_conv_w", _pack([sc_conv_w, ffn_conv_w]), reduce=False)
    per_chip = [_unpack(allc[2 * k], conv_shapes) for k in range(N_CHIPS)]
    scw = jnp.concatenate([p[0] for p in per_chip], axis=-1)[0]
    fcw = jnp.concatenate([p[1] for p in per_chip], axis=-1)
    scb = sc_conv_b

    big_names = ["sc_w_in", "sc_w_out", "attn_w_qkv", "attn_w_out", "ffn_w_up", "ffn_w_down"]
    big = dict(zip(big_names, [sc_w_in, sc_w_out, attn_w_qkv, attn_w_out, ffn_w_up, ffn_w_down]))
    n_gathers = [0]

    def gather(tag, w, layers):
        _, R, C = w.shape
        shard = _cast_bf16("cast_" + tag, w, layers).reshape(2, len(layers) * R // 2, C)
        cid = n_gathers[0]
        n_gathers[0] += 1
        return _allgather_weight("allgather_" + tag, shard, cid).reshape(N_CHIPS, len(layers), R, C)

    w_in = gather("sc_w_in", sc_w_in, (0,))
    w_out = gather("sc_w_out", sc_w_out, (0,))
    w_ups = [gather("ffn_w_up0", ffn_w_up, (0,))]
    w_dn = gather("ffn_w_down", ffn_w_down, (0, 1))
    w_qkv = gather("attn_w_qkv", attn_w_qkv, (0,))
    w_ao = gather("attn_w_out", attn_w_out, (0,))
    w_ups.append(gather("ffn_w_up1", ffn_w_up, (1,)))

    def reduce_scatter(tag, g):
        cid = n_gathers[0] + big_names.index(tag)
        others = _grad_exchange("rs_exchange_" + tag, g, cid)
        return _grad_sum("rs_sum_" + tag, g, others)

    h0 = _rmsnorm_fwd("norm_mix0", x0, mix_norm_g[0:1])[0]
    z = _mm_nn_col("sc_in", h0, w_in, 0)
    y = _sc_fwd("sc_gate", z, scw, scb)
    x1 = _mm_nn_row("sc_out", y, w_out, 0, x0)
    h1 = _rmsnorm_fwd("norm_ffn0", x1, ffn_norm_g[0:1])[0]
    u0 = _mm_nn_col("ffn_up0", h1, w_ups[0], 0)
    f0, v0 = _ffn_fwd("ffn_gate0", u0, fcw[0], ffn_conv_b[0:1])
    x2 = _mm_nn_row("ffn_down0", f0, w_dn, 0, x1)
    h2s = _rmsnorm_fwd("norm_mix1", x2, mix_norm_g[1:2], dilated=True)
    qkvs = [_mm_nn_col(f"attn_qkv{d}", h, w_qkv, 0, col_off=gi * 3 * D, ncols=3 * D)
            for gi, (h, d) in enumerate(zip(h2s, DILATIONS))]
    og, lg = zip(*[_attn_fwd(f"attn_fwd{d}", q, d) for q, d in zip(qkvs, DILATIONS)])
    o32, ob, lse = _attn_combine("attn_combine", list(og), list(lg))
    x3 = _mm_nn_row("attn_out", ob, w_ao, 0, x2)
    h3 = _rmsnorm_fwd("norm_ffn1", x3, ffn_norm_g[1:2])[0]
    u1 = _mm_nn_col("ffn_up1", h3, w_ups[1], 0)
    f1, v1 = _ffn_fwd("ffn_gate1", u1, fcw[1], ffn_conv_b[1:2])
    x4 = _mm_nn_row("ffn_down1", f1, w_dn, 1, x3)

    dx4, dx4b, dg_final, loss_part = _rmsnorm_bwd("loss_norm_bwd", x4, final_norm_g.reshape(1, D), target=tgt)

    def ffn_backward(layer, xin, h, u, v, f, dxo, dxob, gain, g_up, g_dn):
        df = _mm_nt_row(f"ffn_down_dx{layer}", dxob, w_dn, layer)
        g_dn = _mm_tn_row(f"ffn_down_dw{layer}", f, dxob, prev=g_dn, layer=layer)
        du, dwb = _ffn_bwd(f"ffn_gate_bwd{layer}", u, v, df, fcw[layer])
        dh = _mm_nt_col(f"ffn_up_dx{layer}", du, w_ups[layer], 0)
        g_up = _mm_tn_col(f"ffn_up_dw{layer}", h, du, w_ups[layer].shape[3], prev=g_up, layer=layer)
        dxi, dxib, dg = _rmsnorm_bwd(f"norm_ffn_bwd{layer}", xin, gain, dhs=[(dh, 1)], dres=dxo)
        return dxi, dxib, dg, dwb, g_up, g_dn

    dx3, dx3b, dg_ffn1, dwb_ffn1, g_up, g_dn = ffn_backward(1, x3, h3, u1, v1, f1, dx4, dx4b, ffn_norm_g[1:2], None, None)

    big_grads = {}
    do = _mm_nt_row("attn_out_dx", dx3b, w_ao, 0)
    big_grads["attn_w_out"] = reduce_scatter("attn_w_out", _mm_tn_row("attn_out_dw", ob, dx3b))
    dos, lss, dls = _attn_bwd_prep("attn_bwd_prep", do, o32, lse)
    dqkvs = [_attn_bwd(f"attn_bwd{d}", q, a, b, c_, d)
             for q, a, b, c_, d in zip(qkvs, dos, lss, dls, DILATIONS)]
    dh2s = [(_mm_nt_col(f"attn_qkv_dx{d}", dq, w_qkv, 0, col_off=gi * 3 * D), d)
            for gi, (dq, d) in enumerate(zip(dqkvs, DILATIONS))]
    g_qkv = None
    for gi, (h, dq, d) in enumerate(zip(h2s, dqkvs, DILATIONS)):
        g_qkv = _mm_tn_col(f"attn_qkv_dw{d}", h, dq, w_qkv.shape[3], col_off=gi * 3 * D, prev=g_qkv)
    big_grads["attn_w_qkv"] = reduce_scatter("attn_w_qkv", g_qkv)
    dx2, dx2b, dg_mix1 = _rmsnorm_bwd("norm_mix_bwd1", x2, mix_norm_g[1:2], dhs=dh2s, dres=dx3)

    dx1, dx1b, dg_ffn0, dwb_ffn0, g_up, g_dn = ffn_backward(0, x1, h1, u0, v0, f0, dx2, dx2b, ffn_norm_g[0:1], g_up, g_dn)
    big_grads["ffn_w_down"] = reduce_scatter("ffn_w_down", g_dn)
    big_grads["ffn_w_up"] = reduce_scatter("ffn_w_up", g_up)

    dy = _mm_nt_row("sc_out_dx", dx1b, w_out, 0)
    big_grads["sc_w_out"] = reduce_scatter("sc_w_out", _mm_tn_row("sc_out_dw", y, dx1b))
    dz, dwb_sc = _sc_bwd("sc_gate_bwd", z, dy, scw, scb)
    big_grads["sc_w_in"] = reduce_scatter("sc_w_in", _mm_tn_col("sc_in_dw", h0, dz, w_in.shape[3]))
    dh0 = _mm_nt_col("sc_in_dx", dz, w_in, 0)
    dx0, _, dg_mix0 = _rmsnorm_bwd("norm_mix_bwd0", x0, mix_norm_g[0:1], dhs=[(dh0, 1)], dres=dx1)

    dconv_sc = dwb_sc[0:3].reshape(1, 3, D)
    dbias_sc = dwb_sc[3:4]
    dconv_ffn = jnp.stack([dwb_ffn0[0:3], dwb_ffn1[0:3]])
    dbias_ffn = jnp.concatenate([dwb_ffn0[3:4], dwb_ffn1[3:4]], axis=0)
    small_parts = [jnp.concatenate([dg_mix0, dg_mix1], axis=0), jnp.concatenate([dg_ffn0, dg_ffn1], axis=0),
                   dg_final.reshape(D), dconv_sc, dbias_sc, dconv_ffn, dbias_ffn, loss_part[0, 0:1]]
    small_shapes = [a.shape for a in small_parts]
    summed = _unpack(_small_exchange("allreduce_small", _pack(small_parts), reduce=True), small_shapes)
    g_mix, g_ffn, g_final, g_scw_full, g_scb, g_fcw_full, g_fcb, loss = summed
    loss = loss.reshape(())
    cw = sc_conv_w.shape[2]
    g_scw = lax.dynamic_slice_in_dim(g_scw_full, chip * cw, cw, axis=2)
    fw = ffn_conv_w.shape[2]
    g_fcw = lax.dynamic_slice_in_dim(g_fcw_full, chip * fw, fw, axis=2)

    names = ["mix_norm_g", "ffn_norm_g", "final_norm_g", "sc_w_in", "sc_conv_w", "sc_conv_b", "sc_w_out",
             "attn_w_qkv", "attn_w_out", "ffn_w_up", "ffn_conv_w", "ffn_conv_b", "ffn_w_down"]
    ws = dict(zip(names, [mix_norm_g, ffn_norm_g, final_norm_g, sc_w_in, sc_conv_w, sc_conv_b, sc_w_out,
                          attn_w_qkv, attn_w_out, ffn_w_up, ffn_conv_w, ffn_conv_b, ffn_w_down]))
    ms = dict(zip(names, [m_mix_norm_g, m_ffn_norm_g, m_final_norm_g, m_sc_w_in, m_sc_conv_w, m_sc_conv_b, m_sc_w_out,
                          m_attn_w_qkv, m_attn_w_out, m_ffn_w_up, m_ffn_conv_w, m_ffn_conv_b, m_ffn_w_down]))
    vs = dict(zip(names, [v_mix_norm_g, v_ffn_norm_g, v_final_norm_g, v_sc_w_in, v_sc_conv_w, v_sc_conv_b, v_sc_w_out,
                          v_attn_w_qkv, v_attn_w_out, v_ffn_w_up, v_ffn_conv_w, v_ffn_conv_b, v_ffn_w_down]))
    gs = {"mix_norm_g": g_mix, "ffn_norm_g": g_ffn, "final_norm_g": g_final, "sc_conv_w": g_scw,
          "sc_conv_b": g_scb, "ffn_conv_w": g_fcw, "ffn_conv_b": g_fcb}
    filled = _pair_fill("rs_pair_fill", [big_grads[n] for n in big_names])
    gs.update({n: f.reshape(big[n].shape) for n, f in zip(big_names, filled)})

    deltas, new_m, new_v = {}, {}, {}
    small_names = [n for n in names if n not in big_names]
    packed = [_pack([d[n] for n in small_names]) for d in (ws, gs, ms, vs)]
    outs = _adamw("adamw_small", *packed)
    shapes = [ws[n].shape for n in small_names]
    for res, o in zip((deltas, new_m, new_v), outs):
        res.update(dict(zip(small_names, _unpack(o, shapes))))
    for n in big_names:
        shp = ws[n].shape
        two_d = (shp[0] * shp[1], shp[2])
        outs = _adamw("adamw_" + n, *[d[n].reshape(two_d) for d in (ws, gs, ms, vs)], copy_g=True)
        for res, o in zip((deltas, new_m, new_v, gs), outs):
            res[n] = o.reshape(shp)

    return (loss, dx0.reshape(x.shape), *[gs[n] for n in names], *[deltas[n] for n in names],
            *[new_m[n] for n in names], *[new_v[n] for n in names])
```

```python
import math

import jax
import jax.numpy as jnp
from jax import lax
from jax.experimental import pallas as pl
from jax.experimental.pallas import tpu as pltpu
from jax.experimental.pallas import tpu_sc as plsc

F32 = jnp.float32
BF = jnp.bfloat16
MESH = pl.DeviceIdType.MESH

HEAD_DIM = 128
ATTN_HALF = 64
ATTN_BLOCK = 128
DILATIONS = (1, 4, 16)
STAT_LANES = 128
HALO = 16
NORM_EPS = 1e-5
ALIBI_MAX = 8.0
NEG_INF = -1e30
N_CHIPS = 4
VMEM_LIMIT = 56 * 1024 * 1024

ADAM_LR = 0.001
ADAM_B1 = 0.9
ADAM_B2 = 0.999
ADAM_EPS = 1e-08
ADAM_WD = 0.01
ADAM_STEP = 10


def _pick(n, cands):
    for c in cands:
        if n % c == 0:
            return c
    raise ValueError(f"no tile for {n} in {cands}")


def _row_tile(rows, cols, max_elems=1 << 19):
    for c in (512, 256, 128, 64, 32, 16):
        if rows % c == 0 and c * cols <= max_elems:
            return c
    raise ValueError(f"no row tile for {rows}x{cols}")


def _params(*sem):
    return pltpu.CompilerParams(dimension_semantics=sem, vmem_limit_bytes=VMEM_LIMIT)


def _matmul(name, a, b, out_shape, grid, a_spec, b_spec, o_spec, contract, acc_shape,
            res=None, res_spec=None, prev=None, b2_spec=None, after=None):
    nk = grid[2]
    n_b = 1 if b2_spec is None else 2

    def body(*refs):
        refs = list(refs)
        if prev is not None:
            refs.pop(0)
        a_ref, b_ref = refs[0], refs[1]
        res_ref = refs[1 + n_b] if res is not None else None
        o_ref = refs[-2]
        acc_ref = refs[-1]
        bv = b_ref[...]
        if bv.ndim == 3:
            bv = bv.reshape(bv.shape[0] * bv.shape[1], bv.shape[2])
        if b2_spec is None:
            part = lax.dot_general(a_ref[...], bv, contract, preferred_element_type=F32)
        else:
            half = a_ref.shape[1] // 2
            part = (lax.dot_general(a_ref[:, :half], bv, contract, preferred_element_type=F32)
                    + lax.dot_general(a_ref[:, half:], refs[2][...], contract, preferred_element_type=F32))

        def finish(total):
            if res_ref is not None:
                total = total + res_ref[...]
            o_ref[...] = total.reshape(o_ref.shape).astype(o_ref.dtype)

        if nk == 1:
            finish(part)
        else:
            k = pl.program_id(2)

            @pl.when(k == 0)
            def _():
                acc_ref[...] = part

            @pl.when(jnp.logical_and(k > 0, k < nk - 1))
            def _():
                acc_ref[...] += part

            @pl.when(k == nk - 1)
            def _():
                finish(acc_ref[...] + part)

    operands, in_specs, aliases = [], [], {}
    if prev is not None:
        operands.append(prev)
        in_specs.append(pl.BlockSpec(memory_space=pl.ANY))
        aliases = {0: 0}
    operands += [a, b]
    in_specs += [a_spec, b_spec]
    if b2_spec is not None:
        operands.append(b)
        in_specs.append(b2_spec)
    if res is not None:
        operands.append(res)
        in_specs.append(res_spec)
    if after is not None:
        operands.append(after)
        in_specs.append(pl.BlockSpec(memory_space=pl.ANY))
    return pl.pallas_call(
        body, name=name, out_shape=out_shape, grid=grid, in_specs=in_specs, out_specs=o_spec,
        scratch_shapes=[pltpu.VMEM(acc_shape if nk > 1 else (8, 128), F32)],
        input_output_aliases=aliases,
        compiler_params=_params("parallel", "parallel", "arbitrary"),
    )(*operands)


NN = (((1,), (0,)), ((), ()))
NT = (((1,), (1,)), ((), ()))
TN = (((0,), (0,)), ((), ()))

_COL_TILES = (1536, 1408, 1024, 768, 512, 384, 256, 128)
_WIDE_TILES = (2816,) + _COL_TILES


def _mm_nn_col(name, a, w, layer, col_off=0, ncols=None, out_dtype=BF):
    M, K = a.shape
    _, _, R, C = w.shape
    assert R == K
    ncols = N_CHIPS * C if ncols is None else ncols
    tn = _pick(math.gcd(C, math.gcd(ncols, col_off) if col_off else ncols), _WIDE_TILES)
    tm = _pick(M, (1024, 512, 256))
    nb, off = C // tn, col_off // tn
    return _matmul(
        name, a, w, jax.ShapeDtypeStruct((M, ncols), out_dtype), (M // tm, ncols // tn, 1),
        pl.BlockSpec((tm, K), lambda i, j, k: (i, 0)),
        pl.BlockSpec((None, None, K, tn), lambda i, j, k: ((j + off) // nb, layer, 0, (j + off) % nb)),
        pl.BlockSpec((tm, tn), lambda i, j, k: (i, j)), NN, (tm, tn))


def _mm_nn_row(name, a, w, layer, res):
    M, K = a.shape
    _, _, R, C = w.shape
    assert N_CHIPS * R == K
    chips_per_step = N_CHIPS if K <= 2048 else 2
    tk = chips_per_step * R
    tm = _pick(M, (1024, 512, 256))
    tn = _pick(C, (1024, 512, 256))
    return _matmul(
        name, a, w, jax.ShapeDtypeStruct((M, C), F32), (M // tm, C // tn, K // tk),
        pl.BlockSpec((tm, tk), lambda i, j, k: (i, k)),
        pl.BlockSpec((chips_per_step, None, R, tn), lambda i, j, k: (k, layer, 0, j)),
        pl.BlockSpec((tm, tn), lambda i, j, k: (i, j)), NN, (tm, tn),
        res=res, res_spec=pl.BlockSpec((tm, tn), lambda i, j, k: (i, j)))


def _mm_nt_col(name, dy, w, layer, col_off=0, out_dtype=F32, after=None):
    M, n = dy.shape
    _, _, R, C = w.shape
    tk = _pick(math.gcd(C, math.gcd(n, col_off) if col_off else n), _WIDE_TILES)
    tm = _pick(M, (1024, 512, 256))
    tn = _pick(R, (1024, 512, 256))
    nb, off = C // tk, col_off // tk
    per_step = 2 if (tk <= 1536 and (n // tk) % 2 == 0) else 1

    def w_block(t):
        return pl.BlockSpec((None, None, tn, tk), lambda i, j, k: (
            (per_step * k + t + off) // nb, layer, j, (per_step * k + t + off) % nb))

    return _matmul(
        name, dy, w, jax.ShapeDtypeStruct((M, R), out_dtype), (M // tm, R // tn, n // (per_step * tk)),
        pl.BlockSpec((tm, per_step * tk), lambda i, j, k: (i, k)), w_block(0),
        pl.BlockSpec((tm, tn), lambda i, j, k: (i, j)), NT, (tm, tn),
        b2_spec=w_block(1) if per_step == 2 else None, after=after)


def _mm_nt_row(name, dy, w, layer, out_dtype=BF, after=None):
    M, C2 = dy.shape
    _, _, R, C = w.shape
    assert C2 == C
    chips_per_tile = N_CHIPS if N_CHIPS * R <= 2048 else 2
    tn = chips_per_tile * R
    tm = _pick(M, (1024, 512, 256))
    return _matmul(
        name, dy, w, jax.ShapeDtypeStruct((M, N_CHIPS * R), out_dtype), (M // tm, N_CHIPS * R // tn, 1),
        pl.BlockSpec((tm, C), lambda i, j, k: (i, 0)),
        pl.BlockSpec((chips_per_tile, None, R, C), lambda i, j, k: (j, layer, 0, 0)),
        pl.BlockSpec((tm, tn), lambda i, j, k: (i, j)), NT, (tm, tn), after=after)


_TN_DEPTH = (2048, 1024, 512, 256)


def _half_index(rows, layer, tkx):
    if layer is None:
        hb = rows // 2 // tkx
        return rows // 2, lambda i: (i // hb, i % hb)
    return rows, lambda i: (layer, i)


def _mm_tn_col(name, xa, dy, C, col_off=0, prev=None, layer=None):
    M, K = xa.shape
    _, n = dy.shape
    tn = _pick(math.gcd(C, math.gcd(n, col_off) if col_off else n), _WIDE_TILES)
    tkx = _pick(K // 2 if layer is None else K, (1024, 512, 256, 128) if tn <= 1536 else (512, 256, 128))
    tmr = _pick(M, _TN_DEPTH)
    rh, split = _half_index(K, layer, tkx)
    nb, off = C // tn, col_off // tn
    return _matmul(
        name, xa, dy, jax.ShapeDtypeStruct((N_CHIPS, 2, rh, C), BF), (K // tkx, n // tn, M // tmr),
        pl.BlockSpec((tmr, tkx), lambda i, j, k: (k, i)),
        pl.BlockSpec((tmr, tn), lambda i, j, k: (k, j)),
        pl.BlockSpec((None, None, tkx, tn), lambda i, j, k: ((j + off) // nb, *split(i), (j + off) % nb)),
        TN, (tkx, tn), prev=prev)


def _mm_tn_row(name, xa, dy, prev=None, layer=None):
    M, K = xa.shape
    _, C = dy.shape
    R = K // N_CHIPS
    tmr = _pick(M, _TN_DEPTH)
    if layer is None and prev is None and K <= 2048:
        tn = _pick(C, (1024, 512, 256))
        return _matmul(
            name, xa, dy, jax.ShapeDtypeStruct((N_CHIPS, 2, R // 2, C), BF), (1, C // tn, M // tmr),
            pl.BlockSpec((tmr, K), lambda i, j, k: (k, 0)),
            pl.BlockSpec((tmr, tn), lambda i, j, k: (k, j)),
            pl.BlockSpec((N_CHIPS, 2, R // 2, tn), lambda i, j, k: (0, 0, 0, j)),
            TN, (K, tn))
    tkx = _pick(R // 2 if layer is None else R, (1408, 1024, 512, 256, 128))
    tn = _pick(C, (2048, 1024, 512, 256) if tkx <= 512 else (1024, 512, 256))
    rh, split = _half_index(R, layer, tkx)
    rb = R // tkx
    return _matmul(
        name, xa, dy, jax.ShapeDtypeStruct((N_CHIPS, 2, rh, C), BF), (K // tkx, C // tn, M // tmr),
        pl.BlockSpec((tmr, tkx), lambda i, j, k: (k, i)),
        pl.BlockSpec((tmr, tn), lambda i, j, k: (k, j)),
        pl.BlockSpec((None, None, tkx, tn), lambda i, j, k: (i // rb, *split(i % rb), j)),
        TN, (tkx, tn), prev=prev)


NORM_ROWS = 256
LANES = 128


def _chunk_scratch(tm, width):
    return pltpu.VMEM((width // LANES, tm, LANES), F32)


def _store_chunks(scr, value):
    for c in range(scr.shape[0]):
        scr[c] = value[:, c * LANES:(c + 1) * LANES]


def _load_chunks(scr):
    return jnp.concatenate([scr[c] for c in range(scr.shape[0])], axis=1)


def _to_residue_major(scr, o_ref, d, dtype):
    tm = scr.shape[1]
    for c in range(scr.shape[0]):
        for res in range(d):
            o_ref[res, :, c * LANES:(c + 1) * LANES] = scr[c, pl.ds(res, tm // d, stride=d), :].astype(dtype)


def _to_natural(scr, ref, d):
    tm = scr.shape[1]
    for c in range(scr.shape[0]):
        for res in range(d):
            scr[c, pl.ds(res, tm // d, stride=d), :] = ref[res, :, c * LANES:(c + 1) * LANES].astype(F32)


def _rmsnorm_fwd(name, x, g, dilated=False):
    S, D = x.shape
    tm = NORM_ROWS
    dils = DILATIONS[1:] if dilated else ()

    def body(x_ref, g_ref, h_ref, *rest):
        xv = x_ref[...]
        r = lax.rsqrt(jnp.mean(xv * xv, axis=1, keepdims=True) + NORM_EPS)
        h = xv * r * g_ref[...]
        h_ref[...] = h.astype(BF)
        if dils:
            scr = rest[-1]
            _store_chunks(scr, h)
            for o_ref, d in zip(rest[:-1], dils):
                _to_residue_major(scr, o_ref, d, BF)

    out_shape = [jax.ShapeDtypeStruct((S, D), BF)]
    out_specs = [pl.BlockSpec((tm, D), lambda i: (i, 0))]
    for d in dils:
        out_shape.append(jax.ShapeDtypeStruct((d, S // d, D), BF))
        out_specs.append(pl.BlockSpec((d, tm // d, D), lambda i: (0, i, 0)))
    outs = pl.pallas_call(
        body, name=name, out_shape=out_shape, grid=(S // tm,),
        in_specs=[pl.BlockSpec((tm, D), lambda i: (i, 0)), pl.BlockSpec((1, D), lambda i: (0, 0))],
        out_specs=out_specs,
        scratch_shapes=[_chunk_scratch(tm, D)] if dils else [],
        compiler_params=_params("parallel"),
    )(x, g)
    return [outs[0]] + [o.reshape(S, D) for o in outs[1:]]


def _rmsnorm_bwd(name, x, g, dhs=(), dres=None, target=None):
    S, D = x.shape
    tm = NORM_ROWS
    n_dh = len(dhs)

    def body(*refs):
        refs = list(refs)
        x_ref, g_ref = refs[0], refs[1]
        dh_refs = refs[2:2 + n_dh]
        pos = 2 + n_dh
        dres_ref = tgt_ref = None
        if dres is not None:
            dres_ref = refs[pos]
            pos += 1
        if target is not None:
            tgt_ref = refs[pos]
            pos += 1
        dx_ref, dxb_ref, dg_ref = refs[pos:pos + 3]
        pos += 3
        loss_ref = None
        if target is not None:
            loss_ref = refs[pos]
            pos += 1
        scr = refs[pos] if any(d > 1 for _, d in dhs) else None
        i = pl.program_id(0)

        xv = x_ref[...]
        gv = g_ref[...]
        r = lax.rsqrt(jnp.mean(xv * xv, axis=1, keepdims=True) + NORM_EPS)
        xhat = xv * r
        if target is not None:
            err = xhat * gv - tgt_ref[...]
            dh = err * (1.0 / D)
            part = jnp.sum(jnp.sum(err * err, axis=1, keepdims=True), axis=0, keepdims=True) * (0.5 / D)
        else:
            dh = None
            for ref, d in zip(dh_refs, [d for _, d in dhs]):
                if d == 1:
                    v = ref[...]
                else:
                    _to_natural(scr, ref, d)
                    v = _load_chunks(scr)
                dh = v if dh is None else dh + v
        dxhat = dh * gv
        dx = r * (dxhat - xhat * jnp.mean(dxhat * xhat, axis=1, keepdims=True))
        if dres_ref is not None:
            dx = dx + dres_ref[...]
        dx_ref[...] = dx
        dxb_ref[...] = dx.astype(BF)
        dg = jnp.sum(dh * xhat, axis=0, keepdims=True)

        @pl.when(i == 0)
        def _():
            dg_ref[...] = dg
            if loss_ref is not None:
                loss_ref[...] = jnp.broadcast_to(part, loss_ref.shape)

        @pl.when(i > 0)
        def _():
            dg_ref[...] += dg
            if loss_ref is not None:
                loss_ref[...] += jnp.broadcast_to(part, loss_ref.shape)

    row = pl.BlockSpec((tm, D), lambda i: (i, 0))
    operands = [x, g]
    in_specs = [row, pl.BlockSpec((1, D), lambda i: (0, 0))]
    for arr, d in dhs:
        if d == 1:
            operands.append(arr)
            in_specs.append(row)
        else:
            operands.append(arr.reshape(d, S // d, D))
            in_specs.append(pl.BlockSpec((d, tm // d, D), lambda i: (0, i, 0)))
    if dres is not None:
        operands.append(dres)
        in_specs.append(row)
    if target is not None:
        operands.append(target)
        in_specs.append(row)
    out_shape = [jax.ShapeDtypeStruct((S, D), F32), jax.ShapeDtypeStruct((S, D), BF),
                 jax.ShapeDtypeStruct((1, D), F32)]
    out_specs = [row, row, pl.BlockSpec((1, D), lambda i: (0, 0))]
    if target is not None:
        out_shape.append(jax.ShapeDtypeStruct((1, STAT_LANES), F32))
        out_specs.append(pl.BlockSpec((1, STAT_LANES), lambda i: (0, 0)))
    scratch = [_chunk_scratch(tm, D)] if any(d > 1 for _, d in dhs) else []
    return pl.pallas_call(
        body, name=name, out_shape=out_shape, grid=(S // tm,), in_specs=in_specs, out_specs=out_specs,
        scratch_shapes=scratch, compiler_params=_params("arbitrary"),
    )(*operands)


CONV_ROWS = 256
CONV_COLS = 256


def _halo_specs(S, tm, width):
    nh = S // HALO
    per = tm // HALO
    cur = pl.BlockSpec((tm, width), lambda i: (i, 0))
    prev = pl.BlockSpec((HALO, width), lambda i: (jnp.maximum(i * per - 1, 0), 0))
    nxt = pl.BlockSpec((HALO, width), lambda i: (jnp.minimum((i + 1) * per, nh - 1), 0))
    return [cur, prev, nxt]


def _ext(refs, cs, inr):
    cur, prev, nxt = refs
    v = jnp.concatenate([prev[:, cs], cur[:, cs], nxt[:, cs]], axis=0).astype(F32)
    return v if inr is None else jnp.where(inr, v, 0.0)


def _shift_prev(v):
    return pltpu.roll(v, 1, 0)


def _shift_next(v):
    return pltpu.roll(v, v.shape[0] - 1, 0)


def _shifts(v):
    return _shift_prev(v), _shift_next(v)


def _conv3(v, w, cs, b=None, shifted=None):
    vp, vn = _shifts(v) if shifted is None else shifted
    out = w[0:1, cs] * vp + w[1:2, cs] * v + w[2:3, cs] * vn
    return out if b is None else out + b[:, cs]


def _in_range(i, tm, tc, S):
    row = lax.broadcasted_iota(jnp.int32, (tm + 2 * HALO, tc), 0) + (i * tm - HALO)
    return jnp.logical_and(row >= 0, row < S)


def _core(v, tm):
    return v[HALO:HALO + tm, :]


def _acc_rows(ref, rows):
    for r, cs, val in rows:
        ref[r:r + 1, cs] += val


def _zero_first(ref, i):
    @pl.when(i == 0)
    def _():
        ref[...] = jnp.zeros(ref.shape, ref.dtype)


def _sc_fwd(name, z, w, b):
    S, D3 = z.shape
    D = D3 // 3
    tm, tc = CONV_ROWS, _pick(D, (CONV_COLS, 256, 128))

    def body(zc, zp, zn, w_ref, b_ref, y_ref):
        i = pl.program_id(0)
        inr = _in_range(i, tm, tc, S)
        zr = (zc, zp, zn)
        for c in range(D // tc):
            cs = slice(c * tc, (c + 1) * tc)
            u = _ext(zr, cs, inr)
            gc = _ext(zr, slice(2 * D + c * tc, 2 * D + (c + 1) * tc), None)
            conv = _conv3(gc * u, w_ref, cs, b_ref)
            gb = zc[:, D + c * tc:D + (c + 1) * tc].astype(F32)
            y_ref[:, cs] = (gb * _core(conv, tm)).astype(BF)

    return pl.pallas_call(
        body, name=name, out_shape=jax.ShapeDtypeStruct((S, D), BF), grid=(S // tm,),
        in_specs=_halo_specs(S, tm, D3) + [pl.BlockSpec((3, D), lambda i: (0, 0)),
                                           pl.BlockSpec((1, D), lambda i: (0, 0))],
        out_specs=pl.BlockSpec((tm, D), lambda i: (i, 0)),
        compiler_params=_params("parallel"),
    )(z, z, z, w, b)


def _sc_bwd(name, z, dy, w, b):
    S, D3 = z.shape
    D = D3 // 3
    tm, tc = CONV_ROWS, _pick(D, (CONV_COLS, 256, 128))

    def body(zc, zp, zn, dc_, dp_, dn_, w_ref, b_ref, dz_ref, dwb_ref):
        i = pl.program_id(0)
        inr = _in_range(i, tm, tc, S)
        _zero_first(dwb_ref, i)
        zr, dr = (zc, zp, zn), (dc_, dp_, dn_)
        for c in range(D // tc):
            cs = slice(c * tc, (c + 1) * tc)
            u = _ext(zr, cs, inr)
            gb = _ext(zr, slice(D + c * tc, D + (c + 1) * tc), None)
            gc = _ext(zr, slice(2 * D + c * tc, 2 * D + (c + 1) * tc), None)
            dyv = _ext(dr, cs, inr)
            p = gc * u
            p_prev, p_next = _shifts(p)
            conv = _conv3(p, w_ref, cs, b_ref, shifted=(p_prev, p_next))
            dconv = dyv * gb
            dp = w_ref[0:1, cs] * _shift_next(dconv) + w_ref[1:2, cs] * dconv + w_ref[2:3, cs] * _shift_prev(dconv)
            dz_ref[:, cs] = _core(dp * gc, tm).astype(BF)
            dz_ref[:, D + c * tc:D + (c + 1) * tc] = _core(dyv * conv, tm).astype(BF)
            dz_ref[:, 2 * D + c * tc:2 * D + (c + 1) * tc] = _core(dp * u, tm).astype(BF)
            dcc = _core(dconv, tm)
            _acc_rows(dwb_ref, [
                (0, cs, jnp.sum(dcc * _core(p_prev, tm), axis=0, keepdims=True)),
                (1, cs, jnp.sum(dcc * _core(p, tm), axis=0, keepdims=True)),
                (2, cs, jnp.sum(dcc * _core(p_next, tm), axis=0, keepdims=True)),
                (3, cs, jnp.sum(dcc, axis=0, keepdims=True))])

    return pl.pallas_call(
        body, name=name,
        out_shape=[jax.ShapeDtypeStruct((S, D3), BF), jax.ShapeDtypeStruct((4, D), F32)], grid=(S // tm,),
        in_specs=_halo_specs(S, tm, D3) + _halo_specs(S, tm, D) + [
            pl.BlockSpec((3, D), lambda i: (0, 0)), pl.BlockSpec((1, D), lambda i: (0, 0))],
        out_specs=[pl.BlockSpec((tm, D3), lambda i: (i, 0)), pl.BlockSpec((4, D), lambda i: (0, 0))],
        compiler_params=_params("arbitrary"),
    )(z, z, z, dy, dy, dy, w, b)


def _sigmoid(v):
    return 1.0 / (1.0 + jnp.exp(-v))


def _ffn_fwd(name, u, w, b):
    S, F2 = u.shape
    Fh = F2 // 2
    tm, tc = CONV_ROWS, _pick(Fh, (CONV_COLS, 256, 128))

    def body(uc, up, un, w_ref, b_ref, f_ref, v_ref):
        i = pl.program_id(0)
        inr = _in_range(i, tm, tc, S)
        ur = (uc, up, un)
        for c in range(Fh // tc):
            ca = slice(c * tc, (c + 1) * tc)
            cb = slice(Fh + c * tc, Fh + (c + 1) * tc)
            va = _core(_conv3(_ext(ur, ca, inr), w_ref, ca, b_ref), tm)
            vb = _core(_conv3(_ext(ur, cb, inr), w_ref, cb, b_ref), tm)
            v_ref[:, ca] = va.astype(BF)
            v_ref[:, cb] = vb.astype(BF)
            f_ref[:, ca] = (va * _sigmoid(va) * vb).astype(BF)

    return pl.pallas_call(
        body, name=name,
        out_shape=[jax.ShapeDtypeStruct((S, Fh), BF), jax.ShapeDtypeStruct((S, F2), BF)], grid=(S // tm,),
        in_specs=_halo_specs(S, tm, F2) + [pl.BlockSpec((3, F2), lambda i: (0, 0)),
                                           pl.BlockSpec((1, F2), lambda i: (0, 0))],
        out_specs=[pl.BlockSpec((tm, Fh), lambda i: (i, 0)), pl.BlockSpec((tm, F2), lambda i: (i, 0))],
        compiler_params=_params("parallel"),
    )(u, u, u, w, b)


def _ffn_bwd(name, u, v, df, w):
    S, F2 = u.shape
    Fh = F2 // 2
    tm, tc = CONV_ROWS, _pick(Fh, (CONV_COLS, 256, 128))

    def body(u_ref, vc, vp, vn, dc_, dp_, dn_, w_ref, du_ref, dwb_ref):
        i = pl.program_id(0)
        inr = _in_range(i, tm, tc, S)
        _zero_first(dwb_ref, i)
        vr, dr = (vc, vp, vn), (dc_, dp_, dn_)
        for c in range(Fh // tc):
            ca = slice(c * tc, (c + 1) * tc)
            cb = slice(Fh + c * tc, Fh + (c + 1) * tc)
            va, vb = _ext(vr, ca, None), _ext(vr, cb, None)
            dfv = _ext(dr, ca, inr)
            sg = _sigmoid(va)
            dva = dfv * vb * (sg * (1.0 + va * (1.0 - sg)))
            dvb = dfv * (va * sg)
            rows = []
            for cs, dv in ((ca, dva), (cb, dvb)):
                dv_prev, dv_next = _shifts(dv)
                dcore = w_ref[0:1, cs] * dv_next + w_ref[1:2, cs] * dv + w_ref[2:3, cs] * dv_prev
                du_ref[:, cs] = _core(dcore, tm).astype(BF)
                uu = u_ref[:, cs].astype(F32)
                dvc = _core(dv, tm)
                rows += [
                    (0, cs, jnp.sum(_core(dv_next, tm) * uu, axis=0, keepdims=True)),
                    (1, cs, jnp.sum(dvc * uu, axis=0, keepdims=True)),
                    (2, cs, jnp.sum(_core(dv_prev, tm) * uu, axis=0, keepdims=True)),
                    (3, cs, jnp.sum(dvc, axis=0, keepdims=True))]
            _acc_rows(dwb_ref, rows)

    return pl.pallas_call(
        body, name=name,
        out_shape=[jax.ShapeDtypeStruct((S, F2), BF), jax.ShapeDtypeStruct((4, F2), F32)], grid=(S // tm,),
        in_specs=[pl.BlockSpec((tm, F2), lambda i: (i, 0))] + _halo_specs(S, tm, F2) + _halo_specs(S, tm, Fh) + [
            pl.BlockSpec((3, F2), lambda i: (0, 0))],
        out_specs=[pl.BlockSpec((tm, F2), lambda i: (i, 0)), pl.BlockSpec((4, F2), lambda i: (0, 0))],
        compiler_params=_params("arbitrary"),
    )(u, v, v, v, df, df, df, w)


def _alibi_slopes(H):
    return [2.0 ** (-ALIBI_MAX * (h + 1) / H) for h in range(H)]


def _window_specs(S, width, col):
    n64 = S // ATTN_HALF
    cur = pl.BlockSpec((ATTN_BLOCK, width), lambda b: (b, col))
    prev = pl.BlockSpec((ATTN_HALF, width), lambda b: (jnp.maximum(2 * b - 1, 0), col))
    nxt = pl.BlockSpec((ATTN_HALF, width), lambda b: (jnp.minimum(2 * b + 2, n64 - 1), col))
    return [cur, prev, nxt]


def _fill_window(buf, cur, prev, nxt):
    buf[0:ATTN_HALF] = prev[...]
    buf[ATTN_HALF:ATTN_HALF + ATTN_BLOCK] = cur[...]
    buf[ATTN_HALF + ATTN_BLOCK:2 * ATTN_BLOCK] = nxt[...]


def _band(b, L):
    QB, W = ATTN_BLOCK, 2 * ATTN_BLOCK
    a_loc = (b * QB) % L
    row = lax.broadcasted_iota(jnp.int32, (QB, W), 0)
    col = lax.broadcasted_iota(jnp.int32, (QB, W), 1)
    dist = jnp.abs(col - ATTN_HALF - row)
    other = a_loc - ATTN_HALF + col
    valid = jnp.logical_and(dist <= ATTN_HALF, jnp.logical_and(other >= 0, other < L))
    return dist.astype(F32), valid


def _lane_col(stats, h):
    lane = lax.broadcasted_iota(jnp.int32, stats.shape, 1)
    return jnp.sum(jnp.where(lane == h, stats, 0.0), axis=1, keepdims=True)


def _attn_fwd(name, qkv, d):
    S, D3 = qkv.shape
    D = D3 // 3
    H = D // HEAD_DIM
    L = S // d
    scale = HEAD_DIM ** -0.5
    slopes = _alibi_slopes(H)

    def body(q_ref, kc, kp, kn, vc, vp, vn, o_ref, lse_ref, kbuf, vbuf):
        b = pl.program_id(0)
        _fill_window(kbuf, kc, kp, kn)
        _fill_window(vbuf, vc, vp, vn)
        dist, valid = _band(b, L)
        dist = dist * float(d)
        lane = lax.broadcasted_iota(jnp.int32, (ATTN_BLOCK, STAT_LANES), 1)
        lse = jnp.zeros((ATTN_BLOCK, STAT_LANES), F32)
        for h in range(H):
            cs = slice(h * HEAD_DIM, (h + 1) * HEAD_DIM)
            s = lax.dot_general(q_ref[:, cs], kbuf[:, cs], NT, preferred_element_type=F32) * scale
            s = jnp.where(valid, s - slopes[h] * dist, NEG_INF)
            m = jnp.max(s, axis=1, keepdims=True)
            p = jnp.exp(s - m)
            den = jnp.sum(p, axis=1, keepdims=True)
            o = jnp.dot(p.astype(BF), vbuf[:, cs], preferred_element_type=F32)
            o_ref[:, cs] = (o / den).astype(BF)
            lse = jnp.where(lane == h, m + jnp.log(den), lse)
        lse_ref[...] = lse

    return pl.pallas_call(
        body, name=name,
        out_shape=[jax.ShapeDtypeStruct((S, D), BF), jax.ShapeDtypeStruct((S, STAT_LANES), F32)],
        grid=(S // ATTN_BLOCK,),
        in_specs=[pl.BlockSpec((ATTN_BLOCK, D), lambda b: (b, 0))] + _window_specs(S, D, 1) + _window_specs(S, D, 2),
        out_specs=[pl.BlockSpec((ATTN_BLOCK, D), lambda b: (b, 0)),
                   pl.BlockSpec((ATTN_BLOCK, STAT_LANES), lambda b: (b, 0))],
        scratch_shapes=[pltpu.VMEM((2 * ATTN_BLOCK, D), BF), pltpu.VMEM((2 * ATTN_BLOCK, D), BF)],
        compiler_params=_params("parallel"),
    )(qkv, qkv, qkv, qkv, qkv, qkv, qkv)


def _dil_specs(S, tm, width):
    specs = [pl.BlockSpec((tm, width), lambda i: (i, 0))]
    for d in DILATIONS[1:]:
        specs.append(pl.BlockSpec((d, tm // d, width), lambda i: (0, i, 0)))
    return specs


def _attn_combine(name, outs, lses):
    S, D = outs[0].shape
    H = D // HEAD_DIM
    tm = NORM_ROWS

    def body(o1, o4, o16, l1, l4, l16, o_ref, ob_ref, lse_ref, oscr, lscr):
        ls = [l1[...]]
        for ref, d in zip((l4, l16), DILATIONS[1:]):
            _to_natural(lscr, ref, d)
            ls.append(lscr[0])
        top = jnp.maximum(jnp.maximum(ls[0], ls[1]), ls[2])
        es = [jnp.exp(l - top) for l in ls]
        tot = es[0] + es[1] + es[2]
        lse_ref[...] = top + jnp.log(tot)
        ws = [e / tot for e in es]
        for gi, (ref, d) in enumerate(zip((o1, o4, o16), DILATIONS)):
            if d > 1:
                _to_natural(oscr, ref, d)
            for h in range(H):
                cs = slice(h * HEAD_DIM, (h + 1) * HEAD_DIM)
                term = _lane_col(ws[gi], h) * (ref[:, cs] if d == 1 else oscr[h])
                if gi == 0:
                    o_ref[:, cs] = term
                else:
                    o_ref[:, cs] += term
        ob_ref[...] = o_ref[...].astype(BF)

    outs3 = [outs[0]] + [o.reshape(d, S // d, D) for o, d in zip(outs[1:], DILATIONS[1:])]
    lses3 = [lses[0]] + [l.reshape(d, S // d, STAT_LANES) for l, d in zip(lses[1:], DILATIONS[1:])]
    row = pl.BlockSpec((tm, D), lambda i: (i, 0))
    return pl.pallas_call(
        body, name=name,
        out_shape=[jax.ShapeDtypeStruct((S, D), F32), jax.ShapeDtypeStruct((S, D), BF),
                   jax.ShapeDtypeStruct((S, STAT_LANES), F32)],
        grid=(S // tm,),
        in_specs=_dil_specs(S, tm, D) + _dil_specs(S, tm, STAT_LANES),
        out_specs=[row, row, pl.BlockSpec((tm, STAT_LANES), lambda i: (i, 0))],
        scratch_shapes=[_chunk_scratch(tm, D), _chunk_scratch(tm, STAT_LANES)],
        compiler_params=_params("parallel"),
    )(*outs3, *lses3)


def _attn_bwd_prep(name, do, o32, lse):
    S, D = do.shape
    H = D // HEAD_DIM
    tm = NORM_ROWS
    dils = DILATIONS[1:]

    def body(do_ref, o_ref, lse_ref, dl_ref, do4, do16, l4, l16, d4, d16, scr, sscr):
        lane = lax.broadcasted_iota(jnp.int32, (tm, STAT_LANES), 1)
        delta = jnp.zeros((tm, STAT_LANES), F32)
        for h in range(H):
            cs = slice(h * HEAD_DIM, (h + 1) * HEAD_DIM)
            dov = do_ref[:, cs].astype(F32)
            scr[h] = dov
            delta = jnp.where(lane == h, jnp.sum(dov * o_ref[:, cs], axis=1, keepdims=True), delta)
        dl_ref[...] = delta
        for ref, d in zip((do4, do16), dils):
            _to_residue_major(scr, ref, d, BF)
        for val, refs in ((lse_ref[...], (l4, l16)), (delta, (d4, d16))):
            sscr[0] = val
            for ref, d in zip(refs, dils):
                _to_residue_major(sscr, ref, d, F32)

    def perm_shapes(width, dt):
        return [jax.ShapeDtypeStruct((d, S // d, width), dt) for d in dils]

    def perm_specs(width):
        return [pl.BlockSpec((d, tm // d, width), lambda i: (0, i, 0)) for d in dils]

    row = lambda w: pl.BlockSpec((tm, w), lambda i: (i, 0))
    outs = pl.pallas_call(
        body, name=name,
        out_shape=[jax.ShapeDtypeStruct((S, STAT_LANES), F32)]
        + perm_shapes(D, BF) + perm_shapes(STAT_LANES, F32) + perm_shapes(STAT_LANES, F32),
        grid=(S // tm,),
        in_specs=[row(D), row(D), row(STAT_LANES)],
        out_specs=[row(STAT_LANES)] + perm_specs(D) + perm_specs(STAT_LANES) + perm_specs(STAT_LANES),
        scratch_shapes=[_chunk_scratch(tm, D), _chunk_scratch(tm, STAT_LANES)],
        compiler_params=_params("parallel"),
    )(do, o32, lse)
    dos = [do] + [a.reshape(S, D) for a in outs[1:3]]
    lss = [lse] + [a.reshape(S, STAT_LANES) for a in outs[3:5]]
    dls = [outs[0]] + [a.reshape(S, STAT_LANES) for a in outs[5:7]]
    return dos, lss, dls


def _attn_bwd(name, qkv, do, lse, delta, d):
    S, D3 = qkv.shape
    D = D3 // 3
    H = D // HEAD_DIM
    L = S // d
    scale = HEAD_DIM ** -0.5
    slopes = _alibi_slopes(H)
    QB = ATTN_BLOCK

    def body(qc, qp, qn, kc, kp, kn, vc, vp, vn, dc_, dp_, dn_, lc, lp, ln, ec, ep, en,
             out_ref, qbuf, kbuf, vbuf, dobuf, lbuf, ebuf):
        b = pl.program_id(0)
        for buf, trio in ((qbuf, (qc, qp, qn)), (kbuf, (kc, kp, kn)), (vbuf, (vc, vp, vn)),
                          (dobuf, (dc_, dp_, dn_)), (lbuf, (lc, lp, ln)), (ebuf, (ec, ep, en))):
            _fill_window(buf, *trio)
        dist, valid = _band(b, L)
        dist = dist * float(d)
        lse_c, del_c = lc[...], ec[...]
        lse_w, del_w = lbuf[...].T, ebuf[...].T
        for h in range(H):
            cs = slice(h * HEAD_DIM, (h + 1) * HEAD_DIM)
            bias = slopes[h] * dist
            q, do_h = qc[:, cs], dc_[:, cs]
            kw, vw = kbuf[:, cs], vbuf[:, cs]
            s = lax.dot_general(q, kw, NT, preferred_element_type=F32) * scale - bias
            p = jnp.where(valid, jnp.exp(s - _lane_col(lse_c, h)), 0.0)
            dp = lax.dot_general(do_h, vw, NT, preferred_element_type=F32)
            ds = p * (dp - _lane_col(del_c, h))
            dq = jnp.dot(ds.astype(BF), kw, preferred_element_type=F32) * scale
            out_ref[:, cs] = dq.astype(BF)
            qw, dow = qbuf[:, cs], dobuf[:, cs]
            k, v = kc[:, cs], vc[:, cs]
            st = lax.dot_general(k, qw, NT, preferred_element_type=F32) * scale - bias
            pt = jnp.where(valid, jnp.exp(st - lse_w[h:h + 1, :]), 0.0)
            dv = jnp.dot(pt.astype(BF), dow, preferred_element_type=F32)
            dpt = lax.dot_general(v, dow, NT, preferred_element_type=F32)
            dst = pt * (dpt - del_w[h:h + 1, :])
            dk = jnp.dot(dst.astype(BF), qw, preferred_element_type=F32) * scale
            out_ref[:, D + h * HEAD_DIM:D + (h + 1) * HEAD_DIM] = dk.astype(BF)
            out_ref[:, 2 * D + h * HEAD_DIM:2 * D + (h + 1) * HEAD_DIM] = dv.astype(BF)

    W = 2 * QB
    return pl.pallas_call(
        body, name=name, out_shape=jax.ShapeDtypeStruct((S, D3), BF), grid=(S // QB,),
        in_specs=(_window_specs(S, D, 0) + _window_specs(S, D, 1) + _window_specs(S, D, 2)
                  + _window_specs(S, D, 0) + _window_specs(S, STAT_LANES, 0) + _window_specs(S, STAT_LANES, 0)),
        out_specs=pl.BlockSpec((QB, D3), lambda b: (b, 0)),
        scratch_shapes=[pltpu.VMEM((W, D), BF), pltpu.VMEM((W, D), BF), pltpu.VMEM((W, D), BF),
                        pltpu.VMEM((W, D), BF), pltpu.VMEM((W, STAT_LANES), F32), pltpu.VMEM((W, STAT_LANES), F32)],
        compiler_params=_params("parallel"),
    )(qkv, qkv, qkv, qkv, qkv, qkv, qkv, qkv, qkv, do, do, do, lse, lse, lse, delta, delta, delta)


def _cast_bf16(name, w, layers):
    _, R, C = w.shape
    tr = _row_tile(R, C)
    first = layers[0]

    def body(w_ref, o_ref):
        o_ref[...] = w_ref[...].astype(BF)

    return pl.pallas_call(
        body, name=name, out_shape=jax.ShapeDtypeStruct((len(layers), R, C), BF), grid=(len(layers), R // tr),
        in_specs=[pl.BlockSpec((None, tr, C), lambda l, i: (first + l, i, 0))],
        out_specs=pl.BlockSpec((None, tr, C), lambda l, i: (l, i, 0)),
        compiler_params=_params("parallel", "parallel"),
    )(w)


N_PEERS = 7


def _grad_sum(name, g, others):
    _, _, R, C = g.shape
    tr = _row_tile(R, C, 1 << 18)

    def body(g_ref, b_ref, o_ref):
        tot = g_ref[...].astype(F32)
        for s in range(N_PEERS):
            tot = tot + b_ref[s].astype(F32)
        o_ref[...] = tot

    def mine(i):
        return (2 * lax.axis_index("x") + lax.axis_index("y"), lax.axis_index("c"), i, 0)

    return pl.pallas_call(
        body, name=name, out_shape=jax.ShapeDtypeStruct((2, R, C), F32), grid=(R // tr,),
        in_specs=[pl.BlockSpec((None, None, tr, C), mine),
                  pl.BlockSpec((N_PEERS, tr, C), lambda i: (0, i, 0))],
        out_specs=pl.BlockSpec((None, tr, C), lambda i: (lax.axis_index("c"), i, 0)),
        compiler_params=_params("parallel"),
    )(g, others)


def _adamw(name, w, g, m, v, copy_g=False):
    R, C = w.shape
    tr = _row_tile(R, C, 1 << 18) if R % 16 == 0 else R
    c1 = 1.0 - ADAM_B1 ** ADAM_STEP
    c2 = 1.0 - ADAM_B2 ** ADAM_STEP

    def body(w_ref, g_ref, m_ref, v_ref, d_ref, nm_ref, nv_ref, *g_out):
        gv = g_ref[...]
        nm = ADAM_B1 * m_ref[...] + (1.0 - ADAM_B1) * gv
        nv = ADAM_B2 * v_ref[...] + (1.0 - ADAM_B2) * (gv * gv)
        nm_ref[...] = nm
        nv_ref[...] = nv
        d_ref[...] = -ADAM_LR * ((nm / c1) / (jnp.sqrt(nv / c2) + ADAM_EPS) + ADAM_WD * w_ref[...])
        if copy_g:
            g_out[0][...] = gv

    spec = pl.BlockSpec((tr, C), lambda i: (i, 0))
    n_out = 4 if copy_g else 3
    return pl.pallas_call(
        body, name=name, out_shape=[jax.ShapeDtypeStruct((R, C), F32)] * n_out, grid=(R // tr,),
        in_specs=[spec] * 4, out_specs=[spec] * n_out, compiler_params=_params("parallel"),
    )(w, g, m, v)


def _coords():
    return lax.axis_index("x"), lax.axis_index("y"), lax.axis_index("c")


def _flip(x, y, c, k):
    return (1 - x if k & 4 else x, 1 - y if k & 2 else y, 1 - c if k & 1 else c)


def _small_exchange(name, buf, reduce):
    rows = buf.shape[0]

    def body(x_ref, o_ref, land, send_sems, recv_sems):
        x, y, c = _coords()
        me = 4 * x + 2 * y + c

        def copy(k, sending):
            px, py, pc = _flip(x, y, c, k)
            slot = me if sending else 4 * px + 2 * py + pc
            return pltpu.make_async_remote_copy(
                src_ref=x_ref, dst_ref=land.at[slot], send_sem=send_sems.at[k - 1], recv_sem=recv_sems.at[k - 1],
                device_id=(px, py, pc), device_id_type=MESH)

        for k in range(1, 8):
            copy(k, True).start()
        land[me] = x_ref[...]
        for k in range(1, 8):
            copy(k, False).wait()
        if reduce:
            acc = land[0]
            for s in range(1, 8):
                acc = acc + land[s]
            o_ref[...] = acc
        else:
            o_ref[...] = land[...]

    out_shape = jax.ShapeDtypeStruct((rows, 128) if reduce else (8, rows, 128), F32)
    return pl.pallas_call(
        body, name=name, out_shape=out_shape,
        in_specs=[pl.BlockSpec(memory_space=pltpu.VMEM)], out_specs=pl.BlockSpec(memory_space=pltpu.VMEM),
        scratch_shapes=[pltpu.VMEM((8, rows, 128), F32), pltpu.SemaphoreType.DMA((7,)), pltpu.SemaphoreType.DMA((7,))],
        compiler_params=pltpu.CompilerParams(vmem_limit_bytes=VMEM_LIMIT),
    )(buf)


def _handshake(peers):
    barrier = pltpu.get_barrier_semaphore()
    for peer in peers:
        pl.semaphore_signal(barrier, inc=1, device_id=peer, device_id_type=MESH)
    pl.semaphore_wait(barrier, len(peers))


def _sequencer_mesh():
    return plsc.ScalarSubcoreMesh(axis_name="sequencer", num_cores=1)


def _allgather_weight(name, shard, collective_id):
    def body(in_ref, out_ref, send_sems, recv_sems, local_sem):
        x, y, c = _coords()
        chip = 2 * x + y
        sib = (x, y, 1 - c)
        chips = [_flip(x, y, c, k) for k in (4, 2, 6)]
        _handshake([sib] + chips)

        def slab(cx, cy, cc):
            return out_ref.at[2 * cx + cy, cc]

        def copy(k, src, dst, to):
            return pltpu.make_async_remote_copy(
                src_ref=src, dst_ref=dst, send_sem=send_sems.at[k], recv_sem=recv_sems.at[k],
                device_id=to, device_id_type=MESH)

        local = pltpu.make_async_copy(in_ref.at[c], out_ref.at[chip, c], local_sem)
        local.start()
        started = []
        for j, to in enumerate(chips):
            started.append(copy(1 + j, in_ref.at[c], slab(x, y, c), to))
        started.append(copy(0, in_ref.at[c], slab(x, y, c), sib))
        for cp in started:
            cp.start()
        for j, (px, py, pc) in enumerate(chips):
            held = slab(px, py, c)
            copy(1 + j, held, held, (px, py, pc)).wait_recv()
            cp = copy(4 + j, held, held, sib)
            cp.start()
            started.append(cp)
        got = slab(x, y, 1 - c)
        copy(0, got, got, sib).wait_recv()
        for j, (px, py, pc) in enumerate(chips):
            got = slab(px, py, 1 - c)
            copy(4 + j, got, got, sib).wait_recv()
        for cp in started:
            cp.wait_send()
        local.wait()

    return pl.kernel(
        body, out_type=jax.ShapeDtypeStruct((N_CHIPS,) + shard.shape, shard.dtype),
        mesh=_sequencer_mesh(), name=name,
        scratch_types=[pltpu.SemaphoreType.DMA((7,)), pltpu.SemaphoreType.DMA((7,)), pltpu.SemaphoreType.DMA],
        compiler_params=pltpu.CompilerParams(collective_id=collective_id),
    )(shard)


def _grad_exchange(name, g, collective_id):
    def body(in_ref, out_ref, send_sems, recv_sems):
        x, y, c = _coords()
        chip = 2 * x + y
        sib = (x, y, 1 - c)
        chips = [_flip(x, y, c, k) for k in (4, 2, 6)]
        _handshake([sib] + [(px, py, cc) for px, py, _ in chips for cc in (0, 1)])
        cps = []
        for j, (px, py, _) in enumerate(chips):
            for cc in (0, 1):
                cps.append(pltpu.make_async_remote_copy(
                    src_ref=in_ref.at[2 * px + py, cc], dst_ref=out_ref.at[1 + 2 * j + c],
                    send_sem=send_sems.at[1 + 2 * j + cc], recv_sem=recv_sems.at[1 + 2 * j + c],
                    device_id=(px, py, cc), device_id_type=MESH))
        cps.append(pltpu.make_async_remote_copy(
            src_ref=in_ref.at[chip, 1 - c], dst_ref=out_ref.at[0], send_sem=send_sems.at[0],
            recv_sem=recv_sems.at[0], device_id=sib, device_id_type=MESH))
        for cp in cps:
            cp.start()
        for cp in cps:
            cp.wait_send()
        for slot in range(N_PEERS):
            pltpu.make_async_remote_copy(
                src_ref=out_ref.at[slot], dst_ref=out_ref.at[slot], send_sem=send_sems.at[slot],
                recv_sem=recv_sems.at[slot], device_id=sib, device_id_type=MESH).wait_recv()

    return pl.kernel(
        body, out_type=jax.ShapeDtypeStruct((N_PEERS,) + g.shape[2:], g.dtype),
        mesh=_sequencer_mesh(), name=name,
        scratch_types=[pltpu.SemaphoreType.DMA((N_PEERS,)), pltpu.SemaphoreType.DMA((N_PEERS,))],
        compiler_params=pltpu.CompilerParams(collective_id=collective_id),
    )(g)


def _pair_fill(name, fulls):
    T = len(fulls)

    def body(*refs):
        outs = refs[T:2 * T]
        send_sems, recv_sems = refs[2 * T:]
        x, y, c = _coords()
        sib = (x, y, 1 - c)
        cps = []
        for t in range(T):
            send = pltpu.make_async_remote_copy(
                src_ref=outs[t].at[c], dst_ref=outs[t].at[c], send_sem=send_sems.at[t],
                recv_sem=recv_sems.at[t], device_id=sib, device_id_type=MESH)
            recv = pltpu.make_async_remote_copy(
                src_ref=outs[t].at[1 - c], dst_ref=outs[t].at[1 - c], send_sem=send_sems.at[t],
                recv_sem=recv_sems.at[t], device_id=sib, device_id_type=MESH)
            send.start()
            cps.append((send, recv))
        for send, recv in cps:
            send.wait_send()
            recv.wait_recv()

    anyspec = pl.BlockSpec(memory_space=pl.ANY)
    return pl.pallas_call(
        body, name=name,
        out_shape=[jax.ShapeDtypeStruct(f.shape, f.dtype) for f in fulls],
        in_specs=[anyspec] * T, out_specs=[anyspec] * T,
        input_output_aliases={t: t for t in range(T)},
        scratch_shapes=[pltpu.SemaphoreType.DMA((T,)), pltpu.SemaphoreType.DMA((T,))],
    )(*fulls)


def _pack(arrs):
    flat = jnp.concatenate([a.reshape(-1).astype(F32) for a in arrs])
    n = flat.shape[0]
    rows = -(-n // 1024) * 8
    return jnp.pad(flat, (0, rows * 128 - n)).reshape(rows, 128)


def _unpack(buf, shapes):
    flat = buf.reshape(-1)
    out, pos = [], 0
    for s in shapes:
        n = math.prod(s)
        out.append(flat[pos:pos + n].reshape(s))
        pos += n
    return out


def kernel(x, mix_norm_g, ffn_norm_g, final_norm_g, sc_w_in, sc_conv_w, sc_conv_b, sc_w_out, attn_w_qkv, attn_w_out, ffn_w_up, ffn_conv_w, ffn_conv_b, ffn_w_down, loss_target, m_mix_norm_g, m_ffn_norm_g, m_final_norm_g, m_sc_w_in, m_sc_conv_w, m_sc_conv_b, m_sc_w_out, m_attn_w_qkv, m_attn_w_out, m_ffn_w_up, m_ffn_conv_w, m_ffn_conv_b, m_ffn_w_down, v_mix_norm_g, v_ffn_norm_g, v_final_norm_g, v_sc_w_in, v_sc_conv_w, v_sc_conv_b, v_sc_w_out, v_attn_w_qkv, v_attn_w_out, v_ffn_w_up, v_ffn_conv_w, v_ffn_conv_b, v_ffn_w_down):
    S, D = x.shape[1], x.shape[2]
    xi, yi, ci = _coords()
    chip = 2 * xi + yi
    x0 = x.reshape(S, D)
    tgt = loss_target.reshape(S, D)

    conv_shapes = [sc_conv_w.shape, ffn_conv_w.shape]
    allc = _small_exchange("gather_conv_w", _pack([sc_conv_w, ffn_conv_w]), reduce=False)
    per_chip = [_unpack(allc[2 * k], conv_shapes) for k in range(N_CHIPS)]
    scw = jnp.concatenate([p[0] for p in per_chip], axis=-1)[0]
    fcw = jnp.concatenate([p[1] for p in per_chip], axis=-1)
    scb = sc_conv_b

    big_names = ["sc_w_in", "sc_w_out", "attn_w_qkv", "attn_w_out", "ffn_w_up", "ffn_w_down"]
    big = dict(zip(big_names, [sc_w_in, sc_w_out, attn_w_qkv, attn_w_out, ffn_w_up, ffn_w_down]))
    n_gathers = [0]

    def gather(tag, w, layers):
        _, R, C = w.shape
        shard = _cast_bf16("cast_" + tag, w, layers).reshape(2, len(layers) * R // 2, C)
        cid = n_gathers[0]
        n_gathers[0] += 1
        return _allgather_weight("allgather_" + tag, shard, cid).reshape(N_CHIPS, len(layers), R, C)

    w_in = gather("sc_w_in", sc_w_in, (0,))
    w_out = gather("sc_w_out", sc_w_out, (0,))
    w_ups = [gather("ffn_w_up0", ffn_w_up, (0,))]
    w_dn = gather("ffn_w_down", ffn_w_down, (0, 1))
    w_qkv = gather("attn_w_qkv", attn_w_qkv, (0,))
    w_ao = gather("attn_w_out", attn_w_out, (0,))
    w_ups.append(gather("ffn_w_up1", ffn_w_up, (1,)))

    def reduce_scatter(tag, g):
        cid = n_gathers[0] + big_names.index(tag)
        others = _grad_exchange("rs_exchange_" + tag, g, cid)
        return _grad_sum("rs_sum_" + tag, g, others)

    h0 = _rmsnorm_fwd("norm_mix0", x0, mix_norm_g[0:1])[0]
    z = _mm_nn_col("sc_in", h0, w_in, 0)
    y = _sc_fwd("sc_gate", z, scw, scb)
    x1 = _mm_nn_row("sc_out", y, w_out, 0, x0)
    h1 = _rmsnorm_fwd("norm_ffn0", x1, ffn_norm_g[0:1])[0]
    u0 = _mm_nn_col("ffn_up0", h1, w_ups[0], 0)
    f0, v0 = _ffn_fwd("ffn_gate0", u0, fcw[0], ffn_conv_b[0:1])
    x2 = _mm_nn_row("ffn_down0", f0, w_dn, 0, x1)
    h2s = _rmsnorm_fwd("norm_mix1", x2, mix_norm_g[1:2], dilated=True)
    qkvs = [_mm_nn_col(f"attn_qkv{d}", h, w_qkv, 0, col_off=gi * 3 * D, ncols=3 * D)
            for gi, (h, d) in enumerate(zip(h2s, DILATIONS))]
    og, lg = zip(*[_attn_fwd(f"attn_fwd{d}", q, d) for q, d in zip(qkvs, DILATIONS)])
    o32, ob, lse = _attn_combine("attn_combine", list(og), list(lg))
    x3 = _mm_nn_row("attn_out", ob, w_ao, 0, x2)
    h3 = _rmsnorm_fwd("norm_ffn1", x3, ffn_norm_g[1:2])[0]
    u1 = _mm_nn_col("ffn_up1", h3, w_ups[1], 0)
    f1, v1 = _ffn_fwd("ffn_gate1", u1, fcw[1], ffn_conv_b[1:2])
    x4 = _mm_nn_row("ffn_down1", f1, w_dn, 1, x3)

    dx4, dx4b, dg_final, loss_part = _rmsnorm_bwd("loss_norm_bwd", x4, final_norm_g.reshape(1, D), target=tgt)

    def ffn_backward(layer, xin, h, u, v, f, dxo, dxob, gain, g_up, g_dn):
        g_dn = _mm_tn_row(f"ffn_down_dw{layer}", f, dxob, prev=g_dn, layer=layer)
        df = _mm_nt_row(f"ffn_down_dx{layer}", dxob, w_dn, layer, after=g_dn)
        du, dwb = _ffn_bwd(f"ffn_gate_bwd{layer}", u, v, df, fcw[layer])
        g_up = _mm_tn_col(f"ffn_up_dw{layer}", h, du, w_ups[layer].shape[3], prev=g_up, layer=layer)
        dh = _mm_nt_col(f"ffn_up_dx{layer}", du, w_ups[layer], 0, after=g_up)
        dxi, dxib, dg = _rmsnorm_bwd(f"norm_ffn_bwd{layer}", xin, gain, dhs=[(dh, 1)], dres=dxo)
        return dxi, dxib, dg, dwb, g_up, g_dn

    dx3, dx3b, dg_ffn1, dwb_ffn1, g_up, g_dn = ffn_backward(1, x3, h3, u1, v1, f1, dx4, dx4b, ffn_norm_g[1:2], None, None)

    big_grads = {}
    g_ao = _mm_tn_row("attn_out_dw", ob, dx3b)
    big_grads["attn_w_out"] = reduce_scatter("attn_w_out", g_ao)
    do = _mm_nt_row("attn_out_dx", dx3b, w_ao, 0, after=g_ao)
    dos, lss, dls = _attn_bwd_prep("attn_bwd_prep", do, o32, lse)
    dqkvs = [_attn_bwd(f"attn_bwd{d}", q, a, b, c_, d)
             for q, a, b, c_, d in zip(qkvs, dos, lss, dls, DILATIONS)]
    g_qkv = None
    for gi, (h, dq, d) in enumerate(zip(h2s, dqkvs, DILATIONS)):
        g_qkv = _mm_tn_col(f"attn_qkv_dw{d}", h, dq, w_qkv.shape[3], col_off=gi * 3 * D, prev=g_qkv)
    big_grads["attn_w_qkv"] = reduce_scatter("attn_w_qkv", g_qkv)
    dh2s = [(_mm_nt_col(f"attn_qkv_dx{d}", dq, w_qkv, 0, col_off=gi * 3 * D, after=g_qkv), d)
            for gi, (dq, d) in enumerate(zip(dqkvs, DILATIONS))]
    dx2, dx2b, dg_mix1 = _rmsnorm_bwd("norm_mix_bwd1", x2, mix_norm_g[1:2], dhs=dh2s, dres=dx3)

    dx1, dx1b, dg_ffn0, dwb_ffn0, g_up, g_dn = ffn_backward(0, x1, h1, u0, v0, f0, dx2, dx2b, ffn_norm_g[0:1], g_up, g_dn)
    big_grads["ffn_w_down"] = reduce_scatter("ffn_w_down", g_dn)
    big_grads["ffn_w_up"] = reduce_scatter("ffn_w_up", g_up)

    g_out = _mm_tn_row("sc_out_dw", y, dx1b)
    big_grads["sc_w_out"] = reduce_scatter("sc_w_out", g_out)
    dy = _mm_nt_row("sc_out_dx", dx1b, w_out, 0, after=g_out)
    dz, dwb_sc = _sc_bwd("sc_gate_bwd", z, dy, scw, scb)
    g_in = _mm_tn_col("sc_in_dw", h0, dz, w_in.shape[3])
    big_grads["sc_w_in"] = reduce_scatter("sc_w_in", g_in)
    dh0 = _mm_nt_col("sc_in_dx", dz, w_in, 0, after=g_in)
    dx0, _, dg_mix0 = _rmsnorm_bwd("norm_mix_bwd0", x0, mix_norm_g[0:1], dhs=[(dh0, 1)], dres=dx1)

    dconv_sc = dwb_sc[0:3].reshape(1, 3, D)
    dbias_sc = dwb_sc[3:4]
    dconv_ffn = jnp.stack([dwb_ffn0[0:3], dwb_ffn1[0:3]])
    dbias_ffn = jnp.concatenate([dwb_ffn0[3:4], dwb_ffn1[3:4]], axis=0)
    small_parts = [jnp.concatenate([dg_mix0, dg_mix1], axis=0), jnp.concatenate([dg_ffn0, dg_ffn1], axis=0),
                   dg_final.reshape(D), dconv_sc, dbias_sc, dconv_ffn, dbias_ffn, loss_part[0, 0:1]]
    small_shapes = [a.shape for a in small_parts]
    summed = _unpack(_small_exchange("allreduce_small", _pack(small_parts), reduce=True), small_shapes)
    g_mix, g_ffn, g_final, g_scw_full, g_scb, g_fcw_full, g_fcb, loss = summed
    loss = loss.reshape(())
    cw = sc_conv_w.shape[2]
    g_scw = lax.dynamic_slice_in_dim(g_scw_full, chip * cw, cw, axis=2)
    fw = ffn_conv_w.shape[2]
    g_fcw = lax.dynamic_slice_in_dim(g_fcw_full, chip * fw, fw, axis=2)

    names = ["mix_norm_g", "ffn_norm_g", "final_norm_g", "sc_w_in", "sc_conv_w", "sc_conv_b", "sc_w_out",
             "attn_w_qkv", "attn_w_out", "ffn_w_up", "ffn_conv_w", "ffn_conv_b", "ffn_w_down"]
    ws = dict(zip(names, [mix_norm_g, ffn_norm_g, final_norm_g, sc_w_in, sc_conv_w, sc_conv_b, sc_w_out,
                          attn_w_qkv, attn_w_out, ffn_w_up, ffn_conv_w, ffn_conv_b, ffn_w_down]))
    ms = dict(zip(names, [m_mix_norm_g, m_ffn_norm_g, m_final_norm_g, m_sc_w_in, m_sc_conv_w, m_sc_conv_b, m_sc_w_out,
                          m_attn_w_qkv, m_attn_w_out, m_ffn_w_up, m_ffn_conv_w, m_ffn_conv_b, m_ffn_w_down]))
    vs = dict(zip(names, [v_mix_norm_g, v_ffn_norm_g, v_final_norm_g, v_sc_w_in, v_sc_conv_w, v_sc_conv_b, v_sc_w_out,
                          v_attn_w_qkv, v_attn_w_out, v_ffn_w_up, v_ffn_conv_w, v_ffn_conv_b, v_ffn_w_down]))
    gs = {"mix_norm_g": g_mix, "ffn_norm_g": g_ffn, "final_norm_g": g_final, "sc_conv_w": g_scw,
          "sc_conv_b": g_scb, "ffn_conv_w": g_fcw, "ffn_conv_b": g_fcb}
    filled = _pair_fill("rs_pair_fill", [big_grads[n] for n in big_names])
    gs.update({n: f.reshape(big[n].shape) for n, f in zip(big_names, filled)})

    deltas, new_m, new_v = {}, {}, {}
    small_names = [n for n in names if n not in big_names]
    packed = [_pack([d[n] for n in small_names]) for d in (ws, gs, ms, vs)]
    outs = _adamw("adamw_small", *packed)
    shapes = [ws[n].shape for n in small_names]
    for res, o in zip((deltas, new_m, new_v), outs):
        res.update(dict(zip(small_names, _unpack(o, shapes))))
    for n in big_names:
        shp = ws[n].shape
        two_d = (shp[0] * shp[1], shp[2])
        outs = _adamw("adamw_" + n, *[d[n].reshape(two_d) for d in (ws, gs, ms, vs)], copy_g=True)
        for res, o in zip((deltas, new_m, new_v, gs), outs):
            res[n] = o.reshape(shp)

    return (loss, dx0.reshape(x.shape), *[gs[n] for n in names], *[deltas[n] for n in names],
            *[new_m[n] for n in names], *[new_v[n] for n in names])
```

```python
import math

import jax
import jax.numpy as jnp
from jax import lax
from jax.experimental import pallas as pl
from jax.experimental.pallas import tpu as pltpu
from jax.experimental.pallas import tpu_sc as plsc

F32 = jnp.float32
BF = jnp.bfloat16
MESH = pl.DeviceIdType.MESH

HEAD_DIM = 128
ATTN_HALF = 64
ATTN_BLOCK = 128
DILATIONS = (1, 4, 16)
STAT_LANES = 128
HALO = 16
NORM_EPS = 1e-5
ALIBI_MAX = 8.0
NEG_INF = -1e30
N_CHIPS = 4
VMEM_LIMIT = 56 * 1024 * 1024

ADAM_LR = 0.001
ADAM_B1 = 0.9
ADAM_B2 = 0.999
ADAM_EPS = 1e-08
ADAM_WD = 0.01
ADAM_STEP = 10


def _pick(n, cands):
    for c in cands:
        if n % c == 0:
            return c
    raise ValueError(f"no tile for {n} in {cands}")


def _row_tile(rows, cols, max_elems=1 << 19):
    for c in (512, 256, 128, 64, 32, 16):
        if rows % c == 0 and c * cols <= max_elems:
            return c
    raise ValueError(f"no row tile for {rows}x{cols}")


def _params(*sem):
    return pltpu.CompilerParams(dimension_semantics=sem, vmem_limit_bytes=VMEM_LIMIT)


def _matmul(name, a, b, out_shape, grid, a_spec, b_spec, o_spec, contract, acc_shape,
            res=None, res_spec=None, prev=None, b2_spec=None, after=None):
    nk = grid[2]
    n_b = 1 if b2_spec is None else 2

    def body(*refs):
        refs = list(refs)
        if prev is not None:
            refs.pop(0)
        a_ref, b_ref = refs[0], refs[1]
        res_ref = refs[1 + n_b] if res is not None else None
        o_ref = refs[-2]
        acc_ref = refs[-1]
        bv = b_ref[...]
        if bv.ndim == 3:
            bv = bv.reshape(bv.shape[0] * bv.shape[1], bv.shape[2])
        if b2_spec is None:
            part = lax.dot_general(a_ref[...], bv, contract, preferred_element_type=F32)
        else:
            half = a_ref.shape[1] // 2
            part = (lax.dot_general(a_ref[:, :half], bv, contract, preferred_element_type=F32)
                    + lax.dot_general(a_ref[:, half:], refs[2][...], contract, preferred_element_type=F32))

        def finish(total):
            if res_ref is not None:
                total = total + res_ref[...]
            o_ref[...] = total.reshape(o_ref.shape).astype(o_ref.dtype)

        if nk == 1:
            finish(part)
        else:
            k = pl.program_id(2)

            @pl.when(k == 0)
            def _():
                acc_ref[...] = part

            @pl.when(jnp.logical_and(k > 0, k < nk - 1))
            def _():
                acc_ref[...] += part

            @pl.when(k == nk - 1)
            def _():
                finish(acc_ref[...] + part)

    operands, in_specs, aliases = [], [], {}
    if prev is not None:
        operands.append(prev)
        in_specs.append(pl.BlockSpec(memory_space=pl.ANY))
        aliases = {0: 0}
    operands += [a, b]
    in_specs += [a_spec, b_spec]
    if b2_spec is not None:
        operands.append(b)
        in_specs.append(b2_spec)
    if res is not None:
        operands.append(res)
        in_specs.append(res_spec)
    if after is not None:
        operands.append(after)
        in_specs.append(pl.BlockSpec(memory_space=pl.ANY))
    return pl.pallas_call(
        body, name=name, out_shape=out_shape, grid=grid, in_specs=in_specs, out_specs=o_spec,
        scratch_shapes=[pltpu.VMEM(acc_shape if nk > 1 else (8, 128), F32)],
        input_output_aliases=aliases,
        compiler_params=_params("parallel", "parallel", "arbitrary"),
    )(*operands)


NN = (((1,), (0,)), ((), ()))
NT = (((1,), (1,)), ((), ()))
TN = (((0,), (0,)), ((), ()))

_COL_TILES = (1536, 1408, 1024, 768, 512, 384, 256, 128)
_WIDE_TILES = (2816,) + _COL_TILES


def _mm_nn_col(name, a, w, layer, col_off=0, ncols=None, out_dtype=BF):
    M, K = a.shape
    _, _, R, C = w.shape
    assert R == K
    ncols = N_CHIPS * C if ncols is None else ncols
    tn = _pick(math.gcd(C, math.gcd(ncols, col_off) if col_off else ncols), _WIDE_TILES)
    tm = _pick(M, (1024, 512, 256))
    nb, off = C // tn, col_off // tn
    return _matmul(
        name, a, w, jax.ShapeDtypeStruct((M, ncols), out_dtype), (M // tm, ncols // tn, 1),
        pl.BlockSpec((tm, K), lambda i, j, k: (i, 0)),
        pl.BlockSpec((None, None, K, tn), lambda i, j, k: ((j + off) // nb, layer, 0, (j + off) % nb)),
        pl.BlockSpec((tm, tn), lambda i, j, k: (i, j)), NN, (tm, tn))


def _mm_nn_row(name, a, w, layer, res):
    M, K = a.shape
    _, _, R, C = w.shape
    assert N_CHIPS * R == K
    chips_per_step = N_CHIPS if K <= 2048 else 2
    tk = chips_per_step * R
    tm = _pick(M, (1024, 512, 256))
    tn = _pick(C, (1024, 512, 256))
    return _matmul(
        name, a, w, jax.ShapeDtypeStruct((M, C), F32), (M // tm, C // tn, K // tk),
        pl.BlockSpec((tm, tk), lambda i, j, k: (i, k)),
        pl.BlockSpec((chips_per_step, None, R, tn), lambda i, j, k: (k, layer, 0, j)),
        pl.BlockSpec((tm, tn), lambda i, j, k: (i, j)), NN, (tm, tn),
        res=res, res_spec=pl.BlockSpec((tm, tn), lambda i, j, k: (i, j)))


def _mm_nt_col(name, dy, w, layer, col_off=0, out_dtype=F32, after=None):
    M, n = dy.shape
    _, _, R, C = w.shape
    tk = _pick(math.gcd(C, math.gcd(n, col_off) if col_off else n), _WIDE_TILES)
    tm = _pick(M, (1024, 512, 256))
    tn = _pick(R, (1024, 512, 256))
    nb, off = C // tk, col_off // tk
    per_step = 2 if (tk <= 1536 and (n // tk) % 2 == 0) else 1

    def w_block(t):
        return pl.BlockSpec((None, None, tn, tk), lambda i, j, k: (
            (per_step * k + t + off) // nb, layer, j, (per_step * k + t + off) % nb))

    return _matmul(
        name, dy, w, jax.ShapeDtypeStruct((M, R), out_dtype), (M // tm, R // tn, n // (per_step * tk)),
        pl.BlockSpec((tm, per_step * tk), lambda i, j, k: (i, k)), w_block(0),
        pl.BlockSpec((tm, tn), lambda i, j, k: (i, j)), NT, (tm, tn),
        b2_spec=w_block(1) if per_step == 2 else None, after=after)


def _mm_nt_row(name, dy, w, layer, out_dtype=BF, after=None):
    M, C2 = dy.shape
    _, _, R, C = w.shape
    assert C2 == C
    chips_per_tile = N_CHIPS if N_CHIPS * R <= 2048 else 2
    tn = chips_per_tile * R
    tm = _pick(M, (1024, 512, 256))
    return _matmul(
        name, dy, w, jax.ShapeDtypeStruct((M, N_CHIPS * R), out_dtype), (M // tm, N_CHIPS * R // tn, 1),
        pl.BlockSpec((tm, C), lambda i, j, k: (i, 0)),
        pl.BlockSpec((chips_per_tile, None, R, C), lambda i, j, k: (j, layer, 0, 0)),
        pl.BlockSpec((tm, tn), lambda i, j, k: (i, j)), NT, (tm, tn), after=after)


_TN_DEPTH = (2048, 1024, 512, 256)


def _half_index(rows, layer, tkx):
    if layer is None:
        hb = rows // 2 // tkx
        return rows // 2, lambda i: (i // hb, i % hb)
    return rows, lambda i: (layer, i)


def _mm_tn_col(name, xa, dy, C, col_off=0, prev=None, layer=None):
    M, K = xa.shape
    _, n = dy.shape
    tn = _pick(math.gcd(C, math.gcd(n, col_off) if col_off else n), _WIDE_TILES)
    tkx = _pick(K // 2 if layer is None else K, (1024, 512, 256, 128) if tn <= 1536 else (512, 256, 128))
    tmr = _pick(M, _TN_DEPTH)
    rh, split = _half_index(K, layer, tkx)
    nb, off = C // tn, col_off // tn
    return _matmul(
        name, xa, dy, jax.ShapeDtypeStruct((N_CHIPS, 2, rh, C), BF), (K // tkx, n // tn, M // tmr),
        pl.BlockSpec((tmr, tkx), lambda i, j, k: (k, i)),
        pl.BlockSpec((tmr, tn), lambda i, j, k: (k, j)),
        pl.BlockSpec((None, None, tkx, tn), lambda i, j, k: ((j + off) // nb, *split(i), (j + off) % nb)),
        TN, (tkx, tn), prev=prev)


def _mm_tn_row(name, xa, dy, prev=None, layer=None):
    M, K = xa.shape
    _, C = dy.shape
    R = K // N_CHIPS
    tmr = _pick(M, _TN_DEPTH)
    if layer is None and prev is None and K <= 2048:
        tn = _pick(C, (1024, 512, 256))
        return _matmul(
            name, xa, dy, jax.ShapeDtypeStruct((N_CHIPS, 2, R // 2, C), BF), (1, C // tn, M // tmr),
            pl.BlockSpec((tmr, K), lambda i, j, k: (k, 0)),
            pl.BlockSpec((tmr, tn), lambda i, j, k: (k, j)),
            pl.BlockSpec((N_CHIPS, 2, R // 2, tn), lambda i, j, k: (0, 0, 0, j)),
            TN, (K, tn))
    tkx = _pick(R // 2 if layer is None else R, (1408, 1024, 512, 256, 128))
    tn = _pick(C, (2048, 1024, 512, 256) if tkx <= 512 else (1024, 512, 256))
    rh, split = _half_index(R, layer, tkx)
    rb = R // tkx
    return _matmul(
        name, xa, dy, jax.ShapeDtypeStruct((N_CHIPS, 2, rh, C), BF), (K // tkx, C // tn, M // tmr),
        pl.BlockSpec((tmr, tkx), lambda i, j, k: (k, i)),
        pl.BlockSpec((tmr, tn), lambda i, j, k: (k, j)),
        pl.BlockSpec((None, None, tkx, tn), lambda i, j, k: (i // rb, *split(i % rb), j)),
        TN, (tkx, tn), prev=prev)


NORM_ROWS = 256
LANES = 128


def _chunk_scratch(tm, width):
    return pltpu.VMEM((width // LANES, tm, LANES), F32)


def _store_chunks(scr, value):
    for c in range(scr.shape[0]):
        scr[c] = value[:, c * LANES:(c + 1) * LANES]


def _load_chunks(scr):
    return jnp.concatenate([scr[c] for c in range(scr.shape[0])], axis=1)


def _to_residue_major(scr, o_ref, d, dtype):
    tm = scr.shape[1]
    for c in range(scr.shape[0]):
        for res in range(d):
            o_ref[res, :, c * LANES:(c + 1) * LANES] = scr[c, pl.ds(res, tm // d, stride=d), :].astype(dtype)


def _to_natural(scr, ref, d):
    tm = scr.shape[1]
    for c in range(scr.shape[0]):
        for res in range(d):
            scr[c, pl.ds(res, tm // d, stride=d), :] = ref[res, :, c * LANES:(c + 1) * LANES].astype(F32)


def _rmsnorm_fwd(name, x, g, dilated=False):
    S, D = x.shape
    tm = NORM_ROWS
    dils = DILATIONS[1:] if dilated else ()

    def body(x_ref, g_ref, h_ref, *rest):
        xv = x_ref[...]
        r = lax.rsqrt(jnp.mean(xv * xv, axis=1, keepdims=True) + NORM_EPS)
        h = xv * r * g_ref[...]
        h_ref[...] = h.astype(BF)
        if dils:
            scr = rest[-1]
            _store_chunks(scr, h)
            for o_ref, d in zip(rest[:-1], dils):
                _to_residue_major(scr, o_ref, d, BF)

    out_shape = [jax.ShapeDtypeStruct((S, D), BF)]
    out_specs = [pl.BlockSpec((tm, D), lambda i: (i, 0))]
    for d in dils:
        out_shape.append(jax.ShapeDtypeStruct((d, S // d, D), BF))
        out_specs.append(pl.BlockSpec((d, tm // d, D), lambda i: (0, i, 0)))
    outs = pl.pallas_call(
        body, name=name, out_shape=out_shape, grid=(S // tm,),
        in_specs=[pl.BlockSpec((tm, D), lambda i: (i, 0)), pl.BlockSpec((1, D), lambda i: (0, 0))],
        out_specs=out_specs,
        scratch_shapes=[_chunk_scratch(tm, D)] if dils else [],
        compiler_params=_params("parallel"),
    )(x, g)
    return [outs[0]] + [o.reshape(S, D) for o in outs[1:]]


def _rmsnorm_bwd(name, x, g, dhs=(), dres=None, target=None):
    S, D = x.shape
    tm = NORM_ROWS
    n_dh = len(dhs)

    def body(*refs):
        refs = list(refs)
        x_ref, g_ref = refs[0], refs[1]
        dh_refs = refs[2:2 + n_dh]
        pos = 2 + n_dh
        dres_ref = tgt_ref = None
        if dres is not None:
            dres_ref = refs[pos]
            pos += 1
        if target is not None:
            tgt_ref = refs[pos]
            pos += 1
        dx_ref, dxb_ref, dg_ref = refs[pos:pos + 3]
        pos += 3
        loss_ref = None
        if target is not None:
            loss_ref = refs[pos]
            pos += 1
        scr = refs[pos] if any(d > 1 for _, d in dhs) else None
        i = pl.program_id(0)

        xv = x_ref[...]
        gv = g_ref[...]
        r = lax.rsqrt(jnp.mean(xv * xv, axis=1, keepdims=True) + NORM_EPS)
        xhat = xv * r
        if target is not None:
            err = xhat * gv - tgt_ref[...]
            dh = err * (1.0 / D)
            part = jnp.sum(jnp.sum(err * err, axis=1, keepdims=True), axis=0, keepdims=True) * (0.5 / D)
        else:
            dh = None
            for ref, d in zip(dh_refs, [d for _, d in dhs]):
                if d == 1:
                    v = ref[...]
                else:
                    _to_natural(scr, ref, d)
                    v = _load_chunks(scr)
                dh = v if dh is None else dh + v
        dxhat = dh * gv
        dx = r * (dxhat - xhat * jnp.mean(dxhat * xhat, axis=1, keepdims=True))
        if dres_ref is not None:
            dx = dx + dres_ref[...]
        dx_ref[...] = dx
        dxb_ref[...] = dx.astype(BF)
        dg = jnp.sum(dh * xhat, axis=0, keepdims=True)

        @pl.when(i == 0)
        def _():
            dg_ref[...] = dg
            if loss_ref is not None:
                loss_ref[...] = jnp.broadcast_to(part, loss_ref.shape)

        @pl.when(i > 0)
        def _():
            dg_ref[...] += dg
            if loss_ref is not None:
                loss_ref[...] += jnp.broadcast_to(part, loss_ref.shape)

    row = pl.BlockSpec((tm, D), lambda i: (i, 0))
    operands = [x, g]
    in_specs = [row, pl.BlockSpec((1, D), lambda i: (0, 0))]
    for arr, d in dhs:
        if d == 1:
            operands.append(arr)
            in_specs.append(row)
        else:
            operands.append(arr.reshape(d, S // d, D))
            in_specs.append(pl.BlockSpec((d, tm // d, D), lambda i: (0, i, 0)))
    if dres is not None:
        operands.append(dres)
        in_specs.append(row)
    if target is not None:
        operands.append(target)
        in_specs.append(row)
    out_shape = [jax.ShapeDtypeStruct((S, D), F32), jax.ShapeDtypeStruct((S, D), BF),
                 jax.ShapeDtypeStruct((1, D), F32)]
    out_specs = [row, row, pl.BlockSpec((1, D), lambda i: (0, 0))]
    if target is not None:
        out_shape.append(jax.ShapeDtypeStruct((1, STAT_LANES), F32))
        out_specs.append(pl.BlockSpec((1, STAT_LANES), lambda i: (0, 0)))
    scratch = [_chunk_scratch(tm, D)] if any(d > 1 for _, d in dhs) else []
    return pl.pallas_call(
        body, name=name, out_shape=out_shape, grid=(S // tm,), in_specs=in_specs, out_specs=out_specs,
        scratch_shapes=scratch, compiler_params=_params("arbitrary"),
    )(*operands)


CONV_ROWS = 256
CONV_COLS = 256


def _halo_specs(S, tm, width):
    nh = S // HALO
    per = tm // HALO
    cur = pl.BlockSpec((tm, width), lambda i: (i, 0))
    prev = pl.BlockSpec((HALO, width), lambda i: (jnp.maximum(i * per - 1, 0), 0))
    nxt = pl.BlockSpec((HALO, width), lambda i: (jnp.minimum((i + 1) * per, nh - 1), 0))
    return [cur, prev, nxt]


def _ext(refs, cs, inr):
    cur, prev, nxt = refs
    v = jnp.concatenate([prev[:, cs], cur[:, cs], nxt[:, cs]], axis=0).astype(F32)
    return v if inr is None else jnp.where(inr, v, 0.0)


def _shift_prev(v):
    return pltpu.roll(v, 1, 0)


def _shift_next(v):
    return pltpu.roll(v, v.shape[0] - 1, 0)


def _shifts(v):
    return _shift_prev(v), _shift_next(v)


def _conv3(v, w, cs, b=None, shifted=None):
    vp, vn = _shifts(v) if shifted is None else shifted
    out = w[0:1, cs] * vp + w[1:2, cs] * v + w[2:3, cs] * vn
    return out if b is None else out + b[:, cs]


def _in_range(i, tm, tc, S):
    row = lax.broadcasted_iota(jnp.int32, (tm + 2 * HALO, tc), 0) + (i * tm - HALO)
    return jnp.logical_and(row >= 0, row < S)


def _core(v, tm):
    return v[HALO:HALO + tm, :]


def _acc_rows(ref, rows):
    for r, cs, val in rows:
        ref[r:r + 1, cs] += val


def _zero_first(ref, i):
    @pl.when(i == 0)
    def _():
        ref[...] = jnp.zeros(ref.shape, ref.dtype)


def _sc_fwd(name, z, w, b):
    S, D3 = z.shape
    D = D3 // 3
    tm, tc = CONV_ROWS, _pick(D, (CONV_COLS, 256, 128))

    def body(zc, zp, zn, w_ref, b_ref, y_ref):
        i = pl.program_id(0)
        inr = _in_range(i, tm, tc, S)
        zr = (zc, zp, zn)
        for c in range(D // tc):
            cs = slice(c * tc, (c + 1) * tc)
            u = _ext(zr, cs, inr)
            gc = _ext(zr, slice(2 * D + c * tc, 2 * D + (c + 1) * tc), None)
            conv = _conv3(gc * u, w_ref, cs, b_ref)
            gb = zc[:, D + c * tc:D + (c + 1) * tc].astype(F32)
            y_ref[:, cs] = (gb * _core(conv, tm)).astype(BF)

    return pl.pallas_call(
        body, name=name, out_shape=jax.ShapeDtypeStruct((S, D), BF), grid=(S // tm,),
        in_specs=_halo_specs(S, tm, D3) + [pl.BlockSpec((3, D), lambda i: (0, 0)),
                                           pl.BlockSpec((1, D), lambda i: (0, 0))],
        out_specs=pl.BlockSpec((tm, D), lambda i: (i, 0)),
        compiler_params=_params("parallel"),
    )(z, z, z, w, b)


def _sc_bwd(name, z, dy, w, b):
    S, D3 = z.shape
    D = D3 // 3
    tm, tc = CONV_ROWS, _pick(D, (CONV_COLS, 256, 128))

    def body(zc, zp, zn, dc_, dp_, dn_, w_ref, b_ref, dz_ref, dwb_ref):
        i = pl.program_id(0)
        inr = _in_range(i, tm, tc, S)
        _zero_first(dwb_ref, i)
        zr, dr = (zc, zp, zn), (dc_, dp_, dn_)
        for c in range(D // tc):
            cs = slice(c * tc, (c + 1) * tc)
            u = _ext(zr, cs, inr)
            gb = _ext(zr, slice(D + c * tc, D + (c + 1) * tc), None)
            gc = _ext(zr, slice(2 * D + c * tc, 2 * D + (c + 1) * tc), None)
            dyv = _ext(dr, cs, inr)
            p = gc * u
            p_prev, p_next = _shifts(p)
            conv = _conv3(p, w_ref, cs, b_ref, shifted=(p_prev, p_next))
            dconv = dyv * gb
            dp = w_ref[0:1, cs] * _shift_next(dconv) + w_ref[1:2, cs] * dconv + w_ref[2:3, cs] * _shift_prev(dconv)
            dz_ref[:, cs] = _core(dp * gc, tm).astype(BF)
            dz_ref[:, D + c * tc:D + (c + 1) * tc] = _core(dyv * conv, tm).astype(BF)
            dz_ref[:, 2 * D + c * tc:2 * D + (c + 1) * tc] = _core(dp * u, tm).astype(BF)
            dcc = _core(dconv, tm)
            _acc_rows(dwb_ref, [
                (0, cs, jnp.sum(dcc * _core(p_prev, tm), axis=0, keepdims=True)),
                (1, cs, jnp.sum(dcc * _core(p, tm), axis=0, keepdims=True)),
                (2, cs, jnp.sum(dcc * _core(p_next, tm), axis=0, keepdims=True)),
                (3, cs, jnp.sum(dcc, axis=0, keepdims=True))])

    return pl.pallas_call(
        body, name=name,
        out_shape=[jax.ShapeDtypeStruct((S, D3), BF), jax.ShapeDtypeStruct((4, D), F32)], grid=(S // tm,),
        in_specs=_halo_specs(S, tm, D3) + _halo_specs(S, tm, D) + [
            pl.BlockSpec((3, D), lambda i: (0, 0)), pl.BlockSpec((1, D), lambda i: (0, 0))],
        out_specs=[pl.BlockSpec((tm, D3), lambda i: (i, 0)), pl.BlockSpec((4, D), lambda i: (0, 0))],
        compiler_params=_params("arbitrary"),
    )(z, z, z, dy, dy, dy, w, b)


def _sigmoid(v):
    return 1.0 / (1.0 + jnp.exp(-v))


def _ffn_fwd(name, u, w, b):
    S, F2 = u.shape
    Fh = F2 // 2
    tm, tc = CONV_ROWS, _pick(Fh, (CONV_COLS, 256, 128))

    def body(uc, up, un, w_ref, b_ref, f_ref, v_ref):
        i = pl.program_id(0)
        inr = _in_range(i, tm, tc, S)
        ur = (uc, up, un)
        for c in range(Fh // tc):
            ca = slice(c * tc, (c + 1) * tc)
            cb = slice(Fh + c * tc, Fh + (c + 1) * tc)
            va = _core(_conv3(_ext(ur, ca, inr), w_ref, ca, b_ref), tm)
            vb = _core(_conv3(_ext(ur, cb, inr), w_ref, cb, b_ref), tm)
            v_ref[:, ca] = va.astype(BF)
            v_ref[:, cb] = vb.astype(BF)
            f_ref[:, ca] = (va * _sigmoid(va) * vb).astype(BF)

    return pl.pallas_call(
        body, name=name,
        out_shape=[jax.ShapeDtypeStruct((S, Fh), BF), jax.ShapeDtypeStruct((S, F2), BF)], grid=(S // tm,),
        in_specs=_halo_specs(S, tm, F2) + [pl.BlockSpec((3, F2), lambda i: (0, 0)),
                                           pl.BlockSpec((1, F2), lambda i: (0, 0))],
        out_specs=[pl.BlockSpec((tm, Fh), lambda i: (i, 0)), pl.BlockSpec((tm, F2), lambda i: (i, 0))],
        compiler_params=_params("parallel"),
    )(u, u, u, w, b)


def _ffn_bwd(name, u, v, df, w):
    S, F2 = u.shape
    Fh = F2 // 2
    tm, tc = CONV_ROWS, _pick(Fh, (CONV_COLS, 256, 128))

    def body(u_ref, vc, vp, vn, dc_, dp_, dn_, w_ref, du_ref, dwb_ref):
        i = pl.program_id(0)
        inr = _in_range(i, tm, tc, S)
        _zero_first(dwb_ref, i)
        vr, dr = (vc, vp, vn), (dc_, dp_, dn_)
        for c in range(Fh // tc):
            ca = slice(c * tc, (c + 1) * tc)
            cb = slice(Fh + c * tc, Fh + (c + 1) * tc)
            va, vb = _ext(vr, ca, None), _ext(vr, cb, None)
            dfv = _ext(dr, ca, inr)
            sg = _sigmoid(va)
            dva = dfv * vb * (sg * (1.0 + va * (1.0 - sg)))
            dvb = dfv * (va * sg)
            rows = []
            for cs, dv in ((ca, dva), (cb, dvb)):
                dv_prev, dv_next = _shifts(dv)
                dcore = w_ref[0:1, cs] * dv_next + w_ref[1:2, cs] * dv + w_ref[2:3, cs] * dv_prev
                du_ref[:, cs] = _core(dcore, tm).astype(BF)
                uu = u_ref[:, cs].astype(F32)
                dvc = _core(dv, tm)
                rows += [
                    (0, cs, jnp.sum(_core(dv_next, tm) * uu, axis=0, keepdims=True)),
                    (1, cs, jnp.sum(dvc * uu, axis=0, keepdims=True)),
                    (2, cs, jnp.sum(_core(dv_prev, tm) * uu, axis=0, keepdims=True)),
                    (3, cs, jnp.sum(dvc, axis=0, keepdims=True))]
            _acc_rows(dwb_ref, rows)

    return pl.pallas_call(
        body, name=name,
        out_shape=[jax.ShapeDtypeStruct((S, F2), BF), jax.ShapeDtypeStruct((4, F2), F32)], grid=(S // tm,),
        in_specs=[pl.BlockSpec((tm, F2), lambda i: (i, 0))] + _halo_specs(S, tm, F2) + _halo_specs(S, tm, Fh) + [
            pl.BlockSpec((3, F2), lambda i: (0, 0))],
        out_specs=[pl.BlockSpec((tm, F2), lambda i: (i, 0)), pl.BlockSpec((4, F2), lambda i: (0, 0))],
        compiler_params=_params("arbitrary"),
    )(u, v, v, v, df, df, df, w)


def _alibi_slopes(H):
    return [2.0 ** (-ALIBI_MAX * (h + 1) / H) for h in range(H)]


def _window_specs(S, width, col):
    n64 = S // ATTN_HALF
    cur = pl.BlockSpec((ATTN_BLOCK, width), lambda b: (b, col))
    prev = pl.BlockSpec((ATTN_HALF, width), lambda b: (jnp.maximum(2 * b - 1, 0), col))
    nxt = pl.BlockSpec((ATTN_HALF, width), lambda b: (jnp.minimum(2 * b + 2, n64 - 1), col))
    return [cur, prev, nxt]


def _fill_window(buf, cur, prev, nxt):
    buf[0:ATTN_HALF] = prev[...]
    buf[ATTN_HALF:ATTN_HALF + ATTN_BLOCK] = cur[...]
    buf[ATTN_HALF + ATTN_BLOCK:2 * ATTN_BLOCK] = nxt[...]


def _band(b, L):
    QB, W = ATTN_BLOCK, 2 * ATTN_BLOCK
    a_loc = (b * QB) % L
    row = lax.broadcasted_iota(jnp.int32, (QB, W), 0)
    col = lax.broadcasted_iota(jnp.int32, (QB, W), 1)
    dist = jnp.abs(col - ATTN_HALF - row)
    other = a_loc - ATTN_HALF + col
    valid = jnp.logical_and(dist <= ATTN_HALF, jnp.logical_and(other >= 0, other < L))
    return dist.astype(F32), valid


def _lane_col(stats, h):
    lane = lax.broadcasted_iota(jnp.int32, stats.shape, 1)
    return jnp.sum(jnp.where(lane == h, stats, 0.0), axis=1, keepdims=True)


def _attn_fwd(name, qkv, d):
    S, D3 = qkv.shape
    D = D3 // 3
    H = D // HEAD_DIM
    L = S // d
    scale = HEAD_DIM ** -0.5
    slopes = _alibi_slopes(H)

    def body(q_ref, kc, kp, kn, vc, vp, vn, o_ref, lse_ref, kbuf, vbuf):
        b = pl.program_id(0)
        _fill_window(kbuf, kc, kp, kn)
        _fill_window(vbuf, vc, vp, vn)
        dist, valid = _band(b, L)
        dist = dist * float(d)
        lane = lax.broadcasted_iota(jnp.int32, (ATTN_BLOCK, STAT_LANES), 1)
        lse = jnp.zeros((ATTN_BLOCK, STAT_LANES), F32)
        for h in range(H):
            cs = slice(h * HEAD_DIM, (h + 1) * HEAD_DIM)
            s = lax.dot_general(q_ref[:, cs], kbuf[:, cs], NT, preferred_element_type=F32) * scale
            s = jnp.where(valid, s - slopes[h] * dist, NEG_INF)
            m = jnp.max(s, axis=1, keepdims=True)
            p = jnp.exp(s - m)
            den = jnp.sum(p, axis=1, keepdims=True)
            o = jnp.dot(p.astype(BF), vbuf[:, cs], preferred_element_type=F32)
            o_ref[:, cs] = (o / den).astype(BF)
            lse = jnp.where(lane == h, m + jnp.log(den), lse)
        lse_ref[...] = lse

    return pl.pallas_call(
        body, name=name,
        out_shape=[jax.ShapeDtypeStruct((S, D), BF), jax.ShapeDtypeStruct((S, STAT_LANES), F32)],
        grid=(S // ATTN_BLOCK,),
        in_specs=[pl.BlockSpec((ATTN_BLOCK, D), lambda b: (b, 0))] + _window_specs(S, D, 1) + _window_specs(S, D, 2),
        out_specs=[pl.BlockSpec((ATTN_BLOCK, D), lambda b: (b, 0)),
                   pl.BlockSpec((ATTN_BLOCK, STAT_LANES), lambda b: (b, 0))],
        scratch_shapes=[pltpu.VMEM((2 * ATTN_BLOCK, D), BF), pltpu.VMEM((2 * ATTN_BLOCK, D), BF)],
        compiler_params=_params("parallel"),
    )(qkv, qkv, qkv, qkv, qkv, qkv, qkv)


def _dil_specs(S, tm, width):
    specs = [pl.BlockSpec((tm, width), lambda i: (i, 0))]
    for d in DILATIONS[1:]:
        specs.append(pl.BlockSpec((d, tm // d, width), lambda i: (0, i, 0)))
    return specs


def _attn_combine(name, outs, lses):
    S, D = outs[0].shape
    H = D // HEAD_DIM
    tm = NORM_ROWS

    def body(o1, o4, o16, l1, l4, l16, o_ref, ob_ref, lse_ref, oscr, lscr):
        ls = [l1[...]]
        for ref, d in zip((l4, l16), DILATIONS[1:]):
            _to_natural(lscr, ref, d)
            ls.append(lscr[0])
        top = jnp.maximum(jnp.maximum(ls[0], ls[1]), ls[2])
        es = [jnp.exp(l - top) for l in ls]
        tot = es[0] + es[1] + es[2]
        lse_ref[...] = top + jnp.log(tot)
        ws = [e / tot for e in es]
        for gi, (ref, d) in enumerate(zip((o1, o4, o16), DILATIONS)):
            if d > 1:
                _to_natural(oscr, ref, d)
            for h in range(H):
                cs = slice(h * HEAD_DIM, (h + 1) * HEAD_DIM)
                term = _lane_col(ws[gi], h) * (ref[:, cs] if d == 1 else oscr[h])
                if gi == 0:
                    o_ref[:, cs] = term
                else:
                    o_ref[:, cs] += term
        ob_ref[...] = o_ref[...].astype(BF)

    outs3 = [outs[0]] + [o.reshape(d, S // d, D) for o, d in zip(outs[1:], DILATIONS[1:])]
    lses3 = [lses[0]] + [l.reshape(d, S // d, STAT_LANES) for l, d in zip(lses[1:], DILATIONS[1:])]
    row = pl.BlockSpec((tm, D), lambda i: (i, 0))
    return pl.pallas_call(
        body, name=name,
        out_shape=[jax.ShapeDtypeStruct((S, D), F32), jax.ShapeDtypeStruct((S, D), BF),
                   jax.ShapeDtypeStruct((S, STAT_LANES), F32)],
        grid=(S // tm,),
        in_specs=_dil_specs(S, tm, D) + _dil_specs(S, tm, STAT_LANES),
        out_specs=[row, row, pl.BlockSpec((tm, STAT_LANES), lambda i: (i, 0))],
        scratch_shapes=[_chunk_scratch(tm, D), _chunk_scratch(tm, STAT_LANES)],
        compiler_params=_params("parallel"),
    )(*outs3, *lses3)


def _attn_bwd_prep(name, do, o32, lse):
    S, D = do.shape
    H = D // HEAD_DIM
    tm = NORM_ROWS
    dils = DILATIONS[1:]

    def body(do_ref, o_ref, lse_ref, dl_ref, do4, do16, l4, l16, d4, d16, scr, sscr):
        lane = lax.broadcasted_iota(jnp.int32, (tm, STAT_LANES), 1)
        delta = jnp.zeros((tm, STAT_LANES), F32)
        for h in range(H):
            cs = slice(h * HEAD_DIM, (h + 1) * HEAD_DIM)
            dov = do_ref[:, cs].astype(F32)
            scr[h] = dov
            delta = jnp.where(lane == h, jnp.sum(dov * o_ref[:, cs], axis=1, keepdims=True), delta)
        dl_ref[...] = delta
        for ref, d in zip((do4, do16), dils):
            _to_residue_major(scr, ref, d, BF)
        for val, refs in ((lse_ref[...], (l4, l16)), (delta, (d4, d16))):
            sscr[0] = val
            for ref, d in zip(refs, dils):
                _to_residue_major(sscr, ref, d, F32)

    def perm_shapes(width, dt):
        return [jax.ShapeDtypeStruct((d, S // d, width), dt) for d in dils]

    def perm_specs(width):
        return [pl.BlockSpec((d, tm // d, width), lambda i: (0, i, 0)) for d in dils]

    row = lambda w: pl.BlockSpec((tm, w), lambda i: (i, 0))
    outs = pl.pallas_call(
        body, name=name,
        out_shape=[jax.ShapeDtypeStruct((S, STAT_LANES), F32)]
        + perm_shapes(D, BF) + perm_shapes(STAT_LANES, F32) + perm_shapes(STAT_LANES, F32),
        grid=(S // tm,),
        in_specs=[row(D), row(D), row(STAT_LANES)],
        out_specs=[row(STAT_LANES)] + perm_specs(D) + perm_specs(STAT_LANES) + perm_specs(STAT_LANES),
        scratch_shapes=[_chunk_scratch(tm, D), _chunk_scratch(tm, STAT_LANES)],
        compiler_params=_params("parallel"),
    )(do, o32, lse)
    dos = [do] + [a.reshape(S, D) for a in outs[1:3]]
    lss = [lse] + [a.reshape(S, STAT_LANES) for a in outs[3:5]]
    dls = [outs[0]] + [a.reshape(S, STAT_LANES) for a in outs[5:7]]
    return dos, lss, dls


def _attn_bwd(name, qkv, do, lse, delta, d):
    S, D3 = qkv.shape
    D = D3 // 3
    H = D // HEAD_DIM
    L = S // d
    scale = HEAD_DIM ** -0.5
    slopes = _alibi_slopes(H)
    QB = ATTN_BLOCK

    def body(qc, qp, qn, kc, kp, kn, vc, vp, vn, dc_, dp_, dn_, lc, lp, ln, ec, ep, en,
             out_ref, qbuf, kbuf, vbuf, dobuf, lbuf, ebuf):
        b = pl.program_id(0)
        for buf, trio in ((qbuf, (qc, qp, qn)), (kbuf, (kc, kp, kn)), (vbuf, (vc, vp, vn)),
                          (dobuf, (dc_, dp_, dn_)), (lbuf, (lc, lp, ln)), (ebuf, (ec, ep, en))):
            _fill_window(buf, *trio)
        dist, valid = _band(b, L)
        dist = dist * float(d)
        lse_c, del_c = lc[...], ec[...]
        lse_w, del_w = lbuf[...].T, ebuf[...].T
        for h in range(H):
            cs = slice(h * HEAD_DIM, (h + 1) * HEAD_DIM)
            bias = slopes[h] * dist
            q, do_h = qc[:, cs], dc_[:, cs]
            kw, vw = kbuf[:, cs], vbuf[:, cs]
            s = lax.dot_general(q, kw, NT, preferred_element_type=F32) * scale - bias
            p = jnp.where(valid, jnp.exp(s - _lane_col(lse_c, h)), 0.0)
            dp = lax.dot_general(do_h, vw, NT, preferred_element_type=F32)
            ds = p * (dp - _lane_col(del_c, h))
            dq = jnp.dot(ds.astype(BF), kw, preferred_element_type=F32) * scale
            out_ref[:, cs] = dq.astype(BF)
            qw, dow = qbuf[:, cs], dobuf[:, cs]
            k, v = kc[:, cs], vc[:, cs]
            st = lax.dot_general(k, qw, NT, preferred_element_type=F32) * scale - bias
            pt = jnp.where(valid, jnp.exp(st - lse_w[h:h + 1, :]), 0.0)
            dv = jnp.dot(pt.astype(BF), dow, preferred_element_type=F32)
            dpt = lax.dot_general(v, dow, NT, preferred_element_type=F32)
            dst = pt * (dpt - del_w[h:h + 1, :])
            dk = jnp.dot(dst.astype(BF), qw, preferred_element_type=F32) * scale
            out_ref[:, D + h * HEAD_DIM:D + (h + 1) * HEAD_DIM] = dk.astype(BF)
            out_ref[:, 2 * D + h * HEAD_DIM:2 * D + (h + 1) * HEAD_DIM] = dv.astype(BF)

    W = 2 * QB
    return pl.pallas_call(
        body, name=name, out_shape=jax.ShapeDtypeStruct((S, D3), BF), grid=(S // QB,),
        in_specs=(_window_specs(S, D, 0) + _window_specs(S, D, 1) + _window_specs(S, D, 2)
                  + _window_specs(S, D, 0) + _window_specs(S, STAT_LANES, 0) + _window_specs(S, STAT_LANES, 0)),
        out_specs=pl.BlockSpec((QB, D3), lambda b: (b, 0)),
        scratch_shapes=[pltpu.VMEM((W, D), BF), pltpu.VMEM((W, D), BF), pltpu.VMEM((W, D), BF),
                        pltpu.VMEM((W, D), BF), pltpu.VMEM((W, STAT_LANES), F32), pltpu.VMEM((W, STAT_LANES), F32)],
        compiler_params=_params("parallel"),
    )(qkv, qkv, qkv, qkv, qkv, qkv, qkv, qkv, qkv, do, do, do, lse, lse, lse, delta, delta, delta)


def _cast_bf16(name, w, layers):
    _, R, C = w.shape
    tr = _row_tile(R, C)
    first = layers[0]

    def body(w_ref, o_ref):
        o_ref[...] = w_ref[...].astype(BF)

    return pl.pallas_call(
        body, name=name, out_shape=jax.ShapeDtypeStruct((len(layers), R, C), BF), grid=(len(layers), R // tr),
        in_specs=[pl.BlockSpec((None, tr, C), lambda l, i: (first + l, i, 0))],
        out_specs=pl.BlockSpec((None, tr, C), lambda l, i: (l, i, 0)),
        compiler_params=_params("parallel", "parallel"),
    )(w)


N_PEERS = 7


def _grad_sum(name, g, others):
    _, _, R, C = g.shape
    tr = _row_tile(R, C, 1 << 18)

    def body(g_ref, b_ref, o_ref):
        tot = g_ref[...].astype(F32)
        for s in range(N_PEERS):
            tot = tot + b_ref[s].astype(F32)
        o_ref[...] = tot

    def mine(i):
        return (2 * lax.axis_index("x") + lax.axis_index("y"), lax.axis_index("c"), i, 0)

    return pl.pallas_call(
        body, name=name, out_shape=jax.ShapeDtypeStruct((2, R, C), F32), grid=(R // tr,),
        in_specs=[pl.BlockSpec((None, None, tr, C), mine),
                  pl.BlockSpec((N_PEERS, tr, C), lambda i: (0, i, 0))],
        out_specs=pl.BlockSpec((None, tr, C), lambda i: (lax.axis_index("c"), i, 0)),
        compiler_params=_params("parallel"),
    )(g, others)


def _adamw(name, w, g, m, v, copy_g=False):
    R, C = w.shape
    tr = _row_tile(R, C, 1 << 18) if R % 16 == 0 else R
    c1 = 1.0 - ADAM_B1 ** ADAM_STEP
    c2 = 1.0 - ADAM_B2 ** ADAM_STEP

    def body(w_ref, g_ref, m_ref, v_ref, d_ref, nm_ref, nv_ref, *g_out):
        gv = g_ref[...]
        nm = ADAM_B1 * m_ref[...] + (1.0 - ADAM_B1) * gv
        nv = ADAM_B2 * v_ref[...] + (1.0 - ADAM_B2) * (gv * gv)
        nm_ref[...] = nm
        nv_ref[...] = nv
        d_ref[...] = -ADAM_LR * ((nm / c1) / (jnp.sqrt(nv / c2) + ADAM_EPS) + ADAM_WD * w_ref[...])
        if copy_g:
            g_out[0][...] = gv

    spec = pl.BlockSpec((tr, C), lambda i: (i, 0))
    n_out = 4 if copy_g else 3
    return pl.pallas_call(
        body, name=name, out_shape=[jax.ShapeDtypeStruct((R, C), F32)] * n_out, grid=(R // tr,),
        in_specs=[spec] * 4, out_specs=[spec] * n_out, compiler_params=_params("parallel"),
    )(w, g, m, v)


def _coords():
    return lax.axis_index("x"), lax.axis_index("y"), lax.axis_index("c")


def _flip(x, y, c, k):
    return (1 - x if k & 4 else x, 1 - y if k & 2 else y, 1 - c if k & 1 else c)


def _small_exchange(name, buf, reduce):
    rows = buf.shape[0]

    def body(x_ref, o_ref, land, send_sems, recv_sems):
        x, y, c = _coords()
        me = 4 * x + 2 * y + c

        def copy(k, sending):
            px, py, pc = _flip(x, y, c, k)
            slot = me if sending else 4 * px + 2 * py + pc
            return pltpu.make_async_remote_copy(
                src_ref=x_ref, dst_ref=land.at[slot], send_sem=send_sems.at[k - 1], recv_sem=recv_sems.at[k - 1],
                device_id=(px, py, pc), device_id_type=MESH)

        for k in range(1, 8):
            copy(k, True).start()
        land[me] = x_ref[...]
        for k in range(1, 8):
            copy(k, False).wait()
        if reduce:
            acc = land[0]
            for s in range(1, 8):
                acc = acc + land[s]
            o_ref[...] = acc
        else:
            o_ref[...] = land[...]

    out_shape = jax.ShapeDtypeStruct((rows, 128) if reduce else (8, rows, 128), F32)
    return pl.pallas_call(
        body, name=name, out_shape=out_shape,
        in_specs=[pl.BlockSpec(memory_space=pltpu.VMEM)], out_specs=pl.BlockSpec(memory_space=pltpu.VMEM),
        scratch_shapes=[pltpu.VMEM((8, rows, 128), F32), pltpu.SemaphoreType.DMA((7,)), pltpu.SemaphoreType.DMA((7,))],
        compiler_params=pltpu.CompilerParams(vmem_limit_bytes=VMEM_LIMIT),
    )(buf)


def _handshake(peers):
    barrier = pltpu.get_barrier_semaphore()
    for peer in peers:
        pl.semaphore_signal(barrier, inc=1, device_id=peer, device_id_type=MESH)
    pl.semaphore_wait(barrier, len(peers))


def _sequencer_mesh():
    return plsc.ScalarSubcoreMesh(axis_name="sequencer", num_cores=1)


def _allgather_weight(name, shard, collective_id):
    def body(in_ref, out_ref, send_sems, recv_sems, local_sem):
        x, y, c = _coords()
        chip = 2 * x + y
        sib = (x, y, 1 - c)
        chips = [_flip(x, y, c, k) for k in (4, 2, 6)]
        _handshake([sib] + chips)

        def slab(cx, cy, cc):
            return out_ref.at[2 * cx + cy, cc]

        def copy(k, src, dst, to):
            return pltpu.make_async_remote_copy(
                src_ref=src, dst_ref=dst, send_sem=send_sems.at[k], recv_sem=recv_sems.at[k],
                device_id=to, device_id_type=MESH)

        local = pltpu.make_async_copy(in_ref.at[c], out_ref.at[chip, c], local_sem)
        local.start()
        started = []
        for j, to in enumerate(chips):
            started.append(copy(1 + j, in_ref.at[c], slab(x, y, c), to))
        started.append(copy(0, in_ref.at[c], slab(x, y, c), sib))
        for cp in started:
            cp.start()
        for j, (px, py, pc) in enumerate(chips):
            held = slab(px, py, c)
            copy(1 + j, held, held, (px, py, pc)).wait_recv()
            cp = copy(4 + j, held, held, sib)
            cp.start()
            started.append(cp)
        got = slab(x, y, 1 - c)
        copy(0, got, got, sib).wait_recv()
        for j, (px, py, pc) in enumerate(chips):
            got = slab(px, py, 1 - c)
            copy(4 + j, got, got, sib).wait_recv()
        for cp in started:
            cp.wait_send()
        local.wait()

    return pl.kernel(
        body, out_type=jax.ShapeDtypeStruct((N_CHIPS,) + shard.shape, shard.dtype),
        mesh=_sequencer_mesh(), name=name,
        scratch_types=[pltpu.SemaphoreType.DMA((7,)), pltpu.SemaphoreType.DMA((7,)), pltpu.SemaphoreType.DMA],
        compiler_params=pltpu.CompilerParams(collective_id=collective_id),
    )(shard)


def _grad_exchange(name, gs, collective_id, after=None):
    T = len(gs)

    def body(*refs):
        ins, outs = refs[:T], refs[-T - 2:-2]
        send_sems, recv_sems = refs[-2:]
        x, y, c = _coords()
        chip = 2 * x + y
        sib = (x, y, 1 - c)
        chips = [_flip(x, y, c, k) for k in (4, 2, 6)]
        _handshake([sib] + [(px, py, cc) for px, py, _ in chips for cc in (0, 1)])
        cps = []
        for t in range(T):
            for j, (px, py, _) in enumerate(chips):
                for cc in (0, 1):
                    cps.append(pltpu.make_async_remote_copy(
                        src_ref=ins[t].at[2 * px + py, cc], dst_ref=outs[t].at[1 + 2 * j + c],
                        send_sem=send_sems.at[t, 1 + 2 * j + cc], recv_sem=recv_sems.at[t, 1 + 2 * j + c],
                        device_id=(px, py, cc), device_id_type=MESH))
            cps.append(pltpu.make_async_remote_copy(
                src_ref=ins[t].at[chip, 1 - c], dst_ref=outs[t].at[0], send_sem=send_sems.at[t, 0],
                recv_sem=recv_sems.at[t, 0], device_id=sib, device_id_type=MESH))
        for cp in cps:
            cp.start()
        for cp in cps:
            cp.wait_send()
        for t in range(T):
            for slot in range(N_PEERS):
                pltpu.make_async_remote_copy(
                    src_ref=outs[t].at[slot], dst_ref=outs[t].at[slot], send_sem=send_sems.at[t, slot],
                    recv_sem=recv_sems.at[t, slot], device_id=sib, device_id_type=MESH).wait_recv()

    operands = list(gs) + ([after] if after is not None else [])
    return pl.kernel(
        body, out_type=[jax.ShapeDtypeStruct((N_PEERS,) + g.shape[2:], g.dtype) for g in gs],
        mesh=_sequencer_mesh(), name=name,
        scratch_types=[pltpu.SemaphoreType.DMA((T, N_PEERS)), pltpu.SemaphoreType.DMA((T, N_PEERS))],
        compiler_params=pltpu.CompilerParams(collective_id=collective_id),
    )(*operands)


def _pair_fill(name, fulls):
    T = len(fulls)

    def body(*refs):
        outs = refs[T:2 * T]
        send_sems, recv_sems = refs[2 * T:]
        x, y, c = _coords()
        sib = (x, y, 1 - c)
        cps = []
        for t in range(T):
            send = pltpu.make_async_remote_copy(
                src_ref=outs[t].at[c], dst_ref=outs[t].at[c], send_sem=send_sems.at[t],
                recv_sem=recv_sems.at[t], device_id=sib, device_id_type=MESH)
            recv = pltpu.make_async_remote_copy(
                src_ref=outs[t].at[1 - c], dst_ref=outs[t].at[1 - c], send_sem=send_sems.at[t],
                recv_sem=recv_sems.at[t], device_id=sib, device_id_type=MESH)
            send.start()
            cps.append((send, recv))
        for send, recv in cps:
            send.wait_send()
            recv.wait_recv()

    anyspec = pl.BlockSpec(memory_space=pl.ANY)
    return pl.pallas_call(
        body, name=name,
        out_shape=[jax.ShapeDtypeStruct(f.shape, f.dtype) for f in fulls],
        in_specs=[anyspec] * T, out_specs=[anyspec] * T,
        input_output_aliases={t: t for t in range(T)},
        scratch_shapes=[pltpu.SemaphoreType.DMA((T,)), pltpu.SemaphoreType.DMA((T,))],
    )(*fulls)


def _pack(arrs):
    flat = jnp.concatenate([a.reshape(-1).astype(F32) for a in arrs])
    n = flat.shape[0]
    rows = -(-n // 1024) * 8
    return jnp.pad(flat, (0, rows * 128 - n)).reshape(rows, 128)


def _unpack(buf, shapes):
    flat = buf.reshape(-1)
    out, pos = [], 0
    for s in shapes:
        n = math.prod(s)
        out.append(flat[pos:pos + n].reshape(s))
        pos += n
    return out


def kernel(x, mix_norm_g, ffn_norm_g, final_norm_g, sc_w_in, sc_conv_w, sc_conv_b, sc_w_out, attn_w_qkv, attn_w_out, ffn_w_up, ffn_conv_w, ffn_conv_b, ffn_w_down, loss_target, m_mix_norm_g, m_ffn_norm_g, m_final_norm_g, m_sc_w_in, m_sc_conv_w, m_sc_conv_b, m_sc_w_out, m_attn_w_qkv, m_attn_w_out, m_ffn_w_up, m_ffn_conv_w, m_ffn_conv_b, m_ffn_w_down, v_mix_norm_g, v_ffn_norm_g, v_final_norm_g, v_sc_w_in, v_sc_conv_w, v_sc_conv_b, v_sc_w_out, v_attn_w_qkv, v_attn_w_out, v_ffn_w_up, v_ffn_conv_w, v_ffn_conv_b, v_ffn_w_down):
    S, D = x.shape[1], x.shape[2]
    xi, yi, ci = _coords()
    chip = 2 * xi + yi
    x0 = x.reshape(S, D)
    tgt = loss_target.reshape(S, D)

    conv_shapes = [sc_conv_w.shape, ffn_conv_w.shape]
    allc = _small_exchange("gather_conv_w", _pack([sc_conv_w, ffn_conv_w]), reduce=False)
    per_chip = [_unpack(allc[2 * k], conv_shapes) for k in range(N_CHIPS)]
    scw = jnp.concatenate([p[0] for p in per_chip], axis=-1)[0]
    fcw = jnp.concatenate([p[1] for p in per_chip], axis=-1)
    scb = sc_conv_b

    big_names = ["sc_w_in", "sc_w_out", "attn_w_qkv", "attn_w_out", "ffn_w_up", "ffn_w_down"]
    big = dict(zip(big_names, [sc_w_in, sc_w_out, attn_w_qkv, attn_w_out, ffn_w_up, ffn_w_down]))
    n_gathers = [0]

    def gather(tag, w, layers):
        _, R, C = w.shape
        shard = _cast_bf16("cast_" + tag, w, layers).reshape(2, len(layers) * R // 2, C)
        cid = n_gathers[0]
        n_gathers[0] += 1
        return _allgather_weight("allgather_" + tag, shard, cid).reshape(N_CHIPS, len(layers), R, C)

    w_in = gather("sc_w_in", sc_w_in, (0,))
    w_out = gather("sc_w_out", sc_w_out, (0,))
    w_ups = [gather("ffn_w_up0", ffn_w_up, (0,))]
    w_dn = gather("ffn_w_down", ffn_w_down, (0, 1))
    w_qkv = gather("attn_w_qkv", attn_w_qkv, (0,))
    w_ao = gather("attn_w_out", attn_w_out, (0,))
    w_ups.append(gather("ffn_w_up1", ffn_w_up, (1,)))

    big_grads = {}
    exchanges = []

    def reduce_scatter(tags, grads):
        after = exchanges[-1][0] if exchanges else None
        outs = _grad_exchange("rs_exchange_" + tags[0], grads, n_gathers[0] + len(exchanges), after=after)
        exchanges.append(outs)
        for tag, g, others in zip(tags, grads, outs):
            big_grads[tag] = _grad_sum("rs_sum_" + tag, g, others)

    h0 = _rmsnorm_fwd("norm_mix0", x0, mix_norm_g[0:1])[0]
    z = _mm_nn_col("sc_in", h0, w_in, 0)
    y = _sc_fwd("sc_gate", z, scw, scb)
    x1 = _mm_nn_row("sc_out", y, w_out, 0, x0)
    h1 = _rmsnorm_fwd("norm_ffn0", x1, ffn_norm_g[0:1])[0]
    u0 = _mm_nn_col("ffn_up0", h1, w_ups[0], 0)
    f0, v0 = _ffn_fwd("ffn_gate0", u0, fcw[0], ffn_conv_b[0:1])
    x2 = _mm_nn_row("ffn_down0", f0, w_dn, 0, x1)
    h2s = _rmsnorm_fwd("norm_mix1", x2, mix_norm_g[1:2], dilated=True)
    qkvs = [_mm_nn_col(f"attn_qkv{d}", h, w_qkv, 0, col_off=gi * 3 * D, ncols=3 * D)
            for gi, (h, d) in enumerate(zip(h2s, DILATIONS))]
    og, lg = zip(*[_attn_fwd(f"attn_fwd{d}", q, d) for q, d in zip(qkvs, DILATIONS)])
    o32, ob, lse = _attn_combine("attn_combine", list(og), list(lg))
    x3 = _mm_nn_row("attn_out", ob, w_ao, 0, x2)
    h3 = _rmsnorm_fwd("norm_ffn1", x3, ffn_norm_g[1:2])[0]
    u1 = _mm_nn_col("ffn_up1", h3, w_ups[1], 0)
    f1, v1 = _ffn_fwd("ffn_gate1", u1, fcw[1], ffn_conv_b[1:2])
    x4 = _mm_nn_row("ffn_down1", f1, w_dn, 1, x3)

    dx4, dx4b, dg_final, loss_part = _rmsnorm_bwd("loss_norm_bwd", x4, final_norm_g.reshape(1, D), target=tgt)

    def ffn_backward(layer, xin, h, u, v, f, dxo, dxob, gain, g_up, g_dn, g_qkv=None):
        g_dn = _mm_tn_row(f"ffn_down_dw{layer}", f, dxob, prev=g_dn, layer=layer)
        if g_qkv is not None:
            reduce_scatter(["attn_w_qkv", "ffn_w_down"], [g_qkv, g_dn])
        df = _mm_nt_row(f"ffn_down_dx{layer}", dxob, w_dn, layer, after=g_dn)
        du, dwb = _ffn_bwd(f"ffn_gate_bwd{layer}", u, v, df, fcw[layer])
        g_up = _mm_tn_col(f"ffn_up_dw{layer}", h, du, w_ups[layer].shape[3], prev=g_up, layer=layer)
        if g_qkv is not None:
            reduce_scatter(["ffn_w_up"], [g_up])
        dh = _mm_nt_col(f"ffn_up_dx{layer}", du, w_ups[layer], 0, after=g_up)
        dxi, dxib, dg = _rmsnorm_bwd(f"norm_ffn_bwd{layer}", xin, gain, dhs=[(dh, 1)], dres=dxo)
        return dxi, dxib, dg, dwb, g_up, g_dn

    dx3, dx3b, dg_ffn1, dwb_ffn1, g_up, g_dn = ffn_backward(1, x3, h3, u1, v1, f1, dx4, dx4b, ffn_norm_g[1:2], None, None)

    g_ao = _mm_tn_row("attn_out_dw", ob, dx3b)
    reduce_scatter(["attn_w_out"], [g_ao])
    do = _mm_nt_row("attn_out_dx", dx3b, w_ao, 0, after=g_ao)
    dos, lss, dls = _attn_bwd_prep("attn_bwd_prep", do, o32, lse)
    dqkvs = [_attn_bwd(f"attn_bwd{d}", q, a, b, c_, d)
             for q, a, b, c_, d in zip(qkvs, dos, lss, dls, DILATIONS)]
    g_qkv = None
    for gi, (h, dq, d) in enumerate(zip(h2s, dqkvs, DILATIONS)):
        g_qkv = _mm_tn_col(f"attn_qkv_dw{d}", h, dq, w_qkv.shape[3], col_off=gi * 3 * D, prev=g_qkv)
    dh2s = [(_mm_nt_col(f"attn_qkv_dx{d}", dq, w_qkv, 0, col_off=gi * 3 * D, after=g_qkv), d)
            for gi, (dq, d) in enumerate(zip(dqkvs, DILATIONS))]
    dx2, dx2b, dg_mix1 = _rmsnorm_bwd("norm_mix_bwd1", x2, mix_norm_g[1:2], dhs=dh2s, dres=dx3)

    dx1, dx1b, dg_ffn0, dwb_ffn0, g_up, g_dn = ffn_backward(
        0, x1, h1, u0, v0, f0, dx2, dx2b, ffn_norm_g[0:1], g_up, g_dn, g_qkv=g_qkv)

    g_out = _mm_tn_row("sc_out_dw", y, dx1b)
    dy = _mm_nt_row("sc_out_dx", dx1b, w_out, 0, after=g_out)
    dz, dwb_sc = _sc_bwd("sc_gate_bwd", z, dy, scw, scb)
    g_in = _mm_tn_col("sc_in_dw", h0, dz, w_in.shape[3])
    reduce_scatter(["sc_w_out", "sc_w_in"], [g_out, g_in])
    dh0 = _mm_nt_col("sc_in_dx", dz, w_in, 0, after=g_in)
    dx0, _, dg_mix0 = _rmsnorm_bwd("norm_mix_bwd0", x0, mix_norm_g[0:1], dhs=[(dh0, 1)], dres=dx1)

    dconv_sc = dwb_sc[0:3].reshape(1, 3, D)
    dbias_sc = dwb_sc[3:4]
    dconv_ffn = jnp.stack([dwb_ffn0[0:3], dwb_ffn1[0:3]])
    dbias_ffn = jnp.concatenate([dwb_ffn0[3:4], dwb_ffn1[3:4]], axis=0)
    small_parts = [jnp.concatenate([dg_mix0, dg_mix1], axis=0), jnp.concatenate([dg_ffn0, dg_ffn1], axis=0),
                   dg_final.reshape(D), dconv_sc, dbias_sc, dconv_ffn, dbias_ffn, loss_part[0, 0:1]]
    small_shapes = [a.shape for a in small_parts]
    summed = _unpack(_small_exchange("allreduce_small", _pack(small_parts), reduce=True), small_shapes)
    g_mix, g_ffn, g_final, g_scw_full, g_scb, g_fcw_full, g_fcb, loss = summed
    loss = loss.reshape(())
    cw = sc_conv_w.shape[2]
    g_scw = lax.dynamic_slice_in_dim(g_scw_full, chip * cw, cw, axis=2)
    fw = ffn_conv_w.shape[2]
    g_fcw = lax.dynamic_slice_in_dim(g_fcw_full, chip * fw, fw, axis=2)

    names = ["mix_norm_g", "ffn_norm_g", "final_norm_g", "sc_w_in", "sc_conv_w", "sc_conv_b", "sc_w_out",
             "attn_w_qkv", "attn_w_out", "ffn_w_up", "ffn_conv_w", "ffn_conv_b", "ffn_w_down"]
    ws = dict(zip(names, [mix_norm_g, ffn_norm_g, final_norm_g, sc_w_in, sc_conv_w, sc_conv_b, sc_w_out,
                          attn_w_qkv, attn_w_out, ffn_w_up, ffn_conv_w, ffn_conv_b, ffn_w_down]))
    ms = dict(zip(names, [m_mix_norm_g, m_ffn_norm_g, m_final_norm_g, m_sc_w_in, m_sc_conv_w, m_sc_conv_b, m_sc_w_out,
                          m_attn_w_qkv, m_attn_w_out, m_ffn_w_up, m_ffn_conv_w, m_ffn_conv_b, m_ffn_w_down]))
    vs = dict(zip(names, [v_mix_norm_g, v_ffn_norm_g, v_final_norm_g, v_sc_w_in, v_sc_conv_w, v_sc_conv_b, v_sc_w_out,
                          v_attn_w_qkv, v_attn_w_out, v_ffn_w_up, v_ffn_conv_w, v_ffn_conv_b, v_ffn_w_down]))
    gs = {"mix_norm_g": g_mix, "ffn_norm_g": g_ffn, "final_norm_g": g_final, "sc_conv_w": g_scw,
          "sc_conv_b": g_scb, "ffn_conv_w": g_fcw, "ffn_conv_b": g_fcb}
    filled = _pair_fill("rs_pair_fill", [big_grads[n] for n in big_names])
    gs.update({n: f.reshape(big[n].shape) for n, f in zip(big_names, filled)})

    deltas, new_m, new_v = {}, {}, {}
    small_names = [n for n in names if n not in big_names]
    packed = [_pack([d[n] for n in small_names]) for d in (ws, gs, ms, vs)]
    outs = _adamw("adamw_small", *packed)
    shapes = [ws[n].shape for n in small_names]
    for res, o in zip((deltas, new_m, new_v), outs):
        res.update(dict(zip(small_names, _unpack(o, shapes))))
    for n in big_names:
        shp = ws[n].shape
        two_d = (shp[0] * shp[1], shp[2])
        outs = _adamw("adamw_" + n, *[d[n].reshape(two_d) for d in (ws, gs, ms, vs)], copy_g=True)
        for res, o in zip((deltas, new_m, new_v, gs), outs):
            res[n] = o.reshape(shp)

    return (loss, dx0.reshape(x.shape), *[gs[n] for n in names], *[deltas[n] for n in names],
            *[new_m[n] for n in names], *[new_v[n] for n in names])
```

```python
import math

import jax
import jax.numpy as jnp
from jax import lax
from jax.experimental import pallas as pl
from jax.experimental.pallas import tpu as pltpu
from jax.experimental.pallas import tpu_sc as plsc

F32 = jnp.float32
BF = jnp.bfloat16
MESH = pl.DeviceIdType.MESH

HEAD_DIM = 128
ATTN_HALF = 64
ATTN_BLOCK = 128
DILATIONS = (1, 4, 16)
STAT_LANES = 128
HALO = 16
NORM_EPS = 1e-5
ALIBI_MAX = 8.0
NEG_INF = -1e30
N_CHIPS = 4
VMEM_LIMIT = 56 * 1024 * 1024

ADAM_LR = 0.001
ADAM_B1 = 0.9
ADAM_B2 = 0.999
ADAM_EPS = 1e-08
ADAM_WD = 0.01
ADAM_STEP = 10


def _pick(n, cands):
    for c in cands:
        if n % c == 0:
            return c
    raise ValueError(f"no tile for {n} in {cands}")


def _row_tile(rows, cols, max_elems=1 << 19):
    for c in (512, 256, 128, 64, 32, 16):
        if rows % c == 0 and c * cols <= max_elems:
            return c
    raise ValueError(f"no row tile for {rows}x{cols}")


def _params(*sem):
    return pltpu.CompilerParams(dimension_semantics=sem, vmem_limit_bytes=VMEM_LIMIT)


def _matmul(name, a, b, out_shape, grid, a_spec, b_spec, o_spec, contract, acc_shape,
            res=None, res_spec=None, prev=None, b2_spec=None, after=None):
    nk = grid[2]
    n_b = 1 if b2_spec is None else 2

    def body(*refs):
        refs = list(refs)
        if prev is not None:
            refs.pop(0)
        a_ref, b_ref = refs[0], refs[1]
        res_ref = refs[1 + n_b] if res is not None else None
        o_ref = refs[-2]
        acc_ref = refs[-1]
        bv = b_ref[...]
        if bv.ndim == 3:
            bv = bv.reshape(bv.shape[0] * bv.shape[1], bv.shape[2])
        if b2_spec is None:
            part = lax.dot_general(a_ref[...], bv, contract, preferred_element_type=F32)
        else:
            half = a_ref.shape[1] // 2
            part = (lax.dot_general(a_ref[:, :half], bv, contract, preferred_element_type=F32)
                    + lax.dot_general(a_ref[:, half:], refs[2][...], contract, preferred_element_type=F32))

        def finish(total):
            if res_ref is not None:
                total = total + res_ref[...]
            o_ref[...] = total.reshape(o_ref.shape).astype(o_ref.dtype)

        if nk == 1:
            finish(part)
        else:
            k = pl.program_id(2)

            @pl.when(k == 0)
            def _():
                acc_ref[...] = part

            @pl.when(jnp.logical_and(k > 0, k < nk - 1))
            def _():
                acc_ref[...] += part

            @pl.when(k == nk - 1)
            def _():
                finish(acc_ref[...] + part)

    operands, in_specs, aliases = [], [], {}
    if prev is not None:
        operands.append(prev)
        in_specs.append(pl.BlockSpec(memory_space=pl.ANY))
        aliases = {0: 0}
    operands += [a, b]
    in_specs += [a_spec, b_spec]
    if b2_spec is not None:
        operands.append(b)
        in_specs.append(b2_spec)
    if res is not None:
        operands.append(res)
        in_specs.append(res_spec)
    if after is not None:
        operands.append(after)
        in_specs.append(pl.BlockSpec(memory_space=pl.ANY))
    return pl.pallas_call(
        body, name=name, out_shape=out_shape, grid=grid, in_specs=in_specs, out_specs=o_spec,
        scratch_shapes=[pltpu.VMEM(acc_shape if nk > 1 else (8, 128), F32)],
        input_output_aliases=aliases,
        compiler_params=_params("parallel", "parallel", "arbitrary"),
    )(*operands)


NN = (((1,), (0,)), ((), ()))
NT = (((1,), (1,)), ((), ()))
TN = (((0,), (0,)), ((), ()))

_COL_TILES = (1536, 1408, 1024, 768, 512, 384, 256, 128)
_WIDE_TILES = (2816,) + _COL_TILES


def _mm_nn_col(name, a, w, layer, col_off=0, ncols=None, out_dtype=BF):
    M, K = a.shape
    _, _, R, C = w.shape
    assert R == K
    ncols = N_CHIPS * C if ncols is None else ncols
    tn = _pick(math.gcd(C, math.gcd(ncols, col_off) if col_off else ncols), _WIDE_TILES)
    tm = _pick(M, (1024, 512, 256))
    nb, off = C // tn, col_off // tn
    return _matmul(
        name, a, w, jax.ShapeDtypeStruct((M, ncols), out_dtype), (M // tm, ncols // tn, 1),
        pl.BlockSpec((tm, K), lambda i, j, k: (i, 0)),
        pl.BlockSpec((None, None, K, tn), lambda i, j, k: ((j + off) // nb, layer, 0, (j + off) % nb)),
        pl.BlockSpec((tm, tn), lambda i, j, k: (i, j)), NN, (tm, tn))


def _mm_nn_row(name, a, w, layer, res):
    M, K = a.shape
    _, _, R, C = w.shape
    assert N_CHIPS * R == K
    chips_per_step = N_CHIPS if K <= 2048 else 2
    tk = chips_per_step * R
    tm = _pick(M, (1024, 512, 256))
    tn = _pick(C, (1024, 512, 256))
    return _matmul(
        name, a, w, jax.ShapeDtypeStruct((M, C), F32), (M // tm, C // tn, K // tk),
        pl.BlockSpec((tm, tk), lambda i, j, k: (i, k)),
        pl.BlockSpec((chips_per_step, None, R, tn), lambda i, j, k: (k, layer, 0, j)),
        pl.BlockSpec((tm, tn), lambda i, j, k: (i, j)), NN, (tm, tn),
        res=res, res_spec=pl.BlockSpec((tm, tn), lambda i, j, k: (i, j)))


def _mm_nt_col(name, dy, w, layer, col_off=0, out_dtype=F32, after=None):
    M, n = dy.shape
    _, _, R, C = w.shape
    tk = _pick(math.gcd(C, math.gcd(n, col_off) if col_off else n), _WIDE_TILES)
    tm = _pick(M, (1024, 512, 256))
    tn = _pick(R, (1024, 512, 256))
    nb, off = C // tk, col_off // tk
    per_step = 2 if (tk <= 1536 and (n // tk) % 2 == 0) else 1

    def w_block(t):
        return pl.BlockSpec((None, None, tn, tk), lambda i, j, k: (
            (per_step * k + t + off) // nb, layer, j, (per_step * k + t + off) % nb))

    return _matmul(
        name, dy, w, jax.ShapeDtypeStruct((M, R), out_dtype), (M // tm, R // tn, n // (per_step * tk)),
        pl.BlockSpec((tm, per_step * tk), lambda i, j, k: (i, k)), w_block(0),
        pl.BlockSpec((tm, tn), lambda i, j, k: (i, j)), NT, (tm, tn),
        b2_spec=w_block(1) if per_step == 2 else None, after=after)


def _mm_nt_row(name, dy, w, layer, out_dtype=BF, after=None):
    M, C2 = dy.shape
    _, _, R, C = w.shape
    assert C2 == C
    chips_per_tile = N_CHIPS if N_CHIPS * R <= 2048 else 2
    tn = chips_per_tile * R
    tm = _pick(M, (1024, 512, 256))
    return _matmul(
        name, dy, w, jax.ShapeDtypeStruct((M, N_CHIPS * R), out_dtype), (M // tm, N_CHIPS * R // tn, 1),
        pl.BlockSpec((tm, C), lambda i, j, k: (i, 0)),
        pl.BlockSpec((chips_per_tile, None, R, C), lambda i, j, k: (j, layer, 0, 0)),
        pl.BlockSpec((tm, tn), lambda i, j, k: (i, j)), NT, (tm, tn), after=after)


_TN_DEPTH = (2048, 1024, 512, 256)


def _half_index(rows, layer, tkx):
    if layer is None:
        hb = rows // 2 // tkx
        return rows // 2, lambda i: (i // hb, i % hb)
    return rows, lambda i: (layer, i)


def _mm_tn_col(name, xa, dy, C, col_off=0, prev=None, layer=None):
    M, K = xa.shape
    _, n = dy.shape
    tn = _pick(math.gcd(C, math.gcd(n, col_off) if col_off else n), _WIDE_TILES)
    tkx = _pick(K // 2 if layer is None else K, (1024, 512, 256, 128) if tn <= 1536 else (512, 256, 128))
    tmr = _pick(M, _TN_DEPTH)
    rh, split = _half_index(K, layer, tkx)
    nb, off = C // tn, col_off // tn
    return _matmul(
        name, xa, dy, jax.ShapeDtypeStruct((N_CHIPS, 2, rh, C), BF), (K // tkx, n // tn, M // tmr),
        pl.BlockSpec((tmr, tkx), lambda i, j, k: (k, i)),
        pl.BlockSpec((tmr, tn), lambda i, j, k: (k, j)),
        pl.BlockSpec((None, None, tkx, tn), lambda i, j, k: ((j + off) // nb, *split(i), (j + off) % nb)),
        TN, (tkx, tn), prev=prev)


def _mm_tn_row(name, xa, dy, prev=None, layer=None):
    M, K = xa.shape
    _, C = dy.shape
    R = K // N_CHIPS
    tmr = _pick(M, _TN_DEPTH)
    if layer is None and prev is None and K <= 2048:
        tn = _pick(C, (1024, 512, 256))
        return _matmul(
            name, xa, dy, jax.ShapeDtypeStruct((N_CHIPS, 2, R // 2, C), BF), (1, C // tn, M // tmr),
            pl.BlockSpec((tmr, K), lambda i, j, k: (k, 0)),
            pl.BlockSpec((tmr, tn), lambda i, j, k: (k, j)),
            pl.BlockSpec((N_CHIPS, 2, R // 2, tn), lambda i, j, k: (0, 0, 0, j)),
            TN, (K, tn))
    tkx = _pick(R // 2 if layer is None else R, (1408, 1024, 512, 256, 128))
    tn = _pick(C, (2048, 1024, 512, 256) if tkx <= 512 else (1024, 512, 256))
    rh, split = _half_index(R, layer, tkx)
    rb = R // tkx
    return _matmul(
        name, xa, dy, jax.ShapeDtypeStruct((N_CHIPS, 2, rh, C), BF), (K // tkx, C // tn, M // tmr),
        pl.BlockSpec((tmr, tkx), lambda i, j, k: (k, i)),
        pl.BlockSpec((tmr, tn), lambda i, j, k: (k, j)),
        pl.BlockSpec((None, None, tkx, tn), lambda i, j, k: (i // rb, *split(i % rb), j)),
        TN, (tkx, tn), prev=prev)


NORM_ROWS = 256
LANES = 128


def _chunk_scratch(tm, width):
    return pltpu.VMEM((width // LANES, tm, LANES), F32)


def _store_chunks(scr, value):
    for c in range(scr.shape[0]):
        scr[c] = value[:, c * LANES:(c + 1) * LANES]


def _load_chunks(scr):
    return jnp.concatenate([scr[c] for c in range(scr.shape[0])], axis=1)


def _to_residue_major(scr, o_ref, d, dtype):
    tm = scr.shape[1]
    for c in range(scr.shape[0]):
        for res in range(d):
            o_ref[res, :, c * LANES:(c + 1) * LANES] = scr[c, pl.ds(res, tm // d, stride=d), :].astype(dtype)


def _to_natural(scr, ref, d):
    tm = scr.shape[1]
    for c in range(scr.shape[0]):
        for res in range(d):
            scr[c, pl.ds(res, tm // d, stride=d), :] = ref[res, :, c * LANES:(c + 1) * LANES].astype(F32)


def _rmsnorm_fwd(name, x, g, dilated=False):
    S, D = x.shape
    tm = NORM_ROWS
    dils = DILATIONS[1:] if dilated else ()

    def body(x_ref, g_ref, h_ref, *rest):
        xv = x_ref[...]
        r = lax.rsqrt(jnp.mean(xv * xv, axis=1, keepdims=True) + NORM_EPS)
        h = xv * r * g_ref[...]
        h_ref[...] = h.astype(BF)
        if dils:
            scr = rest[-1]
            _store_chunks(scr, h)
            for o_ref, d in zip(rest[:-1], dils):
                _to_residue_major(scr, o_ref, d, BF)

    out_shape = [jax.ShapeDtypeStruct((S, D), BF)]
    out_specs = [pl.BlockSpec((tm, D), lambda i: (i, 0))]
    for d in dils:
        out_shape.append(jax.ShapeDtypeStruct((d, S // d, D), BF))
        out_specs.append(pl.BlockSpec((d, tm // d, D), lambda i: (0, i, 0)))
    outs = pl.pallas_call(
        body, name=name, out_shape=out_shape, grid=(S // tm,),
        in_specs=[pl.BlockSpec((tm, D), lambda i: (i, 0)), pl.BlockSpec((1, D), lambda i: (0, 0))],
        out_specs=out_specs,
        scratch_shapes=[_chunk_scratch(tm, D)] if dils else [],
        compiler_params=_params("parallel"),
    )(x, g)
    return [outs[0]] + [o.reshape(S, D) for o in outs[1:]]


def _rmsnorm_bwd(name, x, g, dhs=(), dres=None, target=None):
    S, D = x.shape
    tm = NORM_ROWS
    n_dh = len(dhs)

    def body(*refs):
        refs = list(refs)
        x_ref, g_ref = refs[0], refs[1]
        dh_refs = refs[2:2 + n_dh]
        pos = 2 + n_dh
        dres_ref = tgt_ref = None
        if dres is not None:
            dres_ref = refs[pos]
            pos += 1
        if target is not None:
            tgt_ref = refs[pos]
            pos += 1
        dx_ref, dxb_ref, dg_ref = refs[pos:pos + 3]
        pos += 3
        loss_ref = None
        if target is not None:
            loss_ref = refs[pos]
            pos += 1
        scr = refs[pos] if any(d > 1 for _, d in dhs) else None
        i = pl.program_id(0)

        xv = x_ref[...]
        gv = g_ref[...]
        r = lax.rsqrt(jnp.mean(xv * xv, axis=1, keepdims=True) + NORM_EPS)
        xhat = xv * r
        if target is not None:
            err = xhat * gv - tgt_ref[...]
            dh = err * (1.0 / D)
            part = jnp.sum(jnp.sum(err * err, axis=1, keepdims=True), axis=0, keepdims=True) * (0.5 / D)
        else:
            dh = None
            for ref, d in zip(dh_refs, [d for _, d in dhs]):
                if d == 1:
                    v = ref[...]
                else:
                    _to_natural(scr, ref, d)
                    v = _load_chunks(scr)
                dh = v if dh is None else dh + v
        dxhat = dh * gv
        dx = r * (dxhat - xhat * jnp.mean(dxhat * xhat, axis=1, keepdims=True))
        if dres_ref is not None:
            dx = dx + dres_ref[...]
        dx_ref[...] = dx
        dxb_ref[...] = dx.astype(BF)
        dg = jnp.sum(dh * xhat, axis=0, keepdims=True)

        @pl.when(i == 0)
        def _():
            dg_ref[...] = dg
            if loss_ref is not None:
                loss_ref[...] = jnp.broadcast_to(part, loss_ref.shape)

        @pl.when(i > 0)
        def _():
            dg_ref[...] += dg
            if loss_ref is not None:
                loss_ref[...] += jnp.broadcast_to(part, loss_ref.shape)

    row = pl.BlockSpec((tm, D), lambda i: (i, 0))
    operands = [x, g]
    in_specs = [row, pl.BlockSpec((1, D), lambda i: (0, 0))]
    for arr, d in dhs:
        if d == 1:
            operands.append(arr)
            in_specs.append(row)
        else:
            operands.append(arr.reshape(d, S // d, D))
            in_specs.append(pl.BlockSpec((d, tm // d, D), lambda i: (0, i, 0)))
    if dres is not None:
        operands.append(dres)
        in_specs.append(row)
    if target is not None:
        operands.append(target)
        in_specs.append(row)
    out_shape = [jax.ShapeDtypeStruct((S, D), F32), jax.ShapeDtypeStruct((S, D), BF),
                 jax.ShapeDtypeStruct((1, D), F32)]
    out_specs = [row, row, pl.BlockSpec((1, D), lambda i: (0, 0))]
    if target is not None:
        out_shape.append(jax.ShapeDtypeStruct((1, STAT_LANES), F32))
        out_specs.append(pl.BlockSpec((1, STAT_LANES), lambda i: (0, 0)))
    scratch = [_chunk_scratch(tm, D)] if any(d > 1 for _, d in dhs) else []
    return pl.pallas_call(
        body, name=name, out_shape=out_shape, grid=(S // tm,), in_specs=in_specs, out_specs=out_specs,
        scratch_shapes=scratch, compiler_params=_params("arbitrary"),
    )(*operands)


CONV_ROWS = 256
CONV_COLS = 256


def _halo_specs(S, tm, width):
    nh = S // HALO
    per = tm // HALO
    cur = pl.BlockSpec((tm, width), lambda i: (i, 0))
    prev = pl.BlockSpec((HALO, width), lambda i: (jnp.maximum(i * per - 1, 0), 0))
    nxt = pl.BlockSpec((HALO, width), lambda i: (jnp.minimum((i + 1) * per, nh - 1), 0))
    return [cur, prev, nxt]


def _ext(refs, cs, inr):
    cur, prev, nxt = refs
    v = jnp.concatenate([prev[:, cs], cur[:, cs], nxt[:, cs]], axis=0).astype(F32)
    return v if inr is None else jnp.where(inr, v, 0.0)


def _shift_prev(v):
    return pltpu.roll(v, 1, 0)


def _shift_next(v):
    return pltpu.roll(v, v.shape[0] - 1, 0)


def _shifts(v):
    return _shift_prev(v), _shift_next(v)


def _conv3(v, w, cs, b=None, shifted=None):
    vp, vn = _shifts(v) if shifted is None else shifted
    out = w[0:1, cs] * vp + w[1:2, cs] * v + w[2:3, cs] * vn
    return out if b is None else out + b[:, cs]


def _in_range(i, tm, tc, S):
    row = lax.broadcasted_iota(jnp.int32, (tm + 2 * HALO, tc), 0) + (i * tm - HALO)
    return jnp.logical_and(row >= 0, row < S)


def _core(v, tm):
    return v[HALO:HALO + tm, :]


def _acc_rows(ref, rows):
    for r, cs, val in rows:
        ref[r:r + 1, cs] += val


def _zero_first(ref, i):
    @pl.when(i == 0)
    def _():
        ref[...] = jnp.zeros(ref.shape, ref.dtype)


def _sc_fwd(name, z, w, b):
    S, D3 = z.shape
    D = D3 // 3
    tm, tc = CONV_ROWS, _pick(D, (CONV_COLS, 256, 128))

    def body(zc, zp, zn, w_ref, b_ref, y_ref):
        i = pl.program_id(0)
        inr = _in_range(i, tm, tc, S)
        zr = (zc, zp, zn)
        for c in range(D // tc):
            cs = slice(c * tc, (c + 1) * tc)
            u = _ext(zr, cs, inr)
            gc = _ext(zr, slice(2 * D + c * tc, 2 * D + (c + 1) * tc), None)
            conv = _conv3(gc * u, w_ref, cs, b_ref)
            gb = zc[:, D + c * tc:D + (c + 1) * tc].astype(F32)
            y_ref[:, cs] = (gb * _core(conv, tm)).astype(BF)

    return pl.pallas_call(
        body, name=name, out_shape=jax.ShapeDtypeStruct((S, D), BF), grid=(S // tm,),
        in_specs=_halo_specs(S, tm, D3) + [pl.BlockSpec((3, D), lambda i: (0, 0)),
                                           pl.BlockSpec((1, D), lambda i: (0, 0))],
        out_specs=pl.BlockSpec((tm, D), lambda i: (i, 0)),
        compiler_params=_params("parallel"),
    )(z, z, z, w, b)


def _sc_bwd(name, z, dy, w, b):
    S, D3 = z.shape
    D = D3 // 3
    tm, tc = CONV_ROWS, _pick(D, (CONV_COLS, 256, 128))

    def body(zc, zp, zn, dc_, dp_, dn_, w_ref, b_ref, dz_ref, dwb_ref):
        i = pl.program_id(0)
        inr = _in_range(i, tm, tc, S)
        _zero_first(dwb_ref, i)
        zr, dr = (zc, zp, zn), (dc_, dp_, dn_)
        for c in range(D // tc):
            cs = slice(c * tc, (c + 1) * tc)
            u = _ext(zr, cs, inr)
            gb = _ext(zr, slice(D + c * tc, D + (c + 1) * tc), None)
            gc = _ext(zr, slice(2 * D + c * tc, 2 * D + (c + 1) * tc), None)
            dyv = _ext(dr, cs, inr)
            p = gc * u
            p_prev, p_next = _shifts(p)
            conv = _conv3(p, w_ref, cs, b_ref, shifted=(p_prev, p_next))
            dconv = dyv * gb
            dp = w_ref[0:1, cs] * _shift_next(dconv) + w_ref[1:2, cs] * dconv + w_ref[2:3, cs] * _shift_prev(dconv)
            dz_ref[:, cs] = _core(dp * gc, tm).astype(BF)
            dz_ref[:, D + c * tc:D + (c + 1) * tc] = _core(dyv * conv, tm).astype(BF)
            dz_ref[:, 2 * D + c * tc:2 * D + (c + 1) * tc] = _core(dp * u, tm).astype(BF)
            dcc = _core(dconv, tm)
            _acc_rows(dwb_ref, [
                (0, cs, jnp.sum(dcc * _core(p_prev, tm), axis=0, keepdims=True)),
                (1, cs, jnp.sum(dcc * _core(p, tm), axis=0, keepdims=True)),
                (2, cs, jnp.sum(dcc * _core(p_next, tm), axis=0, keepdims=True)),
                (3, cs, jnp.sum(dcc, axis=0, keepdims=True))])

    return pl.pallas_call(
        body, name=name,
        out_shape=[jax.ShapeDtypeStruct((S, D3), BF), jax.ShapeDtypeStruct((4, D), F32)], grid=(S // tm,),
        in_specs=_halo_specs(S, tm, D3) + _halo_specs(S, tm, D) + [
            pl.BlockSpec((3, D), lambda i: (0, 0)), pl.BlockSpec((1, D), lambda i: (0, 0))],
        out_specs=[pl.BlockSpec((tm, D3), lambda i: (i, 0)), pl.BlockSpec((4, D), lambda i: (0, 0))],
        compiler_params=_params("arbitrary"),
    )(z, z, z, dy, dy, dy, w, b)


def _sigmoid(v):
    return 1.0 / (1.0 + jnp.exp(-v))


def _ffn_fwd(name, u, w, b):
    S, F2 = u.shape
    Fh = F2 // 2
    tm, tc = CONV_ROWS, _pick(Fh, (CONV_COLS, 256, 128))

    def body(uc, up, un, w_ref, b_ref, f_ref, v_ref):
        i = pl.program_id(0)
        inr = _in_range(i, tm, tc, S)
        ur = (uc, up, un)
        for c in range(Fh // tc):
            ca = slice(c * tc, (c + 1) * tc)
            cb = slice(Fh + c * tc, Fh + (c + 1) * tc)
            va = _core(_conv3(_ext(ur, ca, inr), w_ref, ca, b_ref), tm)
            vb = _core(_conv3(_ext(ur, cb, inr), w_ref, cb, b_ref), tm)
            v_ref[:, ca] = va.astype(BF)
            v_ref[:, cb] = vb.astype(BF)
            f_ref[:, ca] = (va * _sigmoid(va) * vb).astype(BF)

    return pl.pallas_call(
        body, name=name,
        out_shape=[jax.ShapeDtypeStruct((S, Fh), BF), jax.ShapeDtypeStruct((S, F2), BF)], grid=(S // tm,),
        in_specs=_halo_specs(S, tm, F2) + [pl.BlockSpec((3, F2), lambda i: (0, 0)),
                                           pl.BlockSpec((1, F2), lambda i: (0, 0))],
        out_specs=[pl.BlockSpec((tm, Fh), lambda i: (i, 0)), pl.BlockSpec((tm, F2), lambda i: (i, 0))],
        compiler_params=_params("parallel"),
    )(u, u, u, w, b)


def _ffn_bwd(name, u, v, df, w):
    S, F2 = u.shape
    Fh = F2 // 2
    tm, tc = CONV_ROWS, _pick(Fh, (CONV_COLS, 256, 128))

    def body(u_ref, vc, vp, vn, dc_, dp_, dn_, w_ref, du_ref, dwb_ref):
        i = pl.program_id(0)
        inr = _in_range(i, tm, tc, S)
        _zero_first(dwb_ref, i)
        vr, dr = (vc, vp, vn), (dc_, dp_, dn_)
        for c in range(Fh // tc):
            ca = slice(c * tc, (c + 1) * tc)
            cb = slice(Fh + c * tc, Fh + (c + 1) * tc)
            va, vb = _ext(vr, ca, None), _ext(vr, cb, None)
            dfv = _ext(dr, ca, inr)
            sg = _sigmoid(va)
            dva = dfv * vb * (sg * (1.0 + va * (1.0 - sg)))
            dvb = dfv * (va * sg)
            rows = []
            for cs, dv in ((ca, dva), (cb, dvb)):
                dv_prev, dv_next = _shifts(dv)
                dcore = w_ref[0:1, cs] * dv_next + w_ref[1:2, cs] * dv + w_ref[2:3, cs] * dv_prev
                du_ref[:, cs] = _core(dcore, tm).astype(BF)
                uu = u_ref[:, cs].astype(F32)
                dvc = _core(dv, tm)
                rows += [
                    (0, cs, jnp.sum(_core(dv_next, tm) * uu, axis=0, keepdims=True)),
                    (1, cs, jnp.sum(dvc * uu, axis=0, keepdims=True)),
                    (2, cs, jnp.sum(_core(dv_prev, tm) * uu, axis=0, keepdims=True)),
                    (3, cs, jnp.sum(dvc, axis=0, keepdims=True))]
            _acc_rows(dwb_ref, rows)

    return pl.pallas_call(
        body, name=name,
        out_shape=[jax.ShapeDtypeStruct((S, F2), BF), jax.ShapeDtypeStruct((4, F2), F32)], grid=(S // tm,),
        in_specs=[pl.BlockSpec((tm, F2), lambda i: (i, 0))] + _halo_specs(S, tm, F2) + _halo_specs(S, tm, Fh) + [
            pl.BlockSpec((3, F2), lambda i: (0, 0))],
        out_specs=[pl.BlockSpec((tm, F2), lambda i: (i, 0)), pl.BlockSpec((4, F2), lambda i: (0, 0))],
        compiler_params=_params("arbitrary"),
    )(u, v, v, v, df, df, df, w)


def _alibi_slopes(H):
    return [2.0 ** (-ALIBI_MAX * (h + 1) / H) for h in range(H)]


def _window_specs(S, width, col):
    n64 = S // ATTN_HALF
    cur = pl.BlockSpec((ATTN_BLOCK, width), lambda b: (b, col))
    prev = pl.BlockSpec((ATTN_HALF, width), lambda b: (jnp.maximum(2 * b - 1, 0), col))
    nxt = pl.BlockSpec((ATTN_HALF, width), lambda b: (jnp.minimum(2 * b + 2, n64 - 1), col))
    return [cur, prev, nxt]


def _fill_window(buf, cur, prev, nxt):
    buf[0:ATTN_HALF] = prev[...]
    buf[ATTN_HALF:ATTN_HALF + ATTN_BLOCK] = cur[...]
    buf[ATTN_HALF + ATTN_BLOCK:2 * ATTN_BLOCK] = nxt[...]


def _band(b, L):
    QB, W = ATTN_BLOCK, 2 * ATTN_BLOCK
    a_loc = (b * QB) % L
    row = lax.broadcasted_iota(jnp.int32, (QB, W), 0)
    col = lax.broadcasted_iota(jnp.int32, (QB, W), 1)
    dist = jnp.abs(col - ATTN_HALF - row)
    other = a_loc - ATTN_HALF + col
    valid = jnp.logical_and(dist <= ATTN_HALF, jnp.logical_and(other >= 0, other < L))
    return dist.astype(F32), valid


def _lane_col(stats, h):
    lane = lax.broadcasted_iota(jnp.int32, stats.shape, 1)
    return jnp.sum(jnp.where(lane == h, stats, 0.0), axis=1, keepdims=True)


def _attn_fwd(name, qkv, d):
    S, D3 = qkv.shape
    D = D3 // 3
    H = D // HEAD_DIM
    L = S // d
    scale = HEAD_DIM ** -0.5
    slopes = _alibi_slopes(H)

    def body(q_ref, kc, kp, kn, vc, vp, vn, o_ref, lse_ref, kbuf, vbuf):
        b = pl.program_id(0)
        _fill_window(kbuf, kc, kp, kn)
        _fill_window(vbuf, vc, vp, vn)
        dist, valid = _band(b, L)
        dist = dist * float(d)
        lane = lax.broadcasted_iota(jnp.int32, (ATTN_BLOCK, STAT_LANES), 1)
        lse = jnp.zeros((ATTN_BLOCK, STAT_LANES), F32)
        for h in range(H):
            cs = slice(h * HEAD_DIM, (h + 1) * HEAD_DIM)
            s = lax.dot_general(q_ref[:, cs], kbuf[:, cs], NT, preferred_element_type=F32) * scale
            s = jnp.where(valid, s - slopes[h] * dist, NEG_INF)
            m = jnp.max(s, axis=1, keepdims=True)
            p = jnp.exp(s - m)
            den = jnp.sum(p, axis=1, keepdims=True)
            o = jnp.dot(p.astype(BF), vbuf[:, cs], preferred_element_type=F32)
            o_ref[:, cs] = (o / den).astype(BF)
            lse = jnp.where(lane == h, m + jnp.log(den), lse)
        lse_ref[...] = lse

    return pl.pallas_call(
        body, name=name,
        out_shape=[jax.ShapeDtypeStruct((S, D), BF), jax.ShapeDtypeStruct((S, STAT_LANES), F32)],
        grid=(S // ATTN_BLOCK,),
        in_specs=[pl.BlockSpec((ATTN_BLOCK, D), lambda b: (b, 0))] + _window_specs(S, D, 1) + _window_specs(S, D, 2),
        out_specs=[pl.BlockSpec((ATTN_BLOCK, D), lambda b: (b, 0)),
                   pl.BlockSpec((ATTN_BLOCK, STAT_LANES), lambda b: (b, 0))],
        scratch_shapes=[pltpu.VMEM((2 * ATTN_BLOCK, D), BF), pltpu.VMEM((2 * ATTN_BLOCK, D), BF)],
        compiler_params=_params("parallel"),
    )(qkv, qkv, qkv, qkv, qkv, qkv, qkv)


def _dil_specs(S, tm, width):
    specs = [pl.BlockSpec((tm, width), lambda i: (i, 0))]
    for d in DILATIONS[1:]:
        specs.append(pl.BlockSpec((d, tm // d, width), lambda i: (0, i, 0)))
    return specs


def _attn_combine(name, outs, lses):
    S, D = outs[0].shape
    H = D // HEAD_DIM
    tm = NORM_ROWS

    def body(o1, o4, o16, l1, l4, l16, o_ref, ob_ref, lse_ref, oscr, lscr):
        ls = [l1[...]]
        for ref, d in zip((l4, l16), DILATIONS[1:]):
            _to_natural(lscr, ref, d)
            ls.append(lscr[0])
        top = jnp.maximum(jnp.maximum(ls[0], ls[1]), ls[2])
        es = [jnp.exp(l - top) for l in ls]
        tot = es[0] + es[1] + es[2]
        lse_ref[...] = top + jnp.log(tot)
        ws = [e / tot for e in es]
        for gi, (ref, d) in enumerate(zip((o1, o4, o16), DILATIONS)):
            if d > 1:
                _to_natural(oscr, ref, d)
            for h in range(H):
                cs = slice(h * HEAD_DIM, (h + 1) * HEAD_DIM)
                term = _lane_col(ws[gi], h) * (ref[:, cs] if d == 1 else oscr[h])
                if gi == 0:
                    o_ref[:, cs] = term
                else:
                    o_ref[:, cs] += term
        ob_ref[...] = o_ref[...].astype(BF)

    outs3 = [outs[0]] + [o.reshape(d, S // d, D) for o, d in zip(outs[1:], DILATIONS[1:])]
    lses3 = [lses[0]] + [l.reshape(d, S // d, STAT_LANES) for l, d in zip(lses[1:], DILATIONS[1:])]
    row = pl.BlockSpec((tm, D), lambda i: (i, 0))
    return pl.pallas_call(
        body, name=name,
        out_shape=[jax.ShapeDtypeStruct((S, D), F32), jax.ShapeDtypeStruct((S, D), BF),
                   jax.ShapeDtypeStruct((S, STAT_LANES), F32)],
        grid=(S // tm,),
        in_specs=_dil_specs(S, tm, D) + _dil_specs(S, tm, STAT_LANES),
        out_specs=[row, row, pl.BlockSpec((tm, STAT_LANES), lambda i: (i, 0))],
        scratch_shapes=[_chunk_scratch(tm, D), _chunk_scratch(tm, STAT_LANES)],
        compiler_params=_params("parallel"),
    )(*outs3, *lses3)


def _attn_bwd_prep(name, do, o32, lse):
    S, D = do.shape
    H = D // HEAD_DIM
    tm = NORM_ROWS
    dils = DILATIONS[1:]

    def body(do_ref, o_ref, lse_ref, dl_ref, do4, do16, l4, l16, d4, d16, scr, sscr):
        lane = lax.broadcasted_iota(jnp.int32, (tm, STAT_LANES), 1)
        delta = jnp.zeros((tm, STAT_LANES), F32)
        for h in range(H):
            cs = slice(h * HEAD_DIM, (h + 1) * HEAD_DIM)
            dov = do_ref[:, cs].astype(F32)
            scr[h] = dov
            delta = jnp.where(lane == h, jnp.sum(dov * o_ref[:, cs], axis=1, keepdims=True), delta)
        dl_ref[...] = delta
        for ref, d in zip((do4, do16), dils):
            _to_residue_major(scr, ref, d, BF)
        for val, refs in ((lse_ref[...], (l4, l16)), (delta, (d4, d16))):
            sscr[0] = val
            for ref, d in zip(refs, dils):
                _to_residue_major(sscr, ref, d, F32)

    def perm_shapes(width, dt):
        return [jax.ShapeDtypeStruct((d, S // d, width), dt) for d in dils]

    def perm_specs(width):
        return [pl.BlockSpec((d, tm // d, width), lambda i: (0, i, 0)) for d in dils]

    row = lambda w: pl.BlockSpec((tm, w), lambda i: (i, 0))
    outs = pl.pallas_call(
        body, name=name,
        out_shape=[jax.ShapeDtypeStruct((S, STAT_LANES), F32)]
        + perm_shapes(D, BF) + perm_shapes(STAT_LANES, F32) + perm_shapes(STAT_LANES, F32),
        grid=(S // tm,),
        in_specs=[row(D), row(D), row(STAT_LANES)],
        out_specs=[row(STAT_LANES)] + perm_specs(D) + perm_specs(STAT_LANES) + perm_specs(STAT_LANES),
        scratch_shapes=[_chunk_scratch(tm, D), _chunk_scratch(tm, STAT_LANES)],
        compiler_params=_params("parallel"),
    )(do, o32, lse)
    dos = [do] + [a.reshape(S, D) for a in outs[1:3]]
    lss = [lse] + [a.reshape(S, STAT_LANES) for a in outs[3:5]]
    dls = [outs[0]] + [a.reshape(S, STAT_LANES) for a in outs[5:7]]
    return dos, lss, dls


def _attn_bwd(name, qkv, do, lse, delta, d):
    S, D3 = qkv.shape
    D = D3 // 3
    H = D // HEAD_DIM
    L = S // d
    scale = HEAD_DIM ** -0.5
    slopes = _alibi_slopes(H)
    QB = ATTN_BLOCK

    def body(qc, qp, qn, kc, kp, kn, vc, vp, vn, dc_, dp_, dn_, lc, lp, ln, ec, ep, en,
             out_ref, qbuf, kbuf, vbuf, dobuf, lbuf, ebuf):
        b = pl.program_id(0)
        for buf, trio in ((qbuf, (qc, qp, qn)), (kbuf, (kc, kp, kn)), (vbuf, (vc, vp, vn)),
                          (dobuf, (dc_, dp_, dn_)), (lbuf, (lc, lp, ln)), (ebuf, (ec, ep, en))):
            _fill_window(buf, *trio)
        dist, valid = _band(b, L)
        dist = dist * float(d)
        lse_c, del_c = lc[...], ec[...]
        lse_w, del_w = lbuf[...].T, ebuf[...].T
        for h in range(H):
            cs = slice(h * HEAD_DIM, (h + 1) * HEAD_DIM)
            bias = slopes[h] * dist
            q, do_h = qc[:, cs], dc_[:, cs]
            kw, vw = kbuf[:, cs], vbuf[:, cs]
            s = lax.dot_general(q, kw, NT, preferred_element_type=F32) * scale - bias
            p = jnp.where(valid, jnp.exp(s - _lane_col(lse_c, h)), 0.0)
            dp = lax.dot_general(do_h, vw, NT, preferred_element_type=F32)
            ds = p * (dp - _lane_col(del_c, h))
            dq = jnp.dot(ds.astype(BF), kw, preferred_element_type=F32) * scale
            out_ref[:, cs] = dq.astype(BF)
            qw, dow = qbuf[:, cs], dobuf[:, cs]
            k, v = kc[:, cs], vc[:, cs]
            st = lax.dot_general(k, qw, NT, preferred_element_type=F32) * scale - bias
            pt = jnp.where(valid, jnp.exp(st - lse_w[h:h + 1, :]), 0.0)
            dv = jnp.dot(pt.astype(BF), dow, preferred_element_type=F32)
            dpt = lax.dot_general(v, dow, NT, preferred_element_type=F32)
            dst = pt * (dpt - del_w[h:h + 1, :])
            dk = jnp.dot(dst.astype(BF), qw, preferred_element_type=F32) * scale
            out_ref[:, D + h * HEAD_DIM:D + (h + 1) * HEAD_DIM] = dk.astype(BF)
            out_ref[:, 2 * D + h * HEAD_DIM:2 * D + (h + 1) * HEAD_DIM] = dv.astype(BF)

    W = 2 * QB
    return pl.pallas_call(
        body, name=name, out_shape=jax.ShapeDtypeStruct((S, D3), BF), grid=(S // QB,),
        in_specs=(_window_specs(S, D, 0) + _window_specs(S, D, 1) + _window_specs(S, D, 2)
                  + _window_specs(S, D, 0) + _window_specs(S, STAT_LANES, 0) + _window_specs(S, STAT_LANES, 0)),
        out_specs=pl.BlockSpec((QB, D3), lambda b: (b, 0)),
        scratch_shapes=[pltpu.VMEM((W, D), BF), pltpu.VMEM((W, D), BF), pltpu.VMEM((W, D), BF),
                        pltpu.VMEM((W, D), BF), pltpu.VMEM((W, STAT_LANES), F32), pltpu.VMEM((W, STAT_LANES), F32)],
        compiler_params=_params("parallel"),
    )(qkv, qkv, qkv, qkv, qkv, qkv, qkv, qkv, qkv, do, do, do, lse, lse, lse, delta, delta, delta)


def _cast_bf16(name, w, layers):
    _, R, C = w.shape
    tr = _row_tile(R, C)
    first = layers[0]

    def body(w_ref, o_ref):
        o_ref[...] = w_ref[...].astype(BF)

    return pl.pallas_call(
        body, name=name, out_shape=jax.ShapeDtypeStruct((len(layers), R, C), BF), grid=(len(layers), R // tr),
        in_specs=[pl.BlockSpec((None, tr, C), lambda l, i: (first + l, i, 0))],
        out_specs=pl.BlockSpec((None, tr, C), lambda l, i: (l, i, 0)),
        compiler_params=_params("parallel", "parallel"),
    )(w)


N_PEERS = 7


def _grad_sum(name, g, others):
    _, _, R, C = g.shape
    tr = _row_tile(R, C, 1 << 18)

    def body(g_ref, b_ref, o_ref):
        tot = g_ref[...].astype(F32)
        for s in range(N_PEERS):
            tot = tot + b_ref[s].astype(F32)
        o_ref[...] = tot

    def mine(i):
        return (2 * lax.axis_index("x") + lax.axis_index("y"), lax.axis_index("c"), i, 0)

    return pl.pallas_call(
        body, name=name, out_shape=jax.ShapeDtypeStruct((2, R, C), F32), grid=(R // tr,),
        in_specs=[pl.BlockSpec((None, None, tr, C), mine),
                  pl.BlockSpec((N_PEERS, tr, C), lambda i: (0, i, 0))],
        out_specs=pl.BlockSpec((None, tr, C), lambda i: (lax.axis_index("c"), i, 0)),
        compiler_params=_params("parallel"),
    )(g, others)


def _adamw(name, w, g, m, v, copy_g=False):
    R, C = w.shape
    tr = _row_tile(R, C, 1 << 18) if R % 16 == 0 else R
    c1 = 1.0 - ADAM_B1 ** ADAM_STEP
    c2 = 1.0 - ADAM_B2 ** ADAM_STEP

    def body(w_ref, g_ref, m_ref, v_ref, d_ref, nm_ref, nv_ref, *g_out):
        gv = g_ref[...]
        nm = ADAM_B1 * m_ref[...] + (1.0 - ADAM_B1) * gv
        nv = ADAM_B2 * v_ref[...] + (1.0 - ADAM_B2) * (gv * gv)
        nm_ref[...] = nm
        nv_ref[...] = nv
        d_ref[...] = -ADAM_LR * ((nm / c1) / (jnp.sqrt(nv / c2) + ADAM_EPS) + ADAM_WD * w_ref[...])
        if copy_g:
            g_out[0][...] = gv

    spec = pl.BlockSpec((tr, C), lambda i: (i, 0))
    n_out = 4 if copy_g else 3
    return pl.pallas_call(
        body, name=name, out_shape=[jax.ShapeDtypeStruct((R, C), F32)] * n_out, grid=(R // tr,),
        in_specs=[spec] * 4, out_specs=[spec] * n_out, compiler_params=_params("parallel"),
    )(w, g, m, v)


def _coords():
    return lax.axis_index("x"), lax.axis_index("y"), lax.axis_index("c")


def _flip(x, y, c, k):
    return (1 - x if k & 4 else x, 1 - y if k & 2 else y, 1 - c if k & 1 else c)


def _small_exchange(name, buf, reduce):
    rows = buf.shape[0]

    def body(x_ref, o_ref, land, send_sems, recv_sems):
        x, y, c = _coords()
        me = 4 * x + 2 * y + c

        def copy(k, sending):
            px, py, pc = _flip(x, y, c, k)
            slot = me if sending else 4 * px + 2 * py + pc
            return pltpu.make_async_remote_copy(
                src_ref=x_ref, dst_ref=land.at[slot], send_sem=send_sems.at[k - 1], recv_sem=recv_sems.at[k - 1],
                device_id=(px, py, pc), device_id_type=MESH)

        for k in range(1, 8):
            copy(k, True).start()
        land[me] = x_ref[...]
        for k in range(1, 8):
            copy(k, False).wait()
        if reduce:
            acc = land[0]
            for s in range(1, 8):
                acc = acc + land[s]
            o_ref[...] = acc
        else:
            o_ref[...] = land[...]

    out_shape = jax.ShapeDtypeStruct((rows, 128) if reduce else (8, rows, 128), F32)
    return pl.pallas_call(
        body, name=name, out_shape=out_shape,
        in_specs=[pl.BlockSpec(memory_space=pltpu.VMEM)], out_specs=pl.BlockSpec(memory_space=pltpu.VMEM),
        scratch_shapes=[pltpu.VMEM((8, rows, 128), F32), pltpu.SemaphoreType.DMA((7,)), pltpu.SemaphoreType.DMA((7,))],
        compiler_params=pltpu.CompilerParams(vmem_limit_bytes=VMEM_LIMIT),
    )(buf)


def _handshake(peers):
    barrier = pltpu.get_barrier_semaphore()
    for peer in peers:
        pl.semaphore_signal(barrier, inc=1, device_id=peer, device_id_type=MESH)
    pl.semaphore_wait(barrier, len(peers))


def _sequencer_mesh():
    return plsc.ScalarSubcoreMesh(axis_name="sequencer", num_cores=1)


def _allgather_weight(name, shard, collective_id):
    def body(in_ref, out_ref, send_sems, recv_sems, local_sem):
        x, y, c = _coords()
        chip = 2 * x + y
        sib = (x, y, 1 - c)
        chips = [_flip(x, y, c, k) for k in (4, 2, 6)]
        _handshake([sib] + chips)

        def slab(cx, cy, cc):
            return out_ref.at[2 * cx + cy, cc]

        def copy(k, src, dst, to):
            return pltpu.make_async_remote_copy(
                src_ref=src, dst_ref=dst, send_sem=send_sems.at[k], recv_sem=recv_sems.at[k],
                device_id=to, device_id_type=MESH)

        local = pltpu.make_async_copy(in_ref.at[c], out_ref.at[chip, c], local_sem)
        local.start()
        started = []
        for j, to in enumerate(chips):
            started.append(copy(1 + j, in_ref.at[c], slab(x, y, c), to))
        started.append(copy(0, in_ref.at[c], slab(x, y, c), sib))
        for cp in started:
            cp.start()
        for j, (px, py, pc) in enumerate(chips):
            held = slab(px, py, c)
            copy(1 + j, held, held, (px, py, pc)).wait_recv()
            cp = copy(4 + j, held, held, sib)
            cp.start()
            started.append(cp)
        got = slab(x, y, 1 - c)
        copy(0, got, got, sib).wait_recv()
        for j, (px, py, pc) in enumerate(chips):
            got = slab(px, py, 1 - c)
            copy(4 + j, got, got, sib).wait_recv()
        for cp in started:
            cp.wait_send()
        local.wait()

    return pl.kernel(
        body, out_type=jax.ShapeDtypeStruct((N_CHIPS,) + shard.shape, shard.dtype),
        mesh=_sequencer_mesh(), name=name,
        scratch_types=[pltpu.SemaphoreType.DMA((7,)), pltpu.SemaphoreType.DMA((7,)), pltpu.SemaphoreType.DMA],
        compiler_params=pltpu.CompilerParams(collective_id=collective_id),
    )(shard)


def _grad_exchange(name, gs, collective_id, after=None):
    T = len(gs)

    def body(*refs):
        ins, outs = refs[:T], refs[-T - 2:-2]
        send_sems, recv_sems = refs[-2:]
        x, y, c = _coords()
        chip = 2 * x + y
        sib = (x, y, 1 - c)
        chips = [_flip(x, y, c, k) for k in (4, 2, 6)]
        _handshake([sib] + [(px, py, cc) for px, py, _ in chips for cc in (0, 1)])
        cps = []
        for t in range(T):
            for j, (px, py, _) in enumerate(chips):
                for cc in (0, 1):
                    cps.append(pltpu.make_async_remote_copy(
                        src_ref=ins[t].at[2 * px + py, cc], dst_ref=outs[t].at[1 + 2 * j + c],
                        send_sem=send_sems.at[t, 1 + 2 * j + cc], recv_sem=recv_sems.at[t, 1 + 2 * j + c],
                        device_id=(px, py, cc), device_id_type=MESH))
            cps.append(pltpu.make_async_remote_copy(
                src_ref=ins[t].at[chip, 1 - c], dst_ref=outs[t].at[0], send_sem=send_sems.at[t, 0],
                recv_sem=recv_sems.at[t, 0], device_id=sib, device_id_type=MESH))
        for cp in cps:
            cp.start()
        for cp in cps:
            cp.wait_send()
        for t in range(T):
            for slot in range(N_PEERS):
                pltpu.make_async_remote_copy(
                    src_ref=outs[t].at[slot], dst_ref=outs[t].at[slot], send_sem=send_sems.at[t, slot],
                    recv_sem=recv_sems.at[t, slot], device_id=sib, device_id_type=MESH).wait_recv()

    operands = list(gs) + ([after] if after is not None else [])
    return pl.kernel(
        body, out_type=[jax.ShapeDtypeStruct((N_PEERS,) + g.shape[2:], g.dtype) for g in gs],
        mesh=_sequencer_mesh(), name=name,
        scratch_types=[pltpu.SemaphoreType.DMA((T, N_PEERS)), pltpu.SemaphoreType.DMA((T, N_PEERS))],
        compiler_params=pltpu.CompilerParams(collective_id=collective_id),
    )(*operands)


def _pair_fill(name, fulls):
    T = len(fulls)

    def body(*refs):
        outs = refs[T:2 * T]
        send_sems, recv_sems = refs[2 * T:]
        x, y, c = _coords()
        sib = (x, y, 1 - c)
        cps = []
        for t in range(T):
            send = pltpu.make_async_remote_copy(
                src_ref=outs[t].at[c], dst_ref=outs[t].at[c], send_sem=send_sems.at[t],
                recv_sem=recv_sems.at[t], device_id=sib, device_id_type=MESH)
            recv = pltpu.make_async_remote_copy(
                src_ref=outs[t].at[1 - c], dst_ref=outs[t].at[1 - c], send_sem=send_sems.at[t],
                recv_sem=recv_sems.at[t], device_id=sib, device_id_type=MESH)
            send.start()
            cps.append((send, recv))
        for send, recv in cps:
            send.wait_send()
            recv.wait_recv()

    anyspec = pl.BlockSpec(memory_space=pl.ANY)
    return pl.pallas_call(
        body, name=name,
        out_shape=[jax.ShapeDtypeStruct(f.shape, f.dtype) for f in fulls],
        in_specs=[anyspec] * T, out_specs=[anyspec] * T,
        input_output_aliases={t: t for t in range(T)},
        scratch_shapes=[pltpu.SemaphoreType.DMA((T,)), pltpu.SemaphoreType.DMA((T,))],
    )(*fulls)


def _pack(arrs):
    flat = jnp.concatenate([a.reshape(-1).astype(F32) for a in arrs])
    n = flat.shape[0]
    rows = -(-n // 1024) * 8
    return jnp.pad(flat, (0, rows * 128 - n)).reshape(rows, 128)


def _unpack(buf, shapes):
    flat = buf.reshape(-1)
    out, pos = [], 0
    for s in shapes:
        n = math.prod(s)
        out.append(flat[pos:pos + n].reshape(s))
        pos += n
    return out


def kernel(x, mix_norm_g, ffn_norm_g, final_norm_g, sc_w_in, sc_conv_w, sc_conv_b, sc_w_out, attn_w_qkv, attn_w_out, ffn_w_up, ffn_conv_w, ffn_conv_b, ffn_w_down, loss_target, m_mix_norm_g, m_ffn_norm_g, m_final_norm_g, m_sc_w_in, m_sc_conv_w, m_sc_conv_b, m_sc_w_out, m_attn_w_qkv, m_attn_w_out, m_ffn_w_up, m_ffn_conv_w, m_ffn_conv_b, m_ffn_w_down, v_mix_norm_g, v_ffn_norm_g, v_final_norm_g, v_sc_w_in, v_sc_conv_w, v_sc_conv_b, v_sc_w_out, v_attn_w_qkv, v_attn_w_out, v_ffn_w_up, v_ffn_conv_w, v_ffn_conv_b, v_ffn_w_down):
    S, D = x.shape[1], x.shape[2]
    xi, yi, ci = _coords()
    chip = 2 * xi + yi
    x0 = x.reshape(S, D)
    tgt = loss_target.reshape(S, D)

    conv_shapes = [sc_conv_w.shape, ffn_conv_w.shape]
    allc = _small_exchange("gather_conv_w", _pack([sc_conv_w, ffn_conv_w]), reduce=False)
    per_chip = [_unpack(allc[2 * k], conv_shapes) for k in range(N_CHIPS)]
    scw = jnp.concatenate([p[0] for p in per_chip], axis=-1)[0]
    fcw = jnp.concatenate([p[1] for p in per_chip], axis=-1)
    scb = sc_conv_b

    big_names = ["sc_w_in", "sc_w_out", "attn_w_qkv", "attn_w_out", "ffn_w_up", "ffn_w_down"]
    big = dict(zip(big_names, [sc_w_in, sc_w_out, attn_w_qkv, attn_w_out, ffn_w_up, ffn_w_down]))
    n_gathers = [0]

    def gather(tag, w, layers):
        _, R, C = w.shape
        shard = _cast_bf16("cast_" + tag, w, layers).reshape(2, len(layers) * R // 2, C)
        cid = n_gathers[0]
        n_gathers[0] += 1
        return _allgather_weight("allgather_" + tag, shard, cid).reshape(N_CHIPS, len(layers), R, C)

    w_in = gather("sc_w_in", sc_w_in, (0,))
    w_out = gather("sc_w_out", sc_w_out, (0,))
    w_ups = [gather("ffn_w_up0", ffn_w_up, (0,))]
    w_dn = gather("ffn_w_down", ffn_w_down, (0, 1))
    w_qkv = gather("attn_w_qkv", attn_w_qkv, (0,))
    w_ao = gather("attn_w_out", attn_w_out, (0,))
    w_ups.append(gather("ffn_w_up1", ffn_w_up, (1,)))

    big_grads = {}
    exchanges = []

    def reduce_scatter(tags, grads):
        after = exchanges[-1][0] if exchanges else None
        outs = _grad_exchange("rs_exchange_" + tags[0], grads, n_gathers[0] + len(exchanges), after=after)
        exchanges.append(outs)
        for tag, g, others in zip(tags, grads, outs):
            big_grads[tag] = _grad_sum("rs_sum_" + tag, g, others)

    h0 = _rmsnorm_fwd("norm_mix0", x0, mix_norm_g[0:1])[0]
    z = _mm_nn_col("sc_in", h0, w_in, 0)
    y = _sc_fwd("sc_gate", z, scw, scb)
    x1 = _mm_nn_row("sc_out", y, w_out, 0, x0)
    h1 = _rmsnorm_fwd("norm_ffn0", x1, ffn_norm_g[0:1])[0]
    u0 = _mm_nn_col("ffn_up0", h1, w_ups[0], 0)
    f0, v0 = _ffn_fwd("ffn_gate0", u0, fcw[0], ffn_conv_b[0:1])
    x2 = _mm_nn_row("ffn_down0", f0, w_dn, 0, x1)
    h2s = _rmsnorm_fwd("norm_mix1", x2, mix_norm_g[1:2], dilated=True)
    qkvs = [_mm_nn_col(f"attn_qkv{d}", h, w_qkv, 0, col_off=gi * 3 * D, ncols=3 * D)
            for gi, (h, d) in enumerate(zip(h2s, DILATIONS))]
    og, lg = zip(*[_attn_fwd(f"attn_fwd{d}", q, d) for q, d in zip(qkvs, DILATIONS)])
    o32, ob, lse = _attn_combine("attn_combine", list(og), list(lg))
    x3 = _mm_nn_row("attn_out", ob, w_ao, 0, x2)
    h3 = _rmsnorm_fwd("norm_ffn1", x3, ffn_norm_g[1:2])[0]
    u1 = _mm_nn_col("ffn_up1", h3, w_ups[1], 0)
    f1, v1 = _ffn_fwd("ffn_gate1", u1, fcw[1], ffn_conv_b[1:2])
    x4 = _mm_nn_row("ffn_down1", f1, w_dn, 1, x3)

    dx4, dx4b, dg_final, loss_part = _rmsnorm_bwd("loss_norm_bwd", x4, final_norm_g.reshape(1, D), target=tgt)

    def ffn_backward(layer, xin, h, u, v, f, dxo, dxob, gain, g_up, g_dn, last=False):
        g_dn = _mm_tn_row(f"ffn_down_dw{layer}", f, dxob, prev=g_dn, layer=layer)
        if last:
            reduce_scatter(["ffn_w_down"], [g_dn])
        df = _mm_nt_row(f"ffn_down_dx{layer}", dxob, w_dn, layer, after=g_dn)
        du, dwb = _ffn_bwd(f"ffn_gate_bwd{layer}", u, v, df, fcw[layer])
        g_up = _mm_tn_col(f"ffn_up_dw{layer}", h, du, w_ups[layer].shape[3], prev=g_up, layer=layer)
        if last:
            reduce_scatter(["ffn_w_up"], [g_up])
        dh = _mm_nt_col(f"ffn_up_dx{layer}", du, w_ups[layer], 0, after=g_up)
        dxi, dxib, dg = _rmsnorm_bwd(f"norm_ffn_bwd{layer}", xin, gain, dhs=[(dh, 1)], dres=dxo)
        return dxi, dxib, dg, dwb, g_up, g_dn

    dx3, dx3b, dg_ffn1, dwb_ffn1, g_up, g_dn = ffn_backward(1, x3, h3, u1, v1, f1, dx4, dx4b, ffn_norm_g[1:2], None, None)

    g_ao = _mm_tn_row("attn_out_dw", ob, dx3b)
    reduce_scatter(["attn_w_out"], [g_ao])
    do = _mm_nt_row("attn_out_dx", dx3b, w_ao, 0, after=g_ao)
    dos, lss, dls = _attn_bwd_prep("attn_bwd_prep", do, o32, lse)
    dqkvs = [_attn_bwd(f"attn_bwd{d}", q, a, b, c_, d)
             for q, a, b, c_, d in zip(qkvs, dos, lss, dls, DILATIONS)]
    g_qkv = None
    for gi, (h, dq, d) in enumerate(zip(h2s, dqkvs, DILATIONS)):
        g_qkv = _mm_tn_col(f"attn_qkv_dw{d}", h, dq, w_qkv.shape[3], col_off=gi * 3 * D, prev=g_qkv)
    reduce_scatter(["attn_w_qkv"], [g_qkv])
    dh2s = [(_mm_nt_col(f"attn_qkv_dx{d}", dq, w_qkv, 0, col_off=gi * 3 * D, after=g_qkv), d)
            for gi, (dq, d) in enumerate(zip(dqkvs, DILATIONS))]
    dx2, dx2b, dg_mix1 = _rmsnorm_bwd("norm_mix_bwd1", x2, mix_norm_g[1:2], dhs=dh2s, dres=dx3)

    dx1, dx1b, dg_ffn0, dwb_ffn0, g_up, g_dn = ffn_backward(
        0, x1, h1, u0, v0, f0, dx2, dx2b, ffn_norm_g[0:1], g_up, g_dn, last=True)

    g_out = _mm_tn_row("sc_out_dw", y, dx1b)
    dy = _mm_nt_row("sc_out_dx", dx1b, w_out, 0, after=g_out)
    dz, dwb_sc = _sc_bwd("sc_gate_bwd", z, dy, scw, scb)
    g_in = _mm_tn_col("sc_in_dw", h0, dz, w_in.shape[3])
    reduce_scatter(["sc_w_out", "sc_w_in"], [g_out, g_in])
    dh0 = _mm_nt_col("sc_in_dx", dz, w_in, 0, after=g_in)
    dx0, _, dg_mix0 = _rmsnorm_bwd("norm_mix_bwd0", x0, mix_norm_g[0:1], dhs=[(dh0, 1)], dres=dx1)

    dconv_sc = dwb_sc[0:3].reshape(1, 3, D)
    dbias_sc = dwb_sc[3:4]
    dconv_ffn = jnp.stack([dwb_ffn0[0:3], dwb_ffn1[0:3]])
    dbias_ffn = jnp.concatenate([dwb_ffn0[3:4], dwb_ffn1[3:4]], axis=0)
    small_parts = [jnp.concatenate([dg_mix0, dg_mix1], axis=0), jnp.concatenate([dg_ffn0, dg_ffn1], axis=0),
                   dg_final.reshape(D), dconv_sc, dbias_sc, dconv_ffn, dbias_ffn, loss_part[0, 0:1]]
    small_shapes = [a.shape for a in small_parts]
    summed = _unpack(_small_exchange("allreduce_small", _pack(small_parts), reduce=True), small_shapes)
    g_mix, g_ffn, g_final, g_scw_full, g_scb, g_fcw_full, g_fcb, loss = summed
    loss = loss.reshape(())
    cw = sc_conv_w.shape[2]
    g_scw = lax.dynamic_slice_in_dim(g_scw_full, chip * cw, cw, axis=2)
    fw = ffn_conv_w.shape[2]
    g_fcw = lax.dynamic_slice_in_dim(g_fcw_full, chip * fw, fw, axis=2)

    names = ["mix_norm_g", "ffn_norm_g", "final_norm_g", "sc_w_in", "sc_conv_w", "sc_conv_b", "sc_w_out",
             "attn_w_qkv", "attn_w_out", "ffn_w_up", "ffn_conv_w", "ffn_conv_b", "ffn_w_down"]
    ws = dict(zip(names, [mix_norm_g, ffn_norm_g, final_norm_g, sc_w_in, sc_conv_w, sc_conv_b, sc_w_out,
                          attn_w_qkv, attn_w_out, ffn_w_up, ffn_conv_w, ffn_conv_b, ffn_w_down]))
    ms = dict(zip(names, [m_mix_norm_g, m_ffn_norm_g, m_final_norm_g, m_sc_w_in, m_sc_conv_w, m_sc_conv_b, m_sc_w_out,
                          m_attn_w_qkv, m_attn_w_out, m_ffn_w_up, m_ffn_conv_w, m_ffn_conv_b, m_ffn_w_down]))
    vs = dict(zip(names, [v_mix_norm_g, v_ffn_norm_g, v_final_norm_g, v_sc_w_in, v_sc_conv_w, v_sc_conv_b, v_sc_w_out,
                          v_attn_w_qkv, v_attn_w_out, v_ffn_w_up, v_ffn_conv_w, v_ffn_conv_b, v_ffn_w_down]))
    gs = {"mix_norm_g": g_mix, "ffn_norm_g": g_ffn, "final_norm_g": g_final, "sc_conv_w": g_scw,
          "sc_conv_b": g_scb, "ffn_conv_w": g_fcw, "ffn_conv_b": g_fcb}
    filled = _pair_fill("rs_pair_fill", [big_grads[n] for n in big_names])
    gs.update({n: f.reshape(big[n].shape) for n, f in zip(big_names, filled)})

    deltas, new_m, new_v = {}, {}, {}
    small_names = [n for n in names if n not in big_names]
    packed = [_pack([d[n] for n in small_names]) for d in (ws, gs, ms, vs)]
    outs = _adamw("adamw_small", *packed)
    shapes = [ws[n].shape for n in small_names]
    for res, o in zip((deltas, new_m, new_v), outs):
        res.update(dict(zip(small_names, _unpack(o, shapes))))
    for n in big_names:
        shp = ws[n].shape
        two_d = (shp[0] * shp[1], shp[2])
        outs = _adamw("adamw_" + n, *[d[n].reshape(two_d) for d in (ws, gs, ms, vs)], copy_g=True)
        for res, o in zip((deltas, new_m, new_v, gs), outs):
            res[n] = o.reshape(shp)

    return (loss, dx0.reshape(x.shape), *[gs[n] for n in names], *[deltas[n] for n in names],
            *[new_m[n] for n in names], *[new_v[n] for n in names])
```

```python
import math

import jax
import jax.numpy as jnp
from jax import lax
from jax.experimental import pallas as pl
from jax.experimental.pallas import tpu as pltpu
from jax.experimental.pallas import tpu_sc as plsc

F32 = jnp.float32
BF = jnp.bfloat16
MESH = pl.DeviceIdType.MESH

HEAD_DIM = 128
ATTN_HALF = 64
ATTN_BLOCK = 128
DILATIONS = (1, 4, 16)
STAT_LANES = 128
HALO = 16
NORM_EPS = 1e-5
ALIBI_MAX = 8.0
NEG_INF = -1e30
N_CHIPS = 4
VMEM_LIMIT = 56 * 1024 * 1024

ADAM_LR = 0.001
ADAM_B1 = 0.9
ADAM_B2 = 0.999
ADAM_EPS = 1e-08
ADAM_WD = 0.01
ADAM_STEP = 10


def _pick(n, cands):
    for c in cands:
        if n % c == 0:
            return c
    raise ValueError(f"no tile for {n} in {cands}")


def _row_tile(rows, cols, max_elems=1 << 19):
    for c in (512, 256, 128, 64, 32, 16):
        if rows % c == 0 and c * cols <= max_elems:
            return c
    raise ValueError(f"no row tile for {rows}x{cols}")


def _params(*sem):
    return pltpu.CompilerParams(dimension_semantics=sem, vmem_limit_bytes=VMEM_LIMIT)


def _matmul(name, a, b, out_shape, grid, a_spec, b_spec, o_spec, contract, acc_shape,
            res=None, res_spec=None, prev=None, b2_spec=None, after=None, norm_g=None):
    nk = grid[2]
    n_b = 1 if b2_spec is None else 2

    def body(*refs):
        refs = list(refs)
        if prev is not None:
            refs.pop(0)
        a_ref, b_ref = refs[0], refs[1]
        res_ref = refs[1 + n_b] if res is not None else None
        acc_ref = refs[-1]
        if norm_g is None:
            o_ref, h_ref, g_ref = refs[-2], None, None
        else:
            g_ref, o_ref, h_ref = refs[-4], refs[-3], refs[-2]
        bv = b_ref[...]
        if bv.ndim == 3:
            bv = bv.reshape(bv.shape[0] * bv.shape[1], bv.shape[2])
        if b2_spec is None:
            part = lax.dot_general(a_ref[...], bv, contract, preferred_element_type=F32)
        else:
            half = a_ref.shape[1] // 2
            part = (lax.dot_general(a_ref[:, :half], bv, contract, preferred_element_type=F32)
                    + lax.dot_general(a_ref[:, half:], refs[2][...], contract, preferred_element_type=F32))

        def finish(total):
            if res_ref is not None:
                total = total + res_ref[...]
            o_ref[...] = total.reshape(o_ref.shape).astype(o_ref.dtype)
            if h_ref is not None:
                r = lax.rsqrt(jnp.mean(total * total, axis=1, keepdims=True) + NORM_EPS)
                h_ref[...] = (total * r * g_ref[...]).astype(h_ref.dtype)

        if nk == 1:
            finish(part)
        else:
            k = pl.program_id(2)

            @pl.when(k == 0)
            def _():
                acc_ref[...] = part

            @pl.when(jnp.logical_and(k > 0, k < nk - 1))
            def _():
                acc_ref[...] += part

            @pl.when(k == nk - 1)
            def _():
                finish(acc_ref[...] + part)

    operands, in_specs, aliases = [], [], {}
    if prev is not None:
        operands.append(prev)
        in_specs.append(pl.BlockSpec(memory_space=pl.ANY))
        aliases = {0: 0}
    operands += [a, b]
    in_specs += [a_spec, b_spec]
    if b2_spec is not None:
        operands.append(b)
        in_specs.append(b2_spec)
    if res is not None:
        operands.append(res)
        in_specs.append(res_spec)
    if after is not None:
        operands.append(after)
        in_specs.append(pl.BlockSpec(memory_space=pl.ANY))
    if norm_g is not None:
        operands.append(norm_g)
        in_specs.append(pl.BlockSpec(norm_g.shape, lambda i, j, k: (0, 0)))
        out_shape = [out_shape, jax.ShapeDtypeStruct(out_shape.shape, BF)]
        o_spec = [o_spec, o_spec]
    return pl.pallas_call(
        body, name=name, out_shape=out_shape, grid=grid, in_specs=in_specs, out_specs=o_spec,
        scratch_shapes=[pltpu.VMEM(acc_shape if nk > 1 else (8, 128), F32)],
        input_output_aliases=aliases,
        compiler_params=_params("parallel", "parallel", "arbitrary"),
    )(*operands)


NN = (((1,), (0,)), ((), ()))
NT = (((1,), (1,)), ((), ()))
TN = (((0,), (0,)), ((), ()))

_COL_TILES = (1536, 1408, 1024, 768, 512, 384, 256, 128)
_WIDE_TILES = (2816,) + _COL_TILES


def _mm_nn_col(name, a, w, layer, col_off=0, ncols=None, out_dtype=BF):
    M, K = a.shape
    _, _, R, C = w.shape
    assert R == K
    ncols = N_CHIPS * C if ncols is None else ncols
    tn = _pick(math.gcd(C, math.gcd(ncols, col_off) if col_off else ncols), _WIDE_TILES)
    tm = _pick(M, (1024, 512, 256))
    nb, off = C // tn, col_off // tn
    return _matmul(
        name, a, w, jax.ShapeDtypeStruct((M, ncols), out_dtype), (M // tm, ncols // tn, 1),
        pl.BlockSpec((tm, K), lambda i, j, k: (i, 0)),
        pl.BlockSpec((None, None, K, tn), lambda i, j, k: ((j + off) // nb, layer, 0, (j + off) % nb)),
        pl.BlockSpec((tm, tn), lambda i, j, k: (i, j)), NN, (tm, tn))


def _mm_nn_row(name, a, w, layer, res, norm_g=None):
    M, K = a.shape
    _, _, R, C = w.shape
    assert N_CHIPS * R == K
    chips_per_step = N_CHIPS if K <= 2048 else 2
    tk = chips_per_step * R
    tm = _pick(M, (1024, 512, 256) if norm_g is None else (512, 256))
    tn = _pick(C, (1024, 512, 256)) if norm_g is None else C
    assert norm_g is None or K <= 2048
    return _matmul(
        name, a, w, jax.ShapeDtypeStruct((M, C), F32), (M // tm, C // tn, K // tk),
        pl.BlockSpec((tm, tk), lambda i, j, k: (i, k)),
        pl.BlockSpec((chips_per_step, None, R, tn), lambda i, j, k: (k, layer, 0, j)),
        pl.BlockSpec((tm, tn), lambda i, j, k: (i, j)), NN, (tm, tn),
        res=res, res_spec=pl.BlockSpec((tm, tn), lambda i, j, k: (i, j)), norm_g=norm_g)


def _mm_nt_col(name, dy, w, layer, col_off=0, out_dtype=F32, after=None):
    M, n = dy.shape
    _, _, R, C = w.shape
    tk = _pick(math.gcd(C, math.gcd(n, col_off) if col_off else n), _WIDE_TILES)
    tm = _pick(M, (1024, 512, 256))
    tn = _pick(R, (1024, 512, 256))
    nb, off = C // tk, col_off // tk
    per_step = 2 if (tk <= 1536 and (n // tk) % 2 == 0) else 1

    def w_block(t):
        return pl.BlockSpec((None, None, tn, tk), lambda i, j, k: (
            (per_step * k + t + off) // nb, layer, j, (per_step * k + t + off) % nb))

    return _matmul(
        name, dy, w, jax.ShapeDtypeStruct((M, R), out_dtype), (M // tm, R // tn, n // (per_step * tk)),
        pl.BlockSpec((tm, per_step * tk), lambda i, j, k: (i, k)), w_block(0),
        pl.BlockSpec((tm, tn), lambda i, j, k: (i, j)), NT, (tm, tn),
        b2_spec=w_block(1) if per_step == 2 else None, after=after)


def _mm_nt_row(name, dy, w, layer, out_dtype=BF, after=None):
    M, C2 = dy.shape
    _, _, R, C = w.shape
    assert C2 == C
    chips_per_tile = N_CHIPS if N_CHIPS * R <= 2048 else 2
    tn = chips_per_tile * R
    tm = _pick(M, (1024, 512, 256))
    return _matmul(
        name, dy, w, jax.ShapeDtypeStruct((M, N_CHIPS * R), out_dtype), (M // tm, N_CHIPS * R // tn, 1),
        pl.BlockSpec((tm, C), lambda i, j, k: (i, 0)),
        pl.BlockSpec((chips_per_tile, None, R, C), lambda i, j, k: (j, layer, 0, 0)),
        pl.BlockSpec((tm, tn), lambda i, j, k: (i, j)), NT, (tm, tn), after=after)


_TN_DEPTH = (2048, 1024, 512, 256)


def _half_index(rows, layer, tkx):
    if layer is None:
        hb = rows // 2 // tkx
        return rows // 2, lambda i: (i // hb, i % hb)
    return rows, lambda i: (layer, i)


def _mm_tn_col(name, xa, dy, C, col_off=0, prev=None, layer=None):
    M, K = xa.shape
    _, n = dy.shape
    tn = _pick(math.gcd(C, math.gcd(n, col_off) if col_off else n), _WIDE_TILES)
    tkx = _pick(K // 2 if layer is None else K, (1024, 512, 256, 128) if tn <= 1536 else (512, 256, 128))
    tmr = _pick(M, _TN_DEPTH)
    rh, split = _half_index(K, layer, tkx)
    nb, off = C // tn, col_off // tn
    return _matmul(
        name, xa, dy, jax.ShapeDtypeStruct((N_CHIPS, 2, rh, C), BF), (K // tkx, n // tn, M // tmr),
        pl.BlockSpec((tmr, tkx), lambda i, j, k: (k, i)),
        pl.BlockSpec((tmr, tn), lambda i, j, k: (k, j)),
        pl.BlockSpec((None, None, tkx, tn), lambda i, j, k: ((j + off) // nb, *split(i), (j + off) % nb)),
        TN, (tkx, tn), prev=prev)


def _mm_tn_row(name, xa, dy, prev=None, layer=None):
    M, K = xa.shape
    _, C = dy.shape
    R = K // N_CHIPS
    tmr = _pick(M, _TN_DEPTH)
    if layer is None and prev is None and K <= 2048:
        tn = _pick(C, (1024, 512, 256))
        return _matmul(
            name, xa, dy, jax.ShapeDtypeStruct((N_CHIPS, 2, R // 2, C), BF), (1, C // tn, M // tmr),
            pl.BlockSpec((tmr, K), lambda i, j, k: (k, 0)),
            pl.BlockSpec((tmr, tn), lambda i, j, k: (k, j)),
            pl.BlockSpec((N_CHIPS, 2, R // 2, tn), lambda i, j, k: (0, 0, 0, j)),
            TN, (K, tn))
    tkx = _pick(R // 2 if layer is None else R, (1408, 1024, 512, 256, 128))
    tn = _pick(C, (2048, 1024, 512, 256) if tkx <= 512 else (1024, 512, 256))
    rh, split = _half_index(R, layer, tkx)
    rb = R // tkx
    return _matmul(
        name, xa, dy, jax.ShapeDtypeStruct((N_CHIPS, 2, rh, C), BF), (K // tkx, C // tn, M // tmr),
        pl.BlockSpec((tmr, tkx), lambda i, j, k: (k, i)),
        pl.BlockSpec((tmr, tn), lambda i, j, k: (k, j)),
        pl.BlockSpec((None, None, tkx, tn), lambda i, j, k: (i // rb, *split(i % rb), j)),
        TN, (tkx, tn), prev=prev)


NORM_ROWS = 256
LANES = 128


def _chunk_scratch(tm, width):
    return pltpu.VMEM((width // LANES, tm, LANES), F32)


def _store_chunks(scr, value):
    for c in range(scr.shape[0]):
        scr[c] = value[:, c * LANES:(c + 1) * LANES]


def _load_chunks(scr):
    return jnp.concatenate([scr[c] for c in range(scr.shape[0])], axis=1)


def _to_residue_major(scr, o_ref, d, dtype):
    tm = scr.shape[1]
    for c in range(scr.shape[0]):
        for res in range(d):
            o_ref[res, :, c * LANES:(c + 1) * LANES] = scr[c, pl.ds(res, tm // d, stride=d), :].astype(dtype)


def _to_natural(scr, ref, d):
    tm = scr.shape[1]
    for c in range(scr.shape[0]):
        for res in range(d):
            scr[c, pl.ds(res, tm // d, stride=d), :] = ref[res, :, c * LANES:(c + 1) * LANES].astype(F32)


def _rmsnorm_fwd(name, x, g, dilated=False):
    S, D = x.shape
    tm = NORM_ROWS
    dils = DILATIONS[1:] if dilated else ()

    def body(x_ref, g_ref, h_ref, *rest):
        xv = x_ref[...]
        r = lax.rsqrt(jnp.mean(xv * xv, axis=1, keepdims=True) + NORM_EPS)
        h = xv * r * g_ref[...]
        h_ref[...] = h.astype(BF)
        if dils:
            scr = rest[-1]
            _store_chunks(scr, h)
            for o_ref, d in zip(rest[:-1], dils):
                _to_residue_major(scr, o_ref, d, BF)

    out_shape = [jax.ShapeDtypeStruct((S, D), BF)]
    out_specs = [pl.BlockSpec((tm, D), lambda i: (i, 0))]
    for d in dils:
        out_shape.append(jax.ShapeDtypeStruct((d, S // d, D), BF))
        out_specs.append(pl.BlockSpec((d, tm // d, D), lambda i: (0, i, 0)))
    outs = pl.pallas_call(
        body, name=name, out_shape=out_shape, grid=(S // tm,),
        in_specs=[pl.BlockSpec((tm, D), lambda i: (i, 0)), pl.BlockSpec((1, D), lambda i: (0, 0))],
        out_specs=out_specs,
        scratch_shapes=[_chunk_scratch(tm, D)] if dils else [],
        compiler_params=_params("parallel"),
    )(x, g)
    return [outs[0]] + [o.reshape(S, D) for o in outs[1:]]


def _rmsnorm_bwd(name, x, g, dhs=(), dres=None, target=None):
    S, D = x.shape
    tm = NORM_ROWS
    n_dh = len(dhs)

    def body(*refs):
        refs = list(refs)
        x_ref, g_ref = refs[0], refs[1]
        dh_refs = refs[2:2 + n_dh]
        pos = 2 + n_dh
        dres_ref = tgt_ref = None
        if dres is not None:
            dres_ref = refs[pos]
            pos += 1
        if target is not None:
            tgt_ref = refs[pos]
            pos += 1
        dx_ref, dxb_ref, dg_ref = refs[pos:pos + 3]
        pos += 3
        loss_ref = None
        if target is not None:
            loss_ref = refs[pos]
            pos += 1
        scr = refs[pos] if any(d > 1 for _, d in dhs) else None
        i = pl.program_id(0)

        xv = x_ref[...]
        gv = g_ref[...]
        r = lax.rsqrt(jnp.mean(xv * xv, axis=1, keepdims=True) + NORM_EPS)
        xhat = xv * r
        if target is not None:
            err = xhat * gv - tgt_ref[...]
            dh = err * (1.0 / D)
            part = jnp.sum(jnp.sum(err * err, axis=1, keepdims=True), axis=0, keepdims=True) * (0.5 / D)
        else:
            dh = None
            for ref, d in zip(dh_refs, [d for _, d in dhs]):
                if d == 1:
                    v = ref[...]
                else:
                    _to_natural(scr, ref, d)
                    v = _load_chunks(scr)
                dh = v if dh is None else dh + v
        dxhat = dh * gv
        dx = r * (dxhat - xhat * jnp.mean(dxhat * xhat, axis=1, keepdims=True))
        if dres_ref is not None:
            dx = dx + dres_ref[...]
        dx_ref[...] = dx
        dxb_ref[...] = dx.astype(BF)
        dg = jnp.sum(dh * xhat, axis=0, keepdims=True)

        @pl.when(i == 0)
        def _():
            dg_ref[...] = dg
            if loss_ref is not None:
                loss_ref[...] = jnp.broadcast_to(part, loss_ref.shape)

        @pl.when(i > 0)
        def _():
            dg_ref[...] += dg
            if loss_ref is not None:
                loss_ref[...] += jnp.broadcast_to(part, loss_ref.shape)

    row = pl.BlockSpec((tm, D), lambda i: (i, 0))
    operands = [x, g]
    in_specs = [row, pl.BlockSpec((1, D), lambda i: (0, 0))]
    for arr, d in dhs:
        if d == 1:
            operands.append(arr)
            in_specs.append(row)
        else:
            operands.append(arr.reshape(d, S // d, D))
            in_specs.append(pl.BlockSpec((d, tm // d, D), lambda i: (0, i, 0)))
    if dres is not None:
        operands.append(dres)
        in_specs.append(row)
    if target is not None:
        operands.append(target)
        in_specs.append(row)
    out_shape = [jax.ShapeDtypeStruct((S, D), F32), jax.ShapeDtypeStruct((S, D), BF),
                 jax.ShapeDtypeStruct((1, D), F32)]
    out_specs = [row, row, pl.BlockSpec((1, D), lambda i: (0, 0))]
    if target is not None:
        out_shape.append(jax.ShapeDtypeStruct((1, STAT_LANES), F32))
        out_specs.append(pl.BlockSpec((1, STAT_LANES), lambda i: (0, 0)))
    scratch = [_chunk_scratch(tm, D)] if any(d > 1 for _, d in dhs) else []
    return pl.pallas_call(
        body, name=name, out_shape=out_shape, grid=(S // tm,), in_specs=in_specs, out_specs=out_specs,
        scratch_shapes=scratch, compiler_params=_params("arbitrary"),
    )(*operands)


CONV_ROWS = 256
CONV_COLS = 256


def _halo_specs(S, tm, width):
    nh = S // HALO
    per = tm // HALO
    cur = pl.BlockSpec((tm, width), lambda i: (i, 0))
    prev = pl.BlockSpec((HALO, width), lambda i: (jnp.maximum(i * per - 1, 0), 0))
    nxt = pl.BlockSpec((HALO, width), lambda i: (jnp.minimum((i + 1) * per, nh - 1), 0))
    return [cur, prev, nxt]


def _ext(refs, cs, inr):
    cur, prev, nxt = refs
    v = jnp.concatenate([prev[:, cs], cur[:, cs], nxt[:, cs]], axis=0).astype(F32)
    return v if inr is None else jnp.where(inr, v, 0.0)


def _shift_prev(v):
    return pltpu.roll(v, 1, 0)


def _shift_next(v):
    return pltpu.roll(v, v.shape[0] - 1, 0)


def _shifts(v):
    return _shift_prev(v), _shift_next(v)


def _conv3(v, w, cs, b=None, shifted=None):
    vp, vn = _shifts(v) if shifted is None else shifted
    out = w[0:1, cs] * vp + w[1:2, cs] * v + w[2:3, cs] * vn
    return out if b is None else out + b[:, cs]


def _in_range(i, tm, tc, S):
    row = lax.broadcasted_iota(jnp.int32, (tm + 2 * HALO, tc), 0) + (i * tm - HALO)
    return jnp.logical_and(row >= 0, row < S)


def _core(v, tm):
    return v[HALO:HALO + tm, :]


def _acc_rows(ref, rows):
    for r, cs, val in rows:
        ref[r:r + 1, cs] += val


def _zero_first(ref, i):
    @pl.when(i == 0)
    def _():
        ref[...] = jnp.zeros(ref.shape, ref.dtype)


def _sc_fwd(name, z, w, b):
    S, D3 = z.shape
    D = D3 // 3
    tm, tc = CONV_ROWS, _pick(D, (CONV_COLS, 256, 128))

    def body(zc, zp, zn, w_ref, b_ref, y_ref):
        i = pl.program_id(0)
        inr = _in_range(i, tm, tc, S)
        zr = (zc, zp, zn)
        for c in range(D // tc):
            cs = slice(c * tc, (c + 1) * tc)
            u = _ext(zr, cs, inr)
            gc = _ext(zr, slice(2 * D + c * tc, 2 * D + (c + 1) * tc), None)
            conv = _conv3(gc * u, w_ref, cs, b_ref)
            gb = zc[:, D + c * tc:D + (c + 1) * tc].astype(F32)
            y_ref[:, cs] = (gb * _core(conv, tm)).astype(BF)

    return pl.pallas_call(
        body, name=name, out_shape=jax.ShapeDtypeStruct((S, D), BF), grid=(S // tm,),
        in_specs=_halo_specs(S, tm, D3) + [pl.BlockSpec((3, D), lambda i: (0, 0)),
                                           pl.BlockSpec((1, D), lambda i: (0, 0))],
        out_specs=pl.BlockSpec((tm, D), lambda i: (i, 0)),
        compiler_params=_params("parallel"),
    )(z, z, z, w, b)


def _sc_bwd(name, z, dy, w, b):
    S, D3 = z.shape
    D = D3 // 3
    tm, tc = CONV_ROWS, _pick(D, (CONV_COLS, 256, 128))

    def body(zc, zp, zn, dc_, dp_, dn_, w_ref, b_ref, dz_ref, dwb_ref):
        i = pl.program_id(0)
        inr = _in_range(i, tm, tc, S)
        _zero_first(dwb_ref, i)
        zr, dr = (zc, zp, zn), (dc_, dp_, dn_)
        for c in range(D // tc):
            cs = slice(c * tc, (c + 1) * tc)
            u = _ext(zr, cs, inr)
            gb = _ext(zr, slice(D + c * tc, D + (c + 1) * tc), None)
            gc = _ext(zr, slice(2 * D + c * tc, 2 * D + (c + 1) * tc), None)
            dyv = _ext(dr, cs, inr)
            p = gc * u
            p_prev, p_next = _shifts(p)
            conv = _conv3(p, w_ref, cs, b_ref, shifted=(p_prev, p_next))
            dconv = dyv * gb
            dp = w_ref[0:1, cs] * _shift_next(dconv) + w_ref[1:2, cs] * dconv + w_ref[2:3, cs] * _shift_prev(dconv)
            dz_ref[:, cs] = _core(dp * gc, tm).astype(BF)
            dz_ref[:, D + c * tc:D + (c + 1) * tc] = _core(dyv * conv, tm).astype(BF)
            dz_ref[:, 2 * D + c * tc:2 * D + (c + 1) * tc] = _core(dp * u, tm).astype(BF)
            dcc = _core(dconv, tm)
            _acc_rows(dwb_ref, [
                (0, cs, jnp.sum(dcc * _core(p_prev, tm), axis=0, keepdims=True)),
                (1, cs, jnp.sum(dcc * _core(p, tm), axis=0, keepdims=True)),
                (2, cs, jnp.sum(dcc * _core(p_next, tm), axis=0, keepdims=True)),
                (3, cs, jnp.sum(dcc, axis=0, keepdims=True))])

    return pl.pallas_call(
        body, name=name,
        out_shape=[jax.ShapeDtypeStruct((S, D3), BF), jax.ShapeDtypeStruct((4, D), F32)], grid=(S // tm,),
        in_specs=_halo_specs(S, tm, D3) + _halo_specs(S, tm, D) + [
            pl.BlockSpec((3, D), lambda i: (0, 0)), pl.BlockSpec((1, D), lambda i: (0, 0))],
        out_specs=[pl.BlockSpec((tm, D3), lambda i: (i, 0)), pl.BlockSpec((4, D), lambda i: (0, 0))],
        compiler_params=_params("arbitrary"),
    )(z, z, z, dy, dy, dy, w, b)


def _sigmoid(v):
    return 1.0 / (1.0 + jnp.exp(-v))


def _ffn_fwd(name, u, w, b):
    S, F2 = u.shape
    Fh = F2 // 2
    tm, tc = CONV_ROWS, _pick(Fh, (CONV_COLS, 256, 128))

    def body(uc, up, un, w_ref, b_ref, f_ref, v_ref):
        i = pl.program_id(0)
        inr = _in_range(i, tm, tc, S)
        ur = (uc, up, un)
        for c in range(Fh // tc):
            ca = slice(c * tc, (c + 1) * tc)
            cb = slice(Fh + c * tc, Fh + (c + 1) * tc)
            va = _core(_conv3(_ext(ur, ca, inr), w_ref, ca, b_ref), tm)
            vb = _core(_conv3(_ext(ur, cb, inr), w_ref, cb, b_ref), tm)
            v_ref[:, ca] = va.astype(BF)
            v_ref[:, cb] = vb.astype(BF)
            f_ref[:, ca] = (va * _sigmoid(va) * vb).astype(BF)

    return pl.pallas_call(
        body, name=name,
        out_shape=[jax.ShapeDtypeStruct((S, Fh), BF), jax.ShapeDtypeStruct((S, F2), BF)], grid=(S // tm,),
        in_specs=_halo_specs(S, tm, F2) + [pl.BlockSpec((3, F2), lambda i: (0, 0)),
                                           pl.BlockSpec((1, F2), lambda i: (0, 0))],
        out_specs=[pl.BlockSpec((tm, Fh), lambda i: (i, 0)), pl.BlockSpec((tm, F2), lambda i: (i, 0))],
        compiler_params=_params("parallel"),
    )(u, u, u, w, b)


def _ffn_bwd(name, u, v, df, w):
    S, F2 = u.shape
    Fh = F2 // 2
    tm, tc = CONV_ROWS, _pick(Fh, (CONV_COLS, 256, 128))

    def body(u_ref, vc, vp, vn, dc_, dp_, dn_, w_ref, du_ref, dwb_ref):
        i = pl.program_id(0)
        inr = _in_range(i, tm, tc, S)
        _zero_first(dwb_ref, i)
        vr, dr = (vc, vp, vn), (dc_, dp_, dn_)
        for c in range(Fh // tc):
            ca = slice(c * tc, (c + 1) * tc)
            cb = slice(Fh + c * tc, Fh + (c + 1) * tc)
            va, vb = _ext(vr, ca, None), _ext(vr, cb, None)
            dfv = _ext(dr, ca, inr)
            sg = _sigmoid(va)
            dva = dfv * vb * (sg * (1.0 + va * (1.0 - sg)))
            dvb = dfv * (va * sg)
            rows = []
            for cs, dv in ((ca, dva), (cb, dvb)):
                dv_prev, dv_next = _shifts(dv)
                dcore = w_ref[0:1, cs] * dv_next + w_ref[1:2, cs] * dv + w_ref[2:3, cs] * dv_prev
                du_ref[:, cs] = _core(dcore, tm).astype(BF)
                uu = u_ref[:, cs].astype(F32)
                dvc = _core(dv, tm)
                rows += [
                    (0, cs, jnp.sum(_core(dv_next, tm) * uu, axis=0, keepdims=True)),
                    (1, cs, jnp.sum(dvc * uu, axis=0, keepdims=True)),
                    (2, cs, jnp.sum(_core(dv_prev, tm) * uu, axis=0, keepdims=True)),
                    (3, cs, jnp.sum(dvc, axis=0, keepdims=True))]
            _acc_rows(dwb_ref, rows)

    return pl.pallas_call(
        body, name=name,
        out_shape=[jax.ShapeDtypeStruct((S, F2), BF), jax.ShapeDtypeStruct((4, F2), F32)], grid=(S // tm,),
        in_specs=[pl.BlockSpec((tm, F2), lambda i: (i, 0))] + _halo_specs(S, tm, F2) + _halo_specs(S, tm, Fh) + [
            pl.BlockSpec((3, F2), lambda i: (0, 0))],
        out_specs=[pl.BlockSpec((tm, F2), lambda i: (i, 0)), pl.BlockSpec((4, F2), lambda i: (0, 0))],
        compiler_params=_params("arbitrary"),
    )(u, v, v, v, df, df, df, w)


def _alibi_slopes(H):
    return [2.0 ** (-ALIBI_MAX * (h + 1) / H) for h in range(H)]


def _window_specs(S, width, col):
    n64 = S // ATTN_HALF
    cur = pl.BlockSpec((ATTN_BLOCK, width), lambda b: (b, col))
    prev = pl.BlockSpec((ATTN_HALF, width), lambda b: (jnp.maximum(2 * b - 1, 0), col))
    nxt = pl.BlockSpec((ATTN_HALF, width), lambda b: (jnp.minimum(2 * b + 2, n64 - 1), col))
    return [cur, prev, nxt]


def _fill_window(buf, cur, prev, nxt):
    buf[0:ATTN_HALF] = prev[...]
    buf[ATTN_HALF:ATTN_HALF + ATTN_BLOCK] = cur[...]
    buf[ATTN_HALF + ATTN_BLOCK:2 * ATTN_BLOCK] = nxt[...]


def _band(b, L):
    QB, W = ATTN_BLOCK, 2 * ATTN_BLOCK
    a_loc = (b * QB) % L
    row = lax.broadcasted_iota(jnp.int32, (QB, W), 0)
    col = lax.broadcasted_iota(jnp.int32, (QB, W), 1)
    dist = jnp.abs(col - ATTN_HALF - row)
    other = a_loc - ATTN_HALF + col
    valid = jnp.logical_and(dist <= ATTN_HALF, jnp.logical_and(other >= 0, other < L))
    return dist.astype(F32), valid


def _lane_col(stats, h):
    lane = lax.broadcasted_iota(jnp.int32, stats.shape, 1)
    return jnp.sum(jnp.where(lane == h, stats, 0.0), axis=1, keepdims=True)


def _attn_fwd(name, qkv, d):
    S, D3 = qkv.shape
    D = D3 // 3
    H = D // HEAD_DIM
    L = S // d
    scale = HEAD_DIM ** -0.5
    slopes = _alibi_slopes(H)

    def body(q_ref, kc, kp, kn, vc, vp, vn, o_ref, lse_ref, kbuf, vbuf):
        b = pl.program_id(0)
        _fill_window(kbuf, kc, kp, kn)
        _fill_window(vbuf, vc, vp, vn)
        dist, valid = _band(b, L)
        dist = dist * float(d)
        lane = lax.broadcasted_iota(jnp.int32, (ATTN_BLOCK, STAT_LANES), 1)
        lse = jnp.zeros((ATTN_BLOCK, STAT_LANES), F32)
        for h in range(H):
            cs = slice(h * HEAD_DIM, (h + 1) * HEAD_DIM)
            s = lax.dot_general(q_ref[:, cs], kbuf[:, cs], NT, preferred_element_type=F32) * scale
            s = jnp.where(valid, s - slopes[h] * dist, NEG_INF)
            m = jnp.max(s, axis=1, keepdims=True)
            p = jnp.exp(s - m)
            den = jnp.sum(p, axis=1, keepdims=True)
            o = jnp.dot(p.astype(BF), vbuf[:, cs], preferred_element_type=F32)
            o_ref[:, cs] = (o / den).astype(BF)
            lse = jnp.where(lane == h, m + jnp.log(den), lse)
        lse_ref[...] = lse

    return pl.pallas_call(
        body, name=name,
        out_shape=[jax.ShapeDtypeStruct((S, D), BF), jax.ShapeDtypeStruct((S, STAT_LANES), F32)],
        grid=(S // ATTN_BLOCK,),
        in_specs=[pl.BlockSpec((ATTN_BLOCK, D), lambda b: (b, 0))] + _window_specs(S, D, 1) + _window_specs(S, D, 2),
        out_specs=[pl.BlockSpec((ATTN_BLOCK, D), lambda b: (b, 0)),
                   pl.BlockSpec((ATTN_BLOCK, STAT_LANES), lambda b: (b, 0))],
        scratch_shapes=[pltpu.VMEM((2 * ATTN_BLOCK, D), BF), pltpu.VMEM((2 * ATTN_BLOCK, D), BF)],
        compiler_params=_params("parallel"),
    )(qkv, qkv, qkv, qkv, qkv, qkv, qkv)


def _dil_specs(S, tm, width):
    specs = [pl.BlockSpec((tm, width), lambda i: (i, 0))]
    for d in DILATIONS[1:]:
        specs.append(pl.BlockSpec((d, tm // d, width), lambda i: (0, i, 0)))
    return specs


def _attn_combine(name, outs, lses):
    S, D = outs[0].shape
    H = D // HEAD_DIM
    tm = NORM_ROWS

    def body(o1, o4, o16, l1, l4, l16, o_ref, ob_ref, lse_ref, oscr, lscr):
        ls = [l1[...]]
        for ref, d in zip((l4, l16), DILATIONS[1:]):
            _to_natural(lscr, ref, d)
            ls.append(lscr[0])
        top = jnp.maximum(jnp.maximum(ls[0], ls[1]), ls[2])
        es = [jnp.exp(l - top) for l in ls]
        tot = es[0] + es[1] + es[2]
        lse_ref[...] = top + jnp.log(tot)
        ws = [e / tot for e in es]
        for gi, (ref, d) in enumerate(zip((o1, o4, o16), DILATIONS)):
            if d > 1:
                _to_natural(oscr, ref, d)
            for h in range(H):
                cs = slice(h * HEAD_DIM, (h + 1) * HEAD_DIM)
                term = _lane_col(ws[gi], h) * (ref[:, cs] if d == 1 else oscr[h])
                if gi == 0:
                    o_ref[:, cs] = term
                else:
                    o_ref[:, cs] += term
        ob_ref[...] = o_ref[...].astype(BF)

    outs3 = [outs[0]] + [o.reshape(d, S // d, D) for o, d in zip(outs[1:], DILATIONS[1:])]
    lses3 = [lses[0]] + [l.reshape(d, S // d, STAT_LANES) for l, d in zip(lses[1:], DILATIONS[1:])]
    row = pl.BlockSpec((tm, D), lambda i: (i, 0))
    return pl.pallas_call(
        body, name=name,
        out_shape=[jax.ShapeDtypeStruct((S, D), F32), jax.ShapeDtypeStruct((S, D), BF),
                   jax.ShapeDtypeStruct((S, STAT_LANES), F32)],
        grid=(S // tm,),
        in_specs=_dil_specs(S, tm, D) + _dil_specs(S, tm, STAT_LANES),
        out_specs=[row, row, pl.BlockSpec((tm, STAT_LANES), lambda i: (i, 0))],
        scratch_shapes=[_chunk_scratch(tm, D), _chunk_scratch(tm, STAT_LANES)],
        compiler_params=_params("parallel"),
    )(*outs3, *lses3)


def _attn_bwd_prep(name, do, o32, lse):
    S, D = do.shape
    H = D // HEAD_DIM
    tm = NORM_ROWS
    dils = DILATIONS[1:]

    def body(do_ref, o_ref, lse_ref, dl_ref, do4, do16, l4, l16, d4, d16, scr, sscr):
        lane = lax.broadcasted_iota(jnp.int32, (tm, STAT_LANES), 1)
        delta = jnp.zeros((tm, STAT_LANES), F32)
        for h in range(H):
            cs = slice(h * HEAD_DIM, (h + 1) * HEAD_DIM)
            dov = do_ref[:, cs].astype(F32)
            scr[h] = dov
            delta = jnp.where(lane == h, jnp.sum(dov * o_ref[:, cs], axis=1, keepdims=True), delta)
        dl_ref[...] = delta
        for ref, d in zip((do4, do16), dils):
            _to_residue_major(scr, ref, d, BF)
        for val, refs in ((lse_ref[...], (l4, l16)), (delta, (d4, d16))):
            sscr[0] = val
            for ref, d in zip(refs, dils):
                _to_residue_major(sscr, ref, d, F32)

    def perm_shapes(width, dt):
        return [jax.ShapeDtypeStruct((d, S // d, width), dt) for d in dils]

    def perm_specs(width):
        return [pl.BlockSpec((d, tm // d, width), lambda i: (0, i, 0)) for d in dils]

    row = lambda w: pl.BlockSpec((tm, w), lambda i: (i, 0))
    outs = pl.pallas_call(
        body, name=name,
        out_shape=[jax.ShapeDtypeStruct((S, STAT_LANES), F32)]
        + perm_shapes(D, BF) + perm_shapes(STAT_LANES, F32) + perm_shapes(STAT_LANES, F32),
        grid=(S // tm,),
        in_specs=[row(D), row(D), row(STAT_LANES)],
        out_specs=[row(STAT_LANES)] + perm_specs(D) + perm_specs(STAT_LANES) + perm_specs(STAT_LANES),
        scratch_shapes=[_chunk_scratch(tm, D), _chunk_scratch(tm, STAT_LANES)],
        compiler_params=_params("parallel"),
    )(do, o32, lse)
    dos = [do] + [a.reshape(S, D) for a in outs[1:3]]
    lss = [lse] + [a.reshape(S, STAT_LANES) for a in outs[3:5]]
    dls = [outs[0]] + [a.reshape(S, STAT_LANES) for a in outs[5:7]]
    return dos, lss, dls


def _attn_bwd(name, qkv, do, lse, delta, d):
    S, D3 = qkv.shape
    D = D3 // 3
    H = D // HEAD_DIM
    L = S // d
    scale = HEAD_DIM ** -0.5
    slopes = _alibi_slopes(H)
    QB = ATTN_BLOCK

    def body(qc, qp, qn, kc, kp, kn, vc, vp, vn, dc_, dp_, dn_, lc, lp, ln, ec, ep, en,
             out_ref, qbuf, kbuf, vbuf, dobuf, lbuf, ebuf):
        b = pl.program_id(0)
        for buf, trio in ((qbuf, (qc, qp, qn)), (kbuf, (kc, kp, kn)), (vbuf, (vc, vp, vn)),
                          (dobuf, (dc_, dp_, dn_)), (lbuf, (lc, lp, ln)), (ebuf, (ec, ep, en))):
            _fill_window(buf, *trio)
        dist, valid = _band(b, L)
        dist = dist * float(d)
        lse_c, del_c = lc[...], ec[...]
        lse_w, del_w = lbuf[...].T, ebuf[...].T
        for h in range(H):
            cs = slice(h * HEAD_DIM, (h + 1) * HEAD_DIM)
            bias = slopes[h] * dist
            q, do_h = qc[:, cs], dc_[:, cs]
            kw, vw = kbuf[:, cs], vbuf[:, cs]
            s = lax.dot_general(q, kw, NT, preferred_element_type=F32) * scale - bias
            p = jnp.where(valid, jnp.exp(s - _lane_col(lse_c, h)), 0.0)
            dp = lax.dot_general(do_h, vw, NT, preferred_element_type=F32)
            ds = p * (dp - _lane_col(del_c, h))
            dq = jnp.dot(ds.astype(BF), kw, preferred_element_type=F32) * scale
            out_ref[:, cs] = dq.astype(BF)
            qw, dow = qbuf[:, cs], dobuf[:, cs]
            k, v = kc[:, cs], vc[:, cs]
            st = lax.dot_general(k, qw, NT, preferred_element_type=F32) * scale - bias
            pt = jnp.where(valid, jnp.exp(st - lse_w[h:h + 1, :]), 0.0)
            dv = jnp.dot(pt.astype(BF), dow, preferred_element_type=F32)
            dpt = lax.dot_general(v, dow, NT, preferred_element_type=F32)
            dst = pt * (dpt - del_w[h:h + 1, :])
            dk = jnp.dot(dst.astype(BF), qw, preferred_element_type=F32) * scale
            out_ref[:, D + h * HEAD_DIM:D + (h + 1) * HEAD_DIM] = dk.astype(BF)
            out_ref[:, 2 * D + h * HEAD_DIM:2 * D + (h + 1) * HEAD_DIM] = dv.astype(BF)

    W = 2 * QB
    return pl.pallas_call(
        body, name=name, out_shape=jax.ShapeDtypeStruct((S, D3), BF), grid=(S // QB,),
        in_specs=(_window_specs(S, D, 0) + _window_specs(S, D, 1) + _window_specs(S, D, 2)
                  + _window_specs(S, D, 0) + _window_specs(S, STAT_LANES, 0) + _window_specs(S, STAT_LANES, 0)),
        out_specs=pl.BlockSpec((QB, D3), lambda b: (b, 0)),
        scratch_shapes=[pltpu.VMEM((W, D), BF), pltpu.VMEM((W, D), BF), pltpu.VMEM((W, D), BF),
                        pltpu.VMEM((W, D), BF), pltpu.VMEM((W, STAT_LANES), F32), pltpu.VMEM((W, STAT_LANES), F32)],
        compiler_params=_params("parallel"),
    )(qkv, qkv, qkv, qkv, qkv, qkv, qkv, qkv, qkv, do, do, do, lse, lse, lse, delta, delta, delta)


def _cast_bf16(name, w, layers):
    _, R, C = w.shape
    tr = _row_tile(R, C)
    first = layers[0]

    def body(w_ref, o_ref):
        o_ref[...] = w_ref[...].astype(BF)

    return pl.pallas_call(
        body, name=name, out_shape=jax.ShapeDtypeStruct((len(layers), R, C), BF), grid=(len(layers), R // tr),
        in_specs=[pl.BlockSpec((None, tr, C), lambda l, i: (first + l, i, 0))],
        out_specs=pl.BlockSpec((None, tr, C), lambda l, i: (l, i, 0)),
        compiler_params=_params("parallel", "parallel"),
    )(w)


N_PEERS = 7


def _grad_sum(name, g, others):
    _, _, R, C = g.shape
    tr = _row_tile(R, C, 1 << 18)

    def body(g_ref, b_ref, o_ref):
        tot = g_ref[...].astype(F32)
        for s in range(N_PEERS):
            tot = tot + b_ref[s].astype(F32)
        o_ref[...] = tot

    def mine(i):
        return (2 * lax.axis_index("x") + lax.axis_index("y"), lax.axis_index("c"), i, 0)

    return pl.pallas_call(
        body, name=name, out_shape=jax.ShapeDtypeStruct((2, R, C), F32), grid=(R // tr,),
        in_specs=[pl.BlockSpec((None, None, tr, C), mine),
                  pl.BlockSpec((N_PEERS, tr, C), lambda i: (0, i, 0))],
        out_specs=pl.BlockSpec((None, tr, C), lambda i: (lax.axis_index("c"), i, 0)),
        compiler_params=_params("parallel"),
    )(g, others)


def _adamw(name, w, g, m, v, copy_g=False):
    R, C = w.shape
    tr = _row_tile(R, C, 1 << 18) if R % 16 == 0 else R
    c1 = 1.0 - ADAM_B1 ** ADAM_STEP
    c2 = 1.0 - ADAM_B2 ** ADAM_STEP

    def body(w_ref, g_ref, m_ref, v_ref, d_ref, nm_ref, nv_ref, *g_out):
        gv = g_ref[...]
        nm = ADAM_B1 * m_ref[...] + (1.0 - ADAM_B1) * gv
        nv = ADAM_B2 * v_ref[...] + (1.0 - ADAM_B2) * (gv * gv)
        nm_ref[...] = nm
        nv_ref[...] = nv
        d_ref[...] = -ADAM_LR * ((nm / c1) / (jnp.sqrt(nv / c2) + ADAM_EPS) + ADAM_WD * w_ref[...])
        if copy_g:
            g_out[0][...] = gv

    spec = pl.BlockSpec((tr, C), lambda i: (i, 0))
    n_out = 4 if copy_g else 3
    return pl.pallas_call(
        body, name=name, out_shape=[jax.ShapeDtypeStruct((R, C), F32)] * n_out, grid=(R // tr,),
        in_specs=[spec] * 4, out_specs=[spec] * n_out, compiler_params=_params("parallel"),
    )(w, g, m, v)


def _coords():
    return lax.axis_index("x"), lax.axis_index("y"), lax.axis_index("c")


def _flip(x, y, c, k):
    return (1 - x if k & 4 else x, 1 - y if k & 2 else y, 1 - c if k & 1 else c)


def _small_exchange(name, buf, reduce):
    rows = buf.shape[0]

    def body(x_ref, o_ref, land, send_sems, recv_sems):
        x, y, c = _coords()
        me = 4 * x + 2 * y + c

        def copy(k, sending):
            px, py, pc = _flip(x, y, c, k)
            slot = me if sending else 4 * px + 2 * py + pc
            return pltpu.make_async_remote_copy(
                src_ref=x_ref, dst_ref=land.at[slot], send_sem=send_sems.at[k - 1], recv_sem=recv_sems.at[k - 1],
                device_id=(px, py, pc), device_id_type=MESH)

        for k in range(1, 8):
            copy(k, True).start()
        land[me] = x_ref[...]
        for k in range(1, 8):
            copy(k, False).wait()
        if reduce:
            acc = land[0]
            for s in range(1, 8):
                acc = acc + land[s]
            o_ref[...] = acc
        else:
            o_ref[...] = land[...]

    out_shape = jax.ShapeDtypeStruct((rows, 128) if reduce else (8, rows, 128), F32)
    return pl.pallas_call(
        body, name=name, out_shape=out_shape,
        in_specs=[pl.BlockSpec(memory_space=pltpu.VMEM)], out_specs=pl.BlockSpec(memory_space=pltpu.VMEM),
        scratch_shapes=[pltpu.VMEM((8, rows, 128), F32), pltpu.SemaphoreType.DMA((7,)), pltpu.SemaphoreType.DMA((7,))],
        compiler_params=pltpu.CompilerParams(vmem_limit_bytes=VMEM_LIMIT),
    )(buf)


def _handshake(peers):
    barrier = pltpu.get_barrier_semaphore()
    for peer in peers:
        pl.semaphore_signal(barrier, inc=1, device_id=peer, device_id_type=MESH)
    pl.semaphore_wait(barrier, len(peers))


def _sequencer_mesh():
    return plsc.ScalarSubcoreMesh(axis_name="sequencer", num_cores=1)


def _allgather_weight(name, shard, collective_id):
    def body(in_ref, out_ref, send_sems, recv_sems, local_sem):
        x, y, c = _coords()
        chip = 2 * x + y
        sib = (x, y, 1 - c)
        chips = [_flip(x, y, c, k) for k in (4, 2, 6)]
        _handshake([sib] + chips)

        def slab(cx, cy, cc):
            return out_ref.at[2 * cx + cy, cc]

        def copy(k, src, dst, to):
            return pltpu.make_async_remote_copy(
                src_ref=src, dst_ref=dst, send_sem=send_sems.at[k], recv_sem=recv_sems.at[k],
                device_id=to, device_id_type=MESH)

        local = pltpu.make_async_copy(in_ref.at[c], out_ref.at[chip, c], local_sem)
        local.start()
        started = []
        for j, to in enumerate(chips):
            started.append(copy(1 + j, in_ref.at[c], slab(x, y, c), to))
        started.append(copy(0, in_ref.at[c], slab(x, y, c), sib))
        for cp in started:
            cp.start()
        for j, (px, py, pc) in enumerate(chips):
            held = slab(px, py, c)
            copy(1 + j, held, held, (px, py, pc)).wait_recv()
            cp = copy(4 + j, held, held, sib)
            cp.start()
            started.append(cp)
        got = slab(x, y, 1 - c)
        copy(0, got, got, sib).wait_recv()
        for j, (px, py, pc) in enumerate(chips):
            got = slab(px, py, 1 - c)
            copy(4 + j, got, got, sib).wait_recv()
        for cp in started:
            cp.wait_send()
        local.wait()

    return pl.kernel(
        body, out_type=jax.ShapeDtypeStruct((N_CHIPS,) + shard.shape, shard.dtype),
        mesh=_sequencer_mesh(), name=name,
        scratch_types=[pltpu.SemaphoreType.DMA((7,)), pltpu.SemaphoreType.DMA((7,)), pltpu.SemaphoreType.DMA],
        compiler_params=pltpu.CompilerParams(collective_id=collective_id),
    )(shard)


def _grad_exchange(name, gs, collective_id, after=None):
    T = len(gs)

    def body(*refs):
        ins, outs = refs[:T], refs[-T - 2:-2]
        send_sems, recv_sems = refs[-2:]
        x, y, c = _coords()
        chip = 2 * x + y
        sib = (x, y, 1 - c)
        chips = [_flip(x, y, c, k) for k in (4, 2, 6)]
        _handshake([sib] + [(px, py, cc) for px, py, _ in chips for cc in (0, 1)])
        cps = []
        for t in range(T):
            for j, (px, py, _) in enumerate(chips):
                for cc in (0, 1):
                    cps.append(pltpu.make_async_remote_copy(
                        src_ref=ins[t].at[2 * px + py, cc], dst_ref=outs[t].at[1 + 2 * j + c],
                        send_sem=send_sems.at[t, 1 + 2 * j + cc], recv_sem=recv_sems.at[t, 1 + 2 * j + c],
                        device_id=(px, py, cc), device_id_type=MESH))
            cps.append(pltpu.make_async_remote_copy(
                src_ref=ins[t].at[chip, 1 - c], dst_ref=outs[t].at[0], send_sem=send_sems.at[t, 0],
                recv_sem=recv_sems.at[t, 0], device_id=sib, device_id_type=MESH))
        for cp in cps:
            cp.start()
        for cp in cps:
            cp.wait_send()
        for t in range(T):
            for slot in range(N_PEERS):
                pltpu.make_async_remote_copy(
                    src_ref=outs[t].at[slot], dst_ref=outs[t].at[slot], send_sem=send_sems.at[t, slot],
                    recv_sem=recv_sems.at[t, slot], device_id=sib, device_id_type=MESH).wait_recv()

    operands = list(gs) + ([after] if after is not None else [])
    return pl.kernel(
        body, out_type=[jax.ShapeDtypeStruct((N_PEERS,) + g.shape[2:], g.dtype) for g in gs],
        mesh=_sequencer_mesh(), name=name,
        scratch_types=[pltpu.SemaphoreType.DMA((T, N_PEERS)), pltpu.SemaphoreType.DMA((T, N_PEERS))],
        compiler_params=pltpu.CompilerParams(collective_id=collective_id),
    )(*operands)


def _pair_fill(name, fulls):
    T = len(fulls)

    def body(*refs):
        outs = refs[T:2 * T]
        send_sems, recv_sems = refs[2 * T:]
        x, y, c = _coords()
        sib = (x, y, 1 - c)
        cps = []
        for t in range(T):
            send = pltpu.make_async_remote_copy(
                src_ref=outs[t].at[c], dst_ref=outs[t].at[c], send_sem=send_sems.at[t],
                recv_sem=recv_sems.at[t], device_id=sib, device_id_type=MESH)
            recv = pltpu.make_async_remote_copy(
                src_ref=outs[t].at[1 - c], dst_ref=outs[t].at[1 - c], send_sem=send_sems.at[t],
                recv_sem=recv_sems.at[t], device_id=sib, device_id_type=MESH)
            send.start()
            cps.append((send, recv))
        for send, recv in cps:
            send.wait_send()
            recv.wait_recv()

    anyspec = pl.BlockSpec(memory_space=pl.ANY)
    return pl.pallas_call(
        body, name=name,
        out_shape=[jax.ShapeDtypeStruct(f.shape, f.dtype) for f in fulls],
        in_specs=[anyspec] * T, out_specs=[anyspec] * T,
        input_output_aliases={t: t for t in range(T)},
        scratch_shapes=[pltpu.SemaphoreType.DMA((T,)), pltpu.SemaphoreType.DMA((T,))],
    )(*fulls)


def _pack(arrs):
    flat = jnp.concatenate([a.reshape(-1).astype(F32) for a in arrs])
    n = flat.shape[0]
    rows = -(-n // 1024) * 8
    return jnp.pad(flat, (0, rows * 128 - n)).reshape(rows, 128)


def _unpack(buf, shapes):
    flat = buf.reshape(-1)
    out, pos = [], 0
    for s in shapes:
        n = math.prod(s)
        out.append(flat[pos:pos + n].reshape(s))
        pos += n
    return out


def kernel(x, mix_norm_g, ffn_norm_g, final_norm_g, sc_w_in, sc_conv_w, sc_conv_b, sc_w_out, attn_w_qkv, attn_w_out, ffn_w_up, ffn_conv_w, ffn_conv_b, ffn_w_down, loss_target, m_mix_norm_g, m_ffn_norm_g, m_final_norm_g, m_sc_w_in, m_sc_conv_w, m_sc_conv_b, m_sc_w_out, m_attn_w_qkv, m_attn_w_out, m_ffn_w_up, m_ffn_conv_w, m_ffn_conv_b, m_ffn_w_down, v_mix_norm_g, v_ffn_norm_g, v_final_norm_g, v_sc_w_in, v_sc_conv_w, v_sc_conv_b, v_sc_w_out, v_attn_w_qkv, v_attn_w_out, v_ffn_w_up, v_ffn_conv_w, v_ffn_conv_b, v_ffn_w_down):
    S, D = x.shape[1], x.shape[2]
    xi, yi, ci = _coords()
    chip = 2 * xi + yi
    x0 = x.reshape(S, D)
    tgt = loss_target.reshape(S, D)

    conv_shapes = [sc_conv_w.shape, ffn_conv_w.shape]
    allc = _small_exchange("gather_conv_w", _pack([sc_conv_w, ffn_conv_w]), reduce=False)
    per_chip = [_unpack(allc[2 * k], conv_shapes) for k in range(N_CHIPS)]
    scw = jnp.concatenate([p[0] for p in per_chip], axis=-1)[0]
    fcw = jnp.concatenate([p[1] for p in per_chip], axis=-1)
    scb = sc_conv_b

    big_names = ["sc_w_in", "sc_w_out", "attn_w_qkv", "attn_w_out", "ffn_w_up", "ffn_w_down"]
    big = dict(zip(big_names, [sc_w_in, sc_w_out, attn_w_qkv, attn_w_out, ffn_w_up, ffn_w_down]))
    n_gathers = [0]

    def gather(tag, w, layers):
        _, R, C = w.shape
        shard = _cast_bf16("cast_" + tag, w, layers).reshape(2, len(layers) * R // 2, C)
        cid = n_gathers[0]
        n_gathers[0] += 1
        return _allgather_weight("allgather_" + tag, shard, cid).reshape(N_CHIPS, len(layers), R, C)

    w_in = gather("sc_w_in", sc_w_in, (0,))
    w_out = gather("sc_w_out", sc_w_out, (0,))
    w_ups = [gather("ffn_w_up0", ffn_w_up, (0,))]
    w_dn = gather("ffn_w_down", ffn_w_down, (0, 1))
    w_qkv = gather("attn_w_qkv", attn_w_qkv, (0,))
    w_ao = gather("attn_w_out", attn_w_out, (0,))
    w_ups.append(gather("ffn_w_up1", ffn_w_up, (1,)))

    big_grads = {}
    exchanges = []

    def reduce_scatter(tags, grads):
        after = exchanges[-1][0] if exchanges else None
        outs = _grad_exchange("rs_exchange_" + tags[0], grads, n_gathers[0] + len(exchanges), after=after)
        exchanges.append(outs)
        for tag, g, others in zip(tags, grads, outs):
            big_grads[tag] = _grad_sum("rs_sum_" + tag, g, others)

    h0 = _rmsnorm_fwd("norm_mix0", x0, mix_norm_g[0:1])[0]
    z = _mm_nn_col("sc_in", h0, w_in, 0)
    y = _sc_fwd("sc_gate", z, scw, scb)
    x1, h1 = _mm_nn_row("sc_out", y, w_out, 0, x0, norm_g=ffn_norm_g[0:1])
    u0 = _mm_nn_col("ffn_up0", h1, w_ups[0], 0)
    f0, v0 = _ffn_fwd("ffn_gate0", u0, fcw[0], ffn_conv_b[0:1])
    x2 = _mm_nn_row("ffn_down0", f0, w_dn, 0, x1)
    h2s = _rmsnorm_fwd("norm_mix1", x2, mix_norm_g[1:2], dilated=True)
    qkvs = [_mm_nn_col(f"attn_qkv{d}", h, w_qkv, 0, col_off=gi * 3 * D, ncols=3 * D)
            for gi, (h, d) in enumerate(zip(h2s, DILATIONS))]
    og, lg = zip(*[_attn_fwd(f"attn_fwd{d}", q, d) for q, d in zip(qkvs, DILATIONS)])
    o32, ob, lse = _attn_combine("attn_combine", list(og), list(lg))
    x3, h3 = _mm_nn_row("attn_out", ob, w_ao, 0, x2, norm_g=ffn_norm_g[1:2])
    u1 = _mm_nn_col("ffn_up1", h3, w_ups[1], 0)
    f1, v1 = _ffn_fwd("ffn_gate1", u1, fcw[1], ffn_conv_b[1:2])
    x4 = _mm_nn_row("ffn_down1", f1, w_dn, 1, x3)

    dx4, dx4b, dg_final, loss_part = _rmsnorm_bwd("loss_norm_bwd", x4, final_norm_g.reshape(1, D), target=tgt)

    def ffn_backward(layer, xin, h, u, v, f, dxo, dxob, gain, g_up, g_dn, last=False):
        g_dn = _mm_tn_row(f"ffn_down_dw{layer}", f, dxob, prev=g_dn, layer=layer)
        if last:
            reduce_scatter(["ffn_w_down"], [g_dn])
        df = _mm_nt_row(f"ffn_down_dx{layer}", dxob, w_dn, layer, after=g_dn)
        du, dwb = _ffn_bwd(f"ffn_gate_bwd{layer}", u, v, df, fcw[layer])
        g_up = _mm_tn_col(f"ffn_up_dw{layer}", h, du, w_ups[layer].shape[3], prev=g_up, layer=layer)
        if last:
            reduce_scatter(["ffn_w_up"], [g_up])
        dh = _mm_nt_col(f"ffn_up_dx{layer}", du, w_ups[layer], 0, after=g_up)
        dxi, dxib, dg = _rmsnorm_bwd(f"norm_ffn_bwd{layer}", xin, gain, dhs=[(dh, 1)], dres=dxo)
        return dxi, dxib, dg, dwb, g_up, g_dn

    dx3, dx3b, dg_ffn1, dwb_ffn1, g_up, g_dn = ffn_backward(1, x3, h3, u1, v1, f1, dx4, dx4b, ffn_norm_g[1:2], None, None)

    g_ao = _mm_tn_row("attn_out_dw", ob, dx3b)
    reduce_scatter(["attn_w_out"], [g_ao])
    do = _mm_nt_row("attn_out_dx", dx3b, w_ao, 0, after=g_ao)
    dos, lss, dls = _attn_bwd_prep("attn_bwd_prep", do, o32, lse)
    dqkvs = [_attn_bwd(f"attn_bwd{d}", q, a, b, c_, d)
             for q, a, b, c_, d in zip(qkvs, dos, lss, dls, DILATIONS)]
    g_qkv = None
    for gi, (h, dq, d) in enumerate(zip(h2s, dqkvs, DILATIONS)):
        g_qkv = _mm_tn_col(f"attn_qkv_dw{d}", h, dq, w_qkv.shape[3], col_off=gi * 3 * D, prev=g_qkv)
    reduce_scatter(["attn_w_qkv"], [g_qkv])
    dh2s = [(_mm_nt_col(f"attn_qkv_dx{d}", dq, w_qkv, 0, col_off=gi * 3 * D, after=g_qkv), d)
            for gi, (dq, d) in enumerate(zip(dqkvs, DILATIONS))]
    dx2, dx2b, dg_mix1 = _rmsnorm_bwd("norm_mix_bwd1", x2, mix_norm_g[1:2], dhs=dh2s, dres=dx3)

    dx1, dx1b, dg_ffn0, dwb_ffn0, g_up, g_dn = ffn_backward(
        0, x1, h1, u0, v0, f0, dx2, dx2b, ffn_norm_g[0:1], g_up, g_dn, last=True)

    g_out = _mm_tn_row("sc_out_dw", y, dx1b)
    dy = _mm_nt_row("sc_out_dx", dx1b, w_out, 0, after=g_out)
    dz, dwb_sc = _sc_bwd("sc_gate_bwd", z, dy, scw, scb)
    g_in = _mm_tn_col("sc_in_dw", h0, dz, w_in.shape[3])
    reduce_scatter(["sc_w_out", "sc_w_in"], [g_out, g_in])
    dh0 = _mm_nt_col("sc_in_dx", dz, w_in, 0, after=g_in)
    dx0, _, dg_mix0 = _rmsnorm_bwd("norm_mix_bwd0", x0, mix_norm_g[0:1], dhs=[(dh0, 1)], dres=dx1)

    dconv_sc = dwb_sc[0:3].reshape(1, 3, D)
    dbias_sc = dwb_sc[3:4]
    dconv_ffn = jnp.stack([dwb_ffn0[0:3], dwb_ffn1[0:3]])
    dbias_ffn = jnp.concatenate([dwb_ffn0[3:4], dwb_ffn1[3:4]], axis=0)
    small_parts = [jnp.concatenate([dg_mix0, dg_mix1], axis=0), jnp.concatenate([dg_ffn0, dg_ffn1], axis=0),
                   dg_final.reshape(D), dconv_sc, dbias_sc, dconv_ffn, dbias_ffn, loss_part[0, 0:1]]
    small_shapes = [a.shape for a in small_parts]
    summed = _unpack(_small_exchange("allreduce_small", _pack(small_parts), reduce=True), small_shapes)
    g_mix, g_ffn, g_final, g_scw_full, g_scb, g_fcw_full, g_fcb, loss = summed
    loss = loss.reshape(())
    cw = sc_conv_w.shape[2]
    g_scw = lax.dynamic_slice_in_dim(g_scw_full, chip * cw, cw, axis=2)
    fw = ffn_conv_w.shape[2]
    g_fcw = lax.dynamic_slice_in_dim(g_fcw_full, chip * fw, fw, axis=2)

    names = ["mix_norm_g", "ffn_norm_g", "final_norm_g", "sc_w_in", "sc_conv_w", "sc_conv_b", "sc_w_out",
             "attn_w_qkv", "attn_w_out", "ffn_w_up", "ffn_conv_w", "ffn_conv_b", "ffn_w_down"]
    ws = dict(zip(names, [mix_norm_g, ffn_norm_g, final_norm_g, sc_w_in, sc_conv_w, sc_conv_b, sc_w_out,
                          attn_w_qkv, attn_w_out, ffn_w_up, ffn_conv_w, ffn_conv_b, ffn_w_down]))
    ms = dict(zip(names, [m_mix_norm_g, m_ffn_norm_g, m_final_norm_g, m_sc_w_in, m_sc_conv_w, m_sc_conv_b, m_sc_w_out,
                          m_attn_w_qkv, m_attn_w_out, m_ffn_w_up, m_ffn_conv_w, m_ffn_conv_b, m_ffn_w_down]))
    vs = dict(zip(names, [v_mix_norm_g, v_ffn_norm_g, v_final_norm_g, v_sc_w_in, v_sc_conv_w, v_sc_conv_b, v_sc_w_out,
                          v_attn_w_qkv, v_attn_w_out, v_ffn_w_up, v_ffn_conv_w, v_ffn_conv_b, v_ffn_w_down]))
    gs = {"mix_norm_g": g_mix, "ffn_norm_g": g_ffn, "final_norm_g": g_final, "sc_conv_w": g_scw,
          "sc_conv_b": g_scb, "ffn_conv_w": g_fcw, "ffn_conv_b": g_fcb}
    filled = _pair_fill("rs_pair_fill", [big_grads[n] for n in big_names])
    gs.update({n: f.reshape(big[n].shape) for n, f in zip(big_names, filled)})

    deltas, new_m, new_v = {}, {}, {}
    small_names = [n for n in names if n not in big_names]
    packed = [_pack([d[n] for n in small_names]) for d in (ws, gs, ms, vs)]
    outs = _adamw("adamw_small", *packed)
    shapes = [ws[n].shape for n in small_names]
    for res, o in zip((deltas, new_m, new_v), outs):
        res.update(dict(zip(small_names, _unpack(o, shapes))))
    for n in big_names:
        shp = ws[n].shape
        two_d = (shp[0] * shp[1], shp[2])
        outs = _adamw("adamw_" + n, *[d[n].reshape(two_d) for d in (ws, gs, ms, vs)], copy_g=True)
        for res, o in zip((deltas, new_m, new_v, gs), outs):
            res[n] = o.reshape(shp)

    return (loss, dx0.reshape(x.shape), *[gs[n] for n in names], *[deltas[n] for n in names],
            *[new_m[n] for n in names], *[new_v[n] for n in names])
```

```python
import math

import jax
import jax.numpy as jnp
from jax import lax
from jax.experimental import pallas as pl
from jax.experimental.pallas import tpu as pltpu
from jax.experimental.pallas import tpu_sc as plsc

F32 = jnp.float32
BF = jnp.bfloat16
MESH = pl.DeviceIdType.MESH

HEAD_DIM = 128
ATTN_HALF = 64
ATTN_BLOCK = 128
DILATIONS = (1, 4, 16)
STAT_LANES = 128
HALO = 16
NORM_EPS = 1e-5
ALIBI_MAX = 8.0
NEG_INF = -1e30
N_CHIPS = 4
VMEM_LIMIT = 56 * 1024 * 1024

ADAM_LR = 0.001
ADAM_B1 = 0.9
ADAM_B2 = 0.999
ADAM_EPS = 1e-08
ADAM_WD = 0.01
ADAM_STEP = 10


def _pick(n, cands):
    for c in cands:
        if n % c == 0:
            return c
    raise ValueError(f"no tile for {n} in {cands}")


def _row_tile(rows, cols, max_elems=1 << 19):
    for c in (512, 256, 128, 64, 32, 16):
        if rows % c == 0 and c * cols <= max_elems:
            return c
    raise ValueError(f"no row tile for {rows}x{cols}")


def _params(*sem):
    return pltpu.CompilerParams(dimension_semantics=sem, vmem_limit_bytes=VMEM_LIMIT)


def _matmul(name, a, b, out_shape, grid, a_spec, b_spec, o_spec, contract, acc_shape,
            res=None, res_spec=None, prev=None, b2_spec=None, after=None, norm_g=None, pre_g=None):
    nk = grid[2]
    n_b = 1 if b2_spec is None else 2

    def body(*refs):
        refs = list(refs)
        if prev is not None:
            refs.pop(0)
        a_ref, b_ref = refs[0], refs[1]
        res_ref = refs[1 + n_b] if res is not None else None
        acc_ref = refs[-1]
        av = None
        if pre_g is not None:
            g_ref, o_ref, hin_ref, hs_ref = refs[-5], refs[-4], refs[-3], refs[-2]
            h_ref = None

            @pl.when(pl.program_id(1) == 0)
            def _():
                xv = a_ref[...]
                r = lax.rsqrt(jnp.mean(xv * xv, axis=1, keepdims=True) + NORM_EPS)
                hv = (xv * r * g_ref[...]).astype(BF)
                hs_ref[...] = hv
                hin_ref[...] = hv

            av = hs_ref[...]
        elif norm_g is None:
            o_ref, h_ref, g_ref = refs[-2], None, None
        else:
            g_ref, o_ref, h_ref = refs[-4], refs[-3], refs[-2]
        bv = b_ref[...]
        if bv.ndim == 3:
            bv = bv.reshape(bv.shape[0] * bv.shape[1], bv.shape[2])
        if b2_spec is None:
            part = lax.dot_general(a_ref[...] if av is None else av, bv, contract, preferred_element_type=F32)
        else:
            half = a_ref.shape[1] // 2
            part = (lax.dot_general(a_ref[:, :half], bv, contract, preferred_element_type=F32)
                    + lax.dot_general(a_ref[:, half:], refs[2][...], contract, preferred_element_type=F32))

        def finish(total):
            if res_ref is not None:
                total = total + res_ref[...]
            o_ref[...] = total.reshape(o_ref.shape).astype(o_ref.dtype)
            if h_ref is not None:
                r = lax.rsqrt(jnp.mean(total * total, axis=1, keepdims=True) + NORM_EPS)
                h_ref[...] = (total * r * g_ref[...]).astype(h_ref.dtype)

        if nk == 1:
            finish(part)
        else:
            k = pl.program_id(2)

            @pl.when(k == 0)
            def _():
                acc_ref[...] = part

            @pl.when(jnp.logical_and(k > 0, k < nk - 1))
            def _():
                acc_ref[...] += part

            @pl.when(k == nk - 1)
            def _():
                finish(acc_ref[...] + part)

    operands, in_specs, aliases = [], [], {}
    if prev is not None:
        operands.append(prev)
        in_specs.append(pl.BlockSpec(memory_space=pl.ANY))
        aliases = {0: 0}
    operands += [a, b]
    in_specs += [a_spec, b_spec]
    if b2_spec is not None:
        operands.append(b)
        in_specs.append(b2_spec)
    if res is not None:
        operands.append(res)
        in_specs.append(res_spec)
    if after is not None:
        operands.append(after)
        in_specs.append(pl.BlockSpec(memory_space=pl.ANY))
    if norm_g is not None:
        operands.append(norm_g)
        in_specs.append(pl.BlockSpec(norm_g.shape, lambda i, j, k: (0, 0)))
        out_shape = [out_shape, jax.ShapeDtypeStruct(out_shape.shape, BF)]
        o_spec = [o_spec, o_spec]
    scratch = [pltpu.VMEM(acc_shape if nk > 1 else (8, 128), F32)]
    semantics = ("parallel", "parallel", "arbitrary")
    if pre_g is not None:
        assert nk == 1 and norm_g is None and b2_spec is None
        operands.append(pre_g)
        in_specs.append(pl.BlockSpec(pre_g.shape, lambda i, j, k: (0, 0)))
        out_shape = [out_shape, jax.ShapeDtypeStruct(a.shape, BF)]
        o_spec = [o_spec, pl.BlockSpec(a_spec.block_shape, lambda i, j, k: (i, 0))]
        scratch = [pltpu.VMEM(a_spec.block_shape, BF)] + scratch
        semantics = ("parallel", "arbitrary", "arbitrary")
    return pl.pallas_call(
        body, name=name, out_shape=out_shape, grid=grid, in_specs=in_specs, out_specs=o_spec,
        scratch_shapes=scratch, input_output_aliases=aliases, compiler_params=_params(*semantics),
    )(*operands)


NN = (((1,), (0,)), ((), ()))
NT = (((1,), (1,)), ((), ()))
TN = (((0,), (0,)), ((), ()))

_COL_TILES = (1536, 1408, 1024, 768, 512, 384, 256, 128)
_WIDE_TILES = (2816,) + _COL_TILES


def _mm_nn_col(name, a, w, layer, col_off=0, ncols=None, out_dtype=BF, pre_g=None):
    M, K = a.shape
    _, _, R, C = w.shape
    assert R == K
    ncols = N_CHIPS * C if ncols is None else ncols
    tn = _pick(math.gcd(C, math.gcd(ncols, col_off) if col_off else ncols), _WIDE_TILES)
    tm = _pick(M, (1024, 512, 256))
    nb, off = C // tn, col_off // tn
    return _matmul(
        name, a, w, jax.ShapeDtypeStruct((M, ncols), out_dtype), (M // tm, ncols // tn, 1),
        pl.BlockSpec((tm, K), lambda i, j, k: (i, 0)),
        pl.BlockSpec((None, None, K, tn), lambda i, j, k: ((j + off) // nb, layer, 0, (j + off) % nb)),
        pl.BlockSpec((tm, tn), lambda i, j, k: (i, j)), NN, (tm, tn), pre_g=pre_g)


def _mm_nn_row(name, a, w, layer, res, norm_g=None):
    M, K = a.shape
    _, _, R, C = w.shape
    assert N_CHIPS * R == K
    chips_per_step = N_CHIPS if K <= 2048 else 2
    tk = chips_per_step * R
    tm = _pick(M, (1024, 512, 256) if norm_g is None else (512, 256))
    tn = _pick(C, (1024, 512, 256)) if norm_g is None else C
    assert norm_g is None or K <= 2048
    return _matmul(
        name, a, w, jax.ShapeDtypeStruct((M, C), F32), (M // tm, C // tn, K // tk),
        pl.BlockSpec((tm, tk), lambda i, j, k: (i, k)),
        pl.BlockSpec((chips_per_step, None, R, tn), lambda i, j, k: (k, layer, 0, j)),
        pl.BlockSpec((tm, tn), lambda i, j, k: (i, j)), NN, (tm, tn),
        res=res, res_spec=pl.BlockSpec((tm, tn), lambda i, j, k: (i, j)), norm_g=norm_g)


def _mm_nt_col(name, dy, w, layer, col_off=0, out_dtype=F32, after=None):
    M, n = dy.shape
    _, _, R, C = w.shape
    tk = _pick(math.gcd(C, math.gcd(n, col_off) if col_off else n), _WIDE_TILES)
    tm = _pick(M, (1024, 512, 256))
    tn = _pick(R, (1024, 512, 256))
    nb, off = C // tk, col_off // tk
    per_step = 2 if (tk <= 1536 and (n // tk) % 2 == 0) else 1

    def w_block(t):
        return pl.BlockSpec((None, None, tn, tk), lambda i, j, k: (
            (per_step * k + t + off) // nb, layer, j, (per_step * k + t + off) % nb))

    return _matmul(
        name, dy, w, jax.ShapeDtypeStruct((M, R), out_dtype), (M // tm, R // tn, n // (per_step * tk)),
        pl.BlockSpec((tm, per_step * tk), lambda i, j, k: (i, k)), w_block(0),
        pl.BlockSpec((tm, tn), lambda i, j, k: (i, j)), NT, (tm, tn),
        b2_spec=w_block(1) if per_step == 2 else None, after=after)


def _mm_nt_row(name, dy, w, layer, out_dtype=BF, after=None):
    M, C2 = dy.shape
    _, _, R, C = w.shape
    assert C2 == C
    chips_per_tile = N_CHIPS if N_CHIPS * R <= 2048 else 2
    tn = chips_per_tile * R
    tm = _pick(M, (1024, 512, 256))
    return _matmul(
        name, dy, w, jax.ShapeDtypeStruct((M, N_CHIPS * R), out_dtype), (M // tm, N_CHIPS * R // tn, 1),
        pl.BlockSpec((tm, C), lambda i, j, k: (i, 0)),
        pl.BlockSpec((chips_per_tile, None, R, C), lambda i, j, k: (j, layer, 0, 0)),
        pl.BlockSpec((tm, tn), lambda i, j, k: (i, j)), NT, (tm, tn), after=after)


_TN_DEPTH = (2048, 1024, 512, 256)


def _half_index(rows, layer, tkx):
    if layer is None:
        hb = rows // 2 // tkx
        return rows // 2, lambda i: (i // hb, i % hb)
    return rows, lambda i: (layer, i)


def _mm_tn_col(name, xa, dy, C, col_off=0, prev=None, layer=None):
    M, K = xa.shape
    _, n = dy.shape
    tn = _pick(math.gcd(C, math.gcd(n, col_off) if col_off else n), _WIDE_TILES)
    tkx = _pick(K // 2 if layer is None else K, (1024, 512, 256, 128) if tn <= 1536 else (512, 256, 128))
    tmr = _pick(M, _TN_DEPTH)
    rh, split = _half_index(K, layer, tkx)
    nb, off = C // tn, col_off // tn
    return _matmul(
        name, xa, dy, jax.ShapeDtypeStruct((N_CHIPS, 2, rh, C), BF), (K // tkx, n // tn, M // tmr),
        pl.BlockSpec((tmr, tkx), lambda i, j, k: (k, i)),
        pl.BlockSpec((tmr, tn), lambda i, j, k: (k, j)),
        pl.BlockSpec((None, None, tkx, tn), lambda i, j, k: ((j + off) // nb, *split(i), (j + off) % nb)),
        TN, (tkx, tn), prev=prev)


def _mm_tn_row(name, xa, dy, prev=None, layer=None):
    M, K = xa.shape
    _, C = dy.shape
    R = K // N_CHIPS
    tmr = _pick(M, _TN_DEPTH)
    if layer is None and prev is None and K <= 2048:
        tn = _pick(C, (1024, 512, 256))
        return _matmul(
            name, xa, dy, jax.ShapeDtypeStruct((N_CHIPS, 2, R // 2, C), BF), (1, C // tn, M // tmr),
            pl.BlockSpec((tmr, K), lambda i, j, k: (k, 0)),
            pl.BlockSpec((tmr, tn), lambda i, j, k: (k, j)),
            pl.BlockSpec((N_CHIPS, 2, R // 2, tn), lambda i, j, k: (0, 0, 0, j)),
            TN, (K, tn))
    tkx = _pick(R // 2 if layer is None else R, (1408, 1024, 512, 256, 128))
    tn = _pick(C, (2048, 1024, 512, 256) if tkx <= 512 else (1024, 512, 256))
    rh, split = _half_index(R, layer, tkx)
    rb = R // tkx
    return _matmul(
        name, xa, dy, jax.ShapeDtypeStruct((N_CHIPS, 2, rh, C), BF), (K // tkx, C // tn, M // tmr),
        pl.BlockSpec((tmr, tkx), lambda i, j, k: (k, i)),
        pl.BlockSpec((tmr, tn), lambda i, j, k: (k, j)),
        pl.BlockSpec((None, None, tkx, tn), lambda i, j, k: (i // rb, *split(i % rb), j)),
        TN, (tkx, tn), prev=prev)


NORM_ROWS = 256
LANES = 128


def _chunk_scratch(tm, width):
    return pltpu.VMEM((width // LANES, tm, LANES), F32)


def _store_chunks(scr, value):
    for c in range(scr.shape[0]):
        scr[c] = value[:, c * LANES:(c + 1) * LANES]


def _load_chunks(scr):
    return jnp.concatenate([scr[c] for c in range(scr.shape[0])], axis=1)


def _to_residue_major(scr, o_ref, d, dtype):
    tm = scr.shape[1]
    for c in range(scr.shape[0]):
        for res in range(d):
            o_ref[res, :, c * LANES:(c + 1) * LANES] = scr[c, pl.ds(res, tm // d, stride=d), :].astype(dtype)


def _to_natural(scr, ref, d):
    tm = scr.shape[1]
    for c in range(scr.shape[0]):
        for res in range(d):
            scr[c, pl.ds(res, tm // d, stride=d), :] = ref[res, :, c * LANES:(c + 1) * LANES].astype(F32)


def _rmsnorm_fwd(name, x, g, dilated=False):
    S, D = x.shape
    tm = NORM_ROWS
    dils = DILATIONS[1:] if dilated else ()

    def body(x_ref, g_ref, h_ref, *rest):
        xv = x_ref[...]
        r = lax.rsqrt(jnp.mean(xv * xv, axis=1, keepdims=True) + NORM_EPS)
        h = xv * r * g_ref[...]
        h_ref[...] = h.astype(BF)
        if dils:
            scr = rest[-1]
            _store_chunks(scr, h)
            for o_ref, d in zip(rest[:-1], dils):
                _to_residue_major(scr, o_ref, d, BF)

    out_shape = [jax.ShapeDtypeStruct((S, D), BF)]
    out_specs = [pl.BlockSpec((tm, D), lambda i: (i, 0))]
    for d in dils:
        out_shape.append(jax.ShapeDtypeStruct((d, S // d, D), BF))
        out_specs.append(pl.BlockSpec((d, tm // d, D), lambda i: (0, i, 0)))
    outs = pl.pallas_call(
        body, name=name, out_shape=out_shape, grid=(S // tm,),
        in_specs=[pl.BlockSpec((tm, D), lambda i: (i, 0)), pl.BlockSpec((1, D), lambda i: (0, 0))],
        out_specs=out_specs,
        scratch_shapes=[_chunk_scratch(tm, D)] if dils else [],
        compiler_params=_params("parallel"),
    )(x, g)
    return [outs[0]] + [o.reshape(S, D) for o in outs[1:]]


def _rmsnorm_bwd(name, x, g, dhs=(), dres=None, target=None):
    S, D = x.shape
    tm = NORM_ROWS
    n_dh = len(dhs)

    def body(*refs):
        refs = list(refs)
        x_ref, g_ref = refs[0], refs[1]
        dh_refs = refs[2:2 + n_dh]
        pos = 2 + n_dh
        dres_ref = tgt_ref = None
        if dres is not None:
            dres_ref = refs[pos]
            pos += 1
        if target is not None:
            tgt_ref = refs[pos]
            pos += 1
        dx_ref, dxb_ref, dg_ref = refs[pos:pos + 3]
        pos += 3
        loss_ref = None
        if target is not None:
            loss_ref = refs[pos]
            pos += 1
        scr = refs[pos] if any(d > 1 for _, d in dhs) else None
        i = pl.program_id(0)

        xv = x_ref[...]
        gv = g_ref[...]
        r = lax.rsqrt(jnp.mean(xv * xv, axis=1, keepdims=True) + NORM_EPS)
        xhat = xv * r
        if target is not None:
            err = xhat * gv - tgt_ref[...]
            dh = err * (1.0 / D)
            part = jnp.sum(jnp.sum(err * err, axis=1, keepdims=True), axis=0, keepdims=True) * (0.5 / D)
        else:
            dh = None
            for ref, d in zip(dh_refs, [d for _, d in dhs]):
                if d == 1:
                    v = ref[...]
                else:
                    _to_natural(scr, ref, d)
                    v = _load_chunks(scr)
                dh = v if dh is None else dh + v
        dxhat = dh * gv
        dx = r * (dxhat - xhat * jnp.mean(dxhat * xhat, axis=1, keepdims=True))
        if dres_ref is not None:
            dx = dx + dres_ref[...]
        dx_ref[...] = dx
        dxb_ref[...] = dx.astype(BF)
        dg = jnp.sum(dh * xhat, axis=0, keepdims=True)

        @pl.when(i == 0)
        def _():
            dg_ref[...] = dg
            if loss_ref is not None:
                loss_ref[...] = jnp.broadcast_to(part, loss_ref.shape)

        @pl.when(i > 0)
        def _():
            dg_ref[...] += dg
            if loss_ref is not None:
                loss_ref[...] += jnp.broadcast_to(part, loss_ref.shape)

    row = pl.BlockSpec((tm, D), lambda i: (i, 0))
    operands = [x, g]
    in_specs = [row, pl.BlockSpec((1, D), lambda i: (0, 0))]
    for arr, d in dhs:
        if d == 1:
            operands.append(arr)
            in_specs.append(row)
        else:
            operands.append(arr.reshape(d, S // d, D))
            in_specs.append(pl.BlockSpec((d, tm // d, D), lambda i: (0, i, 0)))
    if dres is not None:
        operands.append(dres)
        in_specs.append(row)
    if target is not None:
        operands.append(target)
        in_specs.append(row)
    out_shape = [jax.ShapeDtypeStruct((S, D), F32), jax.ShapeDtypeStruct((S, D), BF),
                 jax.ShapeDtypeStruct((1, D), F32)]
    out_specs = [row, row, pl.BlockSpec((1, D), lambda i: (0, 0))]
    if target is not None:
        out_shape.append(jax.ShapeDtypeStruct((1, STAT_LANES), F32))
        out_specs.append(pl.BlockSpec((1, STAT_LANES), lambda i: (0, 0)))
    scratch = [_chunk_scratch(tm, D)] if any(d > 1 for _, d in dhs) else []
    return pl.pallas_call(
        body, name=name, out_shape=out_shape, grid=(S // tm,), in_specs=in_specs, out_specs=out_specs,
        scratch_shapes=scratch, compiler_params=_params("arbitrary"),
    )(*operands)


CONV_ROWS = 256
CONV_COLS = 256


def _halo_specs(S, tm, width):
    nh = S // HALO
    per = tm // HALO
    cur = pl.BlockSpec((tm, width), lambda i: (i, 0))
    prev = pl.BlockSpec((HALO, width), lambda i: (jnp.maximum(i * per - 1, 0), 0))
    nxt = pl.BlockSpec((HALO, width), lambda i: (jnp.minimum((i + 1) * per, nh - 1), 0))
    return [cur, prev, nxt]


def _ext(refs, cs, inr):
    cur, prev, nxt = refs
    v = jnp.concatenate([prev[:, cs], cur[:, cs], nxt[:, cs]], axis=0).astype(F32)
    return v if inr is None else jnp.where(inr, v, 0.0)


def _shift_prev(v):
    return pltpu.roll(v, 1, 0)


def _shift_next(v):
    return pltpu.roll(v, v.shape[0] - 1, 0)


def _shifts(v):
    return _shift_prev(v), _shift_next(v)


def _conv3(v, w, cs, b=None, shifted=None):
    vp, vn = _shifts(v) if shifted is None else shifted
    out = w[0:1, cs] * vp + w[1:2, cs] * v + w[2:3, cs] * vn
    return out if b is None else out + b[:, cs]


def _in_range(i, tm, tc, S):
    row = lax.broadcasted_iota(jnp.int32, (tm + 2 * HALO, tc), 0) + (i * tm - HALO)
    return jnp.logical_and(row >= 0, row < S)


def _core(v, tm):
    return v[HALO:HALO + tm, :]


def _acc_rows(ref, rows):
    for r, cs, val in rows:
        ref[r:r + 1, cs] += val


def _zero_first(ref, i):
    @pl.when(i == 0)
    def _():
        ref[...] = jnp.zeros(ref.shape, ref.dtype)


def _sc_fwd(name, z, w, b):
    S, D3 = z.shape
    D = D3 // 3
    tm, tc = CONV_ROWS, _pick(D, (CONV_COLS, 256, 128))

    def body(zc, zp, zn, w_ref, b_ref, y_ref):
        i = pl.program_id(0)
        inr = _in_range(i, tm, tc, S)
        zr = (zc, zp, zn)
        for c in range(D // tc):
            cs = slice(c * tc, (c + 1) * tc)
            u = _ext(zr, cs, inr)
            gc = _ext(zr, slice(2 * D + c * tc, 2 * D + (c + 1) * tc), None)
            conv = _conv3(gc * u, w_ref, cs, b_ref)
            gb = zc[:, D + c * tc:D + (c + 1) * tc].astype(F32)
            y_ref[:, cs] = (gb * _core(conv, tm)).astype(BF)

    return pl.pallas_call(
        body, name=name, out_shape=jax.ShapeDtypeStruct((S, D), BF), grid=(S // tm,),
        in_specs=_halo_specs(S, tm, D3) + [pl.BlockSpec((3, D), lambda i: (0, 0)),
                                           pl.BlockSpec((1, D), lambda i: (0, 0))],
        out_specs=pl.BlockSpec((tm, D), lambda i: (i, 0)),
        compiler_params=_params("parallel"),
    )(z, z, z, w, b)


def _sc_bwd(name, z, dy, w, b):
    S, D3 = z.shape
    D = D3 // 3
    tm, tc = CONV_ROWS, _pick(D, (CONV_COLS, 256, 128))

    def body(zc, zp, zn, dc_, dp_, dn_, w_ref, b_ref, dz_ref, dwb_ref):
        i = pl.program_id(0)
        inr = _in_range(i, tm, tc, S)
        _zero_first(dwb_ref, i)
        zr, dr = (zc, zp, zn), (dc_, dp_, dn_)
        for c in range(D // tc):
            cs = slice(c * tc, (c + 1) * tc)
            u = _ext(zr, cs, inr)
            gb = _ext(zr, slice(D + c * tc, D + (c + 1) * tc), None)
            gc = _ext(zr, slice(2 * D + c * tc, 2 * D + (c + 1) * tc), None)
            dyv = _ext(dr, cs, inr)
            p = gc * u
            p_prev, p_next = _shifts(p)
            conv = _conv3(p, w_ref, cs, b_ref, shifted=(p_prev, p_next))
            dconv = dyv * gb
            dp = w_ref[0:1, cs] * _shift_next(dconv) + w_ref[1:2, cs] * dconv + w_ref[2:3, cs] * _shift_prev(dconv)
            dz_ref[:, cs] = _core(dp * gc, tm).astype(BF)
            dz_ref[:, D + c * tc:D + (c + 1) * tc] = _core(dyv * conv, tm).astype(BF)
            dz_ref[:, 2 * D + c * tc:2 * D + (c + 1) * tc] = _core(dp * u, tm).astype(BF)
            dcc = _core(dconv, tm)
            _acc_rows(dwb_ref, [
                (0, cs, jnp.sum(dcc * _core(p_prev, tm), axis=0, keepdims=True)),
                (1, cs, jnp.sum(dcc * _core(p, tm), axis=0, keepdims=True)),
                (2, cs, jnp.sum(dcc * _core(p_next, tm), axis=0, keepdims=True)),
                (3, cs, jnp.sum(dcc, axis=0, keepdims=True))])

    return pl.pallas_call(
        body, name=name,
        out_shape=[jax.ShapeDtypeStruct((S, D3), BF), jax.ShapeDtypeStruct((4, D), F32)], grid=(S // tm,),
        in_specs=_halo_specs(S, tm, D3) + _halo_specs(S, tm, D) + [
            pl.BlockSpec((3, D), lambda i: (0, 0)), pl.BlockSpec((1, D), lambda i: (0, 0))],
        out_specs=[pl.BlockSpec((tm, D3), lambda i: (i, 0)), pl.BlockSpec((4, D), lambda i: (0, 0))],
        compiler_params=_params("arbitrary"),
    )(z, z, z, dy, dy, dy, w, b)


def _sigmoid(v):
    return 1.0 / (1.0 + jnp.exp(-v))


def _ffn_fwd(name, u, w, b):
    S, F2 = u.shape
    Fh = F2 // 2
    tm, tc = CONV_ROWS, _pick(Fh, (CONV_COLS, 256, 128))

    def body(uc, up, un, w_ref, b_ref, f_ref, v_ref):
        i = pl.program_id(0)
        inr = _in_range(i, tm, tc, S)
        ur = (uc, up, un)
        for c in range(Fh // tc):
            ca = slice(c * tc, (c + 1) * tc)
            cb = slice(Fh + c * tc, Fh + (c + 1) * tc)
            va = _core(_conv3(_ext(ur, ca, inr), w_ref, ca, b_ref), tm)
            vb = _core(_conv3(_ext(ur, cb, inr), w_ref, cb, b_ref), tm)
            v_ref[:, ca] = va.astype(BF)
            v_ref[:, cb] = vb.astype(BF)
            f_ref[:, ca] = (va * _sigmoid(va) * vb).astype(BF)

    return pl.pallas_call(
        body, name=name,
        out_shape=[jax.ShapeDtypeStruct((S, Fh), BF), jax.ShapeDtypeStruct((S, F2), BF)], grid=(S // tm,),
        in_specs=_halo_specs(S, tm, F2) + [pl.BlockSpec((3, F2), lambda i: (0, 0)),
                                           pl.BlockSpec((1, F2), lambda i: (0, 0))],
        out_specs=[pl.BlockSpec((tm, Fh), lambda i: (i, 0)), pl.BlockSpec((tm, F2), lambda i: (i, 0))],
        compiler_params=_params("parallel"),
    )(u, u, u, w, b)


def _ffn_bwd(name, u, v, df, w):
    S, F2 = u.shape
    Fh = F2 // 2
    tm, tc = CONV_ROWS, _pick(Fh, (CONV_COLS, 256, 128))

    def body(u_ref, vc, vp, vn, dc_, dp_, dn_, w_ref, du_ref, dwb_ref):
        i = pl.program_id(0)
        inr = _in_range(i, tm, tc, S)
        _zero_first(dwb_ref, i)
        vr, dr = (vc, vp, vn), (dc_, dp_, dn_)
        for c in range(Fh // tc):
            ca = slice(c * tc, (c + 1) * tc)
            cb = slice(Fh + c * tc, Fh + (c + 1) * tc)
            va, vb = _ext(vr, ca, None), _ext(vr, cb, None)
            dfv = _ext(dr, ca, inr)
            sg = _sigmoid(va)
            dva = dfv * vb * (sg * (1.0 + va * (1.0 - sg)))
            dvb = dfv * (va * sg)
            rows = []
            for cs, dv in ((ca, dva), (cb, dvb)):
                dv_prev, dv_next = _shifts(dv)
                dcore = w_ref[0:1, cs] * dv_next + w_ref[1:2, cs] * dv + w_ref[2:3, cs] * dv_prev
                du_ref[:, cs] = _core(dcore, tm).astype(BF)
                uu = u_ref[:, cs].astype(F32)
                dvc = _core(dv, tm)
                rows += [
                    (0, cs, jnp.sum(_core(dv_next, tm) * uu, axis=0, keepdims=True)),
                    (1, cs, jnp.sum(dvc * uu, axis=0, keepdims=True)),
                    (2, cs, jnp.sum(_core(dv_prev, tm) * uu, axis=0, keepdims=True)),
                    (3, cs, jnp.sum(dvc, axis=0, keepdims=True))]
            _acc_rows(dwb_ref, rows)

    return pl.pallas_call(
        body, name=name,
        out_shape=[jax.ShapeDtypeStruct((S, F2), BF), jax.ShapeDtypeStruct((4, F2), F32)], grid=(S // tm,),
        in_specs=[pl.BlockSpec((tm, F2), lambda i: (i, 0))] + _halo_specs(S, tm, F2) + _halo_specs(S, tm, Fh) + [
            pl.BlockSpec((3, F2), lambda i: (0, 0))],
        out_specs=[pl.BlockSpec((tm, F2), lambda i: (i, 0)), pl.BlockSpec((4, F2), lambda i: (0, 0))],
        compiler_params=_params("arbitrary"),
    )(u, v, v, v, df, df, df, w)


def _alibi_slopes(H):
    return [2.0 ** (-ALIBI_MAX * (h + 1) / H) for h in range(H)]


def _window_specs(S, width, col):
    n64 = S // ATTN_HALF
    cur = pl.BlockSpec((ATTN_BLOCK, width), lambda b: (b, col))
    prev = pl.BlockSpec((ATTN_HALF, width), lambda b: (jnp.maximum(2 * b - 1, 0), col))
    nxt = pl.BlockSpec((ATTN_HALF, width), lambda b: (jnp.minimum(2 * b + 2, n64 - 1), col))
    return [cur, prev, nxt]


def _fill_window(buf, cur, prev, nxt):
    buf[0:ATTN_HALF] = prev[...]
    buf[ATTN_HALF:ATTN_HALF + ATTN_BLOCK] = cur[...]
    buf[ATTN_HALF + ATTN_BLOCK:2 * ATTN_BLOCK] = nxt[...]


def _band(b, L):
    QB, W = ATTN_BLOCK, 2 * ATTN_BLOCK
    a_loc = (b * QB) % L
    row = lax.broadcasted_iota(jnp.int32, (QB, W), 0)
    col = lax.broadcasted_iota(jnp.int32, (QB, W), 1)
    dist = jnp.abs(col - ATTN_HALF - row)
    other = a_loc - ATTN_HALF + col
    valid = jnp.logical_and(dist <= ATTN_HALF, jnp.logical_and(other >= 0, other < L))
    return dist.astype(F32), valid


def _lane_col(stats, h):
    lane = lax.broadcasted_iota(jnp.int32, stats.shape, 1)
    return jnp.sum(jnp.where(lane == h, stats, 0.0), axis=1, keepdims=True)


def _attn_fwd(name, qkv, d):
    S, D3 = qkv.shape
    D = D3 // 3
    H = D // HEAD_DIM
    L = S // d
    scale = HEAD_DIM ** -0.5
    slopes = _alibi_slopes(H)

    def body(q_ref, kc, kp, kn, vc, vp, vn, o_ref, lse_ref, kbuf, vbuf):
        b = pl.program_id(0)
        _fill_window(kbuf, kc, kp, kn)
        _fill_window(vbuf, vc, vp, vn)
        dist, valid = _band(b, L)
        dist = dist * float(d)
        lane = lax.broadcasted_iota(jnp.int32, (ATTN_BLOCK, STAT_LANES), 1)
        lse = jnp.zeros((ATTN_BLOCK, STAT_LANES), F32)
        for h in range(H):
            cs = slice(h * HEAD_DIM, (h + 1) * HEAD_DIM)
            s = lax.dot_general(q_ref[:, cs], kbuf[:, cs], NT, preferred_element_type=F32) * scale
            s = jnp.where(valid, s - slopes[h] * dist, NEG_INF)
            m = jnp.max(s, axis=1, keepdims=True)
            p = jnp.exp(s - m)
            den = jnp.sum(p, axis=1, keepdims=True)
            o = jnp.dot(p.astype(BF), vbuf[:, cs], preferred_element_type=F32)
            o_ref[:, cs] = (o / den).astype(BF)
            lse = jnp.where(lane == h, m + jnp.log(den), lse)
        lse_ref[...] = lse

    return pl.pallas_call(
        body, name=name,
        out_shape=[jax.ShapeDtypeStruct((S, D), BF), jax.ShapeDtypeStruct((S, STAT_LANES), F32)],
        grid=(S // ATTN_BLOCK,),
        in_specs=[pl.BlockSpec((ATTN_BLOCK, D), lambda b: (b, 0))] + _window_specs(S, D, 1) + _window_specs(S, D, 2),
        out_specs=[pl.BlockSpec((ATTN_BLOCK, D), lambda b: (b, 0)),
                   pl.BlockSpec((ATTN_BLOCK, STAT_LANES), lambda b: (b, 0))],
        scratch_shapes=[pltpu.VMEM((2 * ATTN_BLOCK, D), BF), pltpu.VMEM((2 * ATTN_BLOCK, D), BF)],
        compiler_params=_params("parallel"),
    )(qkv, qkv, qkv, qkv, qkv, qkv, qkv)


def _dil_specs(S, tm, width):
    specs = [pl.BlockSpec((tm, width), lambda i: (i, 0))]
    for d in DILATIONS[1:]:
        specs.append(pl.BlockSpec((d, tm // d, width), lambda i: (0, i, 0)))
    return specs


def _attn_combine(name, outs, lses):
    S, D = outs[0].shape
    H = D // HEAD_DIM
    tm = NORM_ROWS

    def body(o1, o4, o16, l1, l4, l16, o_ref, ob_ref, lse_ref, oscr, lscr):
        ls = [l1[...]]
        for ref, d in zip((l4, l16), DILATIONS[1:]):
            _to_natural(lscr, ref, d)
            ls.append(lscr[0])
        top = jnp.maximum(jnp.maximum(ls[0], ls[1]), ls[2])
        es = [jnp.exp(l - top) for l in ls]
        tot = es[0] + es[1] + es[2]
        lse_ref[...] = top + jnp.log(tot)
        ws = [e / tot for e in es]
        for gi, (ref, d) in enumerate(zip((o1, o4, o16), DILATIONS)):
            if d > 1:
                _to_natural(oscr, ref, d)
            for h in range(H):
                cs = slice(h * HEAD_DIM, (h + 1) * HEAD_DIM)
                term = _lane_col(ws[gi], h) * (ref[:, cs] if d == 1 else oscr[h])
                if gi == 0:
                    o_ref[:, cs] = term
                else:
                    o_ref[:, cs] += term
        ob_ref[...] = o_ref[...].astype(BF)

    outs3 = [outs[0]] + [o.reshape(d, S // d, D) for o, d in zip(outs[1:], DILATIONS[1:])]
    lses3 = [lses[0]] + [l.reshape(d, S // d, STAT_LANES) for l, d in zip(lses[1:], DILATIONS[1:])]
    row = pl.BlockSpec((tm, D), lambda i: (i, 0))
    return pl.pallas_call(
        body, name=name,
        out_shape=[jax.ShapeDtypeStruct((S, D), F32), jax.ShapeDtypeStruct((S, D), BF),
                   jax.ShapeDtypeStruct((S, STAT_LANES), F32)],
        grid=(S // tm,),
        in_specs=_dil_specs(S, tm, D) + _dil_specs(S, tm, STAT_LANES),
        out_specs=[row, row, pl.BlockSpec((tm, STAT_LANES), lambda i: (i, 0))],
        scratch_shapes=[_chunk_scratch(tm, D), _chunk_scratch(tm, STAT_LANES)],
        compiler_params=_params("parallel"),
    )(*outs3, *lses3)


def _attn_bwd_prep(name, do, o32, lse):
    S, D = do.shape
    H = D // HEAD_DIM
    tm = NORM_ROWS
    dils = DILATIONS[1:]

    def body(do_ref, o_ref, lse_ref, dl_ref, do4, do16, l4, l16, d4, d16, scr, sscr):
        lane = lax.broadcasted_iota(jnp.int32, (tm, STAT_LANES), 1)
        delta = jnp.zeros((tm, STAT_LANES), F32)
        for h in range(H):
            cs = slice(h * HEAD_DIM, (h + 1) * HEAD_DIM)
            dov = do_ref[:, cs].astype(F32)
            scr[h] = dov
            delta = jnp.where(lane == h, jnp.sum(dov * o_ref[:, cs], axis=1, keepdims=True), delta)
        dl_ref[...] = delta
        for ref, d in zip((do4, do16), dils):
            _to_residue_major(scr, ref, d, BF)
        for val, refs in ((lse_ref[...], (l4, l16)), (delta, (d4, d16))):
            sscr[0] = val
            for ref, d in zip(refs, dils):
                _to_residue_major(sscr, ref, d, F32)

    def perm_shapes(width, dt):
        return [jax.ShapeDtypeStruct((d, S // d, width), dt) for d in dils]

    def perm_specs(width):
        return [pl.BlockSpec((d, tm // d, width), lambda i: (0, i, 0)) for d in dils]

    row = lambda w: pl.BlockSpec((tm, w), lambda i: (i, 0))
    outs = pl.pallas_call(
        body, name=name,
        out_shape=[jax.ShapeDtypeStruct((S, STAT_LANES), F32)]
        + perm_shapes(D, BF) + perm_shapes(STAT_LANES, F32) + perm_shapes(STAT_LANES, F32),
        grid=(S // tm,),
        in_specs=[row(D), row(D), row(STAT_LANES)],
        out_specs=[row(STAT_LANES)] + perm_specs(D) + perm_specs(STAT_LANES) + perm_specs(STAT_LANES),
        scratch_shapes=[_chunk_scratch(tm, D), _chunk_scratch(tm, STAT_LANES)],
        compiler_params=_params("parallel"),
    )(do, o32, lse)
    dos = [do] + [a.reshape(S, D) for a in outs[1:3]]
    lss = [lse] + [a.reshape(S, STAT_LANES) for a in outs[3:5]]
    dls = [outs[0]] + [a.reshape(S, STAT_LANES) for a in outs[5:7]]
    return dos, lss, dls


def _attn_bwd(name, qkv, do, lse, delta, d):
    S, D3 = qkv.shape
    D = D3 // 3
    H = D // HEAD_DIM
    L = S // d
    scale = HEAD_DIM ** -0.5
    slopes = _alibi_slopes(H)
    QB = ATTN_BLOCK

    def body(qc, qp, qn, kc, kp, kn, vc, vp, vn, dc_, dp_, dn_, lc, lp, ln, ec, ep, en,
             out_ref, qbuf, kbuf, vbuf, dobuf, lbuf, ebuf):
        b = pl.program_id(0)
        for buf, trio in ((qbuf, (qc, qp, qn)), (kbuf, (kc, kp, kn)), (vbuf, (vc, vp, vn)),
                          (dobuf, (dc_, dp_, dn_)), (lbuf, (lc, lp, ln)), (ebuf, (ec, ep, en))):
            _fill_window(buf, *trio)
        dist, valid = _band(b, L)
        dist = dist * float(d)
        lse_c, del_c = lc[...], ec[...]
        lse_w, del_w = lbuf[...].T, ebuf[...].T
        for h in range(H):
            cs = slice(h * HEAD_DIM, (h + 1) * HEAD_DIM)
            bias = slopes[h] * dist
            q, do_h = qc[:, cs], dc_[:, cs]
            kw, vw = kbuf[:, cs], vbuf[:, cs]
            s = lax.dot_general(q, kw, NT, preferred_element_type=F32) * scale - bias
            p = jnp.where(valid, jnp.exp(s - _lane_col(lse_c, h)), 0.0)
            dp = lax.dot_general(do_h, vw, NT, preferred_element_type=F32)
            ds = p * (dp - _lane_col(del_c, h))
            dq = jnp.dot(ds.astype(BF), kw, preferred_element_type=F32) * scale
            out_ref[:, cs] = dq.astype(BF)
            qw, dow = qbuf[:, cs], dobuf[:, cs]
            k, v = kc[:, cs], vc[:, cs]
            st = lax.dot_general(k, qw, NT, preferred_element_type=F32) * scale - bias
            pt = jnp.where(valid, jnp.exp(st - lse_w[h:h + 1, :]), 0.0)
            dv = jnp.dot(pt.astype(BF), dow, preferred_element_type=F32)
            dpt = lax.dot_general(v, dow, NT, preferred_element_type=F32)
            dst = pt * (dpt - del_w[h:h + 1, :])
            dk = jnp.dot(dst.astype(BF), qw, preferred_element_type=F32) * scale
            out_ref[:, D + h * HEAD_DIM:D + (h + 1) * HEAD_DIM] = dk.astype(BF)
            out_ref[:, 2 * D + h * HEAD_DIM:2 * D + (h + 1) * HEAD_DIM] = dv.astype(BF)

    W = 2 * QB
    return pl.pallas_call(
        body, name=name, out_shape=jax.ShapeDtypeStruct((S, D3), BF), grid=(S // QB,),
        in_specs=(_window_specs(S, D, 0) + _window_specs(S, D, 1) + _window_specs(S, D, 2)
                  + _window_specs(S, D, 0) + _window_specs(S, STAT_LANES, 0) + _window_specs(S, STAT_LANES, 0)),
        out_specs=pl.BlockSpec((QB, D3), lambda b: (b, 0)),
        scratch_shapes=[pltpu.VMEM((W, D), BF), pltpu.VMEM((W, D), BF), pltpu.VMEM((W, D), BF),
                        pltpu.VMEM((W, D), BF), pltpu.VMEM((W, STAT_LANES), F32), pltpu.VMEM((W, STAT_LANES), F32)],
        compiler_params=_params("parallel"),
    )(qkv, qkv, qkv, qkv, qkv, qkv, qkv, qkv, qkv, do, do, do, lse, lse, lse, delta, delta, delta)


def _cast_bf16(name, w, layers):
    _, R, C = w.shape
    tr = _row_tile(R, C)
    first = layers[0]

    def body(w_ref, o_ref):
        o_ref[...] = w_ref[...].astype(BF)

    return pl.pallas_call(
        body, name=name, out_shape=jax.ShapeDtypeStruct((len(layers), R, C), BF), grid=(len(layers), R // tr),
        in_specs=[pl.BlockSpec((None, tr, C), lambda l, i: (first + l, i, 0))],
        out_specs=pl.BlockSpec((None, tr, C), lambda l, i: (l, i, 0)),
        compiler_params=_params("parallel", "parallel"),
    )(w)


N_PEERS = 7


def _grad_sum(name, g, others):
    _, _, R, C = g.shape
    tr = _row_tile(R, C, 1 << 18)

    def body(g_ref, b_ref, o_ref):
        tot = g_ref[...].astype(F32)
        for s in range(N_PEERS):
            tot = tot + b_ref[s].astype(F32)
        o_ref[...] = tot

    def mine(i):
        return (2 * lax.axis_index("x") + lax.axis_index("y"), lax.axis_index("c"), i, 0)

    return pl.pallas_call(
        body, name=name, out_shape=jax.ShapeDtypeStruct((2, R, C), F32), grid=(R // tr,),
        in_specs=[pl.BlockSpec((None, None, tr, C), mine),
                  pl.BlockSpec((N_PEERS, tr, C), lambda i: (0, i, 0))],
        out_specs=pl.BlockSpec((None, tr, C), lambda i: (lax.axis_index("c"), i, 0)),
        compiler_params=_params("parallel"),
    )(g, others)


def _adamw(name, w, g, m, v, copy_g=False):
    R, C = w.shape
    tr = _row_tile(R, C, 1 << 18) if R % 16 == 0 else R
    c1 = 1.0 - ADAM_B1 ** ADAM_STEP
    c2 = 1.0 - ADAM_B2 ** ADAM_STEP

    def body(w_ref, g_ref, m_ref, v_ref, d_ref, nm_ref, nv_ref, *g_out):
        gv = g_ref[...]
        nm = ADAM_B1 * m_ref[...] + (1.0 - ADAM_B1) * gv
        nv = ADAM_B2 * v_ref[...] + (1.0 - ADAM_B2) * (gv * gv)
        nm_ref[...] = nm
        nv_ref[...] = nv
        d_ref[...] = -ADAM_LR * ((nm / c1) / (jnp.sqrt(nv / c2) + ADAM_EPS) + ADAM_WD * w_ref[...])
        if copy_g:
            g_out[0][...] = gv

    spec = pl.BlockSpec((tr, C), lambda i: (i, 0))
    n_out = 4 if copy_g else 3
    return pl.pallas_call(
        body, name=name, out_shape=[jax.ShapeDtypeStruct((R, C), F32)] * n_out, grid=(R // tr,),
        in_specs=[spec] * 4, out_specs=[spec] * n_out, compiler_params=_params("parallel"),
    )(w, g, m, v)


def _coords():
    return lax.axis_index("x"), lax.axis_index("y"), lax.axis_index("c")


def _flip(x, y, c, k):
    return (1 - x if k & 4 else x, 1 - y if k & 2 else y, 1 - c if k & 1 else c)


def _small_exchange(name, buf, reduce):
    rows = buf.shape[0]

    def body(x_ref, o_ref, land, send_sems, recv_sems):
        x, y, c = _coords()
        me = 4 * x + 2 * y + c

        def copy(k, sending):
            px, py, pc = _flip(x, y, c, k)
            slot = me if sending else 4 * px + 2 * py + pc
            return pltpu.make_async_remote_copy(
                src_ref=x_ref, dst_ref=land.at[slot], send_sem=send_sems.at[k - 1], recv_sem=recv_sems.at[k - 1],
                device_id=(px, py, pc), device_id_type=MESH)

        for k in range(1, 8):
            copy(k, True).start()
        land[me] = x_ref[...]
        for k in range(1, 8):
            copy(k, False).wait()
        if reduce:
            acc = land[0]
            for s in range(1, 8):
                acc = acc + land[s]
            o_ref[...] = acc
        else:
            o_ref[...] = land[...]

    out_shape = jax.ShapeDtypeStruct((rows, 128) if reduce else (8, rows, 128), F32)
    return pl.pallas_call(
        body, name=name, out_shape=out_shape,
        in_specs=[pl.BlockSpec(memory_space=pltpu.VMEM)], out_specs=pl.BlockSpec(memory_space=pltpu.VMEM),
        scratch_shapes=[pltpu.VMEM((8, rows, 128), F32), pltpu.SemaphoreType.DMA((7,)), pltpu.SemaphoreType.DMA((7,))],
        compiler_params=pltpu.CompilerParams(vmem_limit_bytes=VMEM_LIMIT),
    )(buf)


def _handshake(peers):
    barrier = pltpu.get_barrier_semaphore()
    for peer in peers:
        pl.semaphore_signal(barrier, inc=1, device_id=peer, device_id_type=MESH)
    pl.semaphore_wait(barrier, len(peers))


def _sequencer_mesh():
    return plsc.ScalarSubcoreMesh(axis_name="sequencer", num_cores=1)


def _allgather_weight(name, shard, collective_id):
    def body(in_ref, out_ref, send_sems, recv_sems, local_sem):
        x, y, c = _coords()
        chip = 2 * x + y
        sib = (x, y, 1 - c)
        chips = [_flip(x, y, c, k) for k in (4, 2, 6)]
        _handshake([sib] + chips)

        def slab(cx, cy, cc):
            return out_ref.at[2 * cx + cy, cc]

        def copy(k, src, dst, to):
            return pltpu.make_async_remote_copy(
                src_ref=src, dst_ref=dst, send_sem=send_sems.at[k], recv_sem=recv_sems.at[k],
                device_id=to, device_id_type=MESH)

        local = pltpu.make_async_copy(in_ref.at[c], out_ref.at[chip, c], local_sem)
        local.start()
        started = []
        for j, to in enumerate(chips):
            started.append(copy(1 + j, in_ref.at[c], slab(x, y, c), to))
        started.append(copy(0, in_ref.at[c], slab(x, y, c), sib))
        for cp in started:
            cp.start()
        for j, (px, py, pc) in enumerate(chips):
            held = slab(px, py, c)
            copy(1 + j, held, held, (px, py, pc)).wait_recv()
            cp = copy(4 + j, held, held, sib)
            cp.start()
            started.append(cp)
        got = slab(x, y, 1 - c)
        copy(0, got, got, sib).wait_recv()
        for j, (px, py, pc) in enumerate(chips):
            got = slab(px, py, 1 - c)
            copy(4 + j, got, got, sib).wait_recv()
        for cp in started:
            cp.wait_send()
        local.wait()

    return pl.kernel(
        body, out_type=jax.ShapeDtypeStruct((N_CHIPS,) + shard.shape, shard.dtype),
        mesh=_sequencer_mesh(), name=name,
        scratch_types=[pltpu.SemaphoreType.DMA((7,)), pltpu.SemaphoreType.DMA((7,)), pltpu.SemaphoreType.DMA],
        compiler_params=pltpu.CompilerParams(collective_id=collective_id),
    )(shard)


def _grad_exchange(name, gs, collective_id, after=None):
    T = len(gs)

    def body(*refs):
        ins, outs = refs[:T], refs[-T - 2:-2]
        send_sems, recv_sems = refs[-2:]
        x, y, c = _coords()
        chip = 2 * x + y
        sib = (x, y, 1 - c)
        chips = [_flip(x, y, c, k) for k in (4, 2, 6)]
        _handshake([sib] + [(px, py, cc) for px, py, _ in chips for cc in (0, 1)])
        cps = []
        for t in range(T):
            for j, (px, py, _) in enumerate(chips):
                for cc in (0, 1):
                    cps.append(pltpu.make_async_remote_copy(
                        src_ref=ins[t].at[2 * px + py, cc], dst_ref=outs[t].at[1 + 2 * j + c],
                        send_sem=send_sems.at[t, 1 + 2 * j + cc], recv_sem=recv_sems.at[t, 1 + 2 * j + c],
                        device_id=(px, py, cc), device_id_type=MESH))
            cps.append(pltpu.make_async_remote_copy(
                src_ref=ins[t].at[chip, 1 - c], dst_ref=outs[t].at[0], send_sem=send_sems.at[t, 0],
                recv_sem=recv_sems.at[t, 0], device_id=sib, device_id_type=MESH))
        for cp in cps:
            cp.start()
        for cp in cps:
            cp.wait_send()
        for t in range(T):
            for slot in range(N_PEERS):
                pltpu.make_async_remote_copy(
                    src_ref=outs[t].at[slot], dst_ref=outs[t].at[slot], send_sem=send_sems.at[t, slot],
                    recv_sem=recv_sems.at[t, slot], device_id=sib, device_id_type=MESH).wait_recv()

    operands = list(gs) + ([after] if after is not None else [])
    return pl.kernel(
        body, out_type=[jax.ShapeDtypeStruct((N_PEERS,) + g.shape[2:], g.dtype) for g in gs],
        mesh=_sequencer_mesh(), name=name,
        scratch_types=[pltpu.SemaphoreType.DMA((T, N_PEERS)), pltpu.SemaphoreType.DMA((T, N_PEERS))],
        compiler_params=pltpu.CompilerParams(collective_id=collective_id),
    )(*operands)


def _pair_fill(name, fulls):
    T = len(fulls)

    def body(*refs):
        outs = refs[T:2 * T]
        send_sems, recv_sems = refs[2 * T:]
        x, y, c = _coords()
        sib = (x, y, 1 - c)
        cps = []
        for t in range(T):
            send = pltpu.make_async_remote_copy(
                src_ref=outs[t].at[c], dst_ref=outs[t].at[c], send_sem=send_sems.at[t],
                recv_sem=recv_sems.at[t], device_id=sib, device_id_type=MESH)
            recv = pltpu.make_async_remote_copy(
                src_ref=outs[t].at[1 - c], dst_ref=outs[t].at[1 - c], send_sem=send_sems.at[t],
                recv_sem=recv_sems.at[t], device_id=sib, device_id_type=MESH)
            send.start()
            cps.append((send, recv))
        for send, recv in cps:
            send.wait_send()
            recv.wait_recv()

    anyspec = pl.BlockSpec(memory_space=pl.ANY)
    return pl.pallas_call(
        body, name=name,
        out_shape=[jax.ShapeDtypeStruct(f.shape, f.dtype) for f in fulls],
        in_specs=[anyspec] * T, out_specs=[anyspec] * T,
        input_output_aliases={t: t for t in range(T)},
        scratch_shapes=[pltpu.SemaphoreType.DMA((T,)), pltpu.SemaphoreType.DMA((T,))],
    )(*fulls)


def _pack(arrs):
    flat = jnp.concatenate([a.reshape(-1).astype(F32) for a in arrs])
    n = flat.shape[0]
    rows = -(-n // 1024) * 8
    return jnp.pad(flat, (0, rows * 128 - n)).reshape(rows, 128)


def _unpack(buf, shapes):
    flat = buf.reshape(-1)
    out, pos = [], 0
    for s in shapes:
        n = math.prod(s)
        out.append(flat[pos:pos + n].reshape(s))
        pos += n
    return out


def kernel(x, mix_norm_g, ffn_norm_g, final_norm_g, sc_w_in, sc_conv_w, sc_conv_b, sc_w_out, attn_w_qkv, attn_w_out, ffn_w_up, ffn_conv_w, ffn_conv_b, ffn_w_down, loss_target, m_mix_norm_g, m_ffn_norm_g, m_final_norm_g, m_sc_w_in, m_sc_conv_w, m_sc_conv_b, m_sc_w_out, m_attn_w_qkv, m_attn_w_out, m_ffn_w_up, m_ffn_conv_w, m_ffn_conv_b, m_ffn_w_down, v_mix_norm_g, v_ffn_norm_g, v_final_norm_g, v_sc_w_in, v_sc_conv_w, v_sc_conv_b, v_sc_w_out, v_attn_w_qkv, v_attn_w_out, v_ffn_w_up, v_ffn_conv_w, v_ffn_conv_b, v_ffn_w_down):
    S, D = x.shape[1], x.shape[2]
    xi, yi, ci = _coords()
    chip = 2 * xi + yi
    x0 = x.reshape(S, D)
    tgt = loss_target.reshape(S, D)

    conv_shapes = [sc_conv_w.shape, ffn_conv_w.shape]
    allc = _small_exchange("gather_conv_w", _pack([sc_conv_w, ffn_conv_w]), reduce=False)
    per_chip = [_unpack(allc[2 * k], conv_shapes) for k in range(N_CHIPS)]
    scw = jnp.concatenate([p[0] for p in per_chip], axis=-1)[0]
    fcw = jnp.concatenate([p[1] for p in per_chip], axis=-1)
    scb = sc_conv_b

    big_names = ["sc_w_in", "sc_w_out", "attn_w_qkv", "attn_w_out", "ffn_w_up", "ffn_w_down"]
    big = dict(zip(big_names, [sc_w_in, sc_w_out, attn_w_qkv, attn_w_out, ffn_w_up, ffn_w_down]))
    n_gathers = [0]

    def gather(tag, w, layers):
        _, R, C = w.shape
        shard = _cast_bf16("cast_" + tag, w, layers).reshape(2, len(layers) * R // 2, C)
        cid = n_gathers[0]
        n_gathers[0] += 1
        return _allgather_weight("allgather_" + tag, shard, cid).reshape(N_CHIPS, len(layers), R, C)

    w_in = gather("sc_w_in", sc_w_in, (0,))
    w_out = gather("sc_w_out", sc_w_out, (0,))
    w_ups = [gather("ffn_w_up0", ffn_w_up, (0,))]
    w_dn = gather("ffn_w_down", ffn_w_down, (0, 1))
    w_qkv = gather("attn_w_qkv", attn_w_qkv, (0,))
    w_ao = gather("attn_w_out", attn_w_out, (0,))
    w_ups.append(gather("ffn_w_up1", ffn_w_up, (1,)))

    big_grads = {}
    exchanges = []

    def reduce_scatter(tags, grads):
        after = exchanges[-1][0] if exchanges else None
        outs = _grad_exchange("rs_exchange_" + tags[0], grads, n_gathers[0] + len(exchanges), after=after)
        exchanges.append(outs)
        for tag, g, others in zip(tags, grads, outs):
            big_grads[tag] = _grad_sum("rs_sum_" + tag, g, others)

    z, h0 = _mm_nn_col("sc_in", x0, w_in, 0, pre_g=mix_norm_g[0:1])
    y = _sc_fwd("sc_gate", z, scw, scb)
    x1, h1 = _mm_nn_row("sc_out", y, w_out, 0, x0, norm_g=ffn_norm_g[0:1])
    u0 = _mm_nn_col("ffn_up0", h1, w_ups[0], 0)
    f0, v0 = _ffn_fwd("ffn_gate0", u0, fcw[0], ffn_conv_b[0:1])
    x2 = _mm_nn_row("ffn_down0", f0, w_dn, 0, x1)
    h2s = _rmsnorm_fwd("norm_mix1", x2, mix_norm_g[1:2], dilated=True)
    qkvs = [_mm_nn_col(f"attn_qkv{d}", h, w_qkv, 0, col_off=gi * 3 * D, ncols=3 * D)
            for gi, (h, d) in enumerate(zip(h2s, DILATIONS))]
    og, lg = zip(*[_attn_fwd(f"attn_fwd{d}", q, d) for q, d in zip(qkvs, DILATIONS)])
    o32, ob, lse = _attn_combine("attn_combine", list(og), list(lg))
    x3, h3 = _mm_nn_row("attn_out", ob, w_ao, 0, x2, norm_g=ffn_norm_g[1:2])
    u1 = _mm_nn_col("ffn_up1", h3, w_ups[1], 0)
    f1, v1 = _ffn_fwd("ffn_gate1", u1, fcw[1], ffn_conv_b[1:2])
    x4 = _mm_nn_row("ffn_down1", f1, w_dn, 1, x3)

    dx4, dx4b, dg_final, loss_part = _rmsnorm_bwd("loss_norm_bwd", x4, final_norm_g.reshape(1, D), target=tgt)

    def ffn_backward(layer, xin, h, u, v, f, dxo, dxob, gain, g_up, g_dn, last=False):
        g_dn = _mm_tn_row(f"ffn_down_dw{layer}", f, dxob, prev=g_dn, layer=layer)
        if last:
            reduce_scatter(["ffn_w_down"], [g_dn])
        df = _mm_nt_row(f"ffn_down_dx{layer}", dxob, w_dn, layer, after=g_dn)
        du, dwb = _ffn_bwd(f"ffn_gate_bwd{layer}", u, v, df, fcw[layer])
        g_up = _mm_tn_col(f"ffn_up_dw{layer}", h, du, w_ups[layer].shape[3], prev=g_up, layer=layer)
        if last:
            reduce_scatter(["ffn_w_up"], [g_up])
        dh = _mm_nt_col(f"ffn_up_dx{layer}", du, w_ups[layer], 0, after=g_up)
        dxi, dxib, dg = _rmsnorm_bwd(f"norm_ffn_bwd{layer}", xin, gain, dhs=[(dh, 1)], dres=dxo)
        return dxi, dxib, dg, dwb, g_up, g_dn

    dx3, dx3b, dg_ffn1, dwb_ffn1, g_up, g_dn = ffn_backward(1, x3, h3, u1, v1, f1, dx4, dx4b, ffn_norm_g[1:2], None, None)

    g_ao = _mm_tn_row("attn_out_dw", ob, dx3b)
    reduce_scatter(["attn_w_out"], [g_ao])
    do = _mm_nt_row("attn_out_dx", dx3b, w_ao, 0, after=g_ao)
    dos, lss, dls = _attn_bwd_prep("attn_bwd_prep", do, o32, lse)
    dqkvs = [_attn_bwd(f"attn_bwd{d}", q, a, b, c_, d)
             for q, a, b, c_, d in zip(qkvs, dos, lss, dls, DILATIONS)]
    g_qkv = None
    for gi, (h, dq, d) in enumerate(zip(h2s, dqkvs, DILATIONS)):
        g_qkv = _mm_tn_col(f"attn_qkv_dw{d}", h, dq, w_qkv.shape[3], col_off=gi * 3 * D, prev=g_qkv)
    reduce_scatter(["attn_w_qkv"], [g_qkv])
    dh2s = [(_mm_nt_col(f"attn_qkv_dx{d}", dq, w_qkv, 0, col_off=gi * 3 * D, after=g_qkv), d)
            for gi, (dq, d) in enumerate(zip(dqkvs, DILATIONS))]
    dx2, dx2b, dg_mix1 = _rmsnorm_bwd("norm_mix_bwd1", x2, mix_norm_g[1:2], dhs=dh2s, dres=dx3)

    dx1, dx1b, dg_ffn0, dwb_ffn0, g_up, g_dn = ffn_backward(
        0, x1, h1, u0, v0, f0, dx2, dx2b, ffn_norm_g[0:1], g_up, g_dn, last=True)

    g_out = _mm_tn_row("sc_out_dw", y, dx1b)
    dy = _mm_nt_row("sc_out_dx", dx1b, w_out, 0, after=g_out)
    dz, dwb_sc = _sc_bwd("sc_gate_bwd", z, dy, scw, scb)
    g_in = _mm_tn_col("sc_in_dw", h0, dz, w_in.shape[3])
    reduce_scatter(["sc_w_out", "sc_w_in"], [g_out, g_in])
    dh0 = _mm_nt_col("sc_in_dx", dz, w_in, 0, after=g_in)
    dx0, _, dg_mix0 = _rmsnorm_bwd("norm_mix_bwd0", x0, mix_norm_g[0:1], dhs=[(dh0, 1)], dres=dx1)

    dconv_sc = dwb_sc[0:3].reshape(1, 3, D)
    dbias_sc = dwb_sc[3:4]
    dconv_ffn = jnp.stack([dwb_ffn0[0:3], dwb_ffn1[0:3]])
    dbias_ffn = jnp.concatenate([dwb_ffn0[3:4], dwb_ffn1[3:4]], axis=0)
    small_parts = [jnp.concatenate([dg_mix0, dg_mix1], axis=0), jnp.concatenate([dg_ffn0, dg_ffn1], axis=0),
                   dg_final.reshape(D), dconv_sc, dbias_sc, dconv_ffn, dbias_ffn, loss_part[0, 0:1]]
    small_shapes = [a.shape for a in small_parts]
    summed = _unpack(_small_exchange("allreduce_small", _pack(small_parts), reduce=True), small_shapes)
    g_mix, g_ffn, g_final, g_scw_full, g_scb, g_fcw_full, g_fcb, loss = summed
    loss = loss.reshape(())
    cw = sc_conv_w.shape[2]
    g_scw = lax.dynamic_slice_in_dim(g_scw_full, chip * cw, cw, axis=2)
    fw = ffn_conv_w.shape[2]
    g_fcw = lax.dynamic_slice_in_dim(g_fcw_full, chip * fw, fw, axis=2)

    names = ["mix_norm_g", "ffn_norm_g", "final_norm_g", "sc_w_in", "sc_conv_w", "sc_conv_b", "sc_w_out",
             "attn_w_qkv", "attn_w_out", "ffn_w_up", "ffn_conv_w", "ffn_conv_b", "ffn_w_down"]
    ws = dict(zip(names, [mix_norm_g, ffn_norm_g, final_norm_g, sc_w_in, sc_conv_w, sc_conv_b, sc_w_out,
                          attn_w_qkv, attn_w_out, ffn_w_up, ffn_conv_w, ffn_conv_b, ffn_w_down]))
    ms = dict(zip(names, [m_mix_norm_g, m_ffn_norm_g, m_final_norm_g, m_sc_w_in, m_sc_conv_w, m_sc_conv_b, m_sc_w_out,
                          m_attn_w_qkv, m_attn_w_out, m_ffn_w_up, m_ffn_conv_w, m_ffn_conv_b, m_ffn_w_down]))
    vs = dict(zip(names, [v_mix_norm_g, v_ffn_norm_g, v_final_norm_g, v_sc_w_in, v_sc_conv_w, v_sc_conv_b, v_sc_w_out,
                          v_attn_w_qkv, v_attn_w_out, v_ffn_w_up, v_ffn_conv_w, v_ffn_conv_b, v_ffn_w_down]))
    gs = {"mix_norm_g": g_mix, "ffn_norm_g": g_ffn, "final_norm_g": g_final, "sc_conv_w": g_scw,
          "sc_conv_b": g_scb, "ffn_conv_w": g_fcw, "ffn_conv_b": g_fcb}
    filled = _pair_fill("rs_pair_fill", [big_grads[n] for n in big_names])
    gs.update({n: f.reshape(big[n].shape) for n, f in zip(big_names, filled)})

    deltas, new_m, new_v = {}, {}, {}
    small_names = [n for n in names if n not in big_names]
    packed = [_pack([d[n] for n in small_names]) for d in (ws, gs, ms, vs)]
    outs = _adamw("adamw_small", *packed)
    shapes = [ws[n].shape for n in small_names]
    for res, o in zip((deltas, new_m, new_v), outs):
        res.update(dict(zip(small_names, _unpack(o, shapes))))
    for n in big_names:
        shp = ws[n].shape
        two_d = (shp[0] * shp[1], shp[2])
        outs = _adamw("adamw_" + n, *[d[n].reshape(two_d) for d in (ws, gs, ms, vs)], copy_g=True)
        for res, o in zip((deltas, new_m, new_v, gs), outs):
            res[n] = o.reshape(shp)

    return (loss, dx0.reshape(x.shape), *[gs[n] for n in names], *[deltas[n] for n in names],
            *[new_m[n] for n in names], *[new_v[n] for n in names])
```
